```python
import math
import jax
import jax.numpy as jnp
from jax import lax
import numpy as np

D_MODEL = 1024
BATCH = 8
SEQ = 8192
DEPTH = 4

D_MIX = D_MODEL
D_CONV = D_MIX // 2
CONV_GROUPS = 8
D_GLA = D_MIX - D_CONV
GLA_HEADS = 4
HEAD_V = D_GLA // GLA_HEADS
HEAD_K = HEAD_V // 2
D_GLA_K = GLA_HEADS * HEAD_K
GATE_RANK = 16
GATE_NORMALIZER = 16.0
CHUNK = 64
D_FF = 2816
CONV_WIDTH = 3
EPS = 1e-6
SPLIT_SIZES = (D_CONV, D_CONV, D_CONV,
               D_GLA_K, D_GLA_K, D_GLA, D_GLA,
               GATE_RANK, GATE_RANK)
D_IN = sum(SPLIT_SIZES)

kernel_name = "hybrid_shortconv_gla_convffn_encoder"


def rmsnorm(x, g):
    xf = x.astype(jnp.float32)
    y = xf * lax.rsqrt(jnp.mean(xf * xf, axis=-1, keepdims=True) + EPS)
    return (y * g.astype(jnp.float32)).astype(x.dtype)


def dwconv3(x, w):
    xp = jnp.pad(x, ((0, 0), (1, 1), (0, 0)))
    return w[0] * xp[:, :-2] + w[1] * xp[:, 1:-1] + w[2] * xp[:, 2:]


def split_cols(p):
    idx, acc = [], 0
    for s in SPLIT_SIZES[:-1]:
        acc += s
        idx.append(acc)
    return jnp.split(p, idx, axis=-1)


def gla_chunked(q, k, v, log_a):
    b_, h_, L, dk = q.shape
    dv = v.shape[-1]
    n = L // CHUNK
    q = q.reshape(b_, h_, n, CHUNK, dk)
    k = k.reshape(b_, h_, n, CHUNK, dk)
    v = v.reshape(b_, h_, n, CHUNK, dv)
    cum = jnp.cumsum(log_a.reshape(b_, h_, n, CHUNK, dk), axis=3)
    cum_last = cum[:, :, :, -1:, :]
    q_in = q * jnp.exp(cum)
    k_in = k * jnp.exp(-cum)
    k_out = k * jnp.exp(cum_last - cum)
    mask = jnp.tril(jnp.ones((CHUNK, CHUNK), dtype=bool))
    scores = jnp.einsum('bhnid,bhnjd->bhnij', q_in, k_in)
    scores = jnp.where(mask, scores, 0.0)
    o_intra = jnp.einsum('bhnij,bhnje->bhnie', scores, v)
    kv = jnp.einsum('bhnjd,bhnje->bhnde', k_out, v)
    decay = jnp.exp(cum_last[:, :, :, 0, :])

    def step(state, inp):
        d_n, kv_n = inp
        return d_n[..., None] * state + kv_n, state

    _, s_prev = lax.scan(step, jnp.zeros((b_, h_, dk, dv), jnp.float32),
                         (jnp.moveaxis(decay, 2, 0), jnp.moveaxis(kv, 2, 0)))
    s_prev = jnp.moveaxis(s_prev, 0, 2)
    o_inter = jnp.einsum('bhnid,bhnde->bhnie', q_in, s_prev)
    return (o_intra + o_inter).reshape(b_, h_, L, dv)


def to_heads(t, d_head):
    b_, L, _ = t.shape
    return t.reshape(b_, L, -1, d_head).transpose(0, 2, 1, 3)


def gate_log_decay(lr, w_up, bias):
    pre = (lr @ w_up + bias).astype(jnp.float32)
    return jax.nn.log_sigmoid(pre) / GATE_NORMALIZER


def mixer(h, w_in, conv_a, gate_up_fwd, gate_bias_fwd, gate_up_bwd, gate_bias_bwd,
          gla_head_norm, w_out):
    p = h @ w_in
    gb, gc, gv, q, k, v, go, lr_f, lr_b = split_cols(p)
    y_a = gb * dwconv3(gc * gv, conv_a)
    la_f = gate_log_decay(lr_f, gate_up_fwd, gate_bias_fwd)
    la_b = gate_log_decay(lr_b, gate_up_bwd, gate_bias_bwd)
    qh = to_heads(q, HEAD_K).astype(jnp.float32) * (HEAD_K ** -0.5)
    kh = to_heads(k, HEAD_K).astype(jnp.float32)
    vh = to_heads(v, HEAD_V).astype(jnp.float32)
    af = to_heads(la_f, HEAD_K)
    ab = to_heads(la_b, HEAD_K)
    flip = lambda t: jnp.flip(t, axis=2)
    o_f = gla_chunked(qh, kh, vh, af)
    o_b = flip(gla_chunked(flip(qh), flip(kh), flip(vh), flip(ab)))
    o = o_f + o_b
    o = o * lax.rsqrt(jnp.mean(o * o, axis=-1, keepdims=True) + EPS) * gla_head_norm.astype(jnp.float32)
    b_, _, L, _ = o.shape
    o = o.transpose(0, 2, 1, 3).reshape(b_, L, D_GLA).astype(h.dtype)
    y_b = jax.nn.silu(go) * o
    y = jnp.concatenate([y_a, y_b], axis=-1)
    return y @ w_out


def conv_mlp(h, w_up, conv_w, w_down):
    u = dwconv3(h @ w_up, conv_w)
    gate, val = jnp.split(u, 2, axis=-1)
    return (jax.nn.silu(gate) * val) @ w_down


def _fwd_setup_inputs(seed: int = 0) -> dict:
    key = jax.random.key(seed)
    ks = jax.random.split(key, 20)
    f32 = jnp.float32
    nrm = lambda k, shape, scale: jax.random.normal(k, shape, f32) * scale
    gain = lambda k, shape: 1.0 + 0.05 * jax.random.normal(k, shape, f32)
    return {
        "x": jax.random.normal(ks[0], (BATCH, SEQ, D_MODEL), f32),
        "norm_mix_pre": gain(ks[1], (DEPTH, D_MODEL)),
        "norm_mix_post": gain(ks[2], (DEPTH, D_MODEL)),
        "norm_ffn_pre": gain(ks[3], (DEPTH, D_MODEL)),
        "norm_ffn_post": gain(ks[4], (DEPTH, D_MODEL)),
        "w_in": nrm(ks[5], (DEPTH, D_MODEL, D_IN), D_MODEL ** -0.5),
        "conv_a": nrm(ks[6], (DEPTH, CONV_WIDTH, D_CONV), CONV_WIDTH ** -0.5),
        "gate_up_fwd": nrm(ks[7], (DEPTH, GATE_RANK, D_GLA_K), GATE_RANK ** -0.5),
        "gate_bias_fwd": nrm(ks[8], (DEPTH, D_GLA_K), 0.1),
        "gate_up_bwd": nrm(ks[9], (DEPTH, GATE_RANK, D_GLA_K), GATE_RANK ** -0.5),
        "gate_bias_bwd": nrm(ks[10], (DEPTH, D_GLA_K), 0.1),
        "gla_head_norm": gain(ks[11], (DEPTH, HEAD_V)),
        "w_out": nrm(ks[12], (DEPTH, D_MIX, D_MODEL), D_MIX ** -0.5),
        "w_up": nrm(ks[13], (DEPTH, D_MODEL, 2 * D_FF), D_MODEL ** -0.5),
        "conv_ffn": nrm(ks[14], (DEPTH, CONV_WIDTH, 2 * D_FF), CONV_WIDTH ** -0.5),
        "w_down": nrm(ks[15], (DEPTH, D_FF, D_MODEL), D_FF ** -0.5),
    }


def _fwd_reference(x, norm_mix_pre, norm_mix_post, norm_ffn_pre, norm_ffn_post, w_in, conv_a,
              gate_up_fwd, gate_bias_fwd, gate_up_bwd, gate_bias_bwd, gla_head_norm,
              w_out, w_up, conv_ffn, w_down):
    for l in range(DEPTH):
        h = rmsnorm(x, norm_mix_pre[l])
        y = mixer(h, w_in[l], conv_a[l], gate_up_fwd[l], gate_bias_fwd[l], gate_up_bwd[l],
                  gate_bias_bwd[l], gla_head_norm[l], w_out[l])
        x = x + rmsnorm(y, norm_mix_post[l])
        h = rmsnorm(x, norm_ffn_pre[l])
        y = conv_mlp(h, w_up[l], conv_ffn[l], w_down[l])
        x = x + rmsnorm(y, norm_ffn_post[l])
    return x


import jax as _jax
import jax.numpy as _jnp

TWIN_FORMAT = 'train_step'
FWD_PARAMS = ['x', 'norm_mix_pre', 'norm_mix_post', 'norm_ffn_pre', 'norm_ffn_post', 'w_in', 'conv_a', 'gate_up_fwd', 'gate_bias_fwd', 'gate_up_bwd', 'gate_bias_bwd', 'gla_head_norm', 'w_out', 'w_up', 'conv_ffn', 'w_down']
TWIN_WEIGHTS = ['norm_mix_pre', 'norm_mix_post', 'norm_ffn_pre', 'norm_ffn_post', 'w_in', 'conv_a', 'gate_up_fwd', 'gate_bias_fwd', 'gate_up_bwd', 'gate_bias_bwd', 'gla_head_norm', 'w_out', 'w_up', 'conv_ffn', 'w_down']
TWIN_DIFF_INPUT = 'x'
TWIN_INPUTS = ['x', 'norm_mix_pre', 'norm_mix_post', 'norm_ffn_pre', 'norm_ffn_post', 'w_in', 'conv_a', 'gate_up_fwd', 'gate_bias_fwd', 'gate_up_bwd', 'gate_bias_bwd', 'gla_head_norm', 'w_out', 'w_up', 'conv_ffn', 'w_down', 'loss_target', 'm_norm_mix_pre', 'm_norm_mix_post', 'm_norm_ffn_pre', 'm_norm_ffn_post', 'm_w_in', 'm_conv_a', 'm_gate_up_fwd', 'm_gate_bias_fwd', 'm_gate_up_bwd', 'm_gate_bias_bwd', 'm_gla_head_norm', 'm_w_out', 'm_w_up', 'm_conv_ffn', 'm_w_down', 'v_norm_mix_pre', 'v_norm_mix_post', 'v_norm_ffn_pre', 'v_norm_ffn_post', 'v_w_in', 'v_conv_a', 'v_gate_up_fwd', 'v_gate_bias_fwd', 'v_gate_up_bwd', 'v_gate_bias_bwd', 'v_gla_head_norm', 'v_w_out', 'v_w_up', 'v_conv_ffn', 'v_w_down']
TWIN_OUTPUTS = ['loss', 'grad_x', 'grad_norm_mix_pre', 'grad_norm_mix_post', 'grad_norm_ffn_pre', 'grad_norm_ffn_post', 'grad_w_in', 'grad_conv_a', 'grad_gate_up_fwd', 'grad_gate_bias_fwd', 'grad_gate_up_bwd', 'grad_gate_bias_bwd', 'grad_gla_head_norm', 'grad_w_out', 'grad_w_up', 'grad_conv_ffn', 'grad_w_down', 'delta_norm_mix_pre', 'delta_norm_mix_post', 'delta_norm_ffn_pre', 'delta_norm_ffn_post', 'delta_w_in', 'delta_conv_a', 'delta_gate_up_fwd', 'delta_gate_bias_fwd', 'delta_gate_up_bwd', 'delta_gate_bias_bwd', 'delta_gla_head_norm', 'delta_w_out', 'delta_w_up', 'delta_conv_ffn', 'delta_w_down', 'new_m_norm_mix_pre', 'new_m_norm_mix_post', 'new_m_norm_ffn_pre', 'new_m_norm_ffn_post', 'new_m_w_in', 'new_m_conv_a', 'new_m_gate_up_fwd', 'new_m_gate_bias_fwd', 'new_m_gate_up_bwd', 'new_m_gate_bias_bwd', 'new_m_gla_head_norm', 'new_m_w_out', 'new_m_w_up', 'new_m_conv_ffn', 'new_m_w_down', 'new_v_norm_mix_pre', 'new_v_norm_mix_post', 'new_v_norm_ffn_pre', 'new_v_norm_ffn_post', 'new_v_w_in', 'new_v_conv_a', 'new_v_gate_up_fwd', 'new_v_gate_bias_fwd', 'new_v_gate_up_bwd', 'new_v_gate_bias_bwd', 'new_v_gla_head_norm', 'new_v_w_out', 'new_v_w_up', 'new_v_conv_ffn', 'new_v_w_down']
TWIN_LEAF_KINDS = {'loss': 'loss', 'grad_x': 'grad_x', 'grad_norm_mix_pre': 'grad_w', 'grad_norm_mix_post': 'grad_w', 'grad_norm_ffn_pre': 'grad_w', 'grad_norm_ffn_post': 'grad_w', 'grad_w_in': 'grad_w', 'grad_conv_a': 'grad_w', 'grad_gate_up_fwd': 'grad_w', 'grad_gate_bias_fwd': 'grad_w', 'grad_gate_up_bwd': 'grad_w', 'grad_gate_bias_bwd': 'grad_w', 'grad_gla_head_norm': 'grad_w', 'grad_w_out': 'grad_w', 'grad_w_up': 'grad_w', 'grad_conv_ffn': 'grad_w', 'grad_w_down': 'grad_w', 'delta_norm_mix_pre': 'delta_w', 'delta_norm_mix_post': 'delta_w', 'delta_norm_ffn_pre': 'delta_w', 'delta_norm_ffn_post': 'delta_w', 'delta_w_in': 'delta_w', 'delta_conv_a': 'delta_w', 'delta_gate_up_fwd': 'delta_w', 'delta_gate_bias_fwd': 'delta_w', 'delta_gate_up_bwd': 'delta_w', 'delta_gate_bias_bwd': 'delta_w', 'delta_gla_head_norm': 'delta_w', 'delta_w_out': 'delta_w', 'delta_w_up': 'delta_w', 'delta_conv_ffn': 'delta_w', 'delta_w_down': 'delta_w', 'new_m_norm_mix_pre': 'new_m', 'new_m_norm_mix_post': 'new_m', 'new_m_norm_ffn_pre': 'new_m', 'new_m_norm_ffn_post': 'new_m', 'new_m_w_in': 'new_m', 'new_m_conv_a': 'new_m', 'new_m_gate_up_fwd': 'new_m', 'new_m_gate_bias_fwd': 'new_m', 'new_m_gate_up_bwd': 'new_m', 'new_m_gate_bias_bwd': 'new_m', 'new_m_gla_head_norm': 'new_m', 'new_m_w_out': 'new_m', 'new_m_w_up': 'new_m', 'new_m_conv_ffn': 'new_m', 'new_m_w_down': 'new_m', 'new_v_norm_mix_pre': 'new_v', 'new_v_norm_mix_post': 'new_v', 'new_v_norm_ffn_pre': 'new_v', 'new_v_norm_ffn_post': 'new_v', 'new_v_w_in': 'new_v', 'new_v_conv_a': 'new_v', 'new_v_gate_up_fwd': 'new_v', 'new_v_gate_bias_fwd': 'new_v', 'new_v_gate_up_bwd': 'new_v', 'new_v_gate_bias_bwd': 'new_v', 'new_v_gla_head_norm': 'new_v', 'new_v_w_out': 'new_v', 'new_v_w_up': 'new_v', 'new_v_conv_ffn': 'new_v', 'new_v_w_down': 'new_v'}


def _forward(args):
    return _fwd_reference(*[args[k] for k in FWD_PARAMS])


def _output_shape():
    def fwd():
        inp = _fwd_setup_inputs(0)
        return _fwd_reference(*[inp[k] for k in FWD_PARAMS])
    out = _jax.eval_shape(fwd)
    return out.shape, out.dtype

N_MICROBATCH = 1
ADAM_LR = 0.001
ADAM_B1 = 0.9
ADAM_B2 = 0.999
ADAM_EPS = 1e-08
ADAM_WD = 0.01
ADAM_STEP = 10
PER_EXAMPLE_BATCH_AXIS = {'x': 0, 'loss_target': 0}
SHARED_INPUTS = []
_WEIGHT_DTYPES = {'norm_mix_pre': _jnp.float32, 'norm_mix_post': _jnp.float32, 'norm_ffn_pre': _jnp.float32, 'norm_ffn_post': _jnp.float32, 'w_in': _jnp.float32, 'conv_a': _jnp.float32, 'gate_up_fwd': _jnp.float32, 'gate_bias_fwd': _jnp.float32, 'gate_up_bwd': _jnp.float32, 'gate_bias_bwd': _jnp.float32, 'gla_head_norm': _jnp.float32, 'w_out': _jnp.float32, 'w_up': _jnp.float32, 'conv_ffn': _jnp.float32, 'w_down': _jnp.float32}
MOMENT_SCALE = {'norm_mix_pre': 4.416532e+00, 'norm_mix_post': 6.277919e+01, 'norm_ffn_pre': 2.422722e+00, 'norm_ffn_post': 6.356531e+01, 'w_in': 2.447318e+00, 'conv_a': 2.834778e+00, 'gate_up_fwd': 1.818854e-01, 'gate_bias_fwd': 8.044887e-01, 'gate_up_bwd': 1.840017e-01, 'gate_bias_bwd': 7.535656e-01, 'gla_head_norm': 3.401280e+00, 'w_out': 2.325699e+00, 'w_up': 1.067082e+00, 'conv_ffn': 1.121397e+00, 'w_down': 1.806445e+00}


def _to_microbatches(a, axis):
    t = _jnp.moveaxis(a, axis, 0)
    t = t.reshape((N_MICROBATCH, t.shape[0] // N_MICROBATCH) + t.shape[1:])
    return _jnp.moveaxis(t, 1, axis + 1)


def setup_inputs(seed: int = 0) -> dict:
    inp = _fwd_setup_inputs(seed)
    key = _jax.random.fold_in(_jax.random.key(seed), 7919)
    shape, _ = _output_shape()
    out = dict(inp)
    out["loss_target"] = _jax.random.normal(_jax.random.fold_in(key, 0), shape, _jnp.float32)
    for i, name in enumerate(TWIN_WEIGHTS):
        w = inp[name].astype(_jnp.float32)
        if MOMENT_SCALE is None:
            s = _jnp.sqrt(_jnp.mean(_jnp.square(w)) + 1e-30)
        else:
            s = MOMENT_SCALE[name]
        km, kv = _jax.random.split(_jax.random.fold_in(key, i + 1))
        out[name] = w
        out["m_" + name] = s * _jax.random.normal(km, w.shape, _jnp.float32)
        out["v_" + name] = (s * s) * _jax.random.uniform(kv, w.shape, _jnp.float32, 0.5, 1.5)
    if N_MICROBATCH > 1:
        for name, axis in PER_EXAMPLE_BATCH_AXIS.items():
            out[name] = _to_microbatches(out[name], axis)
    return {'x': out['x'], 'norm_mix_pre': out['norm_mix_pre'], 'norm_mix_post': out['norm_mix_post'], 'norm_ffn_pre': out['norm_ffn_pre'], 'norm_ffn_post': out['norm_ffn_post'], 'w_in': out['w_in'], 'conv_a': out['conv_a'], 'gate_up_fwd': out['gate_up_fwd'], 'gate_bias_fwd': out['gate_bias_fwd'], 'gate_up_bwd': out['gate_up_bwd'], 'gate_bias_bwd': out['gate_bias_bwd'], 'gla_head_norm': out['gla_head_norm'], 'w_out': out['w_out'], 'w_up': out['w_up'], 'conv_ffn': out['conv_ffn'], 'w_down': out['w_down'], 'loss_target': out['loss_target'], 'm_norm_mix_pre': out['m_norm_mix_pre'], 'm_norm_mix_post': out['m_norm_mix_post'], 'm_norm_ffn_pre': out['m_norm_ffn_pre'], 'm_norm_ffn_post': out['m_norm_ffn_post'], 'm_w_in': out['m_w_in'], 'm_conv_a': out['m_conv_a'], 'm_gate_up_fwd': out['m_gate_up_fwd'], 'm_gate_bias_fwd': out['m_gate_bias_fwd'], 'm_gate_up_bwd': out['m_gate_up_bwd'], 'm_gate_bias_bwd': out['m_gate_bias_bwd'], 'm_gla_head_norm': out['m_gla_head_norm'], 'm_w_out': out['m_w_out'], 'm_w_up': out['m_w_up'], 'm_conv_ffn': out['m_conv_ffn'], 'm_w_down': out['m_w_down'], 'v_norm_mix_pre': out['v_norm_mix_pre'], 'v_norm_mix_post': out['v_norm_mix_post'], 'v_norm_ffn_pre': out['v_norm_ffn_pre'], 'v_norm_ffn_post': out['v_norm_ffn_post'], 'v_w_in': out['v_w_in'], 'v_conv_a': out['v_conv_a'], 'v_gate_up_fwd': out['v_gate_up_fwd'], 'v_gate_bias_fwd': out['v_gate_bias_fwd'], 'v_gate_up_bwd': out['v_gate_up_bwd'], 'v_gate_bias_bwd': out['v_gate_bias_bwd'], 'v_gla_head_norm': out['v_gla_head_norm'], 'v_w_out': out['v_w_out'], 'v_w_up': out['v_w_up'], 'v_conv_ffn': out['v_conv_ffn'], 'v_w_down': out['v_w_down']}


def _loss(weights, diff, rest, loss_target):
    with _jax.named_scope("forward"):
        args = {**rest, TWIN_DIFF_INPUT: diff, **{k: w.astype(_WEIGHT_DTYPES[k]) for k, w in weights.items()}}
        y = _forward(args)
    with _jax.named_scope("loss_head"):
        err = _jnp.square(y.astype(_jnp.float32) - loss_target)
        return 0.5 * _jnp.sum(_jnp.mean(err, axis=-1)) if err.ndim else 0.5 * err


def _adamw(w, g, m, v):
    m = ADAM_B1 * m + (1.0 - ADAM_B1) * g
    v = ADAM_B2 * v + (1.0 - ADAM_B2) * _jnp.square(g)
    m_hat = m / (1.0 - ADAM_B1 ** ADAM_STEP)
    v_hat = v / (1.0 - ADAM_B2 ** ADAM_STEP)
    delta = -ADAM_LR * (m_hat / (_jnp.sqrt(v_hat) + ADAM_EPS) + ADAM_WD * w)
    return delta, m, v


def reference(x, norm_mix_pre, norm_mix_post, norm_ffn_pre, norm_ffn_post, w_in, conv_a, gate_up_fwd, gate_bias_fwd, gate_up_bwd, gate_bias_bwd, gla_head_norm, w_out, w_up, conv_ffn, w_down, loss_target, m_norm_mix_pre, m_norm_mix_post, m_norm_ffn_pre, m_norm_ffn_post, m_w_in, m_conv_a, m_gate_up_fwd, m_gate_bias_fwd, m_gate_up_bwd, m_gate_bias_bwd, m_gla_head_norm, m_w_out, m_w_up, m_conv_ffn, m_w_down, v_norm_mix_pre, v_norm_mix_post, v_norm_ffn_pre, v_norm_ffn_post, v_w_in, v_conv_a, v_gate_up_fwd, v_gate_bias_fwd, v_gate_up_bwd, v_gate_bias_bwd, v_gla_head_norm, v_w_out, v_w_up, v_conv_ffn, v_w_down):
    given = dict(x=x, norm_mix_pre=norm_mix_pre, norm_mix_post=norm_mix_post, norm_ffn_pre=norm_ffn_pre, norm_ffn_post=norm_ffn_post, w_in=w_in, conv_a=conv_a, gate_up_fwd=gate_up_fwd, gate_bias_fwd=gate_bias_fwd, gate_up_bwd=gate_up_bwd, gate_bias_bwd=gate_bias_bwd, gla_head_norm=gla_head_norm, w_out=w_out, w_up=w_up, conv_ffn=conv_ffn, w_down=w_down, loss_target=loss_target, m_norm_mix_pre=m_norm_mix_pre, m_norm_mix_post=m_norm_mix_post, m_norm_ffn_pre=m_norm_ffn_pre, m_norm_ffn_post=m_norm_ffn_post, m_w_in=m_w_in, m_conv_a=m_conv_a, m_gate_up_fwd=m_gate_up_fwd, m_gate_bias_fwd=m_gate_bias_fwd, m_gate_up_bwd=m_gate_up_bwd, m_gate_bias_bwd=m_gate_bias_bwd, m_gla_head_norm=m_gla_head_norm, m_w_out=m_w_out, m_w_up=m_w_up, m_conv_ffn=m_conv_ffn, m_w_down=m_w_down, v_norm_mix_pre=v_norm_mix_pre, v_norm_mix_post=v_norm_mix_post, v_norm_ffn_pre=v_norm_ffn_pre, v_norm_ffn_post=v_norm_ffn_post, v_w_in=v_w_in, v_conv_a=v_conv_a, v_gate_up_fwd=v_gate_up_fwd, v_gate_bias_fwd=v_gate_bias_fwd, v_gate_up_bwd=v_gate_up_bwd, v_gate_bias_bwd=v_gate_bias_bwd, v_gla_head_norm=v_gla_head_norm, v_w_out=v_w_out, v_w_up=v_w_up, v_conv_ffn=v_conv_ffn, v_w_down=v_w_down)
    weights = {n: given[n] for n in TWIN_WEIGHTS}
    shared = {n: given[n] for n in SHARED_INPUTS}
    per_example = {n: given[n] for n in ['x']}
    grad_fn = _jax.value_and_grad(_loss, argnums=(0, 1))

    def one_microbatch(ex, loss_target):
        ex = dict(ex)
        diff = ex.pop(TWIN_DIFF_INPUT)
        return grad_fn(weights, diff, {**shared, **ex}, loss_target)

    if N_MICROBATCH == 1:
        loss, (grad_w, grad_x) = one_microbatch(per_example, given["loss_target"])
    else:
        def body(carry, xs):
            loss_sum, grad_sum = carry
            l_k, (gw_k, gx_k) = one_microbatch(xs[0], xs[1])
            with _jax.named_scope("update"):
                return (loss_sum + l_k, _jax.tree.map(_jnp.add, grad_sum, gw_k)), gx_k

        init = (_jnp.zeros((), _jnp.float32), _jax.tree.map(_jnp.zeros_like, weights))
        (loss, grad_w), grad_x = _jax.lax.scan(body, init, (per_example, given["loss_target"]))
    with _jax.named_scope("update"):
        delta_w, new_m, new_v = {}, {}, {}
        for n in TWIN_WEIGHTS:
            delta_w[n], new_m[n], new_v[n] = _adamw(weights[n], grad_w[n], given["m_" + n], given["v_" + n])
    return (loss, grad_x, *[grad_w[n] for n in TWIN_WEIGHTS], *[delta_w[n] for n in TWIN_WEIGHTS],
            *[new_m[n] for n in TWIN_WEIGHTS], *[new_v[n] for n in TWIN_WEIGHTS])
```

```python
import math

import jax
import jax.numpy as jnp
from jax import lax
from jax.experimental import pallas as pl
from jax.experimental.pallas import tpu as pltpu

F32 = jnp.float32
BF16 = jnp.bfloat16

DEPTH = 4
D = 1024
DC = 512
DG = 512
HEADS = 4
HV = 128
HK = 64
DK = 256
RANK = 16
CH = 64
DFF = 2816
D_IN = 3104
D_INP = 3200
LR_BLK = 128
EPS = 1e-6
HB = 16
N_DEV = 8
N_CHIP = 4
LANES = 1024
SH_IN = D_IN // N_DEV
SH_FF = 2 * DFF // N_DEV
FF_HALF = N_DEV // 2

ADAM_LR, ADAM_B1, ADAM_B2, ADAM_EPS, ADAM_WD, ADAM_STEP = 0.001, 0.9, 0.999, 1e-08, 0.01, 10

VMEM_LIMIT = 48 * 1024 * 1024
TILE_TOKENS = 512
TILE_GLA = 512
TILE_FFN = 512
TILE_MM = 1024

COL_GB, COL_GC, COL_GV = 0, 1, 2
COL_Q, COL_K = 6, 7
COL_V, COL_GO = 4, 5
COL_LR = 24

BIG = ("w_in", "w_out", "w_up", "w_down")
SMALL_SHARDED = (
    ("conv_a", (DEPTH, 3, DC // N_DEV), 2),
    ("gate_up_fwd", (DEPTH, RANK, DK // N_DEV), 2),
    ("gate_up_bwd", (DEPTH, RANK, DK // N_DEV), 2),
    ("conv_ffn", (DEPTH, 3, SH_FF), 2),
)
REPLICATED = (
    ("norm_mix_pre", (DEPTH, D)), ("norm_mix_post", (DEPTH, D)), ("norm_ffn_pre", (DEPTH, D)),
    ("norm_ffn_post", (DEPTH, D)), ("gate_bias_fwd", (DEPTH, DK)), ("gate_bias_bwd", (DEPTH, DK)),
    ("gla_head_norm", (DEPTH, HV)),
)
WEIGHT_ORDER = ("norm_mix_pre", "norm_mix_post", "norm_ffn_pre", "norm_ffn_post", "w_in", "conv_a", "gate_up_fwd",
                "gate_bias_fwd", "gate_up_bwd", "gate_bias_bwd", "gla_head_norm", "w_out", "w_up", "conv_ffn", "w_down")


def _rows_for(n_elems):
    return (-(-n_elems // LANES) + 7) // 8 * 8


ROWS_SSH = _rows_for(sum(math.prod(s) for _, s, _ in SMALL_SHARDED))
ROWS_REP = _rows_for(sum(math.prod(s) for _, s in REPLICATED))
ROWS_SMALL = ROWS_SSH + ROWS_REP


def _params(sem):
    return pltpu.CompilerParams(dimension_semantics=sem, vmem_limit_bytes=VMEM_LIMIT)


def _silu_parts(x):
    s = 1.0 / (1.0 + jnp.exp(-x))
    return x * s, s


def _rstd(xf):
    return lax.rsqrt(jnp.mean(xf * xf, axis=-1, keepdims=True) + EPS)


NN, NT, TN = ((1,), (0,)), ((1,), (1,)), ((0,), (0,))


def _dot(a, b, dims):
    return lax.dot_general(a, b, (dims, ((), ())), preferred_element_type=F32)


def _mm(a, b, *, dims, grid, a_spec, b_spec, o_spec, out_shape, tile, name):
    nk = grid[2]

    def body(a_ref, b_ref, o_ref, *acc):
        prod = _dot(a_ref[...], b_ref[...], dims)
        if nk == 1:
            o_ref[...] = prod.astype(o_ref.dtype)
            return
        acc_ref = acc[0]
        k = pl.program_id(2)

        @pl.when(k == 0)
        def _():
            acc_ref[...] = prod

        @pl.when(k > 0)
        def _():
            acc_ref[...] += prod

        @pl.when(k == nk - 1)
        def _():
            o_ref[...] = acc_ref[...].astype(o_ref.dtype)

    return pl.pallas_call(
        body, name=name, grid=grid, in_specs=[pl.BlockSpec(*a_spec), pl.BlockSpec(*b_spec)],
        out_specs=pl.BlockSpec(*o_spec), out_shape=out_shape,
        scratch_shapes=[pltpu.VMEM(tile, F32)] if nk > 1 else [],
        compiler_params=_params(("parallel", "parallel", "arbitrary")),
    )(a, b)


def _halo_maps(tm, n_rows):
    r, last = tm // HB, n_rows // HB - 1
    return (lambda i: jnp.maximum(i * r - 1, 0)), (lambda i: jnp.minimum((i + 1) * r, last))


def _shift(x, prev_blk, next_blk):
    tm = x.shape[0]
    xs = jnp.concatenate([prev_blk, x, next_blk], axis=0)
    n = xs.shape[0]
    down = pltpu.roll(xs, 1, 0)[HB:HB + tm]
    up = pltpu.roll(xs, n - 1, 0)[HB:HB + tm]
    return down, up


def _edge_scales(i, n):
    return jnp.where(i > 0, 1.0, 0.0).astype(F32), jnp.where(i < n - 1, 1.0, 0.0).astype(F32)


def _gain_spec(l, width=D):
    return pl.BlockSpec((None, 1, width), lambda *_: (l, 0, 0))


def _norm_cast(x, g, l, name):
    L = x.shape[0]
    tm = min(TILE_TOKENS, L)

    def body(x_ref, g_ref, o_ref):
        xf = x_ref[...]
        o_ref[...] = (xf * _rstd(xf) * g_ref[...]).astype(BF16)

    return pl.pallas_call(
        body, name=name, grid=(L // tm,), in_specs=[pl.BlockSpec((tm, D), lambda i: (i, 0)), _gain_spec(l)],
        out_specs=pl.BlockSpec((tm, D), lambda i: (i, 0)), out_shape=jax.ShapeDtypeStruct((L, D), BF16),
        compiler_params=_params(("parallel",)),
    )(x, g)


def _post_pre(x, y, g_post, l_post, g_pre, l_pre, name):
    L = x.shape[0]
    tm = min(TILE_TOKENS, L)

    def body(x_ref, y_ref, gp_ref, gn_ref, x1_ref, h_ref):
        yf = y_ref[...].astype(F32)
        x1 = x_ref[...] + yf * _rstd(yf) * gp_ref[...]
        x1_ref[...] = x1
        h_ref[...] = (x1 * _rstd(x1) * gn_ref[...]).astype(BF16)

    tile = pl.BlockSpec((tm, D), lambda i: (i, 0))
    return pl.pallas_call(
        body, name=name, grid=(L // tm,), in_specs=[tile, tile, _gain_spec(l_post), _gain_spec(l_pre)],
        out_specs=[tile, tile],
        out_shape=[jax.ShapeDtypeStruct((L, D), F32), jax.ShapeDtypeStruct((L, D), BF16)],
        compiler_params=_params(("parallel",)),
    )(x, y, g_post, g_pre)


def _norm_bwd(yin, g, l, dout, dres, name):
    L = yin.shape[0]
    tm = min(TILE_TOKENS, L)
    with_res = dres is not None

    def body(*refs):
        if with_res:
            y_ref, g_ref, do_ref, dr_ref, din_ref, dg_ref = refs
        else:
            y_ref, g_ref, do_ref, din_ref, dg_ref = refs
        i = pl.program_id(0)
        y = y_ref[...].astype(F32)
        r = _rstd(y)
        do = do_ref[...].astype(F32)
        z = do * g_ref[...]
        din = r * z - y * (r * r * r) * jnp.mean(y * z, axis=-1, keepdims=True)
        if with_res:
            din = din + dr_ref[...]
        din_ref[...] = din.astype(din_ref.dtype)
        part = jnp.sum(do * y * r, axis=0, keepdims=True)

        @pl.when(i == 0)
        def _():
            dg_ref[...] = part

        @pl.when(i > 0)
        def _():
            dg_ref[...] += part

    tile = pl.BlockSpec((tm, D), lambda i: (i, 0))
    args = (yin, g, dout) + ((dres,) if with_res else ())
    return pl.pallas_call(
        body, name=name, grid=(L // tm,), in_specs=[tile, _gain_spec(l), tile] + ([tile] if with_res else []),
        out_specs=[tile, pl.BlockSpec((1, D), lambda i: (0, 0))],
        out_shape=[jax.ShapeDtypeStruct((L, D), F32 if with_res else BF16), jax.ShapeDtypeStruct((1, D), F32)],
        compiler_params=_params(("arbitrary",)),
    )(*args)


def _gla_consts(fwd):
    row = lax.broadcasted_iota(jnp.int32, (CH, CH), 0)
    col = lax.broadcasted_iota(jnp.int32, (CH, CH), 1)
    tri = (col <= row) if fwd else (col >= row)
    tri_t = (col >= row) if fwd else (col <= row)
    row_st = lax.broadcasted_iota(jnp.int32, (HEADS * CH, CH), 0) & (CH - 1)
    col_st = lax.broadcasted_iota(jnp.int32, (HEADS * CH, CH), 1)
    tri_st = (col_st <= row_st) if fwd else (col_st >= row_st)
    lane_head = lax.broadcasted_iota(jnp.int32, (1, DK), 1) // HK
    head_masks = [lane_head == h for h in range(HEADS)]
    srow = lax.broadcasted_iota(jnp.int32, (DG, DK), 0) // HV
    scol = lax.broadcasted_iota(jnp.int32, (DG, DK), 1) // HK
    return tri.astype(BF16), tri_t.astype(BF16), tri_st, head_masks, srow == scol


def _dot_hilo(tri_b, x):
    hi = x.astype(BF16)
    lo = (x - hi.astype(F32)).astype(BF16)
    return _dot(tri_b, hi, NN) + _dot(tri_b, lo, NN)


def _gla_chunk_fwd_terms(q_ref, k_ref, lr_ref, gp_ref, bias_ref, rows, tri_b, head_masks):
    pre = _dot(lr_ref[rows, :], gp_ref[...], NN) + bias_ref[...]
    sig_neg = 1.0 / (1.0 + jnp.exp(pre))
    a = (jnp.minimum(pre, 0.0) - jnp.log(1.0 + jnp.exp(-jnp.abs(pre)))) * (1.0 / 16.0)
    cum = _dot_hilo(tri_b, a)
    cl = jnp.sum(a, axis=0, keepdims=True)
    e = jnp.exp(cum)
    einv = jnp.exp(-cum)
    eout = jnp.exp(cl - cum)
    decay = jnp.exp(cl)
    q = q_ref[rows, :].astype(F32)
    k = k_ref[rows, :].astype(F32)
    q_in = q * e * (HK ** -0.5)
    k_in = k * einv
    k_out = k * eout
    q_st = jnp.concatenate([jnp.where(mh, q_in, 0.0) for mh in head_masks], axis=0).astype(BF16)
    return dict(sig_neg=sig_neg, e=e, einv=einv, eout=eout, decay=decay, q_in=q_in, k_in=k_in, k_out=k_out, q_st=q_st)


def _gate_specs(l):
    return [pl.BlockSpec((None, LR_BLK, DK), lambda i: (l, 0, 0)), pl.BlockSpec((None, 1, DK), lambda i: (l, 0, 0))]


def _gla_fwd(p, gpad, bias, l, o_prev, fwd, name):
    L = p.shape[0]
    tb = min(TILE_GLA, L)
    nb, ncb, nch = L // tb, tb // CH, L // CH
    blk = (lambda i: i) if fwd else (lambda i: nb - 1 - i)
    with_prev = o_prev is not None

    def body(*refs):
        if with_prev:
            q_ref, k_ref, v_ref, lr_ref, gp_ref, bias_ref, op_ref, o_ref, sp_ref, s_ref = refs
        else:
            q_ref, k_ref, v_ref, lr_ref, gp_ref, bias_ref, o_ref, sp_ref, s_ref = refs
        i = pl.program_id(0)

        @pl.when(i == 0)
        def _():
            s_ref[...] = jnp.zeros_like(s_ref)

        tri_b, _, tri_st, head_masks, blockmask = _gla_consts(fwd)
        for c in (range(ncb) if fwd else reversed(range(ncb))):
            rows = pl.ds(c * CH, CH)
            t = _gla_chunk_fwd_terms(q_ref, k_ref, lr_ref, gp_ref, bias_ref, rows, tri_b, head_masks)
            v = v_ref[rows, :]
            scores = _dot(t["q_st"], t["k_in"].astype(BF16), NT)
            a_st = jnp.where(tri_st, scores, 0.0).astype(BF16)
            r = _dot(a_st, v, NN)
            o_intra = jnp.concatenate([r[h * CH:(h + 1) * CH, h * HV:(h + 1) * HV] for h in range(HEADS)], axis=1)
            s_b = s_ref[...].astype(BF16)
            sp_ref[c] = s_b
            o = o_intra + _dot(t["q_in"].astype(BF16), s_b, NT)
            if with_prev:
                o = o + op_ref[rows, :]
            o_ref[rows, :] = o
            kv_t = _dot(v, t["k_out"].astype(BF16), TN)
            s_ref[...] = s_ref[...] * t["decay"] + jnp.where(blockmask, kv_t, 0.0)

    def col(width, c):
        return pl.BlockSpec((tb, width), lambda i: (blk(i), c))

    in_specs = [col(DK, COL_Q), col(DK, COL_K), col(DG, COL_V), col(LR_BLK, COL_LR)] + _gate_specs(l)
    args = [p, p, p, p, gpad, bias]
    if with_prev:
        in_specs.append(pl.BlockSpec((tb, DG), lambda i: (blk(i), 0)))
        args.append(o_prev)
    return pl.pallas_call(
        body, name=name, grid=(nb,), in_specs=in_specs,
        out_specs=[pl.BlockSpec((tb, DG), lambda i: (blk(i), 0)), pl.BlockSpec((ncb, DG, DK), lambda i: (blk(i), 0, 0))],
        out_shape=[jax.ShapeDtypeStruct((L, DG), F32), jax.ShapeDtypeStruct((nch, DG, DK), BF16)],
        scratch_shapes=[pltpu.VMEM((DG, DK), F32)],
        compiler_params=_params(("arbitrary",)),
    )(*args)


def _gla_bwd(p, gpad, bias, l, sprev, d_o, prev, fwd, name):
    L = p.shape[0]
    tb = min(TILE_GLA, L)
    nb, ncb = L // tb, tb // CH
    blk = (lambda i: nb - 1 - i) if fwd else (lambda i: i)
    with_prev = prev is not None

    def body(*refs):
        q_ref, k_ref, v_ref, lr_ref, gp_ref, bias_ref, sp_ref, do_ref = refs[:8]
        rest = refs[8:]
        if with_prev:
            pq_ref, pk_ref, pv_ref, plr_ref = rest[:4]
            rest = rest[4:]
        dq_ref, dk_ref, dv_ref, dlr_ref, dg_ref, db_ref, ds_ref = rest
        i = pl.program_id(0)

        @pl.when(i == 0)
        def _():
            ds_ref[...] = jnp.zeros_like(ds_ref)
            dg_ref[...] = jnp.zeros_like(dg_ref)
            db_ref[...] = jnp.zeros_like(db_ref)

        tri_b, tri_t_b, tri_st, head_masks, blockmask = _gla_consts(fwd)
        for c in (reversed(range(ncb)) if fwd else range(ncb)):
            rows = pl.ds(c * CH, CH)
            t = _gla_chunk_fwd_terms(q_ref, k_ref, lr_ref, gp_ref, bias_ref, rows, tri_b, head_masks)
            v = v_ref[rows, :]
            do = do_ref[rows, :]
            q_in, k_in, k_out = t["q_in"], t["k_in"], t["k_out"]
            q_b, k_in_b, k_out_b = q_in.astype(BF16), k_in.astype(BF16), k_out.astype(BF16)
            scores = _dot(t["q_st"], k_in_b, NT)
            a_st = jnp.where(tri_st, scores, 0.0).astype(BF16)
            s_prev = sp_ref[c]
            ds = ds_ref[...]
            ds_b = ds.astype(BF16)

            da_heads = [_dot(do[:, h * HV:(h + 1) * HV], v[:, h * HV:(h + 1) * HV], NT) for h in range(HEADS)]
            da_st = jnp.where(tri_st, jnp.concatenate(da_heads, axis=0), 0.0).astype(BF16)

            dv_heads = [_dot(a_st[h * CH:(h + 1) * CH, :], do[:, h * HV:(h + 1) * HV], TN) for h in range(HEADS)]
            dv = jnp.concatenate(dv_heads, axis=1) + _dot(k_out_b, ds_b, NT)

            x = _dot(da_st, k_in_b, NN)
            dq_in = _dot(do, s_prev, NN)
            for h in range(HEADS):
                dq_in = dq_in + jnp.where(head_masks[h], x[h * CH:(h + 1) * CH, :], 0.0)
            dk_in = _dot(da_st, t["q_st"], TN)
            dk_out = _dot(v, ds_b, NN)
            d_decay = jnp.sum(ds * s_prev.astype(F32), axis=0, keepdims=True)
            ds_ref[...] = ds * t["decay"] + jnp.where(blockmask, _dot(do, q_b, TN), 0.0)

            dq = dq_in * t["e"] * (HK ** -0.5)
            dk = dk_in * t["einv"] + dk_out * t["eout"]
            dko_ko = dk_out * k_out
            dcum = dq_in * q_in - dk_in * k_in - dko_ko
            dcl = jnp.sum(dko_ko, axis=0, keepdims=True) + d_decay * t["decay"]
            da = _dot_hilo(tri_t_b, dcum) + dcl
            dpre = da * t["sig_neg"] * (1.0 / 16.0)
            dpre_b = dpre.astype(BF16)
            dlr = _dot(dpre_b, gp_ref[...], NT)
            dg_ref[...] += _dot(lr_ref[rows, :], dpre_b, TN)
            db_ref[...] += jnp.sum(dpre, axis=0, keepdims=True)
            if with_prev:
                dq = dq + pq_ref[rows, :].astype(F32)
                dk = dk + pk_ref[rows, :].astype(F32)
                dv = dv + pv_ref[rows, :].astype(F32)
                dlr = dlr + plr_ref[rows, :].astype(F32)
            dq_ref[rows, :] = dq.astype(BF16)
            dk_ref[rows, :] = dk.astype(BF16)
            dv_ref[rows, :] = dv.astype(BF16)
            dlr_ref[rows, :] = dlr.astype(BF16)

    def col(width, c):
        return pl.BlockSpec((tb, width), lambda i: (blk(i), c))

    in_specs = [col(DK, COL_Q), col(DK, COL_K), col(DG, COL_V), col(LR_BLK, COL_LR)] + _gate_specs(l) + [
        pl.BlockSpec((ncb, DG, DK), lambda i: (blk(i), 0, 0)), col(DG, 0)]
    args = [p, p, p, p, gpad, bias, sprev, d_o]
    tiles = [col(DK, 0), col(DK, 0), col(DG, 0), col(LR_BLK, 0)]
    if with_prev:
        in_specs += tiles
        args += list(prev)
    return pl.pallas_call(
        body, name=name, grid=(nb,), in_specs=in_specs,
        out_specs=tiles + [pl.BlockSpec((LR_BLK, DK), lambda i: (0, 0)), pl.BlockSpec((1, DK), lambda i: (0, 0))],
        out_shape=[jax.ShapeDtypeStruct((L, DK), BF16), jax.ShapeDtypeStruct((L, DK), BF16),
                   jax.ShapeDtypeStruct((L, DG), BF16), jax.ShapeDtypeStruct((L, LR_BLK), BF16),
                   jax.ShapeDtypeStruct((LR_BLK, DK), F32), jax.ShapeDtypeStruct((1, DK), F32)],
        scratch_shapes=[pltpu.VMEM((DG, DK), F32)],
        compiler_params=_params(("arbitrary",)),
    )(*args)


def _mixer_out(p, conv_a, gh, l, o_tot, name):
    L = p.shape[0]
    tm = min(TILE_TOKENS, L)
    n = L // tm
    pmap, nmap = _halo_maps(tm, L)

    def body(gb_ref, gc_ref, gcp_ref, gcn_ref, gv_ref, gvp_ref, gvn_ref, go_ref, cw_ref, o_ref, gh_ref, y_ref):
        ps, ns = _edge_scales(pl.program_id(0), n)
        z = gc_ref[...].astype(F32) * gv_ref[...].astype(F32)
        zp = gcp_ref[...].astype(F32) * gvp_ref[...].astype(F32) * ps
        zn = gcn_ref[...].astype(F32) * gvn_ref[...].astype(F32) * ns
        z_dn, z_up = _shift(z, zp, zn)
        conv = cw_ref[0:1, :] * z_dn + cw_ref[1:2, :] * z + cw_ref[2:3, :] * z_up
        y_ref[:, 0:DC] = (gb_ref[...].astype(F32) * conv).astype(BF16)
        o = o_ref[...]
        go = go_ref[...].astype(F32)
        for h in range(HEADS):
            oh = o[:, h * HV:(h + 1) * HV]
            on = oh * _rstd(oh) * gh_ref[...]
            act, _ = _silu_parts(go[:, h * HV:(h + 1) * HV])
            y_ref[:, DC + h * HV:DC + (h + 1) * HV] = (act * on).astype(BF16)

    def main(c):
        return pl.BlockSpec((tm, DC), lambda i: (i, c))

    def halo(c, imap):
        return pl.BlockSpec((HB, DC), lambda i: (imap(i), c))

    return pl.pallas_call(
        body, name=name, grid=(n,),
        in_specs=[main(COL_GB), main(COL_GC), halo(COL_GC, pmap), halo(COL_GC, nmap),
                  main(COL_GV), halo(COL_GV, pmap), halo(COL_GV, nmap), main(COL_GO),
                  pl.BlockSpec((None, 3, DC), lambda i: (l, 0, 0)), pl.BlockSpec((tm, DG), lambda i: (i, 0)),
                  _gain_spec(l, HV)],
        out_specs=pl.BlockSpec((tm, D), lambda i: (i, 0)), out_shape=jax.ShapeDtypeStruct((L, D), BF16),
        compiler_params=_params(("parallel",)),
    )(p, p, p, p, p, p, p, p, conv_a, o_tot, gh)


def _mixer_out_bwd(p, conv_a, gh, l, o_tot, dy, name):
    L = p.shape[0]
    tm = min(TILE_TOKENS, L)
    n = L // tm
    pmap, nmap = _halo_maps(tm, L)

    def body(gb_ref, gbp_ref, gbn_ref, gc_ref, gcp_ref, gcn_ref, gv_ref, gvp_ref, gvn_ref, go_ref, cw_ref, o_ref,
             gh_ref, dy_ref, dyp_ref, dyn_ref, dgb_ref, dgc_ref, dgv_ref, dgo_ref, do_ref, dcw_ref, dgh_ref):
        i = pl.program_id(0)
        ps, ns = _edge_scales(i, n)
        gb = gb_ref[...].astype(F32)
        gc = gc_ref[...].astype(F32)
        gv = gv_ref[...].astype(F32)
        z = gc * gv
        zp = gcp_ref[...].astype(F32) * gvp_ref[...].astype(F32) * ps
        zn = gcn_ref[...].astype(F32) * gvn_ref[...].astype(F32) * ns
        z_dn, z_up = _shift(z, zp, zn)
        w0, w1, w2 = cw_ref[0:1, :], cw_ref[1:2, :], cw_ref[2:3, :]
        conv = w0 * z_dn + w1 * z + w2 * z_up
        dya = dy_ref[:, 0:DC].astype(F32)
        dgb_ref[...] = (dya * conv).astype(BF16)
        dc = dya * gb
        dcp = dyp_ref[...].astype(F32) * gbp_ref[...].astype(F32) * ps
        dcn = dyn_ref[...].astype(F32) * gbn_ref[...].astype(F32) * ns
        dc_dn, dc_up = _shift(dc, dcp, dcn)
        dz = w0 * dc_up + w1 * dc + w2 * dc_dn
        dgc_ref[...] = (dz * gv).astype(BF16)
        dgv_ref[...] = (dz * gc).astype(BF16)
        dcw = [jnp.sum(zs * dc, axis=0, keepdims=True) for zs in (z_dn, z, z_up)]

        o = o_ref[...]
        go = go_ref[...].astype(F32)
        dgh = jnp.zeros((1, HV), F32)
        for h in range(HEADS):
            sl = slice(h * HV, (h + 1) * HV)
            oh = o[:, sl]
            r = _rstd(oh)
            act, sg = _silu_parts(go[:, sl])
            dyb = dy_ref[:, DC + h * HV:DC + (h + 1) * HV].astype(F32)
            on = oh * r * gh_ref[...]
            dgo_ref[:, sl] = (dyb * on * (sg + act * (1.0 - sg))).astype(BF16)
            don = dyb * act
            zz = don * gh_ref[...]
            do_ref[:, sl] = (r * zz - oh * (r * r * r) * jnp.mean(oh * zz, axis=-1, keepdims=True)).astype(BF16)
            dgh = dgh + jnp.sum(don * oh * r, axis=0, keepdims=True)

        @pl.when(i == 0)
        def _():
            dcw_ref[...] = jnp.zeros_like(dcw_ref)
            dgh_ref[...] = jnp.zeros_like(dgh_ref)

        for kk in range(3):
            dcw_ref[kk:kk + 1, :] += dcw[kk]
        dgh_ref[...] += dgh

    def main(c):
        return pl.BlockSpec((tm, DC), lambda i: (i, c))

    def halo(c, imap):
        return pl.BlockSpec((HB, DC), lambda i: (imap(i), c))

    tile = pl.BlockSpec((tm, DC), lambda i: (i, 0))
    return pl.pallas_call(
        body, name=name, grid=(n,),
        in_specs=[main(COL_GB), halo(COL_GB, pmap), halo(COL_GB, nmap), main(COL_GC), halo(COL_GC, pmap),
                  halo(COL_GC, nmap), main(COL_GV), halo(COL_GV, pmap), halo(COL_GV, nmap), main(COL_GO),
                  pl.BlockSpec((None, 3, DC), lambda i: (l, 0, 0)), tile, _gain_spec(l, HV),
                  pl.BlockSpec((tm, D), lambda i: (i, 0)), halo(0, pmap), halo(0, nmap)],
        out_specs=[tile, tile, tile, tile, tile, pl.BlockSpec((3, DC), lambda i: (0, 0)),
                   pl.BlockSpec((1, HV), lambda i: (0, 0))],
        out_shape=[jax.ShapeDtypeStruct((L, DC), BF16)] * 5
        + [jax.ShapeDtypeStruct((3, DC), F32), jax.ShapeDtypeStruct((1, HV), F32)],
        compiler_params=_params(("arbitrary",)),
    )(p, p, p, p, p, p, p, p, p, p, conv_a, o_tot, gh, dy, dy, dy)


def _ffn_specs(tm, L, l, row_axis, sh_axis):
    pmap, nmap = _halo_maps(tm, L)

    def u(off, imap=None, rows=tm):
        if imap is None:
            return pl.BlockSpec((None, rows, SH_FF), lambda *g: (g[sh_axis] + off, g[row_axis], 0))
        return pl.BlockSpec((None, rows, SH_FF), lambda *g: (g[sh_axis] + off, imap(g[row_axis]), 0))

    def cw(off):
        return pl.BlockSpec((None, None, 3, SH_FF), lambda *g: (l, g[sh_axis] + off, 0, 0))

    u_specs = [u(0), u(0, pmap, HB), u(0, nmap, HB), u(FF_HALF), u(FF_HALF, pmap, HB), u(FF_HALF, nmap, HB)]
    return u_specs, [cw(0), cw(FF_HALF)]


def _conv3(x_ref, xp_ref, xn_ref, cw_ref, ps, ns):
    x = x_ref[...].astype(F32)
    x_dn, x_up = _shift(x, xp_ref[...].astype(F32) * ps, xn_ref[...].astype(F32) * ns)
    return cw_ref[0:1, :] * x_dn + cw_ref[1:2, :] * x + cw_ref[2:3, :] * x_up, (x_dn, x, x_up)


def _ffn_act(u8, cw, l, name):
    L = u8.shape[1]
    tm = min(TILE_FFN, L)
    n = L // tm
    u_specs, cw_specs = _ffn_specs(tm, L, l, 0, 1)

    def body(g_ref, gp_ref, gn_ref, v_ref, vp_ref, vn_ref, cwg_ref, cwv_ref, a_ref):
        ps, ns = _edge_scales(pl.program_id(0), n)
        gate, _ = _conv3(g_ref, gp_ref, gn_ref, cwg_ref, ps, ns)
        val, _ = _conv3(v_ref, vp_ref, vn_ref, cwv_ref, ps, ns)
        act, _ = _silu_parts(gate)
        a_ref[...] = (act * val).astype(BF16)

    return pl.pallas_call(
        body, name=name, grid=(n, FF_HALF), in_specs=u_specs + cw_specs,
        out_specs=pl.BlockSpec((None, tm, SH_FF), lambda i, d: (d, i, 0)),
        out_shape=jax.ShapeDtypeStruct((FF_HALF, L, SH_FF), BF16),
        compiler_params=_params(("parallel", "parallel")),
    )(u8, u8, u8, u8, u8, u8, cw, cw)


def _ffn_act_bwd(u8, cw, l, da, name):
    L = u8.shape[1]
    tm = min(TILE_FFN, L)
    n = L // tm
    u_specs, cw_specs = _ffn_specs(tm, L, l, 1, 0)

    def body(g_ref, gp_ref, gn_ref, v_ref, vp_ref, vn_ref, cwg_ref, cwv_ref, da_ref, du_ref, dcw_ref):
        i = pl.program_id(1)
        ps, ns = _edge_scales(i, n)
        gate, g_sh = _conv3(g_ref, gp_ref, gn_ref, cwg_ref, ps, ns)
        val, v_sh = _conv3(v_ref, vp_ref, vn_ref, cwv_ref, ps, ns)
        act, sg = _silu_parts(gate)
        da_f = da_ref[...].astype(F32)
        dgate = da_f * val * (sg + act * (1.0 - sg))
        dval = da_f * act
        du_ref[0] = dgate.astype(BF16)
        du_ref[1] = dval.astype(BF16)

        @pl.when(i == 0)
        def _():
            dcw_ref[...] = jnp.zeros_like(dcw_ref)

        for kk in range(3):
            dcw_ref[0, kk:kk + 1, :] += jnp.sum(g_sh[kk] * dgate, axis=0, keepdims=True)
            dcw_ref[1, kk:kk + 1, :] += jnp.sum(v_sh[kk] * dval, axis=0, keepdims=True)

    return pl.pallas_call(
        body, name=name, grid=(FF_HALF, n),
        in_specs=u_specs + cw_specs + [pl.BlockSpec((None, tm, SH_FF), lambda d, i: (d, i, 0))],
        out_specs=[pl.BlockSpec((2, None, tm, SH_FF), lambda d, i: (0, d, i, 0)),
                   pl.BlockSpec((2, None, 3, SH_FF), lambda d, i: (0, d, 0, 0))],
        out_shape=[jax.ShapeDtypeStruct((2, FF_HALF, L, SH_FF), BF16),
                   jax.ShapeDtypeStruct((2, FF_HALF, 3, SH_FF), F32)],
        compiler_params=_params(("parallel", "arbitrary")),
    )(u8, u8, u8, u8, u8, u8, cw, cw, da)


def _ffn_conv_t(du8, cw, l, name):
    L = du8.shape[1]
    tm = min(TILE_FFN, L)
    n = L // tm
    pmap, nmap = _halo_maps(tm, L)

    def body(x_ref, xp_ref, xn_ref, cw_ref, o_ref):
        ps, ns = _edge_scales(pl.program_id(1), n)
        x = x_ref[...].astype(F32)
        x_dn, x_up = _shift(x, xp_ref[...].astype(F32) * ps, xn_ref[...].astype(F32) * ns)
        o_ref[...] = (cw_ref[0:1, :] * x_up + cw_ref[1:2, :] * x + cw_ref[2:3, :] * x_dn).astype(BF16)

    return pl.pallas_call(
        body, name=name, grid=(N_DEV, n),
        in_specs=[pl.BlockSpec((None, tm, SH_FF), lambda d, i: (d, i, 0)),
                  pl.BlockSpec((None, HB, SH_FF), lambda d, i: (d, pmap(i), 0)),
                  pl.BlockSpec((None, HB, SH_FF), lambda d, i: (d, nmap(i), 0)),
                  pl.BlockSpec((None, None, 3, SH_FF), lambda d, i: (l, d, 0, 0))],
        out_specs=pl.BlockSpec((None, tm, SH_FF), lambda d, i: (d, i, 0)),
        out_shape=jax.ShapeDtypeStruct((N_DEV, L, SH_FF), BF16),
        compiler_params=_params(("parallel", "parallel")),
    )(du8, du8, du8, cw)


def _loss_grad(xl, target, name):
    L = xl.shape[0]
    tm = min(TILE_TOKENS, L)

    def body(x_ref, t_ref, dx_ref, sq_ref):
        i = pl.program_id(0)
        err = x_ref[...] - t_ref[...]
        dx_ref[...] = err * (1.0 / D)
        part = jnp.sum(err * err, axis=0, keepdims=True)

        @pl.when(i == 0)
        def _():
            sq_ref[...] = part

        @pl.when(i > 0)
        def _():
            sq_ref[...] += part

    tile = pl.BlockSpec((tm, D), lambda i: (i, 0))
    return pl.pallas_call(
        body, name=name, grid=(L // tm,), in_specs=[tile, tile],
        out_specs=[tile, pl.BlockSpec((1, D), lambda i: (0, 0))],
        out_shape=[jax.ShapeDtypeStruct((L, D), F32), jax.ShapeDtypeStruct((1, D), F32)],
        compiler_params=_params(("arbitrary",)),
    )(xl, target)


MESH = pl.DeviceIdType.MESH
HBM_SPEC = pl.BlockSpec(memory_space=pltpu.HBM)


def _position():
    return lax.axis_index("x"), lax.axis_index("y"), lax.axis_index("c")


def _other_chips(x, y):
    return [(1 - x, y), (x, 1 - y), (1 - x, 1 - y)]


def _all_gather(shards, name):
    n = len(shards)

    def body(*refs):
        x_refs, out_refs = refs[:n], refs[n:2 * n]
        send_sems, recv_sems, local_sems = refs[2 * n:]
        x, y, c = _position()
        me, sibling = (x, y, c), (x, y, 1 - c)
        chips = _other_chips(x, y)

        def slot(t, px, py, pc):
            return out_refs[t].at[:, 4 * px + 2 * py + pc]

        def copy(t, k, block, to, from_input=False):
            return pltpu.make_async_remote_copy(
                src_ref=x_refs[t] if from_input else slot(t, *block), dst_ref=slot(t, *block),
                send_sem=send_sems.at[k * n + t], recv_sem=recv_sems.at[k * n + t], device_id=to, device_id_type=MESH)

        mine = [pltpu.make_async_copy(x_refs[t], slot(t, *me), local_sems.at[t]) for t in range(n)]
        for cp in mine:
            cp.start()
        first = [copy(t, 0, me, sibling, True) for t in range(n)]
        first += [copy(t, 1 + j, me, (*chip, c), True) for j, chip in enumerate(chips) for t in range(n)]
        for cp in first:
            cp.start()
        passed = []
        for j, chip in enumerate(chips):
            for t in range(n):
                copy(t, 1 + j, (*chip, c), me).wait_recv()
                passed.append(copy(t, 4 + j, (*chip, c), sibling))
                passed[-1].start()
        for t in range(n):
            copy(t, 0, sibling, me).wait_recv()
        for j, chip in enumerate(chips):
            for t in range(n):
                copy(t, 4 + j, (*chip, 1 - c), me).wait_recv()
        for cp in first + passed:
            cp.wait_send()
        for cp in mine:
            cp.wait()

    return pl.pallas_call(
        body, name=name,
        out_shape=[jax.ShapeDtypeStruct((s.shape[0], N_DEV) + s.shape[1:], s.dtype) for s in shards],
        in_specs=[HBM_SPEC] * n, out_specs=[HBM_SPEC] * n,
        scratch_shapes=[pltpu.SemaphoreType.DMA((7 * n,)), pltpu.SemaphoreType.DMA((7 * n,)),
                        pltpu.SemaphoreType.DMA((n,))],
    )(*shards)


def _exchange_sibling(grads, name):
    n = len(grads)

    def body(*refs):
        g_refs, out_refs, send_sems, recv_sems = refs[:n], refs[n:2 * n], refs[2 * n], refs[2 * n + 1]
        x, y, c = _position()
        copies = [pltpu.make_async_remote_copy(
            src_ref=g_refs[t].at[:, 2 * k + (1 - c)], dst_ref=out_refs[t].at[:, k], send_sem=send_sems.at[k * n + t],
            recv_sem=recv_sems.at[k * n + t], device_id=(x, y, 1 - c), device_id_type=MESH)
            for k in range(N_CHIP) for t in range(n)]
        for cp in copies:
            cp.start()
        for cp in copies:
            cp.wait()

    return pl.pallas_call(
        body, name=name,
        out_shape=[jax.ShapeDtypeStruct((g.shape[0], N_CHIP) + g.shape[2:], g.dtype) for g in grads],
        in_specs=[HBM_SPEC] * n, out_specs=[HBM_SPEC] * n,
        scratch_shapes=[pltpu.SemaphoreType.DMA((N_CHIP * n,)), pltpu.SemaphoreType.DMA((N_CHIP * n,))],
    )(*grads)


def _row_tile(rows):
    return 256 if rows % 256 == 0 else rows


def _pair_sum(g, recv, c_idx, name):
    lay, _, rows, cols = g.shape
    tr = _row_tile(rows)

    def body(c_ref, g_ref, r_ref, o_ref):
        o_ref[...] = g_ref[...] + r_ref[...]

    def spec(blk_of):
        return pl.BlockSpec((None, None, tr, cols), lambda l, k, r, c_ref: (l, blk_of(k, c_ref), r, 0))

    return pl.pallas_call(
        body, name=name,
        grid_spec=pltpu.PrefetchScalarGridSpec(
            num_scalar_prefetch=1, grid=(lay, N_CHIP, rows // tr),
            in_specs=[spec(lambda k, c_ref: 2 * k + c_ref[0]), spec(lambda k, c_ref: k)],
            out_specs=spec(lambda k, c_ref: k)),
        out_shape=jax.ShapeDtypeStruct((lay, N_CHIP, rows, cols), g.dtype),
        compiler_params=_params(("parallel", "parallel", "parallel")),
    )(c_idx, g, recv)


def _exchange_chips(parts, name):
    n = len(parts)

    def body(*refs):
        p_refs, out_refs = refs[:n], refs[n:2 * n]
        send_sems, recv_sems, local_sems = refs[2 * n:]
        x, y, c = _position()
        my_chip = 2 * x + y
        mine = [pltpu.make_async_copy(p_refs[t].at[:, my_chip], out_refs[t].at[:, my_chip], local_sems.at[t])
                for t in range(n)]
        for cp in mine:
            cp.start()
        copies = [pltpu.make_async_remote_copy(
            src_ref=p_refs[t].at[:, 2 * cx + cy], dst_ref=out_refs[t].at[:, my_chip], send_sem=send_sems.at[j * n + t],
            recv_sem=recv_sems.at[j * n + t], device_id=(cx, cy, c), device_id_type=MESH)
            for j, (cx, cy) in enumerate(_other_chips(x, y)) for t in range(n)]
        for cp in copies:
            cp.start()
        for cp in copies:
            cp.wait()
        for cp in mine:
            cp.wait()

    return pl.pallas_call(
        body, name=name, out_shape=[jax.ShapeDtypeStruct(p.shape, p.dtype) for p in parts],
        in_specs=[HBM_SPEC] * n, out_specs=[HBM_SPEC] * n,
        scratch_shapes=[pltpu.SemaphoreType.DMA((3 * n,)), pltpu.SemaphoreType.DMA((3 * n,)),
                        pltpu.SemaphoreType.DMA((n,))],
    )(*parts)


def _sum_adamw(parts, w, m, v, name):
    lay, rows, cols = w.shape
    tr = _row_tile(rows)

    def body(p_ref, w_ref, m_ref, v_ref, g_ref, d_ref, nm_ref, nv_ref):
        g = ((p_ref[0] + p_ref[1]) + p_ref[2]) + p_ref[3]
        g_ref[...] = g
        nm = ADAM_B1 * m_ref[...] + (1.0 - ADAM_B1) * g
        nv = ADAM_B2 * v_ref[...] + (1.0 - ADAM_B2) * (g * g)
        nm_ref[...] = nm
        nv_ref[...] = nv
        m_hat = nm / (1.0 - ADAM_B1 ** ADAM_STEP)
        v_hat = nv / (1.0 - ADAM_B2 ** ADAM_STEP)
        d_ref[...] = -ADAM_LR * (m_hat / (jnp.sqrt(v_hat) + ADAM_EPS) + ADAM_WD * w_ref[...])

    tile = pl.BlockSpec((None, tr, cols), lambda l, r: (l, r, 0))
    return pl.pallas_call(
        body, name=name, grid=(lay, rows // tr),
        in_specs=[pl.BlockSpec((None, N_CHIP, tr, cols), lambda l, r: (l, 0, r, 0)), tile, tile, tile],
        out_specs=[tile] * 4, out_shape=[jax.ShapeDtypeStruct((lay, rows, cols), F32)] * 4,
        compiler_params=_params(("parallel", "parallel")),
    )(parts, w, m, v)


def _pad_rows(flat, rows):
    return jnp.pad(flat, (0, rows * LANES - flat.shape[0])).reshape(rows, LANES)


def _pack_small(tree):
    sh = jnp.concatenate([tree[n].reshape(-1) for n, _, _ in SMALL_SHARDED])
    rep = jnp.concatenate([tree[n].reshape(-1) for n, _ in REPLICATED])
    return jnp.concatenate([_pad_rows(sh, ROWS_SSH), _pad_rows(rep, ROWS_REP)], axis=0)


def _unpack_small(buf):
    out = {}
    for flat, items in ((buf[:ROWS_SSH].reshape(-1), [(n, s) for n, s, _ in SMALL_SHARDED]),
                        (buf[ROWS_SSH:].reshape(-1), REPLICATED)):
        off = 0
        for n, s in items:
            out[n] = flat[off:off + math.prod(s)].reshape(s)
            off += math.prod(s)
    return out


def _full_from_blocks(blocks, s, ax):
    return jnp.concatenate([blocks[d] for d in range(N_DEV)], axis=ax)


def _blocks_from_full(full, s, ax):
    return jnp.stack([lax.slice_in_dim(full, d * s[ax], (d + 1) * s[ax], axis=ax) for d in range(N_DEV)])


def _pack_small_grads(sharded_blocks, replicated):
    sh = jnp.concatenate([sharded_blocks[n].reshape(N_DEV, -1) for n, _, _ in SMALL_SHARDED], axis=1)
    sh = jnp.pad(sh, ((0, 0), (0, ROWS_SSH * LANES - sh.shape[1]))).reshape(N_DEV, ROWS_SSH, LANES)
    rep = _pad_rows(jnp.concatenate([replicated[n].reshape(-1) for n, _ in REPLICATED]), ROWS_REP)
    return jnp.concatenate([sh, jnp.broadcast_to(rep[None], (N_DEV, ROWS_REP, LANES))], axis=1)


def _layer_fwd(x, h1, wts, l, l_next):
    L = x.shape[0]
    tm = min(TILE_MM, L)
    nt = L // tm
    p = _mm(h1, wts["w_in"], dims=NN, grid=(nt, D_INP // 640, 1),
            a_spec=((tm, D), lambda i, j, k: (i, 0)), b_spec=((None, D, 640), lambda i, j, k: (l, 0, j)),
            o_spec=((tm, 640), lambda i, j, k: (i, j)), out_shape=jax.ShapeDtypeStruct((L, D_INP), BF16),
            tile=(tm, 640), name="proj_in")
    o_f, sp_f = _gla_fwd(p, wts["gpad_f"], wts["bias_f"], l, None, True, "gla_fwd_f")
    o_tot, sp_b = _gla_fwd(p, wts["gpad_b"], wts["bias_b"], l, o_f, False, "gla_fwd_b")
    y_cat = _mixer_out(p, wts["conv_a"], wts["gh"], l, o_tot, "mixer_out")
    y = _mm(y_cat, wts["w_out"], dims=NN, grid=(nt, 1, 1),
            a_spec=((tm, D), lambda i, j, k: (i, 0)), b_spec=((None, D, D), lambda i, j, k: (l, 0, 0)),
            o_spec=((tm, D), lambda i, j, k: (i, 0)), out_shape=jax.ShapeDtypeStruct((L, D), BF16),
            tile=(tm, D), name="proj_out")
    x1, h2 = _post_pre(x, y, wts["g2"], l, wts["g3"], l, "post_pre_mix")
    u8 = _mm(h2, wts["w_up"], dims=NN, grid=(nt, N_DEV, 1),
             a_spec=((tm, D), lambda i, j, k: (i, 0)), b_spec=((None, None, D, SH_FF), lambda i, j, k: (l, j, 0, 0)),
             o_spec=((None, tm, SH_FF), lambda i, j, k: (j, i, 0)),
             out_shape=jax.ShapeDtypeStruct((N_DEV, L, SH_FF), BF16), tile=(tm, SH_FF), name="ffn_up")
    a = _ffn_act(u8, wts["cw"], l, "ffn_act")
    y2 = _mm(a, wts["w_down"], dims=NN, grid=(nt, 1, FF_HALF),
             a_spec=((None, tm, SH_FF), lambda i, j, k: (k, i, 0)),
             b_spec=((None, SH_FF, D), lambda i, j, k: (l, k, 0)),
             o_spec=((tm, D), lambda i, j, k: (i, 0)), out_shape=jax.ShapeDtypeStruct((L, D), BF16),
             tile=(tm, D), name="ffn_down")
    x2, h1_next = _post_pre(x1, y2, wts["g4"], l, wts["g1"], l_next, "post_pre_ffn")
    saved = dict(x=x, h1=h1, p=p, o_tot=o_tot, sp_f=sp_f, sp_b=sp_b, y_cat=y_cat, y=y, x1=x1, h2=h2, u8=u8, a=a, y2=y2)
    return x2, h1_next, saved


def _layer_bwd(dx2, wts, s, l):
    L = dx2.shape[0]
    tm = min(TILE_MM, L)
    nt = L // tm
    dy2, dg4 = _norm_bwd(s["y2"], wts["g4"], l, dx2, None, "norm_bwd_ffn_post")
    da = _mm(dy2, wts["w_down"], dims=NT, grid=(nt, FF_HALF, 1),
             a_spec=((tm, D), lambda i, j, k: (i, 0)), b_spec=((None, SH_FF, D), lambda i, j, k: (l, j, 0)),
             o_spec=((None, tm, SH_FF), lambda i, j, k: (j, i, 0)),
             out_shape=jax.ShapeDtypeStruct((FF_HALF, L, SH_FF), BF16), tile=(tm, SH_FF), name="ffn_down_dx")
    dw_down = _mm(s["a"], dy2, dims=TN, grid=(FF_HALF, 1, nt),
                  a_spec=((None, tm, SH_FF), lambda i, j, k: (i, k, 0)), b_spec=((tm, D), lambda i, j, k: (k, 0)),
                  o_spec=((SH_FF, D), lambda i, j, k: (i, 0)), out_shape=jax.ShapeDtypeStruct((DFF, D), F32),
                  tile=(SH_FF, D), name="ffn_down_dw")
    du, dcw = _ffn_act_bwd(s["u8"], wts["cw"], l, da, "ffn_act_bwd")
    d_u8 = _ffn_conv_t(du.reshape(N_DEV, L, SH_FF), wts["cw"], l, "ffn_conv_t")
    dh2 = _mm(d_u8, wts["w_up"], dims=NT, grid=(nt, 1, N_DEV),
              a_spec=((None, tm, SH_FF), lambda i, j, k: (k, i, 0)),
              b_spec=((None, None, D, SH_FF), lambda i, j, k: (l, k, 0, 0)),
              o_spec=((tm, D), lambda i, j, k: (i, 0)), out_shape=jax.ShapeDtypeStruct((L, D), BF16),
              tile=(tm, D), name="ffn_up_dx")
    dw_up = _mm(s["h2"], d_u8, dims=TN, grid=(1, N_DEV, nt),
                a_spec=((tm, D), lambda i, j, k: (k, 0)), b_spec=((None, tm, SH_FF), lambda i, j, k: (j, k, 0)),
                o_spec=((None, D, SH_FF), lambda i, j, k: (j, 0, 0)),
                out_shape=jax.ShapeDtypeStruct((N_DEV, D, SH_FF), F32), tile=(D, SH_FF), name="ffn_up_dw")
    dx1, dg3 = _norm_bwd(s["x1"], wts["g3"], l, dh2, dx2, "norm_bwd_ffn_pre")
    dy, dg2 = _norm_bwd(s["y"], wts["g2"], l, dx1, None, "norm_bwd_mix_post")
    dy_cat = _mm(dy, wts["w_out"], dims=NT, grid=(nt, 1, 1),
                 a_spec=((tm, D), lambda i, j, k: (i, 0)), b_spec=((None, D, D), lambda i, j, k: (l, 0, 0)),
                 o_spec=((tm, D), lambda i, j, k: (i, 0)), out_shape=jax.ShapeDtypeStruct((L, D), BF16),
                 tile=(tm, D), name="proj_out_dx")
    dw_out = _mm(s["y_cat"], dy, dims=TN, grid=(1, 1, nt),
                 a_spec=((tm, D), lambda i, j, k: (k, 0)), b_spec=((tm, D), lambda i, j, k: (k, 0)),
                 o_spec=((D, D), lambda i, j, k: (0, 0)), out_shape=jax.ShapeDtypeStruct((D, D), F32),
                 tile=(D, D), name="proj_out_dw")
    dgb, dgc, dgv, dgo, d_o, dconv_a, dgh = _mixer_out_bwd(s["p"], wts["conv_a"], wts["gh"], l, s["o_tot"], dy_cat,
                                                          "mixer_out_bwd")
    part_f = _gla_bwd(s["p"], wts["gpad_f"], wts["bias_f"], l, s["sp_f"], d_o, None, True, "gla_bwd_f")
    dq, dk, dv, dlr, dgp_b, dbias_b = _gla_bwd(s["p"], wts["gpad_b"], wts["bias_b"], l, s["sp_b"], d_o, part_f[:4],
                                               False, "gla_bwd_b")
    dp = jnp.concatenate([dgb, dgc, dgv, dq, dk, dv, dgo, dlr], axis=1)
    dh1 = _mm(dp, wts["w_in"], dims=NT, grid=(nt, 1, D_INP // 640),
              a_spec=((tm, 640), lambda i, j, k: (i, k)), b_spec=((None, D, 640), lambda i, j, k: (l, 0, k)),
              o_spec=((tm, D), lambda i, j, k: (i, 0)), out_shape=jax.ShapeDtypeStruct((L, D), BF16),
              tile=(tm, D), name="proj_in_dx")
    dw_in = _mm(s["h1"], dp, dims=TN, grid=(1, D_INP // 640, nt),
                a_spec=((tm, D), lambda i, j, k: (k, 0)), b_spec=((tm, 640), lambda i, j, k: (k, j)),
                o_spec=((D, 640), lambda i, j, k: (0, j)), out_shape=jax.ShapeDtypeStruct((D, D_INP), F32),
                tile=(D, 640), name="proj_in_dw")
    dx0, dg1 = _norm_bwd(s["x"], wts["g1"], l, dh1, dx1, "norm_bwd_mix_pre")
    grads = dict(
        norm_mix_pre=dg1[0], norm_mix_post=dg2[0], norm_ffn_pre=dg3[0], norm_ffn_post=dg4[0],
        gate_bias_fwd=part_f[5][0], gate_bias_bwd=dbias_b[0], gla_head_norm=dgh[0],
        w_in=_blocks_from_full(dw_in, (D, SH_IN), 1), w_out=dw_out.reshape(N_DEV, D // N_DEV, D), w_up=dw_up,
        w_down=dw_down.reshape(N_DEV, DFF // N_DEV, D),
        conv_a=_blocks_from_full(dconv_a, (3, DC // N_DEV), 1),
        gate_up_fwd=_blocks_from_full(part_f[4][:RANK], (RANK, DK // N_DEV), 1),
        gate_up_bwd=_blocks_from_full(dgp_b[RANK:2 * RANK], (RANK, DK // N_DEV), 1),
        conv_ffn=dcw.reshape(N_DEV, 3, SH_FF))
    return dx0, grads


def _kernel_weights(g_in, g_out, g_up, g_down, g_small, rep):
    small = {n: jnp.moveaxis(t, 0, 1) for n, t in jax.vmap(_unpack_small)(
        jnp.concatenate([g_small[0], jnp.zeros((N_DEV, ROWS_REP, LANES), F32)], axis=1)).items()
        if n in [s[0] for s in SMALL_SHARDED]}
    w_in = jnp.concatenate([g_in[:, d] for d in range(N_DEV)] + [jnp.zeros((DEPTH, D, D_INP - D_IN), BF16)], axis=2)
    conv_a = jnp.concatenate([small["conv_a"][:, d] for d in range(N_DEV)], axis=2)
    gate_f = jnp.concatenate([small["gate_up_fwd"][:, d] for d in range(N_DEV)], axis=2).astype(BF16)
    gate_b = jnp.concatenate([small["gate_up_bwd"][:, d] for d in range(N_DEV)], axis=2).astype(BF16)
    zeros = jnp.zeros((DEPTH, LR_BLK, DK), BF16)
    return dict(
        w_in=w_in, w_out=g_out.reshape(DEPTH, D, D), w_up=g_up, w_down=g_down.reshape(DEPTH, DFF, D),
        conv_a=conv_a, cw=small["conv_ffn"],
        gpad_f=zeros.at[:, :RANK].set(gate_f), gpad_b=zeros.at[:, RANK:2 * RANK].set(gate_b),
        bias_f=rep["gate_bias_fwd"][:, None, :], bias_b=rep["gate_bias_bwd"][:, None, :],
        gh=rep["gla_head_norm"][:, None, :],
        g1=rep["norm_mix_pre"][:, None, :], g2=rep["norm_mix_post"][:, None, :],
        g3=rep["norm_ffn_pre"][:, None, :], g4=rep["norm_ffn_post"][:, None, :])


def kernel(x, norm_mix_pre, norm_mix_post, norm_ffn_pre, norm_ffn_post, w_in, conv_a, gate_up_fwd, gate_bias_fwd, gate_up_bwd, gate_bias_bwd, gla_head_norm, w_out, w_up, conv_ffn, w_down, loss_target, m_norm_mix_pre, m_norm_mix_post, m_norm_ffn_pre, m_norm_ffn_post, m_w_in, m_conv_a, m_gate_up_fwd, m_gate_bias_fwd, m_gate_up_bwd, m_gate_bias_bwd, m_gla_head_norm, m_w_out, m_w_up, m_conv_ffn, m_w_down, v_norm_mix_pre, v_norm_mix_post, v_norm_ffn_pre, v_norm_ffn_post, v_w_in, v_conv_a, v_gate_up_fwd, v_gate_bias_fwd, v_gate_up_bwd, v_gate_bias_bwd, v_gla_head_norm, v_w_out, v_w_up, v_conv_ffn, v_w_down):
    w = dict(norm_mix_pre=norm_mix_pre, norm_mix_post=norm_mix_post, norm_ffn_pre=norm_ffn_pre,
             norm_ffn_post=norm_ffn_post, w_in=w_in, conv_a=conv_a, gate_up_fwd=gate_up_fwd,
             gate_bias_fwd=gate_bias_fwd, gate_up_bwd=gate_up_bwd, gate_bias_bwd=gate_bias_bwd,
             gla_head_norm=gla_head_norm, w_out=w_out, w_up=w_up, conv_ffn=conv_ffn, w_down=w_down)
    m = dict(norm_mix_pre=m_norm_mix_pre, norm_mix_post=m_norm_mix_post, norm_ffn_pre=m_norm_ffn_pre,
             norm_ffn_post=m_norm_ffn_post, w_in=m_w_in, conv_a=m_conv_a, gate_up_fwd=m_gate_up_fwd,
             gate_bias_fwd=m_gate_bias_fwd, gate_up_bwd=m_gate_up_bwd, gate_bias_bwd=m_gate_bias_bwd,
             gla_head_norm=m_gla_head_norm, w_out=m_w_out, w_up=m_w_up, conv_ffn=m_conv_ffn, w_down=m_w_down)
    v = dict(norm_mix_pre=v_norm_mix_pre, norm_mix_post=v_norm_mix_post, norm_ffn_pre=v_norm_ffn_pre,
             norm_ffn_post=v_norm_ffn_post, w_in=v_w_in, conv_a=v_conv_a, gate_up_fwd=v_gate_up_fwd,
             gate_bias_fwd=v_gate_bias_fwd, gate_up_bwd=v_gate_up_bwd, gate_bias_bwd=v_gate_bias_bwd,
             gla_head_norm=v_gla_head_norm, w_out=v_w_out, w_up=v_w_up, conv_ffn=v_conv_ffn, w_down=v_w_down)
    axes = ("x", "y", "c")
    L = x.shape[1]
    x0 = x.reshape(L, D)
    target = loss_target.reshape(L, D)

    w_small = _pack_small(w)
    gathered = _all_gather([w[n].astype(BF16) for n in BIG] + [w_small[None, :ROWS_SSH]], "gather_weights")
    wts = _kernel_weights(*gathered, w)

    h1 = _norm_cast(x0, wts["g1"], 0, "norm_first")
    xl, saved = x0, []
    for l in range(DEPTH):
        xl, h1, s = _layer_fwd(xl, h1, wts, l, min(l + 1, DEPTH - 1))
        saved.append(s)
    dx, sq = _loss_grad(xl, target, "loss_grad")
    loss = lax.psum(0.5 * jnp.sum(sq) / D, axes)
    layer_grads = [None] * DEPTH
    for l in reversed(range(DEPTH)):
        dx, layer_grads[l] = _layer_bwd(dx, wts, saved[l], l)

    stacked = {n: jnp.stack([g[n] for g in layer_grads]) for n in layer_grads[0]}
    g_small = _pack_small_grads({n: jnp.moveaxis(stacked[n], 0, 1) for n, _, _ in SMALL_SHARDED}, stacked)
    grads = [stacked[n] for n in BIG] + [g_small[None]]
    c_idx = lax.axis_index("c").astype(jnp.int32).reshape(1)
    from_sibling = _exchange_sibling(grads, "grads_to_sibling")
    pairs = [_pair_sum(g, r, c_idx, "grads_pair_sum") for g, r in zip(grads, from_sibling)]
    parts = _exchange_chips(pairs, "grads_to_chips")

    results = {}
    for n, part in zip(BIG, parts[:-1]):
        results[n] = _sum_adamw(part, w[n], m[n], v[n], "sum_adamw")
    small = _sum_adamw(parts[-1], w_small[None], _pack_small(m)[None], _pack_small(v)[None], "sum_adamw_small")
    small = [_unpack_small(buf[0]) for buf in small]
    outs = [loss, dx.reshape(x.shape)]
    for i in range(4):
        outs += [results[n][i] if n in BIG else small[i][n] for n in WEIGHT_ORDER]
    return tuple(outs)
```

```python
import math

import jax
import jax.numpy as jnp
from jax import lax
from jax.experimental import pallas as pl
from jax.experimental.pallas import tpu as pltpu

F32 = jnp.float32
BF16 = jnp.bfloat16

DEPTH = 4
D = 1024
DC = 512
DG = 512
HEADS = 4
HV = 128
HK = 64
DK = 256
RANK = 16
CH = 64
DFF = 2816
D_IN = 3104
D_INP = 3200
LR_BLK = 128
EPS = 1e-6
HB = 16
N_DEV = 8
N_CHIP = 4
LANES = 1024
SH_IN = D_IN // N_DEV
SH_FF = 2 * DFF // N_DEV
FF_HALF = N_DEV // 2

ADAM_LR, ADAM_B1, ADAM_B2, ADAM_EPS, ADAM_WD, ADAM_STEP = 0.001, 0.9, 0.999, 1e-08, 0.01, 10

VMEM_LIMIT = 48 * 1024 * 1024
TILE_TOKENS = 512
TILE_GLA = 512
TILE_FFN = 1024
TILE_MM = 2048
TILE_MM_KIN = 1024
TILE_MM_TOKENS = 2048

COL_GB, COL_GC, COL_GV = 0, 1, 2
COL_Q, COL_K = 6, 7
COL_V, COL_GO = 4, 5
COL_LR = 24

BIG = ("w_in", "w_out", "w_up", "w_down")
SMALL_SHARDED = (
    ("conv_a", (DEPTH, 3, DC // N_DEV), 2),
    ("gate_up_fwd", (DEPTH, RANK, DK // N_DEV), 2),
    ("gate_up_bwd", (DEPTH, RANK, DK // N_DEV), 2),
    ("conv_ffn", (DEPTH, 3, SH_FF), 2),
)
REPLICATED = (
    ("norm_mix_pre", (DEPTH, D)), ("norm_mix_post", (DEPTH, D)), ("norm_ffn_pre", (DEPTH, D)),
    ("norm_ffn_post", (DEPTH, D)), ("gate_bias_fwd", (DEPTH, DK)), ("gate_bias_bwd", (DEPTH, DK)),
    ("gla_head_norm", (DEPTH, HV)),
)
WEIGHT_ORDER = ("norm_mix_pre", "norm_mix_post", "norm_ffn_pre", "norm_ffn_post", "w_in", "conv_a", "gate_up_fwd",
                "gate_bias_fwd", "gate_up_bwd", "gate_bias_bwd", "gla_head_norm", "w_out", "w_up", "conv_ffn", "w_down")


def _rows_for(n_elems):
    return (-(-n_elems // LANES) + 7) // 8 * 8


ROWS_SSH = _rows_for(sum(math.prod(s) for _, s, _ in SMALL_SHARDED))
ROWS_REP = _rows_for(sum(math.prod(s) for _, s in REPLICATED))
ROWS_SMALL = ROWS_SSH + ROWS_REP


def _params(sem):
    return pltpu.CompilerParams(dimension_semantics=sem, vmem_limit_bytes=VMEM_LIMIT)


def _silu_parts(x):
    s = 1.0 / (1.0 + jnp.exp(-x))
    return x * s, s


def _rstd(xf):
    return lax.rsqrt(jnp.mean(xf * xf, axis=-1, keepdims=True) + EPS)


NN, NT, TN = ((1,), (0,)), ((1,), (1,)), ((0,), (0,))


def _dot(a, b, dims):
    return lax.dot_general(a, b, (dims, ((), ())), preferred_element_type=F32)


def _mm(a, b, *, dims, grid, a_spec, b_spec, o_spec, out_shape, tile, name, kin=0):
    nk = grid[2]

    def body(a_ref, b_ref, o_ref, *acc):
        if kin:
            prod = _dot(a_ref[0], b_ref[0], dims)
            for d in range(1, kin):
                prod = prod + _dot(a_ref[d], b_ref[d], dims)
        else:
            prod = _dot(a_ref[...], b_ref[...], dims)
        if nk == 1:
            o_ref[...] = prod.astype(o_ref.dtype)
            return
        acc_ref = acc[0]
        k = pl.program_id(2)

        @pl.when(k == 0)
        def _():
            acc_ref[...] = prod

        @pl.when(k > 0)
        def _():
            acc_ref[...] += prod

        @pl.when(k == nk - 1)
        def _():
            o_ref[...] = acc_ref[...].astype(o_ref.dtype)

    return pl.pallas_call(
        body, name=name, grid=grid, in_specs=[pl.BlockSpec(*a_spec), pl.BlockSpec(*b_spec)],
        out_specs=pl.BlockSpec(*o_spec), out_shape=out_shape,
        scratch_shapes=[pltpu.VMEM(tile, F32)] if nk > 1 else [],
        compiler_params=_params(("parallel", "parallel", "arbitrary")),
    )(a, b)


def _halo_maps(tm, n_rows):
    r, last = tm // HB, n_rows // HB - 1
    return (lambda i: jnp.maximum(i * r - 1, 0)), (lambda i: jnp.minimum((i + 1) * r, last))


def _shift(x, prev_blk, next_blk):
    tm = x.shape[0]
    xs = jnp.concatenate([prev_blk, x, next_blk], axis=0)
    n = xs.shape[0]
    down = pltpu.roll(xs, 1, 0)[HB:HB + tm]
    up = pltpu.roll(xs, n - 1, 0)[HB:HB + tm]
    return down, up


def _edge_scales(i, n):
    return jnp.where(i > 0, 1.0, 0.0).astype(F32), jnp.where(i < n - 1, 1.0, 0.0).astype(F32)


def _gain_spec(l, width=D):
    return pl.BlockSpec((None, 1, width), lambda *_: (l, 0, 0))


def _norm_cast(x, g, l, name):
    L = x.shape[0]
    tm = min(TILE_TOKENS, L)

    def body(x_ref, g_ref, o_ref):
        xf = x_ref[...]
        o_ref[...] = (xf * _rstd(xf) * g_ref[...]).astype(BF16)

    return pl.pallas_call(
        body, name=name, grid=(L // tm,), in_specs=[pl.BlockSpec((tm, D), lambda i: (i, 0)), _gain_spec(l)],
        out_specs=pl.BlockSpec((tm, D), lambda i: (i, 0)), out_shape=jax.ShapeDtypeStruct((L, D), BF16),
        compiler_params=_params(("parallel",)),
    )(x, g)


def _post_pre(x, y, g_post, l_post, g_pre, l_pre, name):
    L = x.shape[0]
    tm = min(TILE_TOKENS, L)

    def body(x_ref, y_ref, gp_ref, gn_ref, x1_ref, h_ref):
        yf = y_ref[...].astype(F32)
        x1 = x_ref[...] + yf * _rstd(yf) * gp_ref[...]
        x1_ref[...] = x1
        h_ref[...] = (x1 * _rstd(x1) * gn_ref[...]).astype(BF16)

    tile = pl.BlockSpec((tm, D), lambda i: (i, 0))
    return pl.pallas_call(
        body, name=name, grid=(L // tm,), in_specs=[tile, tile, _gain_spec(l_post), _gain_spec(l_pre)],
        out_specs=[tile, tile],
        out_shape=[jax.ShapeDtypeStruct((L, D), F32), jax.ShapeDtypeStruct((L, D), BF16)],
        compiler_params=_params(("parallel",)),
    )(x, y, g_post, g_pre)


def _norm_bwd(yin, g, l, dout, dres, name):
    L = yin.shape[0]
    tm = min(TILE_TOKENS, L)
    with_res = dres is not None

    def body(*refs):
        if with_res:
            y_ref, g_ref, do_ref, dr_ref, din_ref, dg_ref = refs
        else:
            y_ref, g_ref, do_ref, din_ref, dg_ref = refs
        i = pl.program_id(0)
        y = y_ref[...].astype(F32)
        r = _rstd(y)
        do = do_ref[...].astype(F32)
        z = do * g_ref[...]
        din = r * z - y * (r * r * r) * jnp.mean(y * z, axis=-1, keepdims=True)
        if with_res:
            din = din + dr_ref[...]
        din_ref[...] = din.astype(din_ref.dtype)
        part = jnp.sum(do * y * r, axis=0, keepdims=True)

        @pl.when(i == 0)
        def _():
            dg_ref[...] = part

        @pl.when(i > 0)
        def _():
            dg_ref[...] += part

    tile = pl.BlockSpec((tm, D), lambda i: (i, 0))
    args = (yin, g, dout) + ((dres,) if with_res else ())
    return pl.pallas_call(
        body, name=name, grid=(L // tm,), in_specs=[tile, _gain_spec(l), tile] + ([tile] if with_res else []),
        out_specs=[tile, pl.BlockSpec((1, D), lambda i: (0, 0))],
        out_shape=[jax.ShapeDtypeStruct((L, D), F32 if with_res else BF16), jax.ShapeDtypeStruct((1, D), F32)],
        compiler_params=_params(("arbitrary",)),
    )(*args)


def _gla_consts(fwd):
    row = lax.broadcasted_iota(jnp.int32, (CH, CH), 0)
    col = lax.broadcasted_iota(jnp.int32, (CH, CH), 1)
    tri = (col <= row) if fwd else (col >= row)
    tri_t = (col >= row) if fwd else (col <= row)
    row_st = lax.broadcasted_iota(jnp.int32, (HEADS * CH, CH), 0) & (CH - 1)
    col_st = lax.broadcasted_iota(jnp.int32, (HEADS * CH, CH), 1)
    tri_st = (col_st <= row_st) if fwd else (col_st >= row_st)
    lane_head = lax.broadcasted_iota(jnp.int32, (1, DK), 1) // HK
    head_masks = [lane_head == h for h in range(HEADS)]
    srow = lax.broadcasted_iota(jnp.int32, (DG, DK), 0) // HV
    scol = lax.broadcasted_iota(jnp.int32, (DG, DK), 1) // HK
    return tri.astype(BF16), tri_t.astype(BF16), tri_st, head_masks, srow == scol


def _dot_hilo(tri_b, x):
    hi = x.astype(BF16)
    lo = (x - hi.astype(F32)).astype(BF16)
    return _dot(tri_b, hi, NN) + _dot(tri_b, lo, NN)


def _gla_chunk_fwd_terms(q_ref, k_ref, lr_ref, gp_ref, bias_ref, rows, tri_b, head_masks):
    pre = _dot(lr_ref[rows, :], gp_ref[...], NN) + bias_ref[...]
    sig_neg = 1.0 / (1.0 + jnp.exp(pre))
    a = (jnp.minimum(pre, 0.0) - jnp.log(1.0 + jnp.exp(-jnp.abs(pre)))) * (1.0 / 16.0)
    cum = _dot_hilo(tri_b, a)
    cl = jnp.sum(a, axis=0, keepdims=True)
    e = jnp.exp(cum)
    einv = jnp.exp(-cum)
    eout = jnp.exp(cl - cum)
    decay = jnp.exp(cl)
    q = q_ref[rows, :].astype(F32)
    k = k_ref[rows, :].astype(F32)
    q_in = q * e * (HK ** -0.5)
    k_in = k * einv
    k_out = k * eout
    q_st = jnp.concatenate([jnp.where(mh, q_in, 0.0) for mh in head_masks], axis=0).astype(BF16)
    return dict(sig_neg=sig_neg, e=e, einv=einv, eout=eout, decay=decay, q_in=q_in, k_in=k_in, k_out=k_out, q_st=q_st)


def _gate_specs(l):
    return [pl.BlockSpec((None, LR_BLK, DK), lambda i: (l, 0, 0)), pl.BlockSpec((None, 1, DK), lambda i: (l, 0, 0))]


def _gla_fwd(p, gpad, bias, l, o_prev, fwd, name):
    L = p.shape[0]
    tb = min(TILE_GLA, L)
    nb, ncb, nch = L // tb, tb // CH, L // CH
    blk = (lambda i: i) if fwd else (lambda i: nb - 1 - i)
    with_prev = o_prev is not None

    def body(*refs):
        if with_prev:
            q_ref, k_ref, v_ref, lr_ref, gp_ref, bias_ref, op_ref, o_ref, sp_ref, s_ref = refs
        else:
            q_ref, k_ref, v_ref, lr_ref, gp_ref, bias_ref, o_ref, sp_ref, s_ref = refs
        i = pl.program_id(0)

        @pl.when(i == 0)
        def _():
            s_ref[...] = jnp.zeros_like(s_ref)

        tri_b, _, tri_st, head_masks, blockmask = _gla_consts(fwd)
        for c in (range(ncb) if fwd else reversed(range(ncb))):
            rows = pl.ds(c * CH, CH)
            t = _gla_chunk_fwd_terms(q_ref, k_ref, lr_ref, gp_ref, bias_ref, rows, tri_b, head_masks)
            v = v_ref[rows, :]
            scores = _dot(t["q_st"], t["k_in"].astype(BF16), NT)
            a_st = jnp.where(tri_st, scores, 0.0).astype(BF16)
            r = _dot(a_st, v, NN)
            o_intra = jnp.concatenate([r[h * CH:(h + 1) * CH, h * HV:(h + 1) * HV] for h in range(HEADS)], axis=1)
            s_b = s_ref[...].astype(BF16)
            sp_ref[c] = s_b
            o = o_intra + _dot(t["q_in"].astype(BF16), s_b, NT)
            if with_prev:
                o = o + op_ref[rows, :]
            o_ref[rows, :] = o
            kv_t = _dot(v, t["k_out"].astype(BF16), TN)
            s_ref[...] = s_ref[...] * t["decay"] + jnp.where(blockmask, kv_t, 0.0)

    def col(width, c):
        return pl.BlockSpec((tb, width), lambda i: (blk(i), c))

    in_specs = [col(DK, COL_Q), col(DK, COL_K), col(DG, COL_V), col(LR_BLK, COL_LR)] + _gate_specs(l)
    args = [p, p, p, p, gpad, bias]
    if with_prev:
        in_specs.append(pl.BlockSpec((tb, DG), lambda i: (blk(i), 0)))
        args.append(o_prev)
    return pl.pallas_call(
        body, name=name, grid=(nb,), in_specs=in_specs,
        out_specs=[pl.BlockSpec((tb, DG), lambda i: (blk(i), 0)), pl.BlockSpec((ncb, DG, DK), lambda i: (blk(i), 0, 0))],
        out_shape=[jax.ShapeDtypeStruct((L, DG), F32), jax.ShapeDtypeStruct((nch, DG, DK), BF16)],
        scratch_shapes=[pltpu.VMEM((DG, DK), F32)],
        compiler_params=_params(("arbitrary",)),
    )(*args)


def _gla_bwd(p, gpad, bias, l, sprev, d_o, prev, fwd, name):
    L = p.shape[0]
    tb = min(TILE_GLA, L)
    nb, ncb = L // tb, tb // CH
    blk = (lambda i: nb - 1 - i) if fwd else (lambda i: i)
    with_prev = prev is not None

    def body(*refs):
        q_ref, k_ref, v_ref, lr_ref, gp_ref, bias_ref, sp_ref, do_ref = refs[:8]
        rest = refs[8:]
        if with_prev:
            pq_ref, pk_ref, pv_ref, plr_ref = rest[:4]
            rest = rest[4:]
        dq_ref, dk_ref, dv_ref, dlr_ref, dg_ref, db_ref, ds_ref = rest
        i = pl.program_id(0)

        @pl.when(i == 0)
        def _():
            ds_ref[...] = jnp.zeros_like(ds_ref)
            dg_ref[...] = jnp.zeros_like(dg_ref)
            db_ref[...] = jnp.zeros_like(db_ref)

        tri_b, tri_t_b, tri_st, head_masks, blockmask = _gla_consts(fwd)
        for c in (reversed(range(ncb)) if fwd else range(ncb)):
            rows = pl.ds(c * CH, CH)
            t = _gla_chunk_fwd_terms(q_ref, k_ref, lr_ref, gp_ref, bias_ref, rows, tri_b, head_masks)
            v = v_ref[rows, :]
            do = do_ref[rows, :]
            q_in, k_in, k_out = t["q_in"], t["k_in"], t["k_out"]
            q_b, k_in_b, k_out_b = q_in.astype(BF16), k_in.astype(BF16), k_out.astype(BF16)
            scores = _dot(t["q_st"], k_in_b, NT)
            a_st = jnp.where(tri_st, scores, 0.0).astype(BF16)
            s_prev = sp_ref[c]
            ds = ds_ref[...]
            ds_b = ds.astype(BF16)

            da_heads = [_dot(do[:, h * HV:(h + 1) * HV], v[:, h * HV:(h + 1) * HV], NT) for h in range(HEADS)]
            da_st = jnp.where(tri_st, jnp.concatenate(da_heads, axis=0), 0.0).astype(BF16)

            dv_heads = [_dot(a_st[h * CH:(h + 1) * CH, :], do[:, h * HV:(h + 1) * HV], TN) for h in range(HEADS)]
            dv = jnp.concatenate(dv_heads, axis=1) + _dot(k_out_b, ds_b, NT)

            x = _dot(da_st, k_in_b, NN)
            dq_in = _dot(do, s_prev, NN)
            for h in range(HEADS):
                dq_in = dq_in + jnp.where(head_masks[h], x[h * CH:(h + 1) * CH, :], 0.0)
            dk_in = _dot(da_st, t["q_st"], TN)
            dk_out = _dot(v, ds_b, NN)
            d_decay = jnp.sum(ds * s_prev.astype(F32), axis=0, keepdims=True)
            ds_ref[...] = ds * t["decay"] + jnp.where(blockmask, _dot(do, q_b, TN), 0.0)

            dq = dq_in * t["e"] * (HK ** -0.5)
            dk = dk_in * t["einv"] + dk_out * t["eout"]
            dko_ko = dk_out * k_out
            dcum = dq_in * q_in - dk_in * k_in - dko_ko
            dcl = jnp.sum(dko_ko, axis=0, keepdims=True) + d_decay * t["decay"]
            da = _dot_hilo(tri_t_b, dcum) + dcl
            dpre = da * t["sig_neg"] * (1.0 / 16.0)
            dpre_b = dpre.astype(BF16)
            dlr = _dot(dpre_b, gp_ref[...], NT)
            dg_ref[...] += _dot(lr_ref[rows, :], dpre_b, TN)
            db_ref[...] += jnp.sum(dpre, axis=0, keepdims=True)
            if with_prev:
                dq = dq + pq_ref[rows, :].astype(F32)
                dk = dk + pk_ref[rows, :].astype(F32)
                dv = dv + pv_ref[rows, :].astype(F32)
                dlr = dlr + plr_ref[rows, :].astype(F32)
            dq_ref[rows, :] = dq.astype(BF16)
            dk_ref[rows, :] = dk.astype(BF16)
            dv_ref[rows, :] = dv.astype(BF16)
            dlr_ref[rows, :] = dlr.astype(BF16)

    def col(width, c):
        return pl.BlockSpec((tb, width), lambda i: (blk(i), c))

    in_specs = [col(DK, COL_Q), col(DK, COL_K), col(DG, COL_V), col(LR_BLK, COL_LR)] + _gate_specs(l) + [
        pl.BlockSpec((ncb, DG, DK), lambda i: (blk(i), 0, 0)), col(DG, 0)]
    args = [p, p, p, p, gpad, bias, sprev, d_o]
    tiles = [col(DK, 0), col(DK, 0), col(DG, 0), col(LR_BLK, 0)]
    if with_prev:
        in_specs += tiles
        args += list(prev)
    return pl.pallas_call(
        body, name=name, grid=(nb,), in_specs=in_specs,
        out_specs=tiles + [pl.BlockSpec((LR_BLK, DK), lambda i: (0, 0)), pl.BlockSpec((1, DK), lambda i: (0, 0))],
        out_shape=[jax.ShapeDtypeStruct((L, DK), BF16), jax.ShapeDtypeStruct((L, DK), BF16),
                   jax.ShapeDtypeStruct((L, DG), BF16), jax.ShapeDtypeStruct((L, LR_BLK), BF16),
                   jax.ShapeDtypeStruct((LR_BLK, DK), F32), jax.ShapeDtypeStruct((1, DK), F32)],
        scratch_shapes=[pltpu.VMEM((DG, DK), F32)],
        compiler_params=_params(("arbitrary",)),
    )(*args)


def _mixer_out(p, conv_a, gh, l, o_tot, name):
    L = p.shape[0]
    tm = min(TILE_TOKENS, L)
    n = L // tm
    pmap, nmap = _halo_maps(tm, L)

    def body(gb_ref, gc_ref, gcp_ref, gcn_ref, gv_ref, gvp_ref, gvn_ref, go_ref, cw_ref, o_ref, gh_ref, y_ref):
        ps, ns = _edge_scales(pl.program_id(0), n)
        z = gc_ref[...].astype(F32) * gv_ref[...].astype(F32)
        zp = gcp_ref[...].astype(F32) * gvp_ref[...].astype(F32) * ps
        zn = gcn_ref[...].astype(F32) * gvn_ref[...].astype(F32) * ns
        z_dn, z_up = _shift(z, zp, zn)
        conv = cw_ref[0:1, :] * z_dn + cw_ref[1:2, :] * z + cw_ref[2:3, :] * z_up
        y_ref[:, 0:DC] = (gb_ref[...].astype(F32) * conv).astype(BF16)
        o = o_ref[...]
        go = go_ref[...].astype(F32)
        for h in range(HEADS):
            oh = o[:, h * HV:(h + 1) * HV]
            on = oh * _rstd(oh) * gh_ref[...]
            act, _ = _silu_parts(go[:, h * HV:(h + 1) * HV])
            y_ref[:, DC + h * HV:DC + (h + 1) * HV] = (act * on).astype(BF16)

    def main(c):
        return pl.BlockSpec((tm, DC), lambda i: (i, c))

    def halo(c, imap):
        return pl.BlockSpec((HB, DC), lambda i: (imap(i), c))

    return pl.pallas_call(
        body, name=name, grid=(n,),
        in_specs=[main(COL_GB), main(COL_GC), halo(COL_GC, pmap), halo(COL_GC, nmap),
                  main(COL_GV), halo(COL_GV, pmap), halo(COL_GV, nmap), main(COL_GO),
                  pl.BlockSpec((None, 3, DC), lambda i: (l, 0, 0)), pl.BlockSpec((tm, DG), lambda i: (i, 0)),
                  _gain_spec(l, HV)],
        out_specs=pl.BlockSpec((tm, D), lambda i: (i, 0)), out_shape=jax.ShapeDtypeStruct((L, D), BF16),
        compiler_params=_params(("parallel",)),
    )(p, p, p, p, p, p, p, p, conv_a, o_tot, gh)


def _mixer_out_bwd(p, conv_a, gh, l, o_tot, dy, name):
    L = p.shape[0]
    tm = min(TILE_TOKENS, L)
    n = L // tm
    pmap, nmap = _halo_maps(tm, L)

    def body(gb_ref, gbp_ref, gbn_ref, gc_ref, gcp_ref, gcn_ref, gv_ref, gvp_ref, gvn_ref, go_ref, cw_ref, o_ref,
             gh_ref, dy_ref, dyp_ref, dyn_ref, dgb_ref, dgc_ref, dgv_ref, dgo_ref, do_ref, dcw_ref, dgh_ref):
        i = pl.program_id(0)
        ps, ns = _edge_scales(i, n)
        gb = gb_ref[...].astype(F32)
        gc = gc_ref[...].astype(F32)
        gv = gv_ref[...].astype(F32)
        z = gc * gv
        zp = gcp_ref[...].astype(F32) * gvp_ref[...].astype(F32) * ps
        zn = gcn_ref[...].astype(F32) * gvn_ref[...].astype(F32) * ns
        z_dn, z_up = _shift(z, zp, zn)
        w0, w1, w2 = cw_ref[0:1, :], cw_ref[1:2, :], cw_ref[2:3, :]
        conv = w0 * z_dn + w1 * z + w2 * z_up
        dya = dy_ref[:, 0:DC].astype(F32)
        dgb_ref[...] = (dya * conv).astype(BF16)
        dc = dya * gb
        dcp = dyp_ref[...].astype(F32) * gbp_ref[...].astype(F32) * ps
        dcn = dyn_ref[...].astype(F32) * gbn_ref[...].astype(F32) * ns
        dc_dn, dc_up = _shift(dc, dcp, dcn)
        dz = w0 * dc_up + w1 * dc + w2 * dc_dn
        dgc_ref[...] = (dz * gv).astype(BF16)
        dgv_ref[...] = (dz * gc).astype(BF16)
        dcw = [jnp.sum(zs * dc, axis=0, keepdims=True) for zs in (z_dn, z, z_up)]

        o = o_ref[...]
        go = go_ref[...].astype(F32)
        dgh = jnp.zeros((1, HV), F32)
        for h in range(HEADS):
            sl = slice(h * HV, (h + 1) * HV)
            oh = o[:, sl]
            r = _rstd(oh)
            act, sg = _silu_parts(go[:, sl])
            dyb = dy_ref[:, DC + h * HV:DC + (h + 1) * HV].astype(F32)
            on = oh * r * gh_ref[...]
            dgo_ref[:, sl] = (dyb * on * (sg + act * (1.0 - sg))).astype(BF16)
            don = dyb * act
            zz = don * gh_ref[...]
            do_ref[:, sl] = (r * zz - oh * (r * r * r) * jnp.mean(oh * zz, axis=-1, keepdims=True)).astype(BF16)
            dgh = dgh + jnp.sum(don * oh * r, axis=0, keepdims=True)

        @pl.when(i == 0)
        def _():
            dcw_ref[...] = jnp.zeros_like(dcw_ref)
            dgh_ref[...] = jnp.zeros_like(dgh_ref)

        for kk in range(3):
            dcw_ref[kk:kk + 1, :] += dcw[kk]
        dgh_ref[...] += dgh

    def main(c):
        return pl.BlockSpec((tm, DC), lambda i: (i, c))

    def halo(c, imap):
        return pl.BlockSpec((HB, DC), lambda i: (imap(i), c))

    tile = pl.BlockSpec((tm, DC), lambda i: (i, 0))
    return pl.pallas_call(
        body, name=name, grid=(n,),
        in_specs=[main(COL_GB), halo(COL_GB, pmap), halo(COL_GB, nmap), main(COL_GC), halo(COL_GC, pmap),
                  halo(COL_GC, nmap), main(COL_GV), halo(COL_GV, pmap), halo(COL_GV, nmap), main(COL_GO),
                  pl.BlockSpec((None, 3, DC), lambda i: (l, 0, 0)), tile, _gain_spec(l, HV),
                  pl.BlockSpec((tm, D), lambda i: (i, 0)), halo(0, pmap), halo(0, nmap)],
        out_specs=[tile, tile, tile, tile, tile, pl.BlockSpec((3, DC), lambda i: (0, 0)),
                   pl.BlockSpec((1, HV), lambda i: (0, 0))],
        out_shape=[jax.ShapeDtypeStruct((L, DC), BF16)] * 5
        + [jax.ShapeDtypeStruct((3, DC), F32), jax.ShapeDtypeStruct((1, HV), F32)],
        compiler_params=_params(("arbitrary",)),
    )(p, p, p, p, p, p, p, p, p, p, conv_a, o_tot, gh, dy, dy, dy)


def _ffn_specs(tm, L, l, row_axis, sh_axis):
    pmap, nmap = _halo_maps(tm, L)

    def u(off, imap=None, rows=tm):
        if imap is None:
            return pl.BlockSpec((None, rows, SH_FF), lambda *g: (g[sh_axis] + off, g[row_axis], 0))
        return pl.BlockSpec((None, rows, SH_FF), lambda *g: (g[sh_axis] + off, imap(g[row_axis]), 0))

    def cw(off):
        return pl.BlockSpec((None, None, 3, SH_FF), lambda *g: (l, g[sh_axis] + off, 0, 0))

    u_specs = [u(0), u(0, pmap, HB), u(0, nmap, HB), u(FF_HALF), u(FF_HALF, pmap, HB), u(FF_HALF, nmap, HB)]
    return u_specs, [cw(0), cw(FF_HALF)]


def _conv3(x_ref, xp_ref, xn_ref, cw_ref, ps, ns):
    x = x_ref[...].astype(F32)
    x_dn, x_up = _shift(x, xp_ref[...].astype(F32) * ps, xn_ref[...].astype(F32) * ns)
    return cw_ref[0:1, :] * x_dn + cw_ref[1:2, :] * x + cw_ref[2:3, :] * x_up, (x_dn, x, x_up)


def _ffn_act(u8, cw, l, name):
    L = u8.shape[1]
    tm = min(TILE_FFN, L)
    n = L // tm
    u_specs, cw_specs = _ffn_specs(tm, L, l, 0, 1)

    def body(g_ref, gp_ref, gn_ref, v_ref, vp_ref, vn_ref, cwg_ref, cwv_ref, a_ref):
        ps, ns = _edge_scales(pl.program_id(0), n)
        gate, _ = _conv3(g_ref, gp_ref, gn_ref, cwg_ref, ps, ns)
        val, _ = _conv3(v_ref, vp_ref, vn_ref, cwv_ref, ps, ns)
        act, _ = _silu_parts(gate)
        a_ref[...] = (act * val).astype(BF16)

    return pl.pallas_call(
        body, name=name, grid=(n, FF_HALF), in_specs=u_specs + cw_specs,
        out_specs=pl.BlockSpec((None, tm, SH_FF), lambda i, d: (d, i, 0)),
        out_shape=jax.ShapeDtypeStruct((FF_HALF, L, SH_FF), BF16),
        compiler_params=_params(("parallel", "parallel")),
    )(u8, u8, u8, u8, u8, u8, cw, cw)


def _ffn_act_bwd(u8, cw, l, da, name):
    L = u8.shape[1]
    tm = min(TILE_FFN, L)
    n = L // tm
    u_specs, cw_specs = _ffn_specs(tm, L, l, 1, 0)

    def body(g_ref, gp_ref, gn_ref, v_ref, vp_ref, vn_ref, cwg_ref, cwv_ref, da_ref, du_ref, dcw_ref):
        i = pl.program_id(1)
        ps, ns = _edge_scales(i, n)
        gate, g_sh = _conv3(g_ref, gp_ref, gn_ref, cwg_ref, ps, ns)
        val, v_sh = _conv3(v_ref, vp_ref, vn_ref, cwv_ref, ps, ns)
        act, sg = _silu_parts(gate)
        da_f = da_ref[...].astype(F32)
        dgate = da_f * val * (sg + act * (1.0 - sg))
        dval = da_f * act
        du_ref[0] = dgate.astype(BF16)
        du_ref[1] = dval.astype(BF16)

        @pl.when(i == 0)
        def _():
            dcw_ref[...] = jnp.zeros_like(dcw_ref)

        for kk in range(3):
            dcw_ref[0, kk:kk + 1, :] += jnp.sum(g_sh[kk] * dgate, axis=0, keepdims=True)
            dcw_ref[1, kk:kk + 1, :] += jnp.sum(v_sh[kk] * dval, axis=0, keepdims=True)

    return pl.pallas_call(
        body, name=name, grid=(FF_HALF, n),
        in_specs=u_specs + cw_specs + [pl.BlockSpec((None, tm, SH_FF), lambda d, i: (d, i, 0))],
        out_specs=[pl.BlockSpec((2, None, tm, SH_FF), lambda d, i: (0, d, i, 0)),
                   pl.BlockSpec((2, None, 3, SH_FF), lambda d, i: (0, d, 0, 0))],
        out_shape=[jax.ShapeDtypeStruct((2, FF_HALF, L, SH_FF), BF16),
                   jax.ShapeDtypeStruct((2, FF_HALF, 3, SH_FF), F32)],
        compiler_params=_params(("parallel", "arbitrary")),
    )(u8, u8, u8, u8, u8, u8, cw, cw, da)


def _ffn_conv_t(du8, cw, l, name):
    L = du8.shape[1]
    tm = min(TILE_FFN, L)
    n = L // tm
    pmap, nmap = _halo_maps(tm, L)

    def body(x_ref, xp_ref, xn_ref, cw_ref, o_ref):
        ps, ns = _edge_scales(pl.program_id(1), n)
        x = x_ref[...].astype(F32)
        x_dn, x_up = _shift(x, xp_ref[...].astype(F32) * ps, xn_ref[...].astype(F32) * ns)
        o_ref[...] = (cw_ref[0:1, :] * x_up + cw_ref[1:2, :] * x + cw_ref[2:3, :] * x_dn).astype(BF16)

    return pl.pallas_call(
        body, name=name, grid=(N_DEV, n),
        in_specs=[pl.BlockSpec((None, tm, SH_FF), lambda d, i: (d, i, 0)),
                  pl.BlockSpec((None, HB, SH_FF), lambda d, i: (d, pmap(i), 0)),
                  pl.BlockSpec((None, HB, SH_FF), lambda d, i: (d, nmap(i), 0)),
                  pl.BlockSpec((None, None, 3, SH_FF), lambda d, i: (l, d, 0, 0))],
        out_specs=pl.BlockSpec((None, tm, SH_FF), lambda d, i: (d, i, 0)),
        out_shape=jax.ShapeDtypeStruct((N_DEV, L, SH_FF), BF16),
        compiler_params=_params(("parallel", "parallel")),
    )(du8, du8, du8, cw)


def _loss_grad(xl, target, name):
    L = xl.shape[0]
    tm = min(TILE_TOKENS, L)

    def body(x_ref, t_ref, dx_ref, sq_ref):
        i = pl.program_id(0)
        err = x_ref[...] - t_ref[...]
        dx_ref[...] = err * (1.0 / D)
        part = jnp.sum(err * err, axis=0, keepdims=True)

        @pl.when(i == 0)
        def _():
            sq_ref[...] = part

        @pl.when(i > 0)
        def _():
            sq_ref[...] += part

    tile = pl.BlockSpec((tm, D), lambda i: (i, 0))
    return pl.pallas_call(
        body, name=name, grid=(L // tm,), in_specs=[tile, tile],
        out_specs=[tile, pl.BlockSpec((1, D), lambda i: (0, 0))],
        out_shape=[jax.ShapeDtypeStruct((L, D), F32), jax.ShapeDtypeStruct((1, D), F32)],
        compiler_params=_params(("arbitrary",)),
    )(xl, target)


MESH = pl.DeviceIdType.MESH
HBM_SPEC = pl.BlockSpec(memory_space=pltpu.HBM)


def _position():
    return lax.axis_index("x"), lax.axis_index("y"), lax.axis_index("c")


def _other_chips(x, y):
    return [(1 - x, y), (x, 1 - y), (1 - x, 1 - y)]


def _all_gather(shards, name):
    n = len(shards)

    def body(*refs):
        x_refs, out_refs = refs[:n], refs[n:2 * n]
        send_sems, recv_sems, local_sems = refs[2 * n:]
        x, y, c = _position()
        me, sibling = (x, y, c), (x, y, 1 - c)
        chips = _other_chips(x, y)

        def slot(t, px, py, pc):
            return out_refs[t].at[:, 4 * px + 2 * py + pc]

        def copy(t, k, block, to, from_input=False):
            return pltpu.make_async_remote_copy(
                src_ref=x_refs[t] if from_input else slot(t, *block), dst_ref=slot(t, *block),
                send_sem=send_sems.at[k * n + t], recv_sem=recv_sems.at[k * n + t], device_id=to, device_id_type=MESH)

        mine = [pltpu.make_async_copy(x_refs[t], slot(t, *me), local_sems.at[t]) for t in range(n)]
        for cp in mine:
            cp.start()
        first = [copy(t, 0, me, sibling, True) for t in range(n)]
        first += [copy(t, 1 + j, me, (*chip, c), True) for j, chip in enumerate(chips) for t in range(n)]
        for cp in first:
            cp.start()
        passed = []
        for j, chip in enumerate(chips):
            for t in range(n):
                copy(t, 1 + j, (*chip, c), me).wait_recv()
                passed.append(copy(t, 4 + j, (*chip, c), sibling))
                passed[-1].start()
        for t in range(n):
            copy(t, 0, sibling, me).wait_recv()
        for j, chip in enumerate(chips):
            for t in range(n):
                copy(t, 4 + j, (*chip, 1 - c), me).wait_recv()
        for cp in first + passed:
            cp.wait_send()
        for cp in mine:
            cp.wait()

    return pl.pallas_call(
        body, name=name,
        out_shape=[jax.ShapeDtypeStruct((s.shape[0], N_DEV) + s.shape[1:], s.dtype) for s in shards],
        in_specs=[HBM_SPEC] * n, out_specs=[HBM_SPEC] * n,
        scratch_shapes=[pltpu.SemaphoreType.DMA((7 * n,)), pltpu.SemaphoreType.DMA((7 * n,)),
                        pltpu.SemaphoreType.DMA((n,))],
    )(*shards)


def _exchange_sibling(grads, name):
    n = len(grads)

    def body(*refs):
        g_refs, out_refs, send_sems, recv_sems = refs[:n], refs[n:2 * n], refs[2 * n], refs[2 * n + 1]
        x, y, c = _position()
        copies = [pltpu.make_async_remote_copy(
            src_ref=g_refs[t].at[:, 2 * k + (1 - c)], dst_ref=out_refs[t].at[:, k], send_sem=send_sems.at[k * n + t],
            recv_sem=recv_sems.at[k * n + t], device_id=(x, y, 1 - c), device_id_type=MESH)
            for k in range(N_CHIP) for t in range(n)]
        for cp in copies:
            cp.start()
        for cp in copies:
            cp.wait()

    return pl.pallas_call(
        body, name=name,
        out_shape=[jax.ShapeDtypeStruct((g.shape[0], N_CHIP) + g.shape[2:], g.dtype) for g in grads],
        in_specs=[HBM_SPEC] * n, out_specs=[HBM_SPEC] * n,
        scratch_shapes=[pltpu.SemaphoreType.DMA((N_CHIP * n,)), pltpu.SemaphoreType.DMA((N_CHIP * n,))],
    )(*grads)


def _row_tile(rows):
    return 256 if rows % 256 == 0 else rows


def _pair_sum(g, recv, c_idx, out_dtype, name):
    lay, _, rows, cols = g.shape
    tr = _row_tile(rows)

    def body(c_ref, g_ref, r_ref, o_ref):
        o_ref[...] = (g_ref[...] + r_ref[...]).astype(o_ref.dtype)

    def spec(blk_of):
        return pl.BlockSpec((None, None, tr, cols), lambda l, k, r, c_ref: (l, blk_of(k, c_ref), r, 0))

    return pl.pallas_call(
        body, name=name,
        grid_spec=pltpu.PrefetchScalarGridSpec(
            num_scalar_prefetch=1, grid=(lay, N_CHIP, rows // tr),
            in_specs=[spec(lambda k, c_ref: 2 * k + c_ref[0]), spec(lambda k, c_ref: k)],
            out_specs=spec(lambda k, c_ref: k)),
        out_shape=jax.ShapeDtypeStruct((lay, N_CHIP, rows, cols), out_dtype),
        compiler_params=_params(("parallel", "parallel", "parallel")),
    )(c_idx, g, recv)


def _exchange_chips(parts, name):
    n = len(parts)

    def body(*refs):
        p_refs, out_refs = refs[:n], refs[n:2 * n]
        send_sems, recv_sems, local_sems = refs[2 * n:]
        x, y, c = _position()
        my_chip = 2 * x + y
        mine = [pltpu.make_async_copy(p_refs[t].at[:, my_chip], out_refs[t].at[:, my_chip], local_sems.at[t])
                for t in range(n)]
        for cp in mine:
            cp.start()
        copies = [pltpu.make_async_remote_copy(
            src_ref=p_refs[t].at[:, 2 * cx + cy], dst_ref=out_refs[t].at[:, my_chip], send_sem=send_sems.at[j * n + t],
            recv_sem=recv_sems.at[j * n + t], device_id=(cx, cy, c), device_id_type=MESH)
            for j, (cx, cy) in enumerate(_other_chips(x, y)) for t in range(n)]
        for cp in copies:
            cp.start()
        for cp in copies:
            cp.wait()
        for cp in mine:
            cp.wait()

    return pl.pallas_call(
        body, name=name, out_shape=[jax.ShapeDtypeStruct(p.shape, p.dtype) for p in parts],
        in_specs=[HBM_SPEC] * n, out_specs=[HBM_SPEC] * n,
        scratch_shapes=[pltpu.SemaphoreType.DMA((3 * n,)), pltpu.SemaphoreType.DMA((3 * n,)),
                        pltpu.SemaphoreType.DMA((n,))],
    )(*parts)


def _sum_adamw(parts, w, m, v, name):
    lay, rows, cols = w.shape
    tr = _row_tile(rows)

    def body(p_ref, w_ref, m_ref, v_ref, g_ref, d_ref, nm_ref, nv_ref):
        g = ((p_ref[0].astype(F32) + p_ref[1].astype(F32)) + p_ref[2].astype(F32)) + p_ref[3].astype(F32)
        g_ref[...] = g
        nm = ADAM_B1 * m_ref[...] + (1.0 - ADAM_B1) * g
        nv = ADAM_B2 * v_ref[...] + (1.0 - ADAM_B2) * (g * g)
        nm_ref[...] = nm
        nv_ref[...] = nv
        m_hat = nm / (1.0 - ADAM_B1 ** ADAM_STEP)
        v_hat = nv / (1.0 - ADAM_B2 ** ADAM_STEP)
        d_ref[...] = -ADAM_LR * (m_hat / (jnp.sqrt(v_hat) + ADAM_EPS) + ADAM_WD * w_ref[...])

    tile = pl.BlockSpec((None, tr, cols), lambda l, r: (l, r, 0))
    return pl.pallas_call(
        body, name=name, grid=(lay, rows // tr),
        in_specs=[pl.BlockSpec((None, N_CHIP, tr, cols), lambda l, r: (l, 0, r, 0)), tile, tile, tile],
        out_specs=[tile] * 4, out_shape=[jax.ShapeDtypeStruct((lay, rows, cols), F32)] * 4,
        compiler_params=_params(("parallel", "parallel")),
    )(parts, w, m, v)


def _pad_rows(flat, rows):
    return jnp.pad(flat, (0, rows * LANES - flat.shape[0])).reshape(rows, LANES)


def _pack_small(tree):
    sh = jnp.concatenate([tree[n].reshape(-1) for n, _, _ in SMALL_SHARDED])
    rep = jnp.concatenate([tree[n].reshape(-1) for n, _ in REPLICATED])
    return jnp.concatenate([_pad_rows(sh, ROWS_SSH), _pad_rows(rep, ROWS_REP)], axis=0)


def _unpack_small(buf):
    out = {}
    for flat, items in ((buf[:ROWS_SSH].reshape(-1), [(n, s) for n, s, _ in SMALL_SHARDED]),
                        (buf[ROWS_SSH:].reshape(-1), REPLICATED)):
        off = 0
        for n, s in items:
            out[n] = flat[off:off + math.prod(s)].reshape(s)
            off += math.prod(s)
    return out


def _full_from_blocks(blocks, s, ax):
    return jnp.concatenate([blocks[d] for d in range(N_DEV)], axis=ax)


def _blocks_from_full(full, s, ax):
    return jnp.stack([lax.slice_in_dim(full, d * s[ax], (d + 1) * s[ax], axis=ax) for d in range(N_DEV)])


def _pack_small_grads(sharded_blocks, replicated):
    sh = jnp.concatenate([sharded_blocks[n].reshape(N_DEV, -1) for n, _, _ in SMALL_SHARDED], axis=1)
    sh = jnp.pad(sh, ((0, 0), (0, ROWS_SSH * LANES - sh.shape[1]))).reshape(N_DEV, ROWS_SSH, LANES)
    rep = _pad_rows(jnp.concatenate([replicated[n].reshape(-1) for n, _ in REPLICATED]), ROWS_REP)
    return jnp.concatenate([sh, jnp.broadcast_to(rep[None], (N_DEV, ROWS_REP, LANES))], axis=1)


def _layer_fwd(x, h1, wts, l, l_next):
    L = x.shape[0]
    tm = min(TILE_MM, L)
    nt = L // tm
    p = _mm(h1, wts["w_in"], dims=NN, grid=(nt, D_INP // 640, 1),
            a_spec=((tm, D), lambda i, j, k: (i, 0)), b_spec=((None, D, 640), lambda i, j, k: (l, 0, j)),
            o_spec=((tm, 640), lambda i, j, k: (i, j)), out_shape=jax.ShapeDtypeStruct((L, D_INP), BF16),
            tile=(tm, 640), name="proj_in")
    o_f, sp_f = _gla_fwd(p, wts["gpad_f"], wts["bias_f"], l, None, True, "gla_fwd_f")
    o_tot, sp_b = _gla_fwd(p, wts["gpad_b"], wts["bias_b"], l, o_f, False, "gla_fwd_b")
    y_cat = _mixer_out(p, wts["conv_a"], wts["gh"], l, o_tot, "mixer_out")
    y = _mm(y_cat, wts["w_out"], dims=NN, grid=(nt, 1, 1),
            a_spec=((tm, D), lambda i, j, k: (i, 0)), b_spec=((None, D, D), lambda i, j, k: (l, 0, 0)),
            o_spec=((tm, D), lambda i, j, k: (i, 0)), out_shape=jax.ShapeDtypeStruct((L, D), BF16),
            tile=(tm, D), name="proj_out")
    x1, h2 = _post_pre(x, y, wts["g2"], l, wts["g3"], l, "post_pre_mix")
    u8 = _mm(h2, wts["w_up"], dims=NN, grid=(nt, N_DEV, 1),
             a_spec=((tm, D), lambda i, j, k: (i, 0)), b_spec=((None, None, D, SH_FF), lambda i, j, k: (l, j, 0, 0)),
             o_spec=((None, tm, SH_FF), lambda i, j, k: (j, i, 0)),
             out_shape=jax.ShapeDtypeStruct((N_DEV, L, SH_FF), BF16), tile=(tm, SH_FF), name="ffn_up")
    a = _ffn_act(u8, wts["cw"], l, "ffn_act")
    tm1 = min(TILE_MM_KIN, L)
    y2 = _mm(a, wts["w_down"], dims=NN, grid=(L // tm1, 1, 1), kin=FF_HALF,
             a_spec=((FF_HALF, tm1, SH_FF), lambda i, j, k: (0, i, 0)),
             b_spec=((None, FF_HALF, SH_FF, D), lambda i, j, k: (l, 0, 0, 0)),
             o_spec=((tm1, D), lambda i, j, k: (i, 0)), out_shape=jax.ShapeDtypeStruct((L, D), BF16),
             tile=(tm1, D), name="ffn_down")
    x2, h1_next = _post_pre(x1, y2, wts["g4"], l, wts["g1"], l_next, "post_pre_ffn")
    saved = dict(x=x, h1=h1, p=p, o_tot=o_tot, sp_f=sp_f, sp_b=sp_b, y_cat=y_cat, y=y, x1=x1, h2=h2, u8=u8, a=a, y2=y2)
    return x2, h1_next, saved


def _layer_bwd(dx2, wts, s, l):
    L = dx2.shape[0]
    tm = min(TILE_MM, L)
    nt = L // tm
    tm1 = min(TILE_MM_KIN, L)
    tk = min(TILE_MM_TOKENS, L)
    nkt = L // tk
    dy2, dg4 = _norm_bwd(s["y2"], wts["g4"], l, dx2, None, "norm_bwd_ffn_post")
    da = _mm(dy2, wts["w_down"], dims=NT, grid=(nt, FF_HALF, 1),
             a_spec=((tm, D), lambda i, j, k: (i, 0)), b_spec=((None, None, SH_FF, D), lambda i, j, k: (l, j, 0, 0)),
             o_spec=((None, tm, SH_FF), lambda i, j, k: (j, i, 0)),
             out_shape=jax.ShapeDtypeStruct((FF_HALF, L, SH_FF), BF16), tile=(tm, SH_FF), name="ffn_down_dx")
    dw_down = _mm(s["a"], dy2, dims=TN, grid=(FF_HALF, 1, nkt),
                  a_spec=((None, tk, SH_FF), lambda i, j, k: (i, k, 0)), b_spec=((tk, D), lambda i, j, k: (k, 0)),
                  o_spec=((SH_FF, D), lambda i, j, k: (i, 0)), out_shape=jax.ShapeDtypeStruct((DFF, D), F32),
                  tile=(SH_FF, D), name="ffn_down_dw")
    du, dcw = _ffn_act_bwd(s["u8"], wts["cw"], l, da, "ffn_act_bwd")
    d_u8 = _ffn_conv_t(du.reshape(N_DEV, L, SH_FF), wts["cw"], l, "ffn_conv_t")
    dh2 = _mm(d_u8, wts["w_up"], dims=NT, grid=(L // tm1, 1, N_DEV // FF_HALF), kin=FF_HALF,
              a_spec=((FF_HALF, tm1, SH_FF), lambda i, j, k: (k, i, 0)),
              b_spec=((None, FF_HALF, D, SH_FF), lambda i, j, k: (l, k, 0, 0)),
              o_spec=((tm1, D), lambda i, j, k: (i, 0)), out_shape=jax.ShapeDtypeStruct((L, D), BF16),
              tile=(tm1, D), name="ffn_up_dx")
    dw_up = _mm(s["h2"], d_u8, dims=TN, grid=(1, N_DEV, nkt),
                a_spec=((tk, D), lambda i, j, k: (k, 0)), b_spec=((None, tk, SH_FF), lambda i, j, k: (j, k, 0)),
                o_spec=((None, D, SH_FF), lambda i, j, k: (j, 0, 0)),
                out_shape=jax.ShapeDtypeStruct((N_DEV, D, SH_FF), F32), tile=(D, SH_FF), name="ffn_up_dw")
    dx1, dg3 = _norm_bwd(s["x1"], wts["g3"], l, dh2, dx2, "norm_bwd_ffn_pre")
    dy, dg2 = _norm_bwd(s["y"], wts["g2"], l, dx1, None, "norm_bwd_mix_post")
    dy_cat = _mm(dy, wts["w_out"], dims=NT, grid=(nt, 1, 1),
                 a_spec=((tm, D), lambda i, j, k: (i, 0)), b_spec=((None, D, D), lambda i, j, k: (l, 0, 0)),
                 o_spec=((tm, D), lambda i, j, k: (i, 0)), out_shape=jax.ShapeDtypeStruct((L, D), BF16),
                 tile=(tm, D), name="proj_out_dx")
    dw_out = _mm(s["y_cat"], dy, dims=TN, grid=(1, 1, nkt),
                 a_spec=((tk, D), lambda i, j, k: (k, 0)), b_spec=((tk, D), lambda i, j, k: (k, 0)),
                 o_spec=((D, D), lambda i, j, k: (0, 0)), out_shape=jax.ShapeDtypeStruct((D, D), F32),
                 tile=(D, D), name="proj_out_dw")
    dgb, dgc, dgv, dgo, d_o, dconv_a, dgh = _mixer_out_bwd(s["p"], wts["conv_a"], wts["gh"], l, s["o_tot"], dy_cat,
                                                          "mixer_out_bwd")
    part_f = _gla_bwd(s["p"], wts["gpad_f"], wts["bias_f"], l, s["sp_f"], d_o, None, True, "gla_bwd_f")
    dq, dk, dv, dlr, dgp_b, dbias_b = _gla_bwd(s["p"], wts["gpad_b"], wts["bias_b"], l, s["sp_b"], d_o, part_f[:4],
                                               False, "gla_bwd_b")
    dp = jnp.concatenate([dgb, dgc, dgv, dq, dk, dv, dgo, dlr], axis=1)
    dh1 = _mm(dp, wts["w_in"], dims=NT, grid=(L // tm1, 1, 1),
              a_spec=((tm1, D_INP), lambda i, j, k: (i, 0)), b_spec=((None, D, D_INP), lambda i, j, k: (l, 0, 0)),
              o_spec=((tm1, D), lambda i, j, k: (i, 0)), out_shape=jax.ShapeDtypeStruct((L, D), BF16),
              tile=(tm1, D), name="proj_in_dx")
    dw_in = _mm(s["h1"], dp, dims=TN, grid=(1, D_INP // 640, nkt),
                a_spec=((tk, D), lambda i, j, k: (k, 0)), b_spec=((tk, 640), lambda i, j, k: (k, j)),
                o_spec=((D, 640), lambda i, j, k: (0, j)), out_shape=jax.ShapeDtypeStruct((D, D_INP), F32),
                tile=(D, 640), name="proj_in_dw")
    dx0, dg1 = _norm_bwd(s["x"], wts["g1"], l, dh1, dx1, "norm_bwd_mix_pre")
    grads = dict(
        norm_mix_pre=dg1[0], norm_mix_post=dg2[0], norm_ffn_pre=dg3[0], norm_ffn_post=dg4[0],
        gate_bias_fwd=part_f[5][0], gate_bias_bwd=dbias_b[0], gla_head_norm=dgh[0],
        w_in=_blocks_from_full(dw_in, (D, SH_IN), 1), w_out=dw_out.reshape(N_DEV, D // N_DEV, D), w_up=dw_up,
        w_down=dw_down.reshape(N_DEV, DFF // N_DEV, D),
        conv_a=_blocks_from_full(dconv_a, (3, DC // N_DEV), 1),
        gate_up_fwd=_blocks_from_full(part_f[4][:RANK], (RANK, DK // N_DEV), 1),
        gate_up_bwd=_blocks_from_full(dgp_b[RANK:2 * RANK], (RANK, DK // N_DEV), 1),
        conv_ffn=dcw.reshape(N_DEV, 3, SH_FF))
    return dx0, grads


def _kernel_weights(g_in, g_out, g_up, g_down, g_small, rep):
    small = {n: jnp.moveaxis(t, 0, 1) for n, t in jax.vmap(_unpack_small)(
        jnp.concatenate([g_small[0], jnp.zeros((N_DEV, ROWS_REP, LANES), F32)], axis=1)).items()
        if n in [s[0] for s in SMALL_SHARDED]}
    w_in = jnp.concatenate([g_in[:, d] for d in range(N_DEV)] + [jnp.zeros((DEPTH, D, D_INP - D_IN), BF16)], axis=2)
    conv_a = jnp.concatenate([small["conv_a"][:, d] for d in range(N_DEV)], axis=2)
    gate_f = jnp.concatenate([small["gate_up_fwd"][:, d] for d in range(N_DEV)], axis=2).astype(BF16)
    gate_b = jnp.concatenate([small["gate_up_bwd"][:, d] for d in range(N_DEV)], axis=2).astype(BF16)
    zeros = jnp.zeros((DEPTH, LR_BLK, DK), BF16)
    return dict(
        w_in=w_in, w_out=g_out.reshape(DEPTH, D, D), w_up=g_up, w_down=g_down.reshape(DEPTH, FF_HALF, SH_FF, D),
        conv_a=conv_a, cw=small["conv_ffn"],
        gpad_f=zeros.at[:, :RANK].set(gate_f), gpad_b=zeros.at[:, RANK:2 * RANK].set(gate_b),
        bias_f=rep["gate_bias_fwd"][:, None, :], bias_b=rep["gate_bias_bwd"][:, None, :],
        gh=rep["gla_head_norm"][:, None, :],
        g1=rep["norm_mix_pre"][:, None, :], g2=rep["norm_mix_post"][:, None, :],
        g3=rep["norm_ffn_pre"][:, None, :], g4=rep["norm_ffn_post"][:, None, :])


def kernel(x, norm_mix_pre, norm_mix_post, norm_ffn_pre, norm_ffn_post, w_in, conv_a, gate_up_fwd, gate_bias_fwd, gate_up_bwd, gate_bias_bwd, gla_head_norm, w_out, w_up, conv_ffn, w_down, loss_target, m_norm_mix_pre, m_norm_mix_post, m_norm_ffn_pre, m_norm_ffn_post, m_w_in, m_conv_a, m_gate_up_fwd, m_gate_bias_fwd, m_gate_up_bwd, m_gate_bias_bwd, m_gla_head_norm, m_w_out, m_w_up, m_conv_ffn, m_w_down, v_norm_mix_pre, v_norm_mix_post, v_norm_ffn_pre, v_norm_ffn_post, v_w_in, v_conv_a, v_gate_up_fwd, v_gate_bias_fwd, v_gate_up_bwd, v_gate_bias_bwd, v_gla_head_norm, v_w_out, v_w_up, v_conv_ffn, v_w_down):
    w = dict(norm_mix_pre=norm_mix_pre, norm_mix_post=norm_mix_post, norm_ffn_pre=norm_ffn_pre,
             norm_ffn_post=norm_ffn_post, w_in=w_in, conv_a=conv_a, gate_up_fwd=gate_up_fwd,
             gate_bias_fwd=gate_bias_fwd, gate_up_bwd=gate_up_bwd, gate_bias_bwd=gate_bias_bwd,
             gla_head_norm=gla_head_norm, w_out=w_out, w_up=w_up, conv_ffn=conv_ffn, w_down=w_down)
    m = dict(norm_mix_pre=m_norm_mix_pre, norm_mix_post=m_norm_mix_post, norm_ffn_pre=m_norm_ffn_pre,
             norm_ffn_post=m_norm_ffn_post, w_in=m_w_in, conv_a=m_conv_a, gate_up_fwd=m_gate_up_fwd,
             gate_bias_fwd=m_gate_bias_fwd, gate_up_bwd=m_gate_up_bwd, gate_bias_bwd=m_gate_bias_bwd,
             gla_head_norm=m_gla_head_norm, w_out=m_w_out, w_up=m_w_up, conv_ffn=m_conv_ffn, w_down=m_w_down)
    v = dict(norm_mix_pre=v_norm_mix_pre, norm_mix_post=v_norm_mix_post, norm_ffn_pre=v_norm_ffn_pre,
             norm_ffn_post=v_norm_ffn_post, w_in=v_w_in, conv_a=v_conv_a, gate_up_fwd=v_gate_up_fwd,
             gate_bias_fwd=v_gate_bias_fwd, gate_up_bwd=v_gate_up_bwd, gate_bias_bwd=v_gate_bias_bwd,
             gla_head_norm=v_gla_head_norm, w_out=v_w_out, w_up=v_w_up, conv_ffn=v_conv_ffn, w_down=v_w_down)
    axes = ("x", "y", "c")
    L = x.shape[1]
    x0 = x.reshape(L, D)
    target = loss_target.reshape(L, D)

    w_small = _pack_small(w)
    gathered = _all_gather([w[n].astype(BF16) for n in BIG] + [w_small[None, :ROWS_SSH]], "gather_weights")
    wts = _kernel_weights(*gathered, w)

    h1 = _norm_cast(x0, wts["g1"], 0, "norm_first")
    xl, saved = x0, []
    for l in range(DEPTH):
        xl, h1, s = _layer_fwd(xl, h1, wts, l, min(l + 1, DEPTH - 1))
        saved.append(s)
    dx, sq = _loss_grad(xl, target, "loss_grad")
    loss = lax.psum(0.5 * jnp.sum(sq) / D, axes)
    layer_grads = [None] * DEPTH
    for l in reversed(range(DEPTH)):
        dx, layer_grads[l] = _layer_bwd(dx, wts, saved[l], l)

    stacked = {n: jnp.stack([g[n] for g in layer_grads]) for n in layer_grads[0]}
    g_small = _pack_small_grads({n: jnp.moveaxis(stacked[n], 0, 1) for n, _, _ in SMALL_SHARDED}, stacked)
    grads = [stacked[n] for n in BIG] + [g_small[None]]
    c_idx = lax.axis_index("c").astype(jnp.int32).reshape(1)
    from_sibling = _exchange_sibling(grads, "grads_to_sibling")
    pairs = [_pair_sum(g, r, c_idx, BF16 if i < len(BIG) else F32, "grads_pair_sum")
             for i, (g, r) in enumerate(zip(grads, from_sibling))]
    parts = _exchange_chips(pairs, "grads_to_chips")

    results = {}
    for n, part in zip(BIG, parts[:-1]):
        results[n] = _sum_adamw(part, w[n], m[n], v[n], "sum_adamw")
    small = _sum_adamw(parts[-1], w_small[None], _pack_small(m)[None], _pack_small(v)[None], "sum_adamw_small")
    small = [_unpack_small(buf[0]) for buf in small]
    outs = [loss, dx.reshape(x.shape)]
    for i in range(4):
        outs += [results[n][i] if n in BIG else small[i][n] for n in WEIGHT_ORDER]
    return tuple(outs)
```

```python
import math
from typing import Callable, NamedTuple

import jax
import jax.numpy as jnp
from jax import lax
from jax.experimental import pallas as pl
from jax.experimental.pallas import tpu as pltpu

F32 = jnp.float32
BF16 = jnp.bfloat16

DEPTH = 4
D = 1024
DC = 512
DG = 512
HEADS = 4
HV = 128
HK = 64
DK = 256
RANK = 16
CH = 64
DFF = 2816
D_IN = 3104
D_INP = 3200
LR_BLK = 128
EPS = 1e-6
HB = 16
N_DEV = 8
N_CHIP = 4
LANES = 1024
SH_IN = D_IN // N_DEV
SH_FF = 2 * DFF // N_DEV
FF_HALF = N_DEV // 2

ADAM_LR, ADAM_B1, ADAM_B2, ADAM_EPS, ADAM_WD, ADAM_STEP = 0.001, 0.9, 0.999, 1e-08, 0.01, 10

VMEM_LIMIT = 48 * 1024 * 1024
TILE_TOKENS = 512
TILE_GLA = 512
TILE_FFN = 1024
TILE_MM = 2048
TILE_MM_KIN = 1024
TILE_MM_TOKENS = 2048

COL_GB, COL_GC, COL_GV = 0, 1, 2
COL_Q, COL_K = 6, 7
COL_V, COL_GO = 4, 5
COL_LR = 24

BIG = ("w_in", "w_out", "w_up", "w_down")
SMALL_SHARDED = (
    ("conv_a", (DEPTH, 3, DC // N_DEV), 2),
    ("gate_up_fwd", (DEPTH, RANK, DK // N_DEV), 2),
    ("gate_up_bwd", (DEPTH, RANK, DK // N_DEV), 2),
    ("conv_ffn", (DEPTH, 3, SH_FF), 2),
)
REPLICATED = (
    ("norm_mix_pre", (DEPTH, D)), ("norm_mix_post", (DEPTH, D)), ("norm_ffn_pre", (DEPTH, D)),
    ("norm_ffn_post", (DEPTH, D)), ("gate_bias_fwd", (DEPTH, DK)), ("gate_bias_bwd", (DEPTH, DK)),
    ("gla_head_norm", (DEPTH, HV)),
)
WEIGHT_ORDER = ("norm_mix_pre", "norm_mix_post", "norm_ffn_pre", "norm_ffn_post", "w_in", "conv_a", "gate_up_fwd",
                "gate_bias_fwd", "gate_up_bwd", "gate_bias_bwd", "gla_head_norm", "w_out", "w_up", "conv_ffn", "w_down")


def _rows_for(n_elems):
    return (-(-n_elems // LANES) + 7) // 8 * 8


ROWS_SSH = _rows_for(sum(math.prod(s) for _, s, _ in SMALL_SHARDED))
ROWS_REP = _rows_for(sum(math.prod(s) for _, s in REPLICATED))
ROWS_SMALL = ROWS_SSH + ROWS_REP


def _params(sem):
    return pltpu.CompilerParams(dimension_semantics=sem, vmem_limit_bytes=VMEM_LIMIT)


def _silu_parts(x):
    s = 1.0 / (1.0 + jnp.exp(-x))
    return x * s, s


def _rstd(xf):
    return lax.rsqrt(jnp.mean(xf * xf, axis=-1, keepdims=True) + EPS)


NN, NT, TN = ((1,), (0,)), ((1,), (1,)), ((0,), (0,))


def _dot(a, b, dims):
    return lax.dot_general(a, b, (dims, ((), ())), preferred_element_type=F32)


def _mm(a, b, *, dims, grid, a_spec, b_spec, o_spec, out_shape, tile, name, kin=0, comm=None):
    nk = grid[2]
    n_ci = len(comm.ins) if comm else 0
    n_co = len(comm.out_shapes) if comm else 0

    def body(*refs):
        a_ref, b_ref = refs[:2]
        ci = refs[2:2 + n_ci]
        o_ref = refs[2 + n_ci]
        co = refs[3 + n_ci:3 + n_ci + n_co]
        rest = refs[3 + n_ci + n_co:]
        if comm:
            sems, rest = rest[:3], rest[3:]
            step = (pl.program_id(0) * grid[1] + pl.program_id(1)) * grid[2] + pl.program_id(2)

            @pl.when(step == 0)
            def _():
                comm.start(ci, co, *sems)

        if kin:
            prod = _dot(a_ref[0], b_ref[0], dims)
            for d in range(1, kin):
                prod = prod + _dot(a_ref[d], b_ref[d], dims)
        else:
            prod = _dot(a_ref[...], b_ref[...], dims)
        if nk == 1:
            o_ref[...] = prod.astype(o_ref.dtype)
        else:
            acc_ref = rest[0]
            k = pl.program_id(2)

            @pl.when(k == 0)
            def _():
                acc_ref[...] = prod

            @pl.when(k > 0)
            def _():
                acc_ref[...] += prod

            @pl.when(k == nk - 1)
            def _():
                o_ref[...] = acc_ref[...].astype(o_ref.dtype)

        if comm:
            @pl.when(step == grid[0] * grid[1] * grid[2] - 1)
            def _():
                comm.wait(ci, co, *sems)

    acc = [pltpu.VMEM(tile, F32)] if nk > 1 else []
    if not comm:
        return pl.pallas_call(
            body, name=name, grid=grid, in_specs=[pl.BlockSpec(*a_spec), pl.BlockSpec(*b_spec)],
            out_specs=pl.BlockSpec(*o_spec), out_shape=out_shape, scratch_shapes=acc,
            compiler_params=_params(("parallel", "parallel", "arbitrary")),
        )(a, b)
    return pl.pallas_call(
        body, name=name, grid=grid,
        in_specs=[pl.BlockSpec(*a_spec), pl.BlockSpec(*b_spec)] + [HBM_SPEC] * n_ci,
        out_specs=[pl.BlockSpec(*o_spec)] + [HBM_SPEC] * n_co, out_shape=[out_shape] + list(comm.out_shapes),
        scratch_shapes=[pltpu.SemaphoreType.DMA((s,)) for s in comm.sems] + acc,
        input_output_aliases={2 + i: 1 + o for i, o in comm.aliases.items()},
        compiler_params=_params(("arbitrary", "arbitrary", "arbitrary")),
    )(a, b, *comm.ins)


def _halo_maps(tm, n_rows):
    r, last = tm // HB, n_rows // HB - 1
    return (lambda i: jnp.maximum(i * r - 1, 0)), (lambda i: jnp.minimum((i + 1) * r, last))


def _shift(x, prev_blk, next_blk):
    tm = x.shape[0]
    xs = jnp.concatenate([prev_blk, x, next_blk], axis=0)
    n = xs.shape[0]
    down = pltpu.roll(xs, 1, 0)[HB:HB + tm]
    up = pltpu.roll(xs, n - 1, 0)[HB:HB + tm]
    return down, up


def _edge_scales(i, n):
    return jnp.where(i > 0, 1.0, 0.0).astype(F32), jnp.where(i < n - 1, 1.0, 0.0).astype(F32)


def _gain_spec(l, width=D):
    return pl.BlockSpec((None, 1, width), lambda *_: (l, 0, 0))


def _norm_cast(x, g, l, name):
    L = x.shape[0]
    tm = min(TILE_TOKENS, L)

    def body(x_ref, g_ref, o_ref):
        xf = x_ref[...]
        o_ref[...] = (xf * _rstd(xf) * g_ref[...]).astype(BF16)

    return pl.pallas_call(
        body, name=name, grid=(L // tm,), in_specs=[pl.BlockSpec((tm, D), lambda i: (i, 0)), _gain_spec(l)],
        out_specs=pl.BlockSpec((tm, D), lambda i: (i, 0)), out_shape=jax.ShapeDtypeStruct((L, D), BF16),
        compiler_params=_params(("parallel",)),
    )(x, g)


def _post_pre(x, y, g_post, l_post, g_pre, l_pre, name):
    L = x.shape[0]
    tm = min(TILE_TOKENS, L)

    def body(x_ref, y_ref, gp_ref, gn_ref, x1_ref, h_ref):
        yf = y_ref[...].astype(F32)
        x1 = x_ref[...] + yf * _rstd(yf) * gp_ref[...]
        x1_ref[...] = x1
        h_ref[...] = (x1 * _rstd(x1) * gn_ref[...]).astype(BF16)

    tile = pl.BlockSpec((tm, D), lambda i: (i, 0))
    return pl.pallas_call(
        body, name=name, grid=(L // tm,), in_specs=[tile, tile, _gain_spec(l_post), _gain_spec(l_pre)],
        out_specs=[tile, tile],
        out_shape=[jax.ShapeDtypeStruct((L, D), F32), jax.ShapeDtypeStruct((L, D), BF16)],
        compiler_params=_params(("parallel",)),
    )(x, y, g_post, g_pre)


def _norm_bwd(yin, g, l, dout, dres, name):
    L = yin.shape[0]
    tm = min(TILE_TOKENS, L)
    with_res = dres is not None

    def body(*refs):
        if with_res:
            y_ref, g_ref, do_ref, dr_ref, din_ref, dg_ref = refs
        else:
            y_ref, g_ref, do_ref, din_ref, dg_ref = refs
        i = pl.program_id(0)
        y = y_ref[...].astype(F32)
        r = _rstd(y)
        do = do_ref[...].astype(F32)
        z = do * g_ref[...]
        din = r * z - y * (r * r * r) * jnp.mean(y * z, axis=-1, keepdims=True)
        if with_res:
            din = din + dr_ref[...]
        din_ref[...] = din.astype(din_ref.dtype)
        part = jnp.sum(do * y * r, axis=0, keepdims=True)

        @pl.when(i == 0)
        def _():
            dg_ref[...] = part

        @pl.when(i > 0)
        def _():
            dg_ref[...] += part

    tile = pl.BlockSpec((tm, D), lambda i: (i, 0))
    args = (yin, g, dout) + ((dres,) if with_res else ())
    return pl.pallas_call(
        body, name=name, grid=(L // tm,), in_specs=[tile, _gain_spec(l), tile] + ([tile] if with_res else []),
        out_specs=[tile, pl.BlockSpec((1, D), lambda i: (0, 0))],
        out_shape=[jax.ShapeDtypeStruct((L, D), F32 if with_res else BF16), jax.ShapeDtypeStruct((1, D), F32)],
        compiler_params=_params(("arbitrary",)),
    )(*args)


def _gla_consts(fwd):
    row = lax.broadcasted_iota(jnp.int32, (CH, CH), 0)
    col = lax.broadcasted_iota(jnp.int32, (CH, CH), 1)
    tri = (col <= row) if fwd else (col >= row)
    tri_t = (col >= row) if fwd else (col <= row)
    row_st = lax.broadcasted_iota(jnp.int32, (HEADS * CH, CH), 0) & (CH - 1)
    col_st = lax.broadcasted_iota(jnp.int32, (HEADS * CH, CH), 1)
    tri_st = (col_st <= row_st) if fwd else (col_st >= row_st)
    lane_head = lax.broadcasted_iota(jnp.int32, (1, DK), 1) // HK
    head_masks = [lane_head == h for h in range(HEADS)]
    srow = lax.broadcasted_iota(jnp.int32, (DG, DK), 0) // HV
    scol = lax.broadcasted_iota(jnp.int32, (DG, DK), 1) // HK
    return tri.astype(BF16), tri_t.astype(BF16), tri_st, head_masks, srow == scol


def _dot_hilo(tri_b, x):
    hi = x.astype(BF16)
    lo = (x - hi.astype(F32)).astype(BF16)
    return _dot(tri_b, hi, NN) + _dot(tri_b, lo, NN)


def _gla_chunk_fwd_terms(q_ref, k_ref, lr_ref, gp_ref, bias_ref, rows, tri_b, head_masks):
    pre = _dot(lr_ref[rows, :], gp_ref[...], NN) + bias_ref[...]
    sig_neg = 1.0 / (1.0 + jnp.exp(pre))
    a = (jnp.minimum(pre, 0.0) - jnp.log(1.0 + jnp.exp(-jnp.abs(pre)))) * (1.0 / 16.0)
    cum = _dot_hilo(tri_b, a)
    cl = jnp.sum(a, axis=0, keepdims=True)
    e = jnp.exp(cum)
    einv = jnp.exp(-cum)
    eout = jnp.exp(cl - cum)
    decay = jnp.exp(cl)
    q = q_ref[rows, :].astype(F32)
    k = k_ref[rows, :].astype(F32)
    q_in = q * e * (HK ** -0.5)
    k_in = k * einv
    k_out = k * eout
    q_st = jnp.concatenate([jnp.where(mh, q_in, 0.0) for mh in head_masks], axis=0).astype(BF16)
    return dict(sig_neg=sig_neg, e=e, einv=einv, eout=eout, decay=decay, q_in=q_in, k_in=k_in, k_out=k_out, q_st=q_st)


def _gate_specs(l):
    return [pl.BlockSpec((None, LR_BLK, DK), lambda i: (l, 0, 0)), pl.BlockSpec((None, 1, DK), lambda i: (l, 0, 0))]


def _gla_fwd(p, gpad, bias, l, o_prev, fwd, name):
    L = p.shape[0]
    tb = min(TILE_GLA, L)
    nb, ncb, nch = L // tb, tb // CH, L // CH
    blk = (lambda i: i) if fwd else (lambda i: nb - 1 - i)
    with_prev = o_prev is not None

    def body(*refs):
        if with_prev:
            q_ref, k_ref, v_ref, lr_ref, gp_ref, bias_ref, op_ref, o_ref, sp_ref, s_ref = refs
        else:
            q_ref, k_ref, v_ref, lr_ref, gp_ref, bias_ref, o_ref, sp_ref, s_ref = refs
        i = pl.program_id(0)

        @pl.when(i == 0)
        def _():
            s_ref[...] = jnp.zeros_like(s_ref)

        tri_b, _, tri_st, head_masks, blockmask = _gla_consts(fwd)
        for c in (range(ncb) if fwd else reversed(range(ncb))):
            rows = pl.ds(c * CH, CH)
            t = _gla_chunk_fwd_terms(q_ref, k_ref, lr_ref, gp_ref, bias_ref, rows, tri_b, head_masks)
            v = v_ref[rows, :]
            scores = _dot(t["q_st"], t["k_in"].astype(BF16), NT)
            a_st = jnp.where(tri_st, scores, 0.0).astype(BF16)
            r = _dot(a_st, v, NN)
            o_intra = jnp.concatenate([r[h * CH:(h + 1) * CH, h * HV:(h + 1) * HV] for h in range(HEADS)], axis=1)
            s_b = s_ref[...].astype(BF16)
            sp_ref[c] = s_b
            o = o_intra + _dot(t["q_in"].astype(BF16), s_b, NT)
            if with_prev:
                o = o + op_ref[rows, :]
            o_ref[rows, :] = o
            kv_t = _dot(v, t["k_out"].astype(BF16), TN)
            s_ref[...] = s_ref[...] * t["decay"] + jnp.where(blockmask, kv_t, 0.0)

    def col(width, c):
        return pl.BlockSpec((tb, width), lambda i: (blk(i), c))

    in_specs = [col(DK, COL_Q), col(DK, COL_K), col(DG, COL_V), col(LR_BLK, COL_LR)] + _gate_specs(l)
    args = [p, p, p, p, gpad, bias]
    if with_prev:
        in_specs.append(pl.BlockSpec((tb, DG), lambda i: (blk(i), 0)))
        args.append(o_prev)
    return pl.pallas_call(
        body, name=name, grid=(nb,), in_specs=in_specs,
        out_specs=[pl.BlockSpec((tb, DG), lambda i: (blk(i), 0)), pl.BlockSpec((ncb, DG, DK), lambda i: (blk(i), 0, 0))],
        out_shape=[jax.ShapeDtypeStruct((L, DG), F32), jax.ShapeDtypeStruct((nch, DG, DK), BF16)],
        scratch_shapes=[pltpu.VMEM((DG, DK), F32)],
        compiler_params=_params(("arbitrary",)),
    )(*args)


def _gla_bwd(p, gpad, bias, l, sprev, d_o, prev, fwd, name):
    L = p.shape[0]
    tb = min(TILE_GLA, L)
    nb, ncb = L // tb, tb // CH
    blk = (lambda i: nb - 1 - i) if fwd else (lambda i: i)
    with_prev = prev is not None

    def body(*refs):
        q_ref, k_ref, v_ref, lr_ref, gp_ref, bias_ref, sp_ref, do_ref = refs[:8]
        rest = refs[8:]
        if with_prev:
            pq_ref, pk_ref, pv_ref, plr_ref = rest[:4]
            rest = rest[4:]
        dq_ref, dk_ref, dv_ref, dlr_ref, dg_ref, db_ref, ds_ref = rest
        i = pl.program_id(0)

        @pl.when(i == 0)
        def _():
            ds_ref[...] = jnp.zeros_like(ds_ref)
            dg_ref[...] = jnp.zeros_like(dg_ref)
            db_ref[...] = jnp.zeros_like(db_ref)

        tri_b, tri_t_b, tri_st, head_masks, blockmask = _gla_consts(fwd)
        for c in (reversed(range(ncb)) if fwd else range(ncb)):
            rows = pl.ds(c * CH, CH)
            t = _gla_chunk_fwd_terms(q_ref, k_ref, lr_ref, gp_ref, bias_ref, rows, tri_b, head_masks)
            v = v_ref[rows, :]
            do = do_ref[rows, :]
            q_in, k_in, k_out = t["q_in"], t["k_in"], t["k_out"]
            q_b, k_in_b, k_out_b = q_in.astype(BF16), k_in.astype(BF16), k_out.astype(BF16)
            scores = _dot(t["q_st"], k_in_b, NT)
            a_st = jnp.where(tri_st, scores, 0.0).astype(BF16)
            s_prev = sp_ref[c]
            ds = ds_ref[...]
            ds_b = ds.astype(BF16)

            da_heads = [_dot(do[:, h * HV:(h + 1) * HV], v[:, h * HV:(h + 1) * HV], NT) for h in range(HEADS)]
            da_st = jnp.where(tri_st, jnp.concatenate(da_heads, axis=0), 0.0).astype(BF16)

            dv_heads = [_dot(a_st[h * CH:(h + 1) * CH, :], do[:, h * HV:(h + 1) * HV], TN) for h in range(HEADS)]
            dv = jnp.concatenate(dv_heads, axis=1) + _dot(k_out_b, ds_b, NT)

            x = _dot(da_st, k_in_b, NN)
            dq_in = _dot(do, s_prev, NN)
            for h in range(HEADS):
                dq_in = dq_in + jnp.where(head_masks[h], x[h * CH:(h + 1) * CH, :], 0.0)
            dk_in = _dot(da_st, t["q_st"], TN)
            dk_out = _dot(v, ds_b, NN)
            d_decay = jnp.sum(ds * s_prev.astype(F32), axis=0, keepdims=True)
            ds_ref[...] = ds * t["decay"] + jnp.where(blockmask, _dot(do, q_b, TN), 0.0)

            dq = dq_in * t["e"] * (HK ** -0.5)
            dk = dk_in * t["einv"] + dk_out * t["eout"]
            dko_ko = dk_out * k_out
            dcum = dq_in * q_in - dk_in * k_in - dko_ko
            dcl = jnp.sum(dko_ko, axis=0, keepdims=True) + d_decay * t["decay"]
            da = _dot_hilo(tri_t_b, dcum) + dcl
            dpre = da * t["sig_neg"] * (1.0 / 16.0)
            dpre_b = dpre.astype(BF16)
            dlr = _dot(dpre_b, gp_ref[...], NT)
            dg_ref[...] += _dot(lr_ref[rows, :], dpre_b, TN)
            db_ref[...] += jnp.sum(dpre, axis=0, keepdims=True)
            if with_prev:
                dq = dq + pq_ref[rows, :].astype(F32)
                dk = dk + pk_ref[rows, :].astype(F32)
                dv = dv + pv_ref[rows, :].astype(F32)
                dlr = dlr + plr_ref[rows, :].astype(F32)
            dq_ref[rows, :] = dq.astype(BF16)
            dk_ref[rows, :] = dk.astype(BF16)
            dv_ref[rows, :] = dv.astype(BF16)
            dlr_ref[rows, :] = dlr.astype(BF16)

    def col(width, c):
        return pl.BlockSpec((tb, width), lambda i: (blk(i), c))

    in_specs = [col(DK, COL_Q), col(DK, COL_K), col(DG, COL_V), col(LR_BLK, COL_LR)] + _gate_specs(l) + [
        pl.BlockSpec((ncb, DG, DK), lambda i: (blk(i), 0, 0)), col(DG, 0)]
    args = [p, p, p, p, gpad, bias, sprev, d_o]
    tiles = [col(DK, 0), col(DK, 0), col(DG, 0), col(LR_BLK, 0)]
    if with_prev:
        in_specs += tiles
        args += list(prev)
    return pl.pallas_call(
        body, name=name, grid=(nb,), in_specs=in_specs,
        out_specs=tiles + [pl.BlockSpec((LR_BLK, DK), lambda i: (0, 0)), pl.BlockSpec((1, DK), lambda i: (0, 0))],
        out_shape=[jax.ShapeDtypeStruct((L, DK), BF16), jax.ShapeDtypeStruct((L, DK), BF16),
                   jax.ShapeDtypeStruct((L, DG), BF16), jax.ShapeDtypeStruct((L, LR_BLK), BF16),
                   jax.ShapeDtypeStruct((LR_BLK, DK), F32), jax.ShapeDtypeStruct((1, DK), F32)],
        scratch_shapes=[pltpu.VMEM((DG, DK), F32)],
        compiler_params=_params(("arbitrary",)),
    )(*args)


def _mixer_out(p, conv_a, gh, l, o_tot, name):
    L = p.shape[0]
    tm = min(TILE_TOKENS, L)
    n = L // tm
    pmap, nmap = _halo_maps(tm, L)

    def body(gb_ref, gc_ref, gcp_ref, gcn_ref, gv_ref, gvp_ref, gvn_ref, go_ref, cw_ref, o_ref, gh_ref, y_ref):
        ps, ns = _edge_scales(pl.program_id(0), n)
        z = gc_ref[...].astype(F32) * gv_ref[...].astype(F32)
        zp = gcp_ref[...].astype(F32) * gvp_ref[...].astype(F32) * ps
        zn = gcn_ref[...].astype(F32) * gvn_ref[...].astype(F32) * ns
        z_dn, z_up = _shift(z, zp, zn)
        conv = cw_ref[0:1, :] * z_dn + cw_ref[1:2, :] * z + cw_ref[2:3, :] * z_up
        y_ref[:, 0:DC] = (gb_ref[...].astype(F32) * conv).astype(BF16)
        o = o_ref[...]
        go = go_ref[...].astype(F32)
        for h in range(HEADS):
            oh = o[:, h * HV:(h + 1) * HV]
            on = oh * _rstd(oh) * gh_ref[...]
            act, _ = _silu_parts(go[:, h * HV:(h + 1) * HV])
            y_ref[:, DC + h * HV:DC + (h + 1) * HV] = (act * on).astype(BF16)

    def main(c):
        return pl.BlockSpec((tm, DC), lambda i: (i, c))

    def halo(c, imap):
        return pl.BlockSpec((HB, DC), lambda i: (imap(i), c))

    return pl.pallas_call(
        body, name=name, grid=(n,),
        in_specs=[main(COL_GB), main(COL_GC), halo(COL_GC, pmap), halo(COL_GC, nmap),
                  main(COL_GV), halo(COL_GV, pmap), halo(COL_GV, nmap), main(COL_GO),
                  pl.BlockSpec((None, 3, DC), lambda i: (l, 0, 0)), pl.BlockSpec((tm, DG), lambda i: (i, 0)),
                  _gain_spec(l, HV)],
        out_specs=pl.BlockSpec((tm, D), lambda i: (i, 0)), out_shape=jax.ShapeDtypeStruct((L, D), BF16),
        compiler_params=_params(("parallel",)),
    )(p, p, p, p, p, p, p, p, conv_a, o_tot, gh)


def _mixer_out_bwd(p, conv_a, gh, l, o_tot, dy, name):
    L = p.shape[0]
    tm = min(TILE_TOKENS, L)
    n = L // tm
    pmap, nmap = _halo_maps(tm, L)

    def body(gb_ref, gbp_ref, gbn_ref, gc_ref, gcp_ref, gcn_ref, gv_ref, gvp_ref, gvn_ref, go_ref, cw_ref, o_ref,
             gh_ref, dy_ref, dyp_ref, dyn_ref, dgb_ref, dgc_ref, dgv_ref, dgo_ref, do_ref, dcw_ref, dgh_ref):
        i = pl.program_id(0)
        ps, ns = _edge_scales(i, n)
        gb = gb_ref[...].astype(F32)
        gc = gc_ref[...].astype(F32)
        gv = gv_ref[...].astype(F32)
        z = gc * gv
        zp = gcp_ref[...].astype(F32) * gvp_ref[...].astype(F32) * ps
        zn = gcn_ref[...].astype(F32) * gvn_ref[...].astype(F32) * ns
        z_dn, z_up = _shift(z, zp, zn)
        w0, w1, w2 = cw_ref[0:1, :], cw_ref[1:2, :], cw_ref[2:3, :]
        conv = w0 * z_dn + w1 * z + w2 * z_up
        dya = dy_ref[:, 0:DC].astype(F32)
        dgb_ref[...] = (dya * conv).astype(BF16)
        dc = dya * gb
        dcp = dyp_ref[...].astype(F32) * gbp_ref[...].astype(F32) * ps
        dcn = dyn_ref[...].astype(F32) * gbn_ref[...].astype(F32) * ns
        dc_dn, dc_up = _shift(dc, dcp, dcn)
        dz = w0 * dc_up + w1 * dc + w2 * dc_dn
        dgc_ref[...] = (dz * gv).astype(BF16)
        dgv_ref[...] = (dz * gc).astype(BF16)
        dcw = [jnp.sum(zs * dc, axis=0, keepdims=True) for zs in (z_dn, z, z_up)]

        o = o_ref[...]
        go = go_ref[...].astype(F32)
        dgh = jnp.zeros((1, HV), F32)
        for h in range(HEADS):
            sl = slice(h * HV, (h + 1) * HV)
            oh = o[:, sl]
            r = _rstd(oh)
            act, sg = _silu_parts(go[:, sl])
            dyb = dy_ref[:, DC + h * HV:DC + (h + 1) * HV].astype(F32)
            on = oh * r * gh_ref[...]
            dgo_ref[:, sl] = (dyb * on * (sg + act * (1.0 - sg))).astype(BF16)
            don = dyb * act
            zz = don * gh_ref[...]
            do_ref[:, sl] = (r * zz - oh * (r * r * r) * jnp.mean(oh * zz, axis=-1, keepdims=True)).astype(BF16)
            dgh = dgh + jnp.sum(don * oh * r, axis=0, keepdims=True)

        @pl.when(i == 0)
        def _():
            dcw_ref[...] = jnp.zeros_like(dcw_ref)
            dgh_ref[...] = jnp.zeros_like(dgh_ref)

        for kk in range(3):
            dcw_ref[kk:kk + 1, :] += dcw[kk]
        dgh_ref[...] += dgh

    def main(c):
        return pl.BlockSpec((tm, DC), lambda i: (i, c))

    def halo(c, imap):
        return pl.BlockSpec((HB, DC), lambda i: (imap(i), c))

    tile = pl.BlockSpec((tm, DC), lambda i: (i, 0))
    return pl.pallas_call(
        body, name=name, grid=(n,),
        in_specs=[main(COL_GB), halo(COL_GB, pmap), halo(COL_GB, nmap), main(COL_GC), halo(COL_GC, pmap),
                  halo(COL_GC, nmap), main(COL_GV), halo(COL_GV, pmap), halo(COL_GV, nmap), main(COL_GO),
                  pl.BlockSpec((None, 3, DC), lambda i: (l, 0, 0)), tile, _gain_spec(l, HV),
                  pl.BlockSpec((tm, D), lambda i: (i, 0)), halo(0, pmap), halo(0, nmap)],
        out_specs=[tile, tile, tile, tile, tile, pl.BlockSpec((3, DC), lambda i: (0, 0)),
                   pl.BlockSpec((1, HV), lambda i: (0, 0))],
        out_shape=[jax.ShapeDtypeStruct((L, DC), BF16)] * 5
        + [jax.ShapeDtypeStruct((3, DC), F32), jax.ShapeDtypeStruct((1, HV), F32)],
        compiler_params=_params(("arbitrary",)),
    )(p, p, p, p, p, p, p, p, p, p, conv_a, o_tot, gh, dy, dy, dy)


def _ffn_specs(tm, L, l, row_axis, sh_axis):
    pmap, nmap = _halo_maps(tm, L)

    def u(off, imap=None, rows=tm):
        if imap is None:
            return pl.BlockSpec((None, rows, SH_FF), lambda *g: (g[sh_axis] + off, g[row_axis], 0))
        return pl.BlockSpec((None, rows, SH_FF), lambda *g: (g[sh_axis] + off, imap(g[row_axis]), 0))

    def cw(off):
        return pl.BlockSpec((None, None, 3, SH_FF), lambda *g: (l, g[sh_axis] + off, 0, 0))

    u_specs = [u(0), u(0, pmap, HB), u(0, nmap, HB), u(FF_HALF), u(FF_HALF, pmap, HB), u(FF_HALF, nmap, HB)]
    return u_specs, [cw(0), cw(FF_HALF)]


def _conv3(x_ref, xp_ref, xn_ref, cw_ref, ps, ns):
    x = x_ref[...].astype(F32)
    x_dn, x_up = _shift(x, xp_ref[...].astype(F32) * ps, xn_ref[...].astype(F32) * ns)
    return cw_ref[0:1, :] * x_dn + cw_ref[1:2, :] * x + cw_ref[2:3, :] * x_up, (x_dn, x, x_up)


def _ffn_act(u8, cw, l, name):
    L = u8.shape[1]
    tm = min(TILE_FFN, L)
    n = L // tm
    u_specs, cw_specs = _ffn_specs(tm, L, l, 0, 1)

    def body(g_ref, gp_ref, gn_ref, v_ref, vp_ref, vn_ref, cwg_ref, cwv_ref, a_ref):
        ps, ns = _edge_scales(pl.program_id(0), n)
        gate, _ = _conv3(g_ref, gp_ref, gn_ref, cwg_ref, ps, ns)
        val, _ = _conv3(v_ref, vp_ref, vn_ref, cwv_ref, ps, ns)
        act, _ = _silu_parts(gate)
        a_ref[...] = (act * val).astype(BF16)

    return pl.pallas_call(
        body, name=name, grid=(n, FF_HALF), in_specs=u_specs + cw_specs,
        out_specs=pl.BlockSpec((None, tm, SH_FF), lambda i, d: (d, i, 0)),
        out_shape=jax.ShapeDtypeStruct((FF_HALF, L, SH_FF), BF16),
        compiler_params=_params(("parallel", "parallel")),
    )(u8, u8, u8, u8, u8, u8, cw, cw)


def _ffn_act_bwd(u8, cw, l, da, name):
    L = u8.shape[1]
    tm = min(TILE_FFN, L)
    n = L // tm
    u_specs, cw_specs = _ffn_specs(tm, L, l, 1, 0)

    def body(g_ref, gp_ref, gn_ref, v_ref, vp_ref, vn_ref, cwg_ref, cwv_ref, da_ref, du_ref, dcw_ref):
        i = pl.program_id(1)
        ps, ns = _edge_scales(i, n)
        gate, g_sh = _conv3(g_ref, gp_ref, gn_ref, cwg_ref, ps, ns)
        val, v_sh = _conv3(v_ref, vp_ref, vn_ref, cwv_ref, ps, ns)
        act, sg = _silu_parts(gate)
        da_f = da_ref[...].astype(F32)
        dgate = da_f * val * (sg + act * (1.0 - sg))
        dval = da_f * act
        du_ref[0] = dgate.astype(BF16)
        du_ref[1] = dval.astype(BF16)

        @pl.when(i == 0)
        def _():
            dcw_ref[...] = jnp.zeros_like(dcw_ref)

        for kk in range(3):
            dcw_ref[0, kk:kk + 1, :] += jnp.sum(g_sh[kk] * dgate, axis=0, keepdims=True)
            dcw_ref[1, kk:kk + 1, :] += jnp.sum(v_sh[kk] * dval, axis=0, keepdims=True)

    return pl.pallas_call(
        body, name=name, grid=(FF_HALF, n),
        in_specs=u_specs + cw_specs + [pl.BlockSpec((None, tm, SH_FF), lambda d, i: (d, i, 0))],
        out_specs=[pl.BlockSpec((2, None, tm, SH_FF), lambda d, i: (0, d, i, 0)),
                   pl.BlockSpec((2, None, 3, SH_FF), lambda d, i: (0, d, 0, 0))],
        out_shape=[jax.ShapeDtypeStruct((2, FF_HALF, L, SH_FF), BF16),
                   jax.ShapeDtypeStruct((2, FF_HALF, 3, SH_FF), F32)],
        compiler_params=_params(("parallel", "arbitrary")),
    )(u8, u8, u8, u8, u8, u8, cw, cw, da)


def _ffn_conv_t(du8, cw, l, name):
    L = du8.shape[1]
    tm = min(TILE_FFN, L)
    n = L // tm
    pmap, nmap = _halo_maps(tm, L)

    def body(x_ref, xp_ref, xn_ref, cw_ref, o_ref):
        ps, ns = _edge_scales(pl.program_id(1), n)
        x = x_ref[...].astype(F32)
        x_dn, x_up = _shift(x, xp_ref[...].astype(F32) * ps, xn_ref[...].astype(F32) * ns)
        o_ref[...] = (cw_ref[0:1, :] * x_up + cw_ref[1:2, :] * x + cw_ref[2:3, :] * x_dn).astype(BF16)

    return pl.pallas_call(
        body, name=name, grid=(N_DEV, n),
        in_specs=[pl.BlockSpec((None, tm, SH_FF), lambda d, i: (d, i, 0)),
                  pl.BlockSpec((None, HB, SH_FF), lambda d, i: (d, pmap(i), 0)),
                  pl.BlockSpec((None, HB, SH_FF), lambda d, i: (d, nmap(i), 0)),
                  pl.BlockSpec((None, None, 3, SH_FF), lambda d, i: (l, d, 0, 0))],
        out_specs=pl.BlockSpec((None, tm, SH_FF), lambda d, i: (d, i, 0)),
        out_shape=jax.ShapeDtypeStruct((N_DEV, L, SH_FF), BF16),
        compiler_params=_params(("parallel", "parallel")),
    )(du8, du8, du8, cw)


def _loss_grad(xl, target, name):
    L = xl.shape[0]
    tm = min(TILE_TOKENS, L)

    def body(x_ref, t_ref, dx_ref, sq_ref):
        i = pl.program_id(0)
        err = x_ref[...] - t_ref[...]
        dx_ref[...] = err * (1.0 / D)
        part = jnp.sum(err * err, axis=0, keepdims=True)

        @pl.when(i == 0)
        def _():
            sq_ref[...] = part

        @pl.when(i > 0)
        def _():
            sq_ref[...] += part

    tile = pl.BlockSpec((tm, D), lambda i: (i, 0))
    return pl.pallas_call(
        body, name=name, grid=(L // tm,), in_specs=[tile, tile],
        out_specs=[tile, pl.BlockSpec((1, D), lambda i: (0, 0))],
        out_shape=[jax.ShapeDtypeStruct((L, D), F32), jax.ShapeDtypeStruct((1, D), F32)],
        compiler_params=_params(("arbitrary",)),
    )(xl, target)


MESH = pl.DeviceIdType.MESH
HBM_SPEC = pl.BlockSpec(memory_space=pltpu.HBM)


def _position():
    return lax.axis_index("x"), lax.axis_index("y"), lax.axis_index("c")


def _other_chips(x, y):
    return [(1 - x, y), (x, 1 - y), (1 - x, 1 - y)]


def _all_gather(shards, name):
    n = len(shards)

    def body(*refs):
        x_refs, out_refs = refs[:n], refs[n:2 * n]
        send_sems, recv_sems, local_sems = refs[2 * n:]
        x, y, c = _position()
        me, sibling = (x, y, c), (x, y, 1 - c)
        chips = _other_chips(x, y)

        def slot(t, px, py, pc):
            return out_refs[t].at[:, 4 * px + 2 * py + pc]

        def copy(t, k, block, to, from_input=False):
            return pltpu.make_async_remote_copy(
                src_ref=x_refs[t] if from_input else slot(t, *block), dst_ref=slot(t, *block),
                send_sem=send_sems.at[k * n + t], recv_sem=recv_sems.at[k * n + t], device_id=to, device_id_type=MESH)

        mine = [pltpu.make_async_copy(x_refs[t], slot(t, *me), local_sems.at[t]) for t in range(n)]
        for cp in mine:
            cp.start()
        first = [copy(t, 0, me, sibling, True) for t in range(n)]
        first += [copy(t, 1 + j, me, (*chip, c), True) for j, chip in enumerate(chips) for t in range(n)]
        for cp in first:
            cp.start()
        passed = []
        for j, chip in enumerate(chips):
            for t in range(n):
                copy(t, 1 + j, (*chip, c), me).wait_recv()
                passed.append(copy(t, 4 + j, (*chip, c), sibling))
                passed[-1].start()
        for t in range(n):
            copy(t, 0, sibling, me).wait_recv()
        for j, chip in enumerate(chips):
            for t in range(n):
                copy(t, 4 + j, (*chip, 1 - c), me).wait_recv()
        for cp in first + passed:
            cp.wait_send()
        for cp in mine:
            cp.wait()

    return pl.pallas_call(
        body, name=name,
        out_shape=[jax.ShapeDtypeStruct((s.shape[0], N_DEV) + s.shape[1:], s.dtype) for s in shards],
        in_specs=[HBM_SPEC] * n, out_specs=[HBM_SPEC] * n,
        scratch_shapes=[pltpu.SemaphoreType.DMA((7 * n,)), pltpu.SemaphoreType.DMA((7 * n,)),
                        pltpu.SemaphoreType.DMA((n,))],
    )(*shards)


def _exchange_sibling(grads, name):
    n = len(grads)

    def body(*refs):
        g_refs, out_refs, send_sems, recv_sems = refs[:n], refs[n:2 * n], refs[2 * n], refs[2 * n + 1]
        x, y, c = _position()
        copies = [pltpu.make_async_remote_copy(
            src_ref=g_refs[t].at[:, 2 * k + (1 - c)], dst_ref=out_refs[t].at[:, k], send_sem=send_sems.at[k * n + t],
            recv_sem=recv_sems.at[k * n + t], device_id=(x, y, 1 - c), device_id_type=MESH)
            for k in range(N_CHIP) for t in range(n)]
        for cp in copies:
            cp.start()
        for cp in copies:
            cp.wait()

    return pl.pallas_call(
        body, name=name,
        out_shape=[jax.ShapeDtypeStruct((g.shape[0], N_CHIP) + g.shape[2:], g.dtype) for g in grads],
        in_specs=[HBM_SPEC] * n, out_specs=[HBM_SPEC] * n,
        scratch_shapes=[pltpu.SemaphoreType.DMA((N_CHIP * n,)), pltpu.SemaphoreType.DMA((N_CHIP * n,))],
    )(*grads)


class _Comm(NamedTuple):
    ins: tuple
    out_shapes: tuple
    aliases: dict
    sems: tuple
    start: Callable
    wait: Callable


def _comm_of(ins, out_shapes, aliases, sems, copies):
    def start(ci, co, send, recv, local):
        for cp in copies(ci, co, send, recv, local):
            cp.start()

    def wait(ci, co, send, recv, local):
        for cp in copies(ci, co, send, recv, local):
            cp.wait()

    return _Comm(tuple(ins), tuple(out_shapes), aliases, sems, start, wait)


def _remote(src, dst, send, recv, idx, to):
    return pltpu.make_async_remote_copy(src_ref=src, dst_ref=dst, send_sem=send.at[idx], recv_sem=recv.at[idx],
                                        device_id=to, device_id_type=MESH)


def _gather_ici_comm(shards):
    n = len(shards)

    def copies(ci, co, send, recv, local):
        x, y, c = _position()
        me = 4 * x + 2 * y + c
        mine = [pltpu.make_async_copy(ci[t], co[t].at[:, me], local.at[t]) for t in range(n)]
        return mine + [_remote(ci[t], co[t].at[:, me], send, recv, j * n + t, (cx, cy, c))
                       for j, (cx, cy) in enumerate(_other_chips(x, y)) for t in range(n)]

    outs = [jax.ShapeDtypeStruct((1, N_DEV) + s.shape[1:], s.dtype) for s in shards]
    return _comm_of(shards, outs, {}, (3 * n, 3 * n, n), copies)


def _gather_d2d_comm(partials):
    n = len(partials)

    def copies(ci, co, send, recv, local):
        x, y, c = _position()
        return [_remote(co[t].at[:, 4 * cx + 2 * cy + c], co[t].at[:, 4 * cx + 2 * cy + c], send, recv, k * n + t,
                        (x, y, 1 - c))
                for k, (cx, cy) in enumerate([(x, y)] + _other_chips(x, y)) for t in range(n)]

    outs = [jax.ShapeDtypeStruct(p.shape, p.dtype) for p in partials]
    return _comm_of(partials, outs, {t: t for t in range(n)}, (N_CHIP * n, N_CHIP * n, 1), copies)


def _grads_d2d_comm(grads):
    n = len(grads)

    def copies(ci, co, send, recv, local):
        x, y, c = _position()
        return [_remote(ci[t].at[:, 2 * k + (1 - c)], co[t].at[:, k], send, recv, k * n + t, (x, y, 1 - c))
                for k in range(N_CHIP) for t in range(n)]

    outs = [jax.ShapeDtypeStruct((g.shape[0], N_CHIP) + g.shape[2:], g.dtype) for g in grads]
    return _comm_of(grads, outs, {}, (N_CHIP * n, N_CHIP * n, 1), copies)


def _grads_ici_comm(parts):
    n = len(parts)

    def copies(ci, co, send, recv, local):
        x, y, c = _position()
        my_chip = 2 * x + y
        mine = [pltpu.make_async_copy(ci[t].at[:, my_chip], co[t].at[:, my_chip], local.at[t]) for t in range(n)]
        return mine + [_remote(ci[t].at[:, 2 * cx + cy], co[t].at[:, my_chip], send, recv, j * n + t, (cx, cy, c))
                       for j, (cx, cy) in enumerate(_other_chips(x, y)) for t in range(n)]

    outs = [jax.ShapeDtypeStruct(p.shape, p.dtype) for p in parts]
    return _comm_of(parts, outs, {}, (3 * n, 3 * n, n), copies)


def _row_tile(rows):
    return 256 if rows % 256 == 0 else rows


def _pair_sum(g, recv, c_idx, out_dtype, name):
    lay, _, rows, cols = g.shape
    tr = _row_tile(rows)

    def body(c_ref, g_ref, r_ref, o_ref):
        o_ref[...] = (g_ref[...] + r_ref[...]).astype(o_ref.dtype)

    def spec(blk_of):
        return pl.BlockSpec((None, None, tr, cols), lambda l, k, r, c_ref: (l, blk_of(k, c_ref), r, 0))

    return pl.pallas_call(
        body, name=name,
        grid_spec=pltpu.PrefetchScalarGridSpec(
            num_scalar_prefetch=1, grid=(lay, N_CHIP, rows // tr),
            in_specs=[spec(lambda k, c_ref: 2 * k + c_ref[0]), spec(lambda k, c_ref: k)],
            out_specs=spec(lambda k, c_ref: k)),
        out_shape=jax.ShapeDtypeStruct((lay, N_CHIP, rows, cols), out_dtype),
        compiler_params=_params(("parallel", "parallel", "parallel")),
    )(c_idx, g, recv)


def _exchange_chips(parts, name):
    n = len(parts)

    def body(*refs):
        p_refs, out_refs = refs[:n], refs[n:2 * n]
        send_sems, recv_sems, local_sems = refs[2 * n:]
        x, y, c = _position()
        my_chip = 2 * x + y
        mine = [pltpu.make_async_copy(p_refs[t].at[:, my_chip], out_refs[t].at[:, my_chip], local_sems.at[t])
                for t in range(n)]
        for cp in mine:
            cp.start()
        copies = [pltpu.make_async_remote_copy(
            src_ref=p_refs[t].at[:, 2 * cx + cy], dst_ref=out_refs[t].at[:, my_chip], send_sem=send_sems.at[j * n + t],
            recv_sem=recv_sems.at[j * n + t], device_id=(cx, cy, c), device_id_type=MESH)
            for j, (cx, cy) in enumerate(_other_chips(x, y)) for t in range(n)]
        for cp in copies:
            cp.start()
        for cp in copies:
            cp.wait()
        for cp in mine:
            cp.wait()

    return pl.pallas_call(
        body, name=name, out_shape=[jax.ShapeDtypeStruct(p.shape, p.dtype) for p in parts],
        in_specs=[HBM_SPEC] * n, out_specs=[HBM_SPEC] * n,
        scratch_shapes=[pltpu.SemaphoreType.DMA((3 * n,)), pltpu.SemaphoreType.DMA((3 * n,)),
                        pltpu.SemaphoreType.DMA((n,))],
    )(*parts)


def _sum_adamw(parts, w, m, v, name):
    lay, rows, cols = w.shape
    tr = _row_tile(rows)

    def body(p_ref, w_ref, m_ref, v_ref, g_ref, d_ref, nm_ref, nv_ref):
        g = ((p_ref[0].astype(F32) + p_ref[1].astype(F32)) + p_ref[2].astype(F32)) + p_ref[3].astype(F32)
        g_ref[...] = g
        nm = ADAM_B1 * m_ref[...] + (1.0 - ADAM_B1) * g
        nv = ADAM_B2 * v_ref[...] + (1.0 - ADAM_B2) * (g * g)
        nm_ref[...] = nm
        nv_ref[...] = nv
        m_hat = nm / (1.0 - ADAM_B1 ** ADAM_STEP)
        v_hat = nv / (1.0 - ADAM_B2 ** ADAM_STEP)
        d_ref[...] = -ADAM_LR * (m_hat / (jnp.sqrt(v_hat) + ADAM_EPS) + ADAM_WD * w_ref[...])

    tile = pl.BlockSpec((None, tr, cols), lambda l, r: (l, r, 0))
    return pl.pallas_call(
        body, name=name, grid=(lay, rows // tr),
        in_specs=[pl.BlockSpec((None, N_CHIP, tr, cols), lambda l, r: (l, 0, r, 0)), tile, tile, tile],
        out_specs=[tile] * 4, out_shape=[jax.ShapeDtypeStruct((lay, rows, cols), F32)] * 4,
        compiler_params=_params(("parallel", "parallel")),
    )(parts, w, m, v)


def _pad_rows(flat, rows):
    return jnp.pad(flat, (0, rows * LANES - flat.shape[0])).reshape(rows, LANES)


def _pack_small(tree):
    sh = jnp.concatenate([tree[n].reshape(-1) for n, _, _ in SMALL_SHARDED])
    rep = jnp.concatenate([tree[n].reshape(-1) for n, _ in REPLICATED])
    return jnp.concatenate([_pad_rows(sh, ROWS_SSH), _pad_rows(rep, ROWS_REP)], axis=0)


def _unpack_small(buf):
    out = {}
    for flat, items in ((buf[:ROWS_SSH].reshape(-1), [(n, s) for n, s, _ in SMALL_SHARDED]),
                        (buf[ROWS_SSH:].reshape(-1), REPLICATED)):
        off = 0
        for n, s in items:
            out[n] = flat[off:off + math.prod(s)].reshape(s)
            off += math.prod(s)
    return out


def _full_from_blocks(blocks, s, ax):
    return jnp.concatenate([blocks[d] for d in range(N_DEV)], axis=ax)


def _blocks_from_full(full, s, ax):
    return jnp.stack([lax.slice_in_dim(full, d * s[ax], (d + 1) * s[ax], axis=ax) for d in range(N_DEV)])


def _pack_small_grads(sharded_blocks, replicated):
    sh = jnp.concatenate([sharded_blocks[n].reshape(N_DEV, -1) for n, _, _ in SMALL_SHARDED], axis=1)
    sh = jnp.pad(sh, ((0, 0), (0, ROWS_SSH * LANES - sh.shape[1]))).reshape(N_DEV, ROWS_SSH, LANES)
    rep = _pad_rows(jnp.concatenate([replicated[n].reshape(-1) for n, _ in REPLICATED]), ROWS_REP)
    return jnp.concatenate([sh, jnp.broadcast_to(rep[None], (N_DEV, ROWS_REP, LANES))], axis=1)


def _layer_fwd(x, h1, wts, big, l, l_next, next_shards):
    L = x.shape[0]
    tm = min(TILE_MM, L)
    nt = L // tm
    p = _mm(h1, big["w_in"], dims=NN, grid=(nt, D_INP // 640, 1),
            a_spec=((tm, D), lambda i, j, k: (i, 0)), b_spec=((None, D, 640), lambda i, j, k: (0, 0, j)),
            o_spec=((tm, 640), lambda i, j, k: (i, j)), out_shape=jax.ShapeDtypeStruct((L, D_INP), BF16),
            tile=(tm, 640), name="proj_in")
    o_f, sp_f = _gla_fwd(p, wts["gpad_f"], wts["bias_f"], l, None, True, "gla_fwd_f")
    o_tot, sp_b = _gla_fwd(p, wts["gpad_b"], wts["bias_b"], l, o_f, False, "gla_fwd_b")
    y_cat = _mixer_out(p, wts["conv_a"], wts["gh"], l, o_tot, "mixer_out")
    y = _mm(y_cat, big["w_out"], dims=NN, grid=(nt, 1, 1),
            a_spec=((tm, D), lambda i, j, k: (i, 0)), b_spec=((None, D, D), lambda i, j, k: (0, 0, 0)),
            o_spec=((tm, D), lambda i, j, k: (i, 0)), out_shape=jax.ShapeDtypeStruct((L, D), BF16),
            tile=(tm, D), name="proj_out")
    x1, h2 = _post_pre(x, y, wts["g2"], l, wts["g3"], l, "post_pre_mix")
    u8 = _mm(h2, big["w_up"], dims=NN, grid=(nt, N_DEV, 1),
             a_spec=((tm, D), lambda i, j, k: (i, 0)), b_spec=((None, None, D, SH_FF), lambda i, j, k: (0, j, 0, 0)),
             o_spec=((None, tm, SH_FF), lambda i, j, k: (j, i, 0)),
             out_shape=jax.ShapeDtypeStruct((N_DEV, L, SH_FF), BF16), tile=(tm, SH_FF), name="ffn_up",
             comm=_gather_ici_comm(next_shards) if next_shards else None)
    gathered = None
    if next_shards:
        u8, gathered = u8[0], u8[1:]
    a = _ffn_act(u8, wts["cw"], l, "ffn_act")
    tm1 = min(TILE_MM_KIN, L)
    y2 = _mm(a, big["w_down"], dims=NN, grid=(L // tm1, 1, 1), kin=FF_HALF,
             a_spec=((FF_HALF, tm1, SH_FF), lambda i, j, k: (0, i, 0)),
             b_spec=((None, FF_HALF, SH_FF, D), lambda i, j, k: (0, 0, 0, 0)),
             o_spec=((tm1, D), lambda i, j, k: (i, 0)), out_shape=jax.ShapeDtypeStruct((L, D), BF16),
             tile=(tm1, D), name="ffn_down", comm=_gather_d2d_comm(gathered) if next_shards else None)
    if next_shards:
        y2, gathered = y2[0], y2[1:]
    x2, h1_next = _post_pre(x1, y2, wts["g4"], l, wts["g1"], l_next, "post_pre_ffn")
    saved = dict(x=x, h1=h1, p=p, o_tot=o_tot, sp_f=sp_f, sp_b=sp_b, y_cat=y_cat, y=y, x1=x1, h2=h2, u8=u8, a=a, y2=y2,
                 big=big)
    return x2, h1_next, saved, gathered


def _layer_bwd(dx2, wts, s, l, pending, c_idx):
    L = dx2.shape[0]
    big = s["big"]
    tm = min(TILE_MM, L)
    nt = L // tm
    tm1 = min(TILE_MM_KIN, L)
    tk = min(TILE_MM_TOKENS, L)
    nkt = L // tk
    dy2, dg4 = _norm_bwd(s["y2"], wts["g4"], l, dx2, None, "norm_bwd_ffn_post")
    da = _mm(dy2, big["w_down"], dims=NT, grid=(nt, FF_HALF, 1),
             a_spec=((tm, D), lambda i, j, k: (i, 0)), b_spec=((None, None, SH_FF, D), lambda i, j, k: (0, j, 0, 0)),
             o_spec=((None, tm, SH_FF), lambda i, j, k: (j, i, 0)),
             out_shape=jax.ShapeDtypeStruct((FF_HALF, L, SH_FF), BF16), tile=(tm, SH_FF), name="ffn_down_dx",
             comm=_grads_d2d_comm(pending) if pending else None)
    pairs = None
    if pending:
        da, from_sibling = da[0], da[1:]
        pairs = [_pair_sum(g, r, c_idx, BF16, "grads_pair_sum") for g, r in zip(pending, from_sibling)]
    dw_down = _mm(s["a"], dy2, dims=TN, grid=(FF_HALF, 1, nkt),
                  a_spec=((None, tk, SH_FF), lambda i, j, k: (i, k, 0)), b_spec=((tk, D), lambda i, j, k: (k, 0)),
                  o_spec=((SH_FF, D), lambda i, j, k: (i, 0)), out_shape=jax.ShapeDtypeStruct((DFF, D), F32),
                  tile=(SH_FF, D), name="ffn_down_dw")
    du, dcw = _ffn_act_bwd(s["u8"], wts["cw"], l, da, "ffn_act_bwd")
    d_u8 = _ffn_conv_t(du.reshape(N_DEV, L, SH_FF), wts["cw"], l, "ffn_conv_t")
    dh2 = _mm(d_u8, big["w_up"], dims=NT, grid=(L // tm1, 1, N_DEV // FF_HALF), kin=FF_HALF,
              a_spec=((FF_HALF, tm1, SH_FF), lambda i, j, k: (k, i, 0)),
              b_spec=((None, FF_HALF, D, SH_FF), lambda i, j, k: (0, k, 0, 0)),
              o_spec=((tm1, D), lambda i, j, k: (i, 0)), out_shape=jax.ShapeDtypeStruct((L, D), BF16),
              tile=(tm1, D), name="ffn_up_dx")
    dw_up = _mm(s["h2"], d_u8, dims=TN, grid=(1, N_DEV, nkt),
                a_spec=((tk, D), lambda i, j, k: (k, 0)), b_spec=((None, tk, SH_FF), lambda i, j, k: (j, k, 0)),
                o_spec=((None, D, SH_FF), lambda i, j, k: (j, 0, 0)),
                out_shape=jax.ShapeDtypeStruct((N_DEV, D, SH_FF), F32), tile=(D, SH_FF), name="ffn_up_dw",
                comm=_grads_ici_comm(pairs) if pending else None)
    parts = None
    if pending:
        dw_up, parts = dw_up[0], dw_up[1:]
    dx1, dg3 = _norm_bwd(s["x1"], wts["g3"], l, dh2, dx2, "norm_bwd_ffn_pre")
    dy, dg2 = _norm_bwd(s["y"], wts["g2"], l, dx1, None, "norm_bwd_mix_post")
    dy_cat = _mm(dy, big["w_out"], dims=NT, grid=(nt, 1, 1),
                 a_spec=((tm, D), lambda i, j, k: (i, 0)), b_spec=((None, D, D), lambda i, j, k: (0, 0, 0)),
                 o_spec=((tm, D), lambda i, j, k: (i, 0)), out_shape=jax.ShapeDtypeStruct((L, D), BF16),
                 tile=(tm, D), name="proj_out_dx")
    dw_out = _mm(s["y_cat"], dy, dims=TN, grid=(1, 1, nkt),
                 a_spec=((tk, D), lambda i, j, k: (k, 0)), b_spec=((tk, D), lambda i, j, k: (k, 0)),
                 o_spec=((D, D), lambda i, j, k: (0, 0)), out_shape=jax.ShapeDtypeStruct((D, D), F32),
                 tile=(D, D), name="proj_out_dw")
    dgb, dgc, dgv, dgo, d_o, dconv_a, dgh = _mixer_out_bwd(s["p"], wts["conv_a"], wts["gh"], l, s["o_tot"], dy_cat,
                                                          "mixer_out_bwd")
    part_f = _gla_bwd(s["p"], wts["gpad_f"], wts["bias_f"], l, s["sp_f"], d_o, None, True, "gla_bwd_f")
    dq, dk, dv, dlr, dgp_b, dbias_b = _gla_bwd(s["p"], wts["gpad_b"], wts["bias_b"], l, s["sp_b"], d_o, part_f[:4],
                                               False, "gla_bwd_b")
    dp = jnp.concatenate([dgb, dgc, dgv, dq, dk, dv, dgo, dlr], axis=1)
    dh1 = _mm(dp, big["w_in"], dims=NT, grid=(L // tm1, 1, 1),
              a_spec=((tm1, D_INP), lambda i, j, k: (i, 0)), b_spec=((None, D, D_INP), lambda i, j, k: (0, 0, 0)),
              o_spec=((tm1, D), lambda i, j, k: (i, 0)), out_shape=jax.ShapeDtypeStruct((L, D), BF16),
              tile=(tm1, D), name="proj_in_dx")
    dw_in = _mm(s["h1"], dp, dims=TN, grid=(1, D_INP // 640, nkt),
                a_spec=((tk, D), lambda i, j, k: (k, 0)), b_spec=((tk, 640), lambda i, j, k: (k, j)),
                o_spec=((D, 640), lambda i, j, k: (0, j)), out_shape=jax.ShapeDtypeStruct((D, D_INP), F32),
                tile=(D, 640), name="proj_in_dw")
    dx0, dg1 = _norm_bwd(s["x"], wts["g1"], l, dh1, dx1, "norm_bwd_mix_pre")
    grads = dict(
        norm_mix_pre=dg1[0], norm_mix_post=dg2[0], norm_ffn_pre=dg3[0], norm_ffn_post=dg4[0],
        gate_bias_fwd=part_f[5][0], gate_bias_bwd=dbias_b[0], gla_head_norm=dgh[0],
        w_in=_blocks_from_full(dw_in, (D, SH_IN), 1), w_out=dw_out.reshape(N_DEV, D // N_DEV, D), w_up=dw_up,
        w_down=dw_down.reshape(N_DEV, DFF // N_DEV, D),
        conv_a=_blocks_from_full(dconv_a, (3, DC // N_DEV), 1),
        gate_up_fwd=_blocks_from_full(part_f[4][:RANK], (RANK, DK // N_DEV), 1),
        gate_up_bwd=_blocks_from_full(dgp_b[RANK:2 * RANK], (RANK, DK // N_DEV), 1),
        conv_ffn=dcw.reshape(N_DEV, 3, SH_FF))
    return dx0, grads, parts


def _matmul_weights(g_in, g_out, g_up, g_down):
    w_in = jnp.concatenate([g_in[:, d] for d in range(N_DEV)] + [jnp.zeros((1, D, D_INP - D_IN), BF16)], axis=2)
    return dict(w_in=w_in, w_out=g_out.reshape(1, D, D), w_up=g_up, w_down=g_down.reshape(1, FF_HALF, SH_FF, D))


def _small_weights(g_small, rep):
    small = {n: jnp.moveaxis(t, 0, 1) for n, t in jax.vmap(_unpack_small)(
        jnp.concatenate([g_small[0], jnp.zeros((N_DEV, ROWS_REP, LANES), F32)], axis=1)).items()
        if n in [s[0] for s in SMALL_SHARDED]}
    conv_a = jnp.concatenate([small["conv_a"][:, d] for d in range(N_DEV)], axis=2)
    gate_f = jnp.concatenate([small["gate_up_fwd"][:, d] for d in range(N_DEV)], axis=2).astype(BF16)
    gate_b = jnp.concatenate([small["gate_up_bwd"][:, d] for d in range(N_DEV)], axis=2).astype(BF16)
    zeros = jnp.zeros((DEPTH, LR_BLK, DK), BF16)
    return dict(
        conv_a=conv_a, cw=small["conv_ffn"],
        gpad_f=zeros.at[:, :RANK].set(gate_f), gpad_b=zeros.at[:, RANK:2 * RANK].set(gate_b),
        bias_f=rep["gate_bias_fwd"][:, None, :], bias_b=rep["gate_bias_bwd"][:, None, :],
        gh=rep["gla_head_norm"][:, None, :],
        g1=rep["norm_mix_pre"][:, None, :], g2=rep["norm_mix_post"][:, None, :],
        g3=rep["norm_ffn_pre"][:, None, :], g4=rep["norm_ffn_post"][:, None, :])


def kernel(x, norm_mix_pre, norm_mix_post, norm_ffn_pre, norm_ffn_post, w_in, conv_a, gate_up_fwd, gate_bias_fwd, gate_up_bwd, gate_bias_bwd, gla_head_norm, w_out, w_up, conv_ffn, w_down, loss_target, m_norm_mix_pre, m_norm_mix_post, m_norm_ffn_pre, m_norm_ffn_post, m_w_in, m_conv_a, m_gate_up_fwd, m_gate_bias_fwd, m_gate_up_bwd, m_gate_bias_bwd, m_gla_head_norm, m_w_out, m_w_up, m_conv_ffn, m_w_down, v_norm_mix_pre, v_norm_mix_post, v_norm_ffn_pre, v_norm_ffn_post, v_w_in, v_conv_a, v_gate_up_fwd, v_gate_bias_fwd, v_gate_up_bwd, v_gate_bias_bwd, v_gla_head_norm, v_w_out, v_w_up, v_conv_ffn, v_w_down):
    w = dict(norm_mix_pre=norm_mix_pre, norm_mix_post=norm_mix_post, norm_ffn_pre=norm_ffn_pre,
             norm_ffn_post=norm_ffn_post, w_in=w_in, conv_a=conv_a, gate_up_fwd=gate_up_fwd,
             gate_bias_fwd=gate_bias_fwd, gate_up_bwd=gate_up_bwd, gate_bias_bwd=gate_bias_bwd,
             gla_head_norm=gla_head_norm, w_out=w_out, w_up=w_up, conv_ffn=conv_ffn, w_down=w_down)
    m = dict(norm_mix_pre=m_norm_mix_pre, norm_mix_post=m_norm_mix_post, norm_ffn_pre=m_norm_ffn_pre,
             norm_ffn_post=m_norm_ffn_post, w_in=m_w_in, conv_a=m_conv_a, gate_up_fwd=m_gate_up_fwd,
             gate_bias_fwd=m_gate_bias_fwd, gate_up_bwd=m_gate_up_bwd, gate_bias_bwd=m_gate_bias_bwd,
             gla_head_norm=m_gla_head_norm, w_out=m_w_out, w_up=m_w_up, conv_ffn=m_conv_ffn, w_down=m_w_down)
    v = dict(norm_mix_pre=v_norm_mix_pre, norm_mix_post=v_norm_mix_post, norm_ffn_pre=v_norm_ffn_pre,
             norm_ffn_post=v_norm_ffn_post, w_in=v_w_in, conv_a=v_conv_a, gate_up_fwd=v_gate_up_fwd,
             gate_bias_fwd=v_gate_bias_fwd, gate_up_bwd=v_gate_up_bwd, gate_bias_bwd=v_gate_bias_bwd,
             gla_head_norm=v_gla_head_norm, w_out=v_w_out, w_up=v_w_up, conv_ffn=v_conv_ffn, w_down=v_w_down)
    axes = ("x", "y", "c")
    L = x.shape[1]
    x0 = x.reshape(L, D)
    target = loss_target.reshape(L, D)

    w_small = _pack_small(w)
    w16 = {n: w[n].astype(BF16) for n in BIG}
    gathered = _all_gather([w16[n][0:1] for n in BIG] + [w_small[None, :ROWS_SSH]], "gather_weights")
    wts = _small_weights(gathered[-1], w)
    big = _matmul_weights(*gathered[:-1])

    h1 = _norm_cast(x0, wts["g1"], 0, "norm_first")
    xl, saved = x0, []
    for l in range(DEPTH):
        nxt = [w16[n][l + 1:l + 2] for n in BIG] if l + 1 < DEPTH else None
        xl, h1, s, gathered = _layer_fwd(xl, h1, wts, big, l, min(l + 1, DEPTH - 1), nxt)
        saved.append(s)
        if nxt:
            big = _matmul_weights(*gathered)
    dx, sq = _loss_grad(xl, target, "loss_grad")
    loss = lax.psum(0.5 * jnp.sum(sq) / D, axes)

    c_idx = lax.axis_index("c").astype(jnp.int32).reshape(1)
    layer_grads, layer_parts, pending = [None] * DEPTH, [None] * DEPTH, None
    for l in reversed(range(DEPTH)):
        dx, layer_grads[l], done = _layer_bwd(dx, wts, saved[l], l, pending, c_idx)
        if pending:
            layer_parts[l + 1] = done
        pending = [layer_grads[l][n][None] for n in BIG]
    small_names = [n for n, _, _ in SMALL_SHARDED] + [n for n, _ in REPLICATED]
    stacked = {n: jnp.stack([g[n] for g in layer_grads]) for n in small_names}
    g_small = _pack_small_grads({n: jnp.moveaxis(stacked[n], 0, 1) for n, _, _ in SMALL_SHARDED}, stacked)
    last = pending + [g_small[None]]
    from_sibling = _exchange_sibling(last, "grads_to_sibling")
    pairs = [_pair_sum(g, r, c_idx, BF16 if i < len(BIG) else F32, "grads_pair_sum")
             for i, (g, r) in enumerate(zip(last, from_sibling))]
    parts = _exchange_chips(pairs, "grads_to_chips")
    layer_parts[0] = parts[:-1]

    results = {}
    for i, n in enumerate(BIG):
        part = jnp.concatenate([layer_parts[l][i] for l in range(DEPTH)], axis=0)
        results[n] = _sum_adamw(part, w[n], m[n], v[n], "sum_adamw")
    small = _sum_adamw(parts[-1], w_small[None], _pack_small(m)[None], _pack_small(v)[None], "sum_adamw_small")
    small = [_unpack_small(buf[0]) for buf in small]
    outs = [loss, dx.reshape(x.shape)]
    for i in range(4):
        outs += [results[n][i] if n in BIG else small[i][n] for n in WEIGHT_ORDER]
    return tuple(outs)
```

```python
import math
from typing import Callable, NamedTuple

import jax
import jax.numpy as jnp
from jax import lax
from jax.experimental import pallas as pl
from jax.experimental.pallas import tpu as pltpu

F32 = jnp.float32
BF16 = jnp.bfloat16

DEPTH = 4
D = 1024
DC = 512
DG = 512
HEADS = 4
HV = 128
HK = 64
DK = 256
RANK = 16
CH = 64
DFF = 2816
D_IN = 3104
D_INP = 3200
LR_BLK = 128
EPS = 1e-6
HB = 16
N_DEV = 8
N_CHIP = 4
LANES = 1024
SH_IN = D_IN // N_DEV
SH_FF = 2 * DFF // N_DEV
FF_HALF = N_DEV // 2

ADAM_LR, ADAM_B1, ADAM_B2, ADAM_EPS, ADAM_WD, ADAM_STEP = 0.001, 0.9, 0.999, 1e-08, 0.01, 10

VMEM_LIMIT = 48 * 1024 * 1024
TILE_TOKENS = 512
TILE_GLA = 512
TILE_FFN = 1024
TILE_MM = 2048
TILE_MM_KIN = 1024
TILE_MM_TOKENS = 2048

COL_GB, COL_GC, COL_GV = 0, 1, 2
COL_Q, COL_K = 6, 7
COL_V, COL_GO = 4, 5
COL_LR = 24

BIG = ("w_in", "w_out", "w_up", "w_down")
SMALL_SHARDED = (
    ("conv_a", (DEPTH, 3, DC // N_DEV), 2),
    ("gate_up_fwd", (DEPTH, RANK, DK // N_DEV), 2),
    ("gate_up_bwd", (DEPTH, RANK, DK // N_DEV), 2),
    ("conv_ffn", (DEPTH, 3, SH_FF), 2),
)
REPLICATED = (
    ("norm_mix_pre", (DEPTH, D)), ("norm_mix_post", (DEPTH, D)), ("norm_ffn_pre", (DEPTH, D)),
    ("norm_ffn_post", (DEPTH, D)), ("gate_bias_fwd", (DEPTH, DK)), ("gate_bias_bwd", (DEPTH, DK)),
    ("gla_head_norm", (DEPTH, HV)),
)
WEIGHT_ORDER = ("norm_mix_pre", "norm_mix_post", "norm_ffn_pre", "norm_ffn_post", "w_in", "conv_a", "gate_up_fwd",
                "gate_bias_fwd", "gate_up_bwd", "gate_bias_bwd", "gla_head_norm", "w_out", "w_up", "conv_ffn", "w_down")


def _rows_for(n_elems):
    return (-(-n_elems // LANES) + 7) // 8 * 8


ROWS_SSH = _rows_for(sum(math.prod(s) for _, s, _ in SMALL_SHARDED))
ROWS_REP = _rows_for(sum(math.prod(s) for _, s in REPLICATED))
ROWS_SMALL = ROWS_SSH + ROWS_REP


def _params(sem):
    return pltpu.CompilerParams(dimension_semantics=sem, vmem_limit_bytes=VMEM_LIMIT)


def _silu_parts(x):
    s = 1.0 / (1.0 + jnp.exp(-x))
    return x * s, s


def _rstd(xf):
    return lax.rsqrt(jnp.mean(xf * xf, axis=-1, keepdims=True) + EPS)


NN, NT, TN = ((1,), (0,)), ((1,), (1,)), ((0,), (0,))


def _dot(a, b, dims):
    return lax.dot_general(a, b, (dims, ((), ())), preferred_element_type=F32)


def _mm(a, b, *, dims, grid, a_spec, b_spec, o_spec, out_shape, tile, name, kin=0, comm=None):
    nk = grid[2]
    n_ci = len(comm.ins) if comm else 0
    n_co = len(comm.out_shapes) if comm else 0

    def body(*refs):
        a_ref, b_ref = refs[:2]
        ci = refs[2:2 + n_ci]
        o_ref = refs[2 + n_ci]
        co = refs[3 + n_ci:3 + n_ci + n_co]
        rest = refs[3 + n_ci + n_co:]
        if comm:
            sems, rest = rest[:3], rest[3:]
            step = (pl.program_id(0) * grid[1] + pl.program_id(1)) * grid[2] + pl.program_id(2)

            @pl.when(step == 0)
            def _():
                comm.start(ci, co, *sems)

        if kin:
            prod = _dot(a_ref[0], b_ref[0], dims)
            for d in range(1, kin):
                prod = prod + _dot(a_ref[d], b_ref[d], dims)
        else:
            prod = _dot(a_ref[...], b_ref[...], dims)
        if nk == 1:
            o_ref[...] = prod.astype(o_ref.dtype)
        else:
            acc_ref = rest[0]
            k = pl.program_id(2)

            @pl.when(k == 0)
            def _():
                acc_ref[...] = prod

            @pl.when(k > 0)
            def _():
                acc_ref[...] += prod

            @pl.when(k == nk - 1)
            def _():
                o_ref[...] = acc_ref[...].astype(o_ref.dtype)

        if comm:
            @pl.when(step == grid[0] * grid[1] * grid[2] - 1)
            def _():
                comm.wait(ci, co, *sems)

    acc = [pltpu.VMEM(tile, F32)] if nk > 1 else []
    if not comm:
        return pl.pallas_call(
            body, name=name, grid=grid, in_specs=[pl.BlockSpec(*a_spec), pl.BlockSpec(*b_spec)],
            out_specs=pl.BlockSpec(*o_spec), out_shape=out_shape, scratch_shapes=acc,
            compiler_params=_params(("parallel", "parallel", "arbitrary")),
        )(a, b)
    return pl.pallas_call(
        body, name=name, grid=grid,
        in_specs=[pl.BlockSpec(*a_spec), pl.BlockSpec(*b_spec)] + [HBM_SPEC] * n_ci,
        out_specs=[pl.BlockSpec(*o_spec)] + [HBM_SPEC] * n_co, out_shape=[out_shape] + list(comm.out_shapes),
        scratch_shapes=[pltpu.SemaphoreType.DMA((s,)) for s in comm.sems] + acc,
        input_output_aliases={2 + i: 1 + o for i, o in comm.aliases.items()},
        compiler_params=_params(("arbitrary", "arbitrary", "arbitrary")),
    )(a, b, *comm.ins)


def _halo_maps(tm, n_rows):
    r, last = tm // HB, n_rows // HB - 1
    return (lambda i: jnp.maximum(i * r - 1, 0)), (lambda i: jnp.minimum((i + 1) * r, last))


def _shift(x, prev_blk, next_blk):
    tm = x.shape[0]
    xs = jnp.concatenate([prev_blk, x, next_blk], axis=0)
    n = xs.shape[0]
    down = pltpu.roll(xs, 1, 0)[HB:HB + tm]
    up = pltpu.roll(xs, n - 1, 0)[HB:HB + tm]
    return down, up


def _edge_scales(i, n):
    return jnp.where(i > 0, 1.0, 0.0).astype(F32), jnp.where(i < n - 1, 1.0, 0.0).astype(F32)


def _gain_spec(l, width=D):
    return pl.BlockSpec((None, 1, width), lambda *_: (l, 0, 0))


def _norm_cast(x, g, l, name):
    L = x.shape[0]
    tm = min(TILE_TOKENS, L)

    def body(x_ref, g_ref, o_ref):
        xf = x_ref[...]
        o_ref[...] = (xf * _rstd(xf) * g_ref[...]).astype(BF16)

    return pl.pallas_call(
        body, name=name, grid=(L // tm,), in_specs=[pl.BlockSpec((tm, D), lambda i: (i, 0)), _gain_spec(l)],
        out_specs=pl.BlockSpec((tm, D), lambda i: (i, 0)), out_shape=jax.ShapeDtypeStruct((L, D), BF16),
        compiler_params=_params(("parallel",)),
    )(x, g)


def _post_pre(x, y, g_post, l_post, g_pre, l_pre, name):
    L = x.shape[0]
    tm = min(TILE_TOKENS, L)

    def body(x_ref, y_ref, gp_ref, gn_ref, x1_ref, h_ref):
        yf = y_ref[...].astype(F32)
        x1 = x_ref[...] + yf * _rstd(yf) * gp_ref[...]
        x1_ref[...] = x1
        h_ref[...] = (x1 * _rstd(x1) * gn_ref[...]).astype(BF16)

    tile = pl.BlockSpec((tm, D), lambda i: (i, 0))
    return pl.pallas_call(
        body, name=name, grid=(L // tm,), in_specs=[tile, tile, _gain_spec(l_post), _gain_spec(l_pre)],
        out_specs=[tile, tile],
        out_shape=[jax.ShapeDtypeStruct((L, D), F32), jax.ShapeDtypeStruct((L, D), BF16)],
        compiler_params=_params(("parallel",)),
    )(x, y, g_post, g_pre)


def _norm_bwd(yin, g, l, dout, dres, name):
    L = yin.shape[0]
    tm = min(TILE_TOKENS, L)
    with_res = dres is not None

    def body(*refs):
        if with_res:
            y_ref, g_ref, do_ref, dr_ref, din_ref, dg_ref = refs
        else:
            y_ref, g_ref, do_ref, din_ref, dg_ref = refs
        i = pl.program_id(0)
        y = y_ref[...].astype(F32)
        r = _rstd(y)
        do = do_ref[...].astype(F32)
        z = do * g_ref[...]
        din = r * z - y * (r * r * r) * jnp.mean(y * z, axis=-1, keepdims=True)
        if with_res:
            din = din + dr_ref[...]
        din_ref[...] = din.astype(din_ref.dtype)
        part = jnp.sum(do * y * r, axis=0, keepdims=True)

        @pl.when(i == 0)
        def _():
            dg_ref[...] = part

        @pl.when(i > 0)
        def _():
            dg_ref[...] += part

    tile = pl.BlockSpec((tm, D), lambda i: (i, 0))
    args = (yin, g, dout) + ((dres,) if with_res else ())
    return pl.pallas_call(
        body, name=name, grid=(L // tm,), in_specs=[tile, _gain_spec(l), tile] + ([tile] if with_res else []),
        out_specs=[tile, pl.BlockSpec((1, D), lambda i: (0, 0))],
        out_shape=[jax.ShapeDtypeStruct((L, D), F32 if with_res else BF16), jax.ShapeDtypeStruct((1, D), F32)],
        compiler_params=_params(("arbitrary",)),
    )(*args)


def _gla_consts(fwd, tb):
    row = lax.broadcasted_iota(jnp.int32, (tb, tb), 0)
    col = lax.broadcasted_iota(jnp.int32, (tb, tb), 1)
    same = (row // CH) == (col // CH)
    tri = same & ((col <= row) if fwd else (col >= row))
    tri_t = same & ((col >= row) if fwd else (col <= row))
    row_st = lax.broadcasted_iota(jnp.int32, (HEADS * CH, CH), 0) & (CH - 1)
    col_st = lax.broadcasted_iota(jnp.int32, (HEADS * CH, CH), 1)
    tri_st = (col_st <= row_st) if fwd else (col_st >= row_st)
    lane_head = lax.broadcasted_iota(jnp.int32, (1, DK), 1) // HK
    head_masks = [lane_head == h for h in range(HEADS)]
    srow = lax.broadcasted_iota(jnp.int32, (DG, DK), 0) // HV
    scol = lax.broadcasted_iota(jnp.int32, (DG, DK), 1) // HK
    return tri.astype(BF16), tri_t.astype(BF16), same.astype(BF16), tri_st, head_masks, srow == scol


def _dot_hilo(tri_b, x):
    hi = x.astype(BF16)
    lo = (x - hi.astype(F32)).astype(BF16)
    return _dot(tri_b, hi, NN) + _dot(tri_b, lo, NN)


def _gla_block_terms(q_ref, k_ref, lr_ref, gp_ref, bias_ref, tri_b, same_b):
    pre = _dot(lr_ref[...], gp_ref[...], NN) + bias_ref[...]
    sig_neg = 1.0 / (1.0 + jnp.exp(pre))
    a = (jnp.minimum(pre, 0.0) - jnp.log(1.0 + jnp.exp(-jnp.abs(pre)))) * (1.0 / 16.0)
    cum = _dot_hilo(tri_b, a)
    cl = _dot_hilo(same_b, a)
    e = jnp.exp(cum)
    einv = jnp.exp(-cum)
    eout = jnp.exp(cl - cum)
    q_in = q_ref[...].astype(F32) * e * (HK ** -0.5)
    k = k_ref[...].astype(F32)
    return dict(sig_neg=sig_neg, e=e, einv=einv, eout=eout, decay=jnp.exp(cl), q_in=q_in, k_in=k * einv,
                k_out=k * eout)


def _gla_chunk(t, c, head_masks):
    sl = slice(c * CH, (c + 1) * CH)
    tc = {n: x[sl] for n, x in t.items() if n != "decay"}
    tc["decay"] = jnp.max(t["decay"][c * CH:c * CH + 8], axis=0, keepdims=True)
    tc["q_st"] = jnp.concatenate([jnp.where(mh, tc["q_in"], 0.0) for mh in head_masks], axis=0).astype(BF16)
    return tc


def _gate_specs(l):
    return [pl.BlockSpec((None, LR_BLK, DK), lambda i: (l, 0, 0)), pl.BlockSpec((None, 1, DK), lambda i: (l, 0, 0))]


def _gla_fwd(p, gpad, bias, l, o_prev, fwd, name):
    L = p.shape[0]
    tb = min(TILE_GLA, L)
    nb, ncb, nch = L // tb, tb // CH, L // CH
    blk = (lambda i: i) if fwd else (lambda i: nb - 1 - i)
    with_prev = o_prev is not None

    def body(*refs):
        if with_prev:
            q_ref, k_ref, v_ref, lr_ref, gp_ref, bias_ref, op_ref, o_ref, sp_ref, s_ref = refs
        else:
            q_ref, k_ref, v_ref, lr_ref, gp_ref, bias_ref, o_ref, sp_ref, s_ref = refs
        i = pl.program_id(0)

        @pl.when(i == 0)
        def _():
            s_ref[...] = jnp.zeros_like(s_ref)

        tri_b, _, same_b, tri_st, head_masks, blockmask = _gla_consts(fwd, tb)
        terms = _gla_block_terms(q_ref, k_ref, lr_ref, gp_ref, bias_ref, tri_b, same_b)
        for c in (range(ncb) if fwd else reversed(range(ncb))):
            rows = pl.ds(c * CH, CH)
            t = _gla_chunk(terms, c, head_masks)
            v = v_ref[rows, :]
            scores = _dot(t["q_st"], t["k_in"].astype(BF16), NT)
            a_st = jnp.where(tri_st, scores, 0.0).astype(BF16)
            r = _dot(a_st, v, NN)
            o_intra = jnp.concatenate([r[h * CH:(h + 1) * CH, h * HV:(h + 1) * HV] for h in range(HEADS)], axis=1)
            s_b = s_ref[...].astype(BF16)
            sp_ref[c] = s_b
            o = o_intra + _dot(t["q_in"].astype(BF16), s_b, NT)
            if with_prev:
                o = o + op_ref[rows, :]
            o_ref[rows, :] = o
            kv_t = _dot(v, t["k_out"].astype(BF16), TN)
            s_ref[...] = s_ref[...] * t["decay"] + jnp.where(blockmask, kv_t, 0.0)

    def col(width, c):
        return pl.BlockSpec((tb, width), lambda i: (blk(i), c))

    in_specs = [col(DK, COL_Q), col(DK, COL_K), col(DG, COL_V), col(LR_BLK, COL_LR)] + _gate_specs(l)
    args = [p, p, p, p, gpad, bias]
    if with_prev:
        in_specs.append(pl.BlockSpec((tb, DG), lambda i: (blk(i), 0)))
        args.append(o_prev)
    return pl.pallas_call(
        body, name=name, grid=(nb,), in_specs=in_specs,
        out_specs=[pl.BlockSpec((tb, DG), lambda i: (blk(i), 0)), pl.BlockSpec((ncb, DG, DK), lambda i: (blk(i), 0, 0))],
        out_shape=[jax.ShapeDtypeStruct((L, DG), F32), jax.ShapeDtypeStruct((nch, DG, DK), BF16)],
        scratch_shapes=[pltpu.VMEM((DG, DK), F32)],
        compiler_params=_params(("arbitrary",)),
    )(*args)


def _gla_bwd(p, gpad, bias, l, sprev, d_o, prev, fwd, name):
    L = p.shape[0]
    tb = min(TILE_GLA, L)
    nb, ncb = L // tb, tb // CH
    blk = (lambda i: nb - 1 - i) if fwd else (lambda i: i)
    with_prev = prev is not None

    def body(*refs):
        q_ref, k_ref, v_ref, lr_ref, gp_ref, bias_ref, sp_ref, do_ref = refs[:8]
        rest = refs[8:]
        if with_prev:
            pq_ref, pk_ref, pv_ref, plr_ref = rest[:4]
            rest = rest[4:]
        dq_ref, dk_ref, dv_ref, dlr_ref, dg_ref, db_ref, ds_ref = rest
        i = pl.program_id(0)

        @pl.when(i == 0)
        def _():
            ds_ref[...] = jnp.zeros_like(ds_ref)
            dg_ref[...] = jnp.zeros_like(dg_ref)
            db_ref[...] = jnp.zeros_like(db_ref)

        tri_b, tri_t_b, same_b, tri_st, head_masks, blockmask = _gla_consts(fwd, tb)
        terms = _gla_block_terms(q_ref, k_ref, lr_ref, gp_ref, bias_ref, tri_b, same_b)
        dcum_of, dcl_of = [None] * ncb, [None] * ncb
        for c in (reversed(range(ncb)) if fwd else range(ncb)):
            rows = pl.ds(c * CH, CH)
            t = _gla_chunk(terms, c, head_masks)
            v = v_ref[rows, :]
            do = do_ref[rows, :]
            q_in, k_in, k_out = t["q_in"], t["k_in"], t["k_out"]
            q_b, k_in_b, k_out_b = q_in.astype(BF16), k_in.astype(BF16), k_out.astype(BF16)
            scores = _dot(t["q_st"], k_in_b, NT)
            a_st = jnp.where(tri_st, scores, 0.0).astype(BF16)
            s_prev = sp_ref[c]
            ds = ds_ref[...]
            ds_b = ds.astype(BF16)

            da_heads = [_dot(do[:, h * HV:(h + 1) * HV], v[:, h * HV:(h + 1) * HV], NT) for h in range(HEADS)]
            da_st = jnp.where(tri_st, jnp.concatenate(da_heads, axis=0), 0.0).astype(BF16)

            dv_heads = [_dot(a_st[h * CH:(h + 1) * CH, :], do[:, h * HV:(h + 1) * HV], TN) for h in range(HEADS)]
            dv = jnp.concatenate(dv_heads, axis=1) + _dot(k_out_b, ds_b, NT)

            x = _dot(da_st, k_in_b, NN)
            dq_in = _dot(do, s_prev, NN)
            for h in range(HEADS):
                dq_in = dq_in + jnp.where(head_masks[h], x[h * CH:(h + 1) * CH, :], 0.0)
            dk_in = _dot(da_st, t["q_st"], TN)
            dk_out = _dot(v, ds_b, NN)
            d_decay = jnp.sum(ds * s_prev.astype(F32), axis=0, keepdims=True)
            ds_ref[...] = ds * t["decay"] + jnp.where(blockmask, _dot(do, q_b, TN), 0.0)

            dq = dq_in * t["e"] * (HK ** -0.5)
            dk = dk_in * t["einv"] + dk_out * t["eout"]
            dko_ko = dk_out * k_out
            dcum_of[c] = dq_in * q_in - dk_in * k_in - dko_ko
            dcl = jnp.sum(dko_ko, axis=0, keepdims=True) + d_decay * t["decay"]
            dcl_of[c] = jnp.broadcast_to(dcl, (CH, DK))
            if with_prev:
                dq = dq + pq_ref[rows, :].astype(F32)
                dk = dk + pk_ref[rows, :].astype(F32)
                dv = dv + pv_ref[rows, :].astype(F32)
            dq_ref[rows, :] = dq.astype(BF16)
            dk_ref[rows, :] = dk.astype(BF16)
            dv_ref[rows, :] = dv.astype(BF16)

        da = _dot_hilo(tri_t_b, jnp.concatenate(dcum_of, axis=0)) + jnp.concatenate(dcl_of, axis=0)
        dpre = da * terms["sig_neg"] * (1.0 / 16.0)
        dpre_b = dpre.astype(BF16)
        dlr = _dot(dpre_b, gp_ref[...], NT)
        dg_ref[...] += _dot(lr_ref[...], dpre_b, TN)
        db_ref[...] += jnp.sum(dpre, axis=0, keepdims=True)
        if with_prev:
            dlr = dlr + plr_ref[...].astype(F32)
        dlr_ref[...] = dlr.astype(BF16)

    def col(width, c):
        return pl.BlockSpec((tb, width), lambda i: (blk(i), c))

    in_specs = [col(DK, COL_Q), col(DK, COL_K), col(DG, COL_V), col(LR_BLK, COL_LR)] + _gate_specs(l) + [
        pl.BlockSpec((ncb, DG, DK), lambda i: (blk(i), 0, 0)), col(DG, 0)]
    args = [p, p, p, p, gpad, bias, sprev, d_o]
    tiles = [col(DK, 0), col(DK, 0), col(DG, 0), col(LR_BLK, 0)]
    if with_prev:
        in_specs += tiles
        args += list(prev)
    return pl.pallas_call(
        body, name=name, grid=(nb,), in_specs=in_specs,
        out_specs=tiles + [pl.BlockSpec((LR_BLK, DK), lambda i: (0, 0)), pl.BlockSpec((1, DK), lambda i: (0, 0))],
        out_shape=[jax.ShapeDtypeStruct((L, DK), BF16), jax.ShapeDtypeStruct((L, DK), BF16),
                   jax.ShapeDtypeStruct((L, DG), BF16), jax.ShapeDtypeStruct((L, LR_BLK), BF16),
                   jax.ShapeDtypeStruct((LR_BLK, DK), F32), jax.ShapeDtypeStruct((1, DK), F32)],
        scratch_shapes=[pltpu.VMEM((DG, DK), F32)],
        compiler_params=_params(("arbitrary",)),
    )(*args)


def _mixer_out(p, conv_a, gh, l, o_tot, name):
    L = p.shape[0]
    tm = min(TILE_TOKENS, L)
    n = L // tm
    pmap, nmap = _halo_maps(tm, L)

    def body(gb_ref, gc_ref, gcp_ref, gcn_ref, gv_ref, gvp_ref, gvn_ref, go_ref, cw_ref, o_ref, gh_ref, y_ref):
        ps, ns = _edge_scales(pl.program_id(0), n)
        z = gc_ref[...].astype(F32) * gv_ref[...].astype(F32)
        zp = gcp_ref[...].astype(F32) * gvp_ref[...].astype(F32) * ps
        zn = gcn_ref[...].astype(F32) * gvn_ref[...].astype(F32) * ns
        z_dn, z_up = _shift(z, zp, zn)
        conv = cw_ref[0:1, :] * z_dn + cw_ref[1:2, :] * z + cw_ref[2:3, :] * z_up
        y_ref[:, 0:DC] = (gb_ref[...].astype(F32) * conv).astype(BF16)
        o = o_ref[...]
        go = go_ref[...].astype(F32)
        for h in range(HEADS):
            oh = o[:, h * HV:(h + 1) * HV]
            on = oh * _rstd(oh) * gh_ref[...]
            act, _ = _silu_parts(go[:, h * HV:(h + 1) * HV])
            y_ref[:, DC + h * HV:DC + (h + 1) * HV] = (act * on).astype(BF16)

    def main(c):
        return pl.BlockSpec((tm, DC), lambda i: (i, c))

    def halo(c, imap):
        return pl.BlockSpec((HB, DC), lambda i: (imap(i), c))

    return pl.pallas_call(
        body, name=name, grid=(n,),
        in_specs=[main(COL_GB), main(COL_GC), halo(COL_GC, pmap), halo(COL_GC, nmap),
                  main(COL_GV), halo(COL_GV, pmap), halo(COL_GV, nmap), main(COL_GO),
                  pl.BlockSpec((None, 3, DC), lambda i: (l, 0, 0)), pl.BlockSpec((tm, DG), lambda i: (i, 0)),
                  _gain_spec(l, HV)],
        out_specs=pl.BlockSpec((tm, D), lambda i: (i, 0)), out_shape=jax.ShapeDtypeStruct((L, D), BF16),
        compiler_params=_params(("parallel",)),
    )(p, p, p, p, p, p, p, p, conv_a, o_tot, gh)


def _mixer_out_bwd(p, conv_a, gh, l, o_tot, dy, name):
    L = p.shape[0]
    tm = min(TILE_TOKENS, L)
    n = L // tm
    pmap, nmap = _halo_maps(tm, L)

    def body(gb_ref, gbp_ref, gbn_ref, gc_ref, gcp_ref, gcn_ref, gv_ref, gvp_ref, gvn_ref, go_ref, cw_ref, o_ref,
             gh_ref, dy_ref, dyp_ref, dyn_ref, dgb_ref, dgc_ref, dgv_ref, dgo_ref, do_ref, dcw_ref, dgh_ref):
        i = pl.program_id(0)
        ps, ns = _edge_scales(i, n)
        gb = gb_ref[...].astype(F32)
        gc = gc_ref[...].astype(F32)
        gv = gv_ref[...].astype(F32)
        z = gc * gv
        zp = gcp_ref[...].astype(F32) * gvp_ref[...].astype(F32) * ps
        zn = gcn_ref[...].astype(F32) * gvn_ref[...].astype(F32) * ns
        z_dn, z_up = _shift(z, zp, zn)
        w0, w1, w2 = cw_ref[0:1, :], cw_ref[1:2, :], cw_ref[2:3, :]
        conv = w0 * z_dn + w1 * z + w2 * z_up
        dya = dy_ref[:, 0:DC].astype(F32)
        dgb_ref[...] = (dya * conv).astype(BF16)
        dc = dya * gb
        dcp = dyp_ref[...].astype(F32) * gbp_ref[...].astype(F32) * ps
        dcn = dyn_ref[...].astype(F32) * gbn_ref[...].astype(F32) * ns
        dc_dn, dc_up = _shift(dc, dcp, dcn)
        dz = w0 * dc_up + w1 * dc + w2 * dc_dn
        dgc_ref[...] = (dz * gv).astype(BF16)
        dgv_ref[...] = (dz * gc).astype(BF16)
        dcw = [jnp.sum(zs * dc, axis=0, keepdims=True) for zs in (z_dn, z, z_up)]

        o = o_ref[...]
        go = go_ref[...].astype(F32)
        dgh = jnp.zeros((1, HV), F32)
        for h in range(HEADS):
            sl = slice(h * HV, (h + 1) * HV)
            oh = o[:, sl]
            r = _rstd(oh)
            act, sg = _silu_parts(go[:, sl])
            dyb = dy_ref[:, DC + h * HV:DC + (h + 1) * HV].astype(F32)
            on = oh * r * gh_ref[...]
            dgo_ref[:, sl] = (dyb * on * (sg + act * (1.0 - sg))).astype(BF16)
            don = dyb * act
            zz = don * gh_ref[...]
            do_ref[:, sl] = (r * zz - oh * (r * r * r) * jnp.mean(oh * zz, axis=-1, keepdims=True)).astype(BF16)
            dgh = dgh + jnp.sum(don * oh * r, axis=0, keepdims=True)

        @pl.when(i == 0)
        def _():
            dcw_ref[...] = jnp.zeros_like(dcw_ref)
            dgh_ref[...] = jnp.zeros_like(dgh_ref)

        for kk in range(3):
            dcw_ref[kk:kk + 1, :] += dcw[kk]
        dgh_ref[...] += dgh

    def main(c):
        return pl.BlockSpec((tm, DC), lambda i: (i, c))

    def halo(c, imap):
        return pl.BlockSpec((HB, DC), lambda i: (imap(i), c))

    tile = pl.BlockSpec((tm, DC), lambda i: (i, 0))
    return pl.pallas_call(
        body, name=name, grid=(n,),
        in_specs=[main(COL_GB), halo(COL_GB, pmap), halo(COL_GB, nmap), main(COL_GC), halo(COL_GC, pmap),
                  halo(COL_GC, nmap), main(COL_GV), halo(COL_GV, pmap), halo(COL_GV, nmap), main(COL_GO),
                  pl.BlockSpec((None, 3, DC), lambda i: (l, 0, 0)), tile, _gain_spec(l, HV),
                  pl.BlockSpec((tm, D), lambda i: (i, 0)), halo(0, pmap), halo(0, nmap)],
        out_specs=[tile, tile, tile, tile, tile, pl.BlockSpec((3, DC), lambda i: (0, 0)),
                   pl.BlockSpec((1, HV), lambda i: (0, 0))],
        out_shape=[jax.ShapeDtypeStruct((L, DC), BF16)] * 5
        + [jax.ShapeDtypeStruct((3, DC), F32), jax.ShapeDtypeStruct((1, HV), F32)],
        compiler_params=_params(("arbitrary",)),
    )(p, p, p, p, p, p, p, p, p, p, conv_a, o_tot, gh, dy, dy, dy)


def _ffn_specs(tm, L, l, row_axis, sh_axis):
    pmap, nmap = _halo_maps(tm, L)

    def u(off, imap=None, rows=tm):
        if imap is None:
            return pl.BlockSpec((None, rows, SH_FF), lambda *g: (g[sh_axis] + off, g[row_axis], 0))
        return pl.BlockSpec((None, rows, SH_FF), lambda *g: (g[sh_axis] + off, imap(g[row_axis]), 0))

    def cw(off):
        return pl.BlockSpec((None, None, 3, SH_FF), lambda *g: (l, g[sh_axis] + off, 0, 0))

    u_specs = [u(0), u(0, pmap, HB), u(0, nmap, HB), u(FF_HALF), u(FF_HALF, pmap, HB), u(FF_HALF, nmap, HB)]
    return u_specs, [cw(0), cw(FF_HALF)]


def _conv3(x_ref, xp_ref, xn_ref, cw_ref, ps, ns):
    x = x_ref[...].astype(F32)
    x_dn, x_up = _shift(x, xp_ref[...].astype(F32) * ps, xn_ref[...].astype(F32) * ns)
    return cw_ref[0:1, :] * x_dn + cw_ref[1:2, :] * x + cw_ref[2:3, :] * x_up, (x_dn, x, x_up)


def _ffn_act(u8, cw, l, name):
    L = u8.shape[1]
    tm = min(TILE_FFN, L)
    n = L // tm
    u_specs, cw_specs = _ffn_specs(tm, L, l, 0, 1)

    def body(g_ref, gp_ref, gn_ref, v_ref, vp_ref, vn_ref, cwg_ref, cwv_ref, a_ref):
        ps, ns = _edge_scales(pl.program_id(0), n)
        gate, _ = _conv3(g_ref, gp_ref, gn_ref, cwg_ref, ps, ns)
        val, _ = _conv3(v_ref, vp_ref, vn_ref, cwv_ref, ps, ns)
        act, _ = _silu_parts(gate)
        a_ref[...] = (act * val).astype(BF16)

    return pl.pallas_call(
        body, name=name, grid=(n, FF_HALF), in_specs=u_specs + cw_specs,
        out_specs=pl.BlockSpec((None, tm, SH_FF), lambda i, d: (d, i, 0)),
        out_shape=jax.ShapeDtypeStruct((FF_HALF, L, SH_FF), BF16),
        compiler_params=_params(("parallel", "parallel")),
    )(u8, u8, u8, u8, u8, u8, cw, cw)


def _ffn_act_bwd(u8, cw, l, da, name):
    L = u8.shape[1]
    tm = min(TILE_FFN, L)
    n = L // tm
    u_specs, cw_specs = _ffn_specs(tm, L, l, 1, 0)

    def body(g_ref, gp_ref, gn_ref, v_ref, vp_ref, vn_ref, cwg_ref, cwv_ref, da_ref, du_ref, dcw_ref):
        i = pl.program_id(1)
        ps, ns = _edge_scales(i, n)
        gate, g_sh = _conv3(g_ref, gp_ref, gn_ref, cwg_ref, ps, ns)
        val, v_sh = _conv3(v_ref, vp_ref, vn_ref, cwv_ref, ps, ns)
        act, sg = _silu_parts(gate)
        da_f = da_ref[...].astype(F32)
        dgate = da_f * val * (sg + act * (1.0 - sg))
        dval = da_f * act
        du_ref[0] = dgate.astype(BF16)
        du_ref[1] = dval.astype(BF16)

        @pl.when(i == 0)
        def _():
            dcw_ref[...] = jnp.zeros_like(dcw_ref)

        for kk in range(3):
            dcw_ref[0, kk:kk + 1, :] += jnp.sum(g_sh[kk] * dgate, axis=0, keepdims=True)
            dcw_ref[1, kk:kk + 1, :] += jnp.sum(v_sh[kk] * dval, axis=0, keepdims=True)

    return pl.pallas_call(
        body, name=name, grid=(FF_HALF, n),
        in_specs=u_specs + cw_specs + [pl.BlockSpec((None, tm, SH_FF), lambda d, i: (d, i, 0))],
        out_specs=[pl.BlockSpec((2, None, tm, SH_FF), lambda d, i: (0, d, i, 0)),
                   pl.BlockSpec((2, None, 3, SH_FF), lambda d, i: (0, d, 0, 0))],
        out_shape=[jax.ShapeDtypeStruct((2, FF_HALF, L, SH_FF), BF16),
                   jax.ShapeDtypeStruct((2, FF_HALF, 3, SH_FF), F32)],
        compiler_params=_params(("parallel", "arbitrary")),
    )(u8, u8, u8, u8, u8, u8, cw, cw, da)


def _ffn_conv_t(du8, cw, l, name):
    L = du8.shape[1]
    tm = min(TILE_FFN, L)
    n = L // tm
    pmap, nmap = _halo_maps(tm, L)

    def body(x_ref, xp_ref, xn_ref, cw_ref, o_ref):
        ps, ns = _edge_scales(pl.program_id(1), n)
        x = x_ref[...].astype(F32)
        x_dn, x_up = _shift(x, xp_ref[...].astype(F32) * ps, xn_ref[...].astype(F32) * ns)
        o_ref[...] = (cw_ref[0:1, :] * x_up + cw_ref[1:2, :] * x + cw_ref[2:3, :] * x_dn).astype(BF16)

    return pl.pallas_call(
        body, name=name, grid=(N_DEV, n),
        in_specs=[pl.BlockSpec((None, tm, SH_FF), lambda d, i: (d, i, 0)),
                  pl.BlockSpec((None, HB, SH_FF), lambda d, i: (d, pmap(i), 0)),
                  pl.BlockSpec((None, HB, SH_FF), lambda d, i: (d, nmap(i), 0)),
                  pl.BlockSpec((None, None, 3, SH_FF), lambda d, i: (l, d, 0, 0))],
        out_specs=pl.BlockSpec((None, tm, SH_FF), lambda d, i: (d, i, 0)),
        out_shape=jax.ShapeDtypeStruct((N_DEV, L, SH_FF), BF16),
        compiler_params=_params(("parallel", "parallel")),
    )(du8, du8, du8, cw)


def _loss_grad(xl, target, name):
    L = xl.shape[0]
    tm = min(TILE_TOKENS, L)

    def body(x_ref, t_ref, dx_ref, sq_ref):
        i = pl.program_id(0)
        err = x_ref[...] - t_ref[...]
        dx_ref[...] = err * (1.0 / D)
        part = jnp.sum(err * err, axis=0, keepdims=True)

        @pl.when(i == 0)
        def _():
            sq_ref[...] = part

        @pl.when(i > 0)
        def _():
            sq_ref[...] += part

    tile = pl.BlockSpec((tm, D), lambda i: (i, 0))
    return pl.pallas_call(
        body, name=name, grid=(L // tm,), in_specs=[tile, tile],
        out_specs=[tile, pl.BlockSpec((1, D), lambda i: (0, 0))],
        out_shape=[jax.ShapeDtypeStruct((L, D), F32), jax.ShapeDtypeStruct((1, D), F32)],
        compiler_params=_params(("arbitrary",)),
    )(xl, target)


MESH = pl.DeviceIdType.MESH
HBM_SPEC = pl.BlockSpec(memory_space=pltpu.HBM)


def _position():
    return lax.axis_index("x"), lax.axis_index("y"), lax.axis_index("c")


def _other_chips(x, y):
    return [(1 - x, y), (x, 1 - y), (1 - x, 1 - y)]


def _all_gather(shards, name):
    n = len(shards)

    def body(*refs):
        x_refs, out_refs = refs[:n], refs[n:2 * n]
        send_sems, recv_sems, local_sems = refs[2 * n:]
        x, y, c = _position()
        me, sibling = (x, y, c), (x, y, 1 - c)
        chips = _other_chips(x, y)

        def slot(t, px, py, pc):
            return out_refs[t].at[:, 4 * px + 2 * py + pc]

        def copy(t, k, block, to, from_input=False):
            return pltpu.make_async_remote_copy(
                src_ref=x_refs[t] if from_input else slot(t, *block), dst_ref=slot(t, *block),
                send_sem=send_sems.at[k * n + t], recv_sem=recv_sems.at[k * n + t], device_id=to, device_id_type=MESH)

        mine = [pltpu.make_async_copy(x_refs[t], slot(t, *me), local_sems.at[t]) for t in range(n)]
        for cp in mine:
            cp.start()
        first = [copy(t, 0, me, sibling, True) for t in range(n)]
        first += [copy(t, 1 + j, me, (*chip, c), True) for j, chip in enumerate(chips) for t in range(n)]
        for cp in first:
            cp.start()
        passed = []
        for j, chip in enumerate(chips):
            for t in range(n):
                copy(t, 1 + j, (*chip, c), me).wait_recv()
                passed.append(copy(t, 4 + j, (*chip, c), sibling))
                passed[-1].start()
        for t in range(n):
            copy(t, 0, sibling, me).wait_recv()
        for j, chip in enumerate(chips):
            for t in range(n):
                copy(t, 4 + j, (*chip, 1 - c), me).wait_recv()
        for cp in first + passed:
            cp.wait_send()
        for cp in mine:
            cp.wait()

    return pl.pallas_call(
        body, name=name,
        out_shape=[jax.ShapeDtypeStruct((s.shape[0], N_DEV) + s.shape[1:], s.dtype) for s in shards],
        in_specs=[HBM_SPEC] * n, out_specs=[HBM_SPEC] * n,
        scratch_shapes=[pltpu.SemaphoreType.DMA((7 * n,)), pltpu.SemaphoreType.DMA((7 * n,)),
                        pltpu.SemaphoreType.DMA((n,))],
    )(*shards)


def _exchange_sibling(grads, name):
    n = len(grads)

    def body(*refs):
        g_refs, out_refs, send_sems, recv_sems = refs[:n], refs[n:2 * n], refs[2 * n], refs[2 * n + 1]
        x, y, c = _position()
        copies = [pltpu.make_async_remote_copy(
            src_ref=g_refs[t].at[:, 2 * k + (1 - c)], dst_ref=out_refs[t].at[:, k], send_sem=send_sems.at[k * n + t],
            recv_sem=recv_sems.at[k * n + t], device_id=(x, y, 1 - c), device_id_type=MESH)
            for k in range(N_CHIP) for t in range(n)]
        for cp in copies:
            cp.start()
        for cp in copies:
            cp.wait()

    return pl.pallas_call(
        body, name=name,
        out_shape=[jax.ShapeDtypeStruct((g.shape[0], N_CHIP) + g.shape[2:], g.dtype) for g in grads],
        in_specs=[HBM_SPEC] * n, out_specs=[HBM_SPEC] * n,
        scratch_shapes=[pltpu.SemaphoreType.DMA((N_CHIP * n,)), pltpu.SemaphoreType.DMA((N_CHIP * n,))],
    )(*grads)


class _Comm(NamedTuple):
    ins: tuple
    out_shapes: tuple
    aliases: dict
    sems: tuple
    start: Callable
    wait: Callable


def _comm_of(ins, out_shapes, aliases, sems, copies):
    def start(ci, co, send, recv, local):
        for cp in copies(ci, co, send, recv, local):
            cp.start()

    def wait(ci, co, send, recv, local):
        for cp in copies(ci, co, send, recv, local):
            cp.wait()

    return _Comm(tuple(ins), tuple(out_shapes), aliases, sems, start, wait)


def _remote(src, dst, send, recv, idx, to):
    return pltpu.make_async_remote_copy(src_ref=src, dst_ref=dst, send_sem=send.at[idx], recv_sem=recv.at[idx],
                                        device_id=to, device_id_type=MESH)


def _gather_ici_comm(shards):
    n = len(shards)

    def copies(ci, co, send, recv, local):
        x, y, c = _position()
        me = 4 * x + 2 * y + c
        mine = [pltpu.make_async_copy(ci[t], co[t].at[:, me], local.at[t]) for t in range(n)]
        return mine + [_remote(ci[t], co[t].at[:, me], send, recv, j * n + t, (cx, cy, c))
                       for j, (cx, cy) in enumerate(_other_chips(x, y)) for t in range(n)]

    outs = [jax.ShapeDtypeStruct((1, N_DEV) + s.shape[1:], s.dtype) for s in shards]
    return _comm_of(shards, outs, {}, (3 * n, 3 * n, n), copies)


def _gather_d2d_comm(partials):
    n = len(partials)

    def copies(ci, co, send, recv, local):
        x, y, c = _position()
        return [_remote(co[t].at[:, 4 * cx + 2 * cy + c], co[t].at[:, 4 * cx + 2 * cy + c], send, recv, k * n + t,
                        (x, y, 1 - c))
                for k, (cx, cy) in enumerate([(x, y)] + _other_chips(x, y)) for t in range(n)]

    outs = [jax.ShapeDtypeStruct(p.shape, p.dtype) for p in partials]
    return _comm_of(partials, outs, {t: t for t in range(n)}, (N_CHIP * n, N_CHIP * n, 1), copies)


def _grads_d2d_comm(grads):
    n = len(grads)

    def copies(ci, co, send, recv, local):
        x, y, c = _position()
        return [_remote(ci[t].at[:, 2 * k + (1 - c)], co[t].at[:, k], send, recv, k * n + t, (x, y, 1 - c))
                for k in range(N_CHIP) for t in range(n)]

    outs = [jax.ShapeDtypeStruct((g.shape[0], N_CHIP) + g.shape[2:], g.dtype) for g in grads]
    return _comm_of(grads, outs, {}, (N_CHIP * n, N_CHIP * n, 1), copies)


def _grads_ici_comm(parts):
    n = len(parts)

    def copies(ci, co, send, recv, local):
        x, y, c = _position()
        my_chip = 2 * x + y
        mine = [pltpu.make_async_copy(ci[t].at[:, my_chip], co[t].at[:, my_chip], local.at[t]) for t in range(n)]
        return mine + [_remote(ci[t].at[:, 2 * cx + cy], co[t].at[:, my_chip], send, recv, j * n + t, (cx, cy, c))
                       for j, (cx, cy) in enumerate(_other_chips(x, y)) for t in range(n)]

    outs = [jax.ShapeDtypeStruct(p.shape, p.dtype) for p in parts]
    return _comm_of(parts, outs, {}, (3 * n, 3 * n, n), copies)


def _row_tile(rows):
    return 256 if rows % 256 == 0 else rows


def _pair_sum(g, recv, c_idx, out_dtype, name):
    lay, _, rows, cols = g.shape
    tr = _row_tile(rows)

    def body(c_ref, g_ref, r_ref, o_ref):
        o_ref[...] = (g_ref[...] + r_ref[...]).astype(o_ref.dtype)

    def spec(blk_of):
        return pl.BlockSpec((None, None, tr, cols), lambda l, k, r, c_ref: (l, blk_of(k, c_ref), r, 0))

    return pl.pallas_call(
        body, name=name,
        grid_spec=pltpu.PrefetchScalarGridSpec(
            num_scalar_prefetch=1, grid=(lay, N_CHIP, rows // tr),
            in_specs=[spec(lambda k, c_ref: 2 * k + c_ref[0]), spec(lambda k, c_ref: k)],
            out_specs=spec(lambda k, c_ref: k)),
        out_shape=jax.ShapeDtypeStruct((lay, N_CHIP, rows, cols), out_dtype),
        compiler_params=_params(("parallel", "parallel", "parallel")),
    )(c_idx, g, recv)


def _exchange_chips(parts, name):
    n = len(parts)

    def body(*refs):
        p_refs, out_refs = refs[:n], refs[n:2 * n]
        send_sems, recv_sems, local_sems = refs[2 * n:]
        x, y, c = _position()
        my_chip = 2 * x + y
        mine = [pltpu.make_async_copy(p_refs[t].at[:, my_chip], out_refs[t].at[:, my_chip], local_sems.at[t])
                for t in range(n)]
        for cp in mine:
            cp.start()
        copies = [pltpu.make_async_remote_copy(
            src_ref=p_refs[t].at[:, 2 * cx + cy], dst_ref=out_refs[t].at[:, my_chip], send_sem=send_sems.at[j * n + t],
            recv_sem=recv_sems.at[j * n + t], device_id=(cx, cy, c), device_id_type=MESH)
            for j, (cx, cy) in enumerate(_other_chips(x, y)) for t in range(n)]
        for cp in copies:
            cp.start()
        for cp in copies:
            cp.wait()
        for cp in mine:
            cp.wait()

    return pl.pallas_call(
        body, name=name, out_shape=[jax.ShapeDtypeStruct(p.shape, p.dtype) for p in parts],
        in_specs=[HBM_SPEC] * n, out_specs=[HBM_SPEC] * n,
        scratch_shapes=[pltpu.SemaphoreType.DMA((3 * n,)), pltpu.SemaphoreType.DMA((3 * n,)),
                        pltpu.SemaphoreType.DMA((n,))],
    )(*parts)


def _sum_adamw(parts, w, m, v, name):
    lay, rows, cols = w.shape
    tr = _row_tile(rows)

    def body(p_ref, w_ref, m_ref, v_ref, g_ref, d_ref, nm_ref, nv_ref):
        g = ((p_ref[0].astype(F32) + p_ref[1].astype(F32)) + p_ref[2].astype(F32)) + p_ref[3].astype(F32)
        g_ref[...] = g
        nm = ADAM_B1 * m_ref[...] + (1.0 - ADAM_B1) * g
        nv = ADAM_B2 * v_ref[...] + (1.0 - ADAM_B2) * (g * g)
        nm_ref[...] = nm
        nv_ref[...] = nv
        m_hat = nm / (1.0 - ADAM_B1 ** ADAM_STEP)
        v_hat = nv / (1.0 - ADAM_B2 ** ADAM_STEP)
        d_ref[...] = -ADAM_LR * (m_hat / (jnp.sqrt(v_hat) + ADAM_EPS) + ADAM_WD * w_ref[...])

    tile = pl.BlockSpec((None, tr, cols), lambda l, r: (l, r, 0))
    return pl.pallas_call(
        body, name=name, grid=(lay, rows // tr),
        in_specs=[pl.BlockSpec((None, N_CHIP, tr, cols), lambda l, r: (l, 0, r, 0)), tile, tile, tile],
        out_specs=[tile] * 4, out_shape=[jax.ShapeDtypeStruct((lay, rows, cols), F32)] * 4,
        compiler_params=_params(("parallel", "parallel")),
    )(parts, w, m, v)


def _pad_rows(flat, rows):
    return jnp.pad(flat, (0, rows * LANES - flat.shape[0])).reshape(rows, LANES)


def _pack_small(tree):
    sh = jnp.concatenate([tree[n].reshape(-1) for n, _, _ in SMALL_SHARDED])
    rep = jnp.concatenate([tree[n].reshape(-1) for n, _ in REPLICATED])
    return jnp.concatenate([_pad_rows(sh, ROWS_SSH), _pad_rows(rep, ROWS_REP)], axis=0)


def _unpack_small(buf):
    out = {}
    for flat, items in ((buf[:ROWS_SSH].reshape(-1), [(n, s) for n, s, _ in SMALL_SHARDED]),
                        (buf[ROWS_SSH:].reshape(-1), REPLICATED)):
        off = 0
        for n, s in items:
            out[n] = flat[off:off + math.prod(s)].reshape(s)
            off += math.prod(s)
    return out


def _full_from_blocks(blocks, s, ax):
    return jnp.concatenate([blocks[d] for d in range(N_DEV)], axis=ax)


def _blocks_from_full(full, s, ax):
    return jnp.stack([lax.slice_in_dim(full, d * s[ax], (d + 1) * s[ax], axis=ax) for d in range(N_DEV)])


def _pack_small_grads(sharded_blocks, replicated):
    sh = jnp.concatenate([sharded_blocks[n].reshape(N_DEV, -1) for n, _, _ in SMALL_SHARDED], axis=1)
    sh = jnp.pad(sh, ((0, 0), (0, ROWS_SSH * LANES - sh.shape[1]))).reshape(N_DEV, ROWS_SSH, LANES)
    rep = _pad_rows(jnp.concatenate([replicated[n].reshape(-1) for n, _ in REPLICATED]), ROWS_REP)
    return jnp.concatenate([sh, jnp.broadcast_to(rep[None], (N_DEV, ROWS_REP, LANES))], axis=1)


def _layer_fwd(x, h1, wts, big, l, l_next, next_shards):
    L = x.shape[0]
    tm = min(TILE_MM, L)
    nt = L // tm
    rest_next = [s for n, s in zip(BIG, next_shards) if n != "w_up"] if next_shards else None
    p = _mm(h1, big["w_in"], dims=NN, grid=(nt, D_INP // 640, 1),
            a_spec=((tm, D), lambda i, j, k: (i, 0)), b_spec=((None, D, 640), lambda i, j, k: (0, 0, j)),
            o_spec=((tm, 640), lambda i, j, k: (i, j)), out_shape=jax.ShapeDtypeStruct((L, D_INP), BF16),
            tile=(tm, 640), name="proj_in", comm=_gather_ici_comm(rest_next) if next_shards else None)
    if next_shards:
        p, rest_next = p[0], p[1:]
    o_f, sp_f = _gla_fwd(p, wts["gpad_f"], wts["bias_f"], l, None, True, "gla_fwd_f")
    o_tot, sp_b = _gla_fwd(p, wts["gpad_b"], wts["bias_b"], l, o_f, False, "gla_fwd_b")
    y_cat = _mixer_out(p, wts["conv_a"], wts["gh"], l, o_tot, "mixer_out")
    y = _mm(y_cat, big["w_out"], dims=NN, grid=(nt, 1, 1),
            a_spec=((tm, D), lambda i, j, k: (i, 0)), b_spec=((None, D, D), lambda i, j, k: (0, 0, 0)),
            o_spec=((tm, D), lambda i, j, k: (i, 0)), out_shape=jax.ShapeDtypeStruct((L, D), BF16),
            tile=(tm, D), name="proj_out")
    x1, h2 = _post_pre(x, y, wts["g2"], l, wts["g3"], l, "post_pre_mix")
    u8 = _mm(h2, big["w_up"], dims=NN, grid=(nt, N_DEV, 1),
             a_spec=((tm, D), lambda i, j, k: (i, 0)), b_spec=((None, None, D, SH_FF), lambda i, j, k: (0, j, 0, 0)),
             o_spec=((None, tm, SH_FF), lambda i, j, k: (j, i, 0)),
             out_shape=jax.ShapeDtypeStruct((N_DEV, L, SH_FF), BF16), tile=(tm, SH_FF), name="ffn_up",
             comm=_gather_ici_comm([next_shards[BIG.index("w_up")]]) if next_shards else None)
    gathered = None
    if next_shards:
        u8, up_next = u8[0], u8[1]
        gathered = [rest_next[0], rest_next[1], up_next, rest_next[2]]
    a = _ffn_act(u8, wts["cw"], l, "ffn_act")
    tm1 = min(TILE_MM_KIN, L)
    y2 = _mm(a, big["w_down"], dims=NN, grid=(L // tm1, 1, 1), kin=FF_HALF,
             a_spec=((FF_HALF, tm1, SH_FF), lambda i, j, k: (0, i, 0)),
             b_spec=((None, FF_HALF, SH_FF, D), lambda i, j, k: (0, 0, 0, 0)),
             o_spec=((tm1, D), lambda i, j, k: (i, 0)), out_shape=jax.ShapeDtypeStruct((L, D), BF16),
             tile=(tm1, D), name="ffn_down", comm=_gather_d2d_comm(gathered) if next_shards else None)
    if next_shards:
        y2, gathered = y2[0], y2[1:]
    x2, h1_next = _post_pre(x1, y2, wts["g4"], l, wts["g1"], l_next, "post_pre_ffn")
    saved = dict(x=x, h1=h1, p=p, o_tot=o_tot, sp_f=sp_f, sp_b=sp_b, y_cat=y_cat, y=y, x1=x1, h2=h2, u8=u8, a=a, y2=y2,
                 big=big)
    return x2, h1_next, saved, gathered


def _layer_bwd(dx2, wts, s, l, pending, c_idx):
    L = dx2.shape[0]
    big = s["big"]
    tm = min(TILE_MM, L)
    nt = L // tm
    tm1 = min(TILE_MM_KIN, L)
    tk = min(TILE_MM_TOKENS, L)
    nkt = L // tk
    dy2, dg4 = _norm_bwd(s["y2"], wts["g4"], l, dx2, None, "norm_bwd_ffn_post")
    da = _mm(dy2, big["w_down"], dims=NT, grid=(nt, FF_HALF, 1),
             a_spec=((tm, D), lambda i, j, k: (i, 0)), b_spec=((None, None, SH_FF, D), lambda i, j, k: (0, j, 0, 0)),
             o_spec=((None, tm, SH_FF), lambda i, j, k: (j, i, 0)),
             out_shape=jax.ShapeDtypeStruct((FF_HALF, L, SH_FF), BF16), tile=(tm, SH_FF), name="ffn_down_dx",
             comm=_grads_d2d_comm(pending) if pending else None)
    pairs = None
    if pending:
        da, from_sibling = da[0], da[1:]
        pairs = [_pair_sum(g, r, c_idx, BF16, "grads_pair_sum") for g, r in zip(pending, from_sibling)]
    dw_down = _mm(s["a"], dy2, dims=TN, grid=(FF_HALF, 1, nkt),
                  a_spec=((None, tk, SH_FF), lambda i, j, k: (i, k, 0)), b_spec=((tk, D), lambda i, j, k: (k, 0)),
                  o_spec=((SH_FF, D), lambda i, j, k: (i, 0)), out_shape=jax.ShapeDtypeStruct((DFF, D), F32),
                  tile=(SH_FF, D), name="ffn_down_dw")
    du, dcw = _ffn_act_bwd(s["u8"], wts["cw"], l, da, "ffn_act_bwd")
    d_u8 = _ffn_conv_t(du.reshape(N_DEV, L, SH_FF), wts["cw"], l, "ffn_conv_t")
    dh2 = _mm(d_u8, big["w_up"], dims=NT, grid=(L // tm1, 1, N_DEV // FF_HALF), kin=FF_HALF,
              a_spec=((FF_HALF, tm1, SH_FF), lambda i, j, k: (k, i, 0)),
              b_spec=((None, FF_HALF, D, SH_FF), lambda i, j, k: (0, k, 0, 0)),
              o_spec=((tm1, D), lambda i, j, k: (i, 0)), out_shape=jax.ShapeDtypeStruct((L, D), BF16),
              tile=(tm1, D), name="ffn_up_dx",
              comm=_grads_ici_comm([q for n, q in zip(BIG, pairs) if n != "w_up"]) if pending else None)
    if pending:
        dh2, rest_parts = dh2[0], dh2[1:]
    dw_up = _mm(s["h2"], d_u8, dims=TN, grid=(1, N_DEV, nkt),
                a_spec=((tk, D), lambda i, j, k: (k, 0)), b_spec=((None, tk, SH_FF), lambda i, j, k: (j, k, 0)),
                o_spec=((None, D, SH_FF), lambda i, j, k: (j, 0, 0)),
                out_shape=jax.ShapeDtypeStruct((N_DEV, D, SH_FF), F32), tile=(D, SH_FF), name="ffn_up_dw",
                comm=_grads_ici_comm([pairs[BIG.index("w_up")]]) if pending else None)
    parts = None
    if pending:
        dw_up, up_part = dw_up[0], dw_up[1]
        parts = [rest_parts[0], rest_parts[1], up_part, rest_parts[2]]
    dx1, dg3 = _norm_bwd(s["x1"], wts["g3"], l, dh2, dx2, "norm_bwd_ffn_pre")
    dy, dg2 = _norm_bwd(s["y"], wts["g2"], l, dx1, None, "norm_bwd_mix_post")
    dy_cat = _mm(dy, big["w_out"], dims=NT, grid=(nt, 1, 1),
                 a_spec=((tm, D), lambda i, j, k: (i, 0)), b_spec=((None, D, D), lambda i, j, k: (0, 0, 0)),
                 o_spec=((tm, D), lambda i, j, k: (i, 0)), out_shape=jax.ShapeDtypeStruct((L, D), BF16),
                 tile=(tm, D), name="proj_out_dx")
    dw_out = _mm(s["y_cat"], dy, dims=TN, grid=(1, 1, nkt),
                 a_spec=((tk, D), lambda i, j, k: (k, 0)), b_spec=((tk, D), lambda i, j, k: (k, 0)),
                 o_spec=((D, D), lambda i, j, k: (0, 0)), out_shape=jax.ShapeDtypeStruct((D, D), F32),
                 tile=(D, D), name="proj_out_dw")
    dgb, dgc, dgv, dgo, d_o, dconv_a, dgh = _mixer_out_bwd(s["p"], wts["conv_a"], wts["gh"], l, s["o_tot"], dy_cat,
                                                          "mixer_out_bwd")
    part_f = _gla_bwd(s["p"], wts["gpad_f"], wts["bias_f"], l, s["sp_f"], d_o, None, True, "gla_bwd_f")
    dq, dk, dv, dlr, dgp_b, dbias_b = _gla_bwd(s["p"], wts["gpad_b"], wts["bias_b"], l, s["sp_b"], d_o, part_f[:4],
                                               False, "gla_bwd_b")
    dp = jnp.concatenate([dgb, dgc, dgv, dq, dk, dv, dgo, dlr], axis=1)
    dh1 = _mm(dp, big["w_in"], dims=NT, grid=(L // tm1, 1, 1),
              a_spec=((tm1, D_INP), lambda i, j, k: (i, 0)), b_spec=((None, D, D_INP), lambda i, j, k: (0, 0, 0)),
              o_spec=((tm1, D), lambda i, j, k: (i, 0)), out_shape=jax.ShapeDtypeStruct((L, D), BF16),
              tile=(tm1, D), name="proj_in_dx")
    dw_in = _mm(s["h1"], dp, dims=TN, grid=(1, D_INP // 640, nkt),
                a_spec=((tk, D), lambda i, j, k: (k, 0)), b_spec=((tk, 640), lambda i, j, k: (k, j)),
                o_spec=((D, 640), lambda i, j, k: (0, j)), out_shape=jax.ShapeDtypeStruct((D, D_INP), F32),
                tile=(D, 640), name="proj_in_dw")
    dx0, dg1 = _norm_bwd(s["x"], wts["g1"], l, dh1, dx1, "norm_bwd_mix_pre")
    grads = dict(
        norm_mix_pre=dg1[0], norm_mix_post=dg2[0], norm_ffn_pre=dg3[0], norm_ffn_post=dg4[0],
        gate_bias_fwd=part_f[5][0], gate_bias_bwd=dbias_b[0], gla_head_norm=dgh[0],
        w_in=_blocks_from_full(dw_in, (D, SH_IN), 1), w_out=dw_out.reshape(N_DEV, D // N_DEV, D), w_up=dw_up,
        w_down=dw_down.reshape(N_DEV, DFF // N_DEV, D),
        conv_a=_blocks_from_full(dconv_a, (3, DC // N_DEV), 1),
        gate_up_fwd=_blocks_from_full(part_f[4][:RANK], (RANK, DK // N_DEV), 1),
        gate_up_bwd=_blocks_from_full(dgp_b[RANK:2 * RANK], (RANK, DK // N_DEV), 1),
        conv_ffn=dcw.reshape(N_DEV, 3, SH_FF))
    return dx0, grads, parts


def _matmul_weights(g_in, g_out, g_up, g_down):
    w_in = jnp.concatenate([g_in[:, d] for d in range(N_DEV)] + [jnp.zeros((1, D, D_INP - D_IN), BF16)], axis=2)
    return dict(w_in=w_in, w_out=g_out.reshape(1, D, D), w_up=g_up, w_down=g_down.reshape(1, FF_HALF, SH_FF, D))


def _small_weights(g_small, rep):
    small = {n: jnp.moveaxis(t, 0, 1) for n, t in jax.vmap(_unpack_small)(
        jnp.concatenate([g_small[0], jnp.zeros((N_DEV, ROWS_REP, LANES), F32)], axis=1)).items()
        if n in [s[0] for s in SMALL_SHARDED]}
    conv_a = jnp.concatenate([small["conv_a"][:, d] for d in range(N_DEV)], axis=2)
    gate_f = jnp.concatenate([small["gate_up_fwd"][:, d] for d in range(N_DEV)], axis=2).astype(BF16)
    gate_b = jnp.concatenate([small["gate_up_bwd"][:, d] for d in range(N_DEV)], axis=2).astype(BF16)
    zeros = jnp.zeros((DEPTH, LR_BLK, DK), BF16)
    return dict(
        conv_a=conv_a, cw=small["conv_ffn"],
        gpad_f=zeros.at[:, :RANK].set(gate_f), gpad_b=zeros.at[:, RANK:2 * RANK].set(gate_b),
        bias_f=rep["gate_bias_fwd"][:, None, :], bias_b=rep["gate_bias_bwd"][:, None, :],
        gh=rep["gla_head_norm"][:, None, :],
        g1=rep["norm_mix_pre"][:, None, :], g2=rep["norm_mix_post"][:, None, :],
        g3=rep["norm_ffn_pre"][:, None, :], g4=rep["norm_ffn_post"][:, None, :])


def kernel(x, norm_mix_pre, norm_mix_post, norm_ffn_pre, norm_ffn_post, w_in, conv_a, gate_up_fwd, gate_bias_fwd, gate_up_bwd, gate_bias_bwd, gla_head_norm, w_out, w_up, conv_ffn, w_down, loss_target, m_norm_mix_pre, m_norm_mix_post, m_norm_ffn_pre, m_norm_ffn_post, m_w_in, m_conv_a, m_gate_up_fwd, m_gate_bias_fwd, m_gate_up_bwd, m_gate_bias_bwd, m_gla_head_norm, m_w_out, m_w_up, m_conv_ffn, m_w_down, v_norm_mix_pre, v_norm_mix_post, v_norm_ffn_pre, v_norm_ffn_post, v_w_in, v_conv_a, v_gate_up_fwd, v_gate_bias_fwd, v_gate_up_bwd, v_gate_bias_bwd, v_gla_head_norm, v_w_out, v_w_up, v_conv_ffn, v_w_down):
    w = dict(norm_mix_pre=norm_mix_pre, norm_mix_post=norm_mix_post, norm_ffn_pre=norm_ffn_pre,
             norm_ffn_post=norm_ffn_post, w_in=w_in, conv_a=conv_a, gate_up_fwd=gate_up_fwd,
             gate_bias_fwd=gate_bias_fwd, gate_up_bwd=gate_up_bwd, gate_bias_bwd=gate_bias_bwd,
             gla_head_norm=gla_head_norm, w_out=w_out, w_up=w_up, conv_ffn=conv_ffn, w_down=w_down)
    m = dict(norm_mix_pre=m_norm_mix_pre, norm_mix_post=m_norm_mix_post, norm_ffn_pre=m_norm_ffn_pre,
             norm_ffn_post=m_norm_ffn_post, w_in=m_w_in, conv_a=m_conv_a, gate_up_fwd=m_gate_up_fwd,
             gate_bias_fwd=m_gate_bias_fwd, gate_up_bwd=m_gate_up_bwd, gate_bias_bwd=m_gate_bias_bwd,
             gla_head_norm=m_gla_head_norm, w_out=m_w_out, w_up=m_w_up, conv_ffn=m_conv_ffn, w_down=m_w_down)
    v = dict(norm_mix_pre=v_norm_mix_pre, norm_mix_post=v_norm_mix_post, norm_ffn_pre=v_norm_ffn_pre,
             norm_ffn_post=v_norm_ffn_post, w_in=v_w_in, conv_a=v_conv_a, gate_up_fwd=v_gate_up_fwd,
             gate_bias_fwd=v_gate_bias_fwd, gate_up_bwd=v_gate_up_bwd, gate_bias_bwd=v_gate_bias_bwd,
             gla_head_norm=v_gla_head_norm, w_out=v_w_out, w_up=v_w_up, conv_ffn=v_conv_ffn, w_down=v_w_down)
    axes = ("x", "y", "c")
    L = x.shape[1]
    x0 = x.reshape(L, D)
    target = loss_target.reshape(L, D)

    w_small = _pack_small(w)
    w16 = {n: w[n].astype(BF16) for n in BIG}
    gathered = _all_gather([w16[n][0:1] for n in BIG] + [w_small[None, :ROWS_SSH]], "gather_weights")
    wts = _small_weights(gathered[-1], w)
    big = _matmul_weights(*gathered[:-1])

    h1 = _norm_cast(x0, wts["g1"], 0, "norm_first")
    xl, saved = x0, []
    for l in range(DEPTH):
        nxt = [w16[n][l + 1:l + 2] for n in BIG] if l + 1 < DEPTH else None
        xl, h1, s, gathered = _layer_fwd(xl, h1, wts, big, l, min(l + 1, DEPTH - 1), nxt)
        saved.append(s)
        if nxt:
            big = _matmul_weights(*gathered)
    dx, sq = _loss_grad(xl, target, "loss_grad")
    loss = lax.psum(0.5 * jnp.sum(sq) / D, axes)

    c_idx = lax.axis_index("c").astype(jnp.int32).reshape(1)
    layer_grads, layer_parts, pending = [None] * DEPTH, [None] * DEPTH, None
    for l in reversed(range(DEPTH)):
        dx, layer_grads[l], done = _layer_bwd(dx, wts, saved[l], l, pending, c_idx)
        if pending:
            layer_parts[l + 1] = done
        pending = [layer_grads[l][n][None] for n in BIG]
    small_names = [n for n, _, _ in SMALL_SHARDED] + [n for n, _ in REPLICATED]
    stacked = {n: jnp.stack([g[n] for g in layer_grads]) for n in small_names}
    g_small = _pack_small_grads({n: jnp.moveaxis(stacked[n], 0, 1) for n, _, _ in SMALL_SHARDED}, stacked)
    last = pending + [g_small[None]]
    from_sibling = _exchange_sibling(last, "grads_to_sibling")
    pairs = [_pair_sum(g, r, c_idx, BF16 if i < len(BIG) else F32, "grads_pair_sum")
             for i, (g, r) in enumerate(zip(last, from_sibling))]
    parts = _exchange_chips(pairs, "grads_to_chips")
    layer_parts[0] = parts[:-1]

    results = {}
    for i, n in enumerate(BIG):
        part = jnp.concatenate([layer_parts[l][i] for l in range(DEPTH)], axis=0)
        results[n] = _sum_adamw(part, w[n], m[n], v[n], "sum_adamw")
    small = _sum_adamw(parts[-1], w_small[None], _pack_small(m)[None], _pack_small(v)[None], "sum_adamw_small")
    small = [_unpack_small(buf[0]) for buf in small]
    outs = [loss, dx.reshape(x.shape)]
    for i in range(4):
        outs += [results[n][i] if n in BIG else small[i][n] for n in WEIGHT_ORDER]
    return tuple(outs)
```

```python
import math
from typing import Callable, NamedTuple

import jax
import jax.numpy as jnp
from jax import lax
from jax.experimental import pallas as pl
from jax.experimental.pallas import tpu as pltpu

F32 = jnp.float32
BF16 = jnp.bfloat16

DEPTH = 4
D = 1024
DC = 512
DG = 512
HEADS = 4
HV = 128
HK = 64
DK = 256
RANK = 16
CH = 64
DFF = 2816
D_IN = 3104
D_INP = 3200
LR_BLK = 128
EPS = 1e-6
HB = 16
N_DEV = 8
N_CHIP = 4
LANES = 1024
SH_IN = D_IN // N_DEV
SH_FF = 2 * DFF // N_DEV
FF_HALF = N_DEV // 2

ADAM_LR, ADAM_B1, ADAM_B2, ADAM_EPS, ADAM_WD, ADAM_STEP = 0.001, 0.9, 0.999, 1e-08, 0.01, 10

VMEM_LIMIT = 48 * 1024 * 1024
TILE_TOKENS = 512
TILE_GLA = 512
TILE_FFN = 1024
TILE_NORM = 1024
TILE_MM = 2048
TILE_MM_KIN = 1024
TILE_MM_TOKENS = 2048

COL_GB, COL_GC, COL_GV = 0, 1, 2
COL_Q, COL_K = 6, 7
COL_V, COL_GO = 4, 5
COL_LR = 24

BIG = ("w_in", "w_out", "w_up", "w_down")
SMALL_SHARDED = (
    ("conv_a", (DEPTH, 3, DC // N_DEV), 2),
    ("gate_up_fwd", (DEPTH, RANK, DK // N_DEV), 2),
    ("gate_up_bwd", (DEPTH, RANK, DK // N_DEV), 2),
    ("conv_ffn", (DEPTH, 3, SH_FF), 2),
)
REPLICATED = (
    ("norm_mix_pre", (DEPTH, D)), ("norm_mix_post", (DEPTH, D)), ("norm_ffn_pre", (DEPTH, D)),
    ("norm_ffn_post", (DEPTH, D)), ("gate_bias_fwd", (DEPTH, DK)), ("gate_bias_bwd", (DEPTH, DK)),
    ("gla_head_norm", (DEPTH, HV)),
)
WEIGHT_ORDER = ("norm_mix_pre", "norm_mix_post", "norm_ffn_pre", "norm_ffn_post", "w_in", "conv_a", "gate_up_fwd",
                "gate_bias_fwd", "gate_up_bwd", "gate_bias_bwd", "gla_head_norm", "w_out", "w_up", "conv_ffn", "w_down")


def _rows_for(n_elems):
    return (-(-n_elems // LANES) + 7) // 8 * 8


ROWS_SSH = _rows_for(sum(math.prod(s) for _, s, _ in SMALL_SHARDED))
ROWS_REP = _rows_for(sum(math.prod(s) for _, s in REPLICATED))
ROWS_SMALL = ROWS_SSH + ROWS_REP


def _params(sem):
    return pltpu.CompilerParams(dimension_semantics=sem, vmem_limit_bytes=VMEM_LIMIT)


def _silu_parts(x):
    s = 1.0 / (1.0 + jnp.exp(-x))
    return x * s, s


def _rstd(xf):
    return lax.rsqrt(jnp.mean(xf * xf, axis=-1, keepdims=True) + EPS)


NN, NT, TN = ((1,), (0,)), ((1,), (1,)), ((0,), (0,))


def _dot(a, b, dims):
    return lax.dot_general(a, b, (dims, ((), ())), preferred_element_type=F32)


def _mm(a, b, *, dims, grid, a_spec, b_spec, o_spec, out_shape, tile, name, kin=0, comm=None):
    nk = grid[2]
    n_ci = len(comm.ins) if comm else 0
    n_co = len(comm.out_shapes) if comm else 0

    def body(*refs):
        a_ref, b_ref = refs[:2]
        ci = refs[2:2 + n_ci]
        o_ref = refs[2 + n_ci]
        co = refs[3 + n_ci:3 + n_ci + n_co]
        rest = refs[3 + n_ci + n_co:]
        if comm:
            sems, rest = rest[:3], rest[3:]
            step = (pl.program_id(0) * grid[1] + pl.program_id(1)) * grid[2] + pl.program_id(2)

            @pl.when(step == 0)
            def _():
                comm.start(ci, co, *sems)

        if kin:
            prod = _dot(a_ref[0], b_ref[0], dims)
            for d in range(1, kin):
                prod = prod + _dot(a_ref[d], b_ref[d], dims)
        else:
            prod = _dot(a_ref[...], b_ref[...], dims)
        if nk == 1:
            o_ref[...] = prod.astype(o_ref.dtype)
        else:
            acc_ref = rest[0]
            k = pl.program_id(2)

            @pl.when(k == 0)
            def _():
                acc_ref[...] = prod

            @pl.when(k > 0)
            def _():
                acc_ref[...] += prod

            @pl.when(k == nk - 1)
            def _():
                o_ref[...] = acc_ref[...].astype(o_ref.dtype)

        if comm:
            @pl.when(step == grid[0] * grid[1] * grid[2] - 1)
            def _():
                comm.wait(ci, co, *sems)

    acc = [pltpu.VMEM(tile, F32)] if nk > 1 else []
    if not comm:
        return pl.pallas_call(
            body, name=name, grid=grid, in_specs=[pl.BlockSpec(*a_spec), pl.BlockSpec(*b_spec)],
            out_specs=pl.BlockSpec(*o_spec), out_shape=out_shape, scratch_shapes=acc,
            compiler_params=_params(("parallel", "parallel", "arbitrary")),
        )(a, b)
    return pl.pallas_call(
        body, name=name, grid=grid,
        in_specs=[pl.BlockSpec(*a_spec), pl.BlockSpec(*b_spec)] + [HBM_SPEC] * n_ci,
        out_specs=[pl.BlockSpec(*o_spec)] + [HBM_SPEC] * n_co, out_shape=[out_shape] + list(comm.out_shapes),
        scratch_shapes=[pltpu.SemaphoreType.DMA((s,)) for s in comm.sems] + acc,
        input_output_aliases={2 + i: 1 + o for i, o in comm.aliases.items()},
        compiler_params=_params(("arbitrary", "arbitrary", "arbitrary")),
    )(a, b, *comm.ins)


def _halo_maps(tm, n_rows):
    r, last = tm // HB, n_rows // HB - 1
    return (lambda i: jnp.maximum(i * r - 1, 0)), (lambda i: jnp.minimum((i + 1) * r, last))


def _shift(x, prev_blk, next_blk):
    tm = x.shape[0]
    xs = jnp.concatenate([prev_blk, x, next_blk], axis=0)
    n = xs.shape[0]
    down = pltpu.roll(xs, 1, 0)[HB:HB + tm]
    up = pltpu.roll(xs, n - 1, 0)[HB:HB + tm]
    return down, up


def _edge_scales(i, n):
    return jnp.where(i > 0, 1.0, 0.0).astype(F32), jnp.where(i < n - 1, 1.0, 0.0).astype(F32)


def _gain_spec(l, width=D):
    return pl.BlockSpec((None, 1, width), lambda *_: (l, 0, 0))


def _norm_cast(x, g, l, name):
    L = x.shape[0]
    tm = min(TILE_NORM, L)

    def body(x_ref, g_ref, o_ref):
        xf = x_ref[...]
        o_ref[...] = (xf * _rstd(xf) * g_ref[...]).astype(BF16)

    return pl.pallas_call(
        body, name=name, grid=(L // tm,), in_specs=[pl.BlockSpec((tm, D), lambda i: (i, 0)), _gain_spec(l)],
        out_specs=pl.BlockSpec((tm, D), lambda i: (i, 0)), out_shape=jax.ShapeDtypeStruct((L, D), BF16),
        compiler_params=_params(("parallel",)),
    )(x, g)


def _post_pre(x, y, g_post, l_post, g_pre, l_pre, name):
    L = x.shape[0]
    tm = min(TILE_NORM, L)

    def body(x_ref, y_ref, gp_ref, gn_ref, x1_ref, h_ref):
        yf = y_ref[...].astype(F32)
        x1 = x_ref[...] + yf * _rstd(yf) * gp_ref[...]
        x1_ref[...] = x1
        h_ref[...] = (x1 * _rstd(x1) * gn_ref[...]).astype(BF16)

    tile = pl.BlockSpec((tm, D), lambda i: (i, 0))
    return pl.pallas_call(
        body, name=name, grid=(L // tm,), in_specs=[tile, tile, _gain_spec(l_post), _gain_spec(l_pre)],
        out_specs=[tile, tile],
        out_shape=[jax.ShapeDtypeStruct((L, D), F32), jax.ShapeDtypeStruct((L, D), BF16)],
        compiler_params=_params(("parallel",)),
    )(x, y, g_post, g_pre)


def _norm_bwd(yin, g, l, dout, dres, name):
    L = yin.shape[0]
    tm = min(TILE_NORM, L)
    with_res = dres is not None

    def body(*refs):
        if with_res:
            y_ref, g_ref, do_ref, dr_ref, din_ref, dg_ref = refs
        else:
            y_ref, g_ref, do_ref, din_ref, dg_ref = refs
        i = pl.program_id(0)
        y = y_ref[...].astype(F32)
        r = _rstd(y)
        do = do_ref[...].astype(F32)
        z = do * g_ref[...]
        din = r * z - y * (r * r * r) * jnp.mean(y * z, axis=-1, keepdims=True)
        if with_res:
            din = din + dr_ref[...]
        din_ref[...] = din.astype(din_ref.dtype)
        part = jnp.sum(do * y * r, axis=0, keepdims=True)

        @pl.when(i == 0)
        def _():
            dg_ref[...] = part

        @pl.when(i > 0)
        def _():
            dg_ref[...] += part

    tile = pl.BlockSpec((tm, D), lambda i: (i, 0))
    args = (yin, g, dout) + ((dres,) if with_res else ())
    return pl.pallas_call(
        body, name=name, grid=(L // tm,), in_specs=[tile, _gain_spec(l), tile] + ([tile] if with_res else []),
        out_specs=[tile, pl.BlockSpec((1, D), lambda i: (0, 0))],
        out_shape=[jax.ShapeDtypeStruct((L, D), F32 if with_res else BF16), jax.ShapeDtypeStruct((1, D), F32)],
        compiler_params=_params(("arbitrary",)),
    )(*args)


def _gla_consts(fwd, tb):
    row = lax.broadcasted_iota(jnp.int32, (tb, tb), 0)
    col = lax.broadcasted_iota(jnp.int32, (tb, tb), 1)
    same = (row // CH) == (col // CH)
    tri = same & ((col <= row) if fwd else (col >= row))
    tri_t = same & ((col >= row) if fwd else (col <= row))
    row_st = lax.broadcasted_iota(jnp.int32, (HEADS * CH, CH), 0) & (CH - 1)
    col_st = lax.broadcasted_iota(jnp.int32, (HEADS * CH, CH), 1)
    tri_st = (col_st <= row_st) if fwd else (col_st >= row_st)
    lane_head = lax.broadcasted_iota(jnp.int32, (1, DK), 1) // HK
    head_masks = [lane_head == h for h in range(HEADS)]
    srow = lax.broadcasted_iota(jnp.int32, (DG, DK), 0) // HV
    scol = lax.broadcasted_iota(jnp.int32, (DG, DK), 1) // HK
    return tri.astype(BF16), tri_t.astype(BF16), same.astype(BF16), tri_st, head_masks, srow == scol


def _dot_hilo(tri_b, x):
    hi = x.astype(BF16)
    lo = (x - hi.astype(F32)).astype(BF16)
    return _dot(tri_b, hi, NN) + _dot(tri_b, lo, NN)


def _gla_block_terms(q_ref, k_ref, lr_ref, gp_ref, bias_ref, tri_b, same_b):
    pre = _dot(lr_ref[...], gp_ref[...], NN) + bias_ref[...]
    sig_neg = 1.0 / (1.0 + jnp.exp(pre))
    a = (jnp.minimum(pre, 0.0) - jnp.log(1.0 + jnp.exp(-jnp.abs(pre)))) * (1.0 / 16.0)
    cum = _dot_hilo(tri_b, a)
    cl = _dot_hilo(same_b, a)
    e = jnp.exp(cum)
    einv = jnp.exp(-cum)
    eout = jnp.exp(cl - cum)
    q_in = q_ref[...].astype(F32) * e * (HK ** -0.5)
    k = k_ref[...].astype(F32)
    return dict(sig_neg=sig_neg, e=e, einv=einv, eout=eout, decay=jnp.exp(cl), q_in=q_in, k_in=k * einv,
                k_out=k * eout)


def _gla_chunk(t, c, head_masks):
    sl = slice(c * CH, (c + 1) * CH)
    tc = {n: x[sl] for n, x in t.items() if n != "decay"}
    tc["decay"] = jnp.max(t["decay"][c * CH:c * CH + 8], axis=0, keepdims=True)
    tc["q_st"] = jnp.concatenate([jnp.where(mh, tc["q_in"], 0.0) for mh in head_masks], axis=0).astype(BF16)
    return tc


def _gate_specs(l):
    return [pl.BlockSpec((None, LR_BLK, DK), lambda i: (l, 0, 0)), pl.BlockSpec((None, 1, DK), lambda i: (l, 0, 0))]


def _gla_fwd(p, gpad, bias, l, o_prev, fwd, name):
    L = p.shape[0]
    tb = min(TILE_GLA, L)
    nb, ncb, nch = L // tb, tb // CH, L // CH
    blk = (lambda i: i) if fwd else (lambda i: nb - 1 - i)
    with_prev = o_prev is not None

    def body(*refs):
        if with_prev:
            q_ref, k_ref, v_ref, lr_ref, gp_ref, bias_ref, op_ref, o_ref, sp_ref, s_ref = refs
        else:
            q_ref, k_ref, v_ref, lr_ref, gp_ref, bias_ref, o_ref, sp_ref, s_ref = refs
        i = pl.program_id(0)

        @pl.when(i == 0)
        def _():
            s_ref[...] = jnp.zeros_like(s_ref)

        tri_b, _, same_b, tri_st, head_masks, blockmask = _gla_consts(fwd, tb)
        terms = _gla_block_terms(q_ref, k_ref, lr_ref, gp_ref, bias_ref, tri_b, same_b)
        for c in (range(ncb) if fwd else reversed(range(ncb))):
            rows = pl.ds(c * CH, CH)
            t = _gla_chunk(terms, c, head_masks)
            v = v_ref[rows, :]
            scores = _dot(t["q_st"], t["k_in"].astype(BF16), NT)
            a_st = jnp.where(tri_st, scores, 0.0).astype(BF16)
            r = _dot(a_st, v, NN)
            o_intra = jnp.concatenate([r[h * CH:(h + 1) * CH, h * HV:(h + 1) * HV] for h in range(HEADS)], axis=1)
            s_b = s_ref[...].astype(BF16)
            sp_ref[c] = s_b
            o = o_intra + _dot(t["q_in"].astype(BF16), s_b, NT)
            if with_prev:
                o = o + op_ref[rows, :]
            o_ref[rows, :] = o
            kv_t = _dot(v, t["k_out"].astype(BF16), TN)
            s_ref[...] = s_ref[...] * t["decay"] + jnp.where(blockmask, kv_t, 0.0)

    def col(width, c):
        return pl.BlockSpec((tb, width), lambda i: (blk(i), c))

    in_specs = [col(DK, COL_Q), col(DK, COL_K), col(DG, COL_V), col(LR_BLK, COL_LR)] + _gate_specs(l)
    args = [p, p, p, p, gpad, bias]
    if with_prev:
        in_specs.append(pl.BlockSpec((tb, DG), lambda i: (blk(i), 0)))
        args.append(o_prev)
    return pl.pallas_call(
        body, name=name, grid=(nb,), in_specs=in_specs,
        out_specs=[pl.BlockSpec((tb, DG), lambda i: (blk(i), 0)), pl.BlockSpec((ncb, DG, DK), lambda i: (blk(i), 0, 0))],
        out_shape=[jax.ShapeDtypeStruct((L, DG), F32), jax.ShapeDtypeStruct((nch, DG, DK), BF16)],
        scratch_shapes=[pltpu.VMEM((DG, DK), F32)],
        compiler_params=_params(("arbitrary",)),
    )(*args)


P_COLS = dict(gb=(0, DC), gc=(DC, DC), gv=(2 * DC, DC), q=(3 * DC, DK), k=(3 * DC + DK, DK), v=(3 * DC + 2 * DK, DG),
              go=(3 * DC + 2 * DK + DG, DG), lr=(3 * DC + 2 * DK + 2 * DG, LR_BLK))


def _gla_bwd(p, gpad, bias, l, sprev, d_o, prev, fwd, name):
    L = p.shape[0]
    tb = min(TILE_GLA, L)
    nb, ncb = L // tb, tb // CH
    blk = (lambda i: nb - 1 - i) if fwd else (lambda i: i)
    with_prev = prev is not None

    def body(*refs):
        q_ref, k_ref, v_ref, lr_ref, gp_ref, bias_ref, sp_ref, do_ref = refs[:8]
        rest = refs[8:]
        if with_prev:
            pq_ref, pk_ref, pv_ref, plr_ref, dgb_ref, dgc_ref, dgv_ref, dgo_ref = rest[:8]
            dp_ref, dg_ref, db_ref, ds_ref = rest[8:]

            def put(what, rows, val):
                c0, width = P_COLS[what]
                dp_ref[rows, c0:c0 + width] = val

            for what, ref in (("gb", dgb_ref), ("gc", dgc_ref), ("gv", dgv_ref), ("go", dgo_ref)):
                put(what, slice(None), ref[...])
        else:
            dq_ref, dk_ref, dv_ref, dlr_ref, dg_ref, db_ref, ds_ref = rest
            out_of = dict(q=dq_ref, k=dk_ref, v=dv_ref, lr=dlr_ref)

            def put(what, rows, val):
                out_of[what][rows, :] = val

        i = pl.program_id(0)

        @pl.when(i == 0)
        def _():
            ds_ref[...] = jnp.zeros_like(ds_ref)
            dg_ref[...] = jnp.zeros_like(dg_ref)
            db_ref[...] = jnp.zeros_like(db_ref)

        tri_b, tri_t_b, same_b, tri_st, head_masks, blockmask = _gla_consts(fwd, tb)
        terms = _gla_block_terms(q_ref, k_ref, lr_ref, gp_ref, bias_ref, tri_b, same_b)
        dcum_of, dcl_of = [None] * ncb, [None] * ncb
        for c in (reversed(range(ncb)) if fwd else range(ncb)):
            rows = pl.ds(c * CH, CH)
            t = _gla_chunk(terms, c, head_masks)
            v = v_ref[rows, :]
            do = do_ref[rows, :]
            q_in, k_in, k_out = t["q_in"], t["k_in"], t["k_out"]
            q_b, k_in_b, k_out_b = q_in.astype(BF16), k_in.astype(BF16), k_out.astype(BF16)
            scores = _dot(t["q_st"], k_in_b, NT)
            a_st = jnp.where(tri_st, scores, 0.0).astype(BF16)
            s_prev = sp_ref[c]
            ds = ds_ref[...]
            ds_b = ds.astype(BF16)

            da_heads = [_dot(do[:, h * HV:(h + 1) * HV], v[:, h * HV:(h + 1) * HV], NT) for h in range(HEADS)]
            da_st = jnp.where(tri_st, jnp.concatenate(da_heads, axis=0), 0.0).astype(BF16)

            dv_heads = [_dot(a_st[h * CH:(h + 1) * CH, :], do[:, h * HV:(h + 1) * HV], TN) for h in range(HEADS)]
            dv = jnp.concatenate(dv_heads, axis=1) + _dot(k_out_b, ds_b, NT)

            x = _dot(da_st, k_in_b, NN)
            dq_in = _dot(do, s_prev, NN)
            for h in range(HEADS):
                dq_in = dq_in + jnp.where(head_masks[h], x[h * CH:(h + 1) * CH, :], 0.0)
            dk_in = _dot(da_st, t["q_st"], TN)
            dk_out = _dot(v, ds_b, NN)
            d_decay = jnp.sum(ds * s_prev.astype(F32), axis=0, keepdims=True)
            ds_ref[...] = ds * t["decay"] + jnp.where(blockmask, _dot(do, q_b, TN), 0.0)

            dq = dq_in * t["e"] * (HK ** -0.5)
            dk = dk_in * t["einv"] + dk_out * t["eout"]
            dko_ko = dk_out * k_out
            dcum_of[c] = dq_in * q_in - dk_in * k_in - dko_ko
            dcl = jnp.sum(dko_ko, axis=0, keepdims=True) + d_decay * t["decay"]
            dcl_of[c] = jnp.broadcast_to(dcl, (CH, DK))
            if with_prev:
                dq = dq + pq_ref[rows, :].astype(F32)
                dk = dk + pk_ref[rows, :].astype(F32)
                dv = dv + pv_ref[rows, :].astype(F32)
            put("q", rows, dq.astype(BF16))
            put("k", rows, dk.astype(BF16))
            put("v", rows, dv.astype(BF16))

        da = _dot_hilo(tri_t_b, jnp.concatenate(dcum_of, axis=0)) + jnp.concatenate(dcl_of, axis=0)
        dpre = da * terms["sig_neg"] * (1.0 / 16.0)
        dpre_b = dpre.astype(BF16)
        dlr = _dot(dpre_b, gp_ref[...], NT)
        dg_ref[...] += _dot(lr_ref[...], dpre_b, TN)
        db_ref[...] += jnp.sum(dpre, axis=0, keepdims=True)
        if with_prev:
            dlr = dlr + plr_ref[...].astype(F32)
        put("lr", slice(None), dlr.astype(BF16))

    def col(width, c):
        return pl.BlockSpec((tb, width), lambda i: (blk(i), c))

    in_specs = [col(DK, COL_Q), col(DK, COL_K), col(DG, COL_V), col(LR_BLK, COL_LR)] + _gate_specs(l) + [
        pl.BlockSpec((ncb, DG, DK), lambda i: (blk(i), 0, 0)), col(DG, 0)]
    args = [p, p, p, p, gpad, bias, sprev, d_o]
    tiles = [col(DK, 0), col(DK, 0), col(DG, 0), col(LR_BLK, 0)]
    shapes = [jax.ShapeDtypeStruct((L, DK), BF16), jax.ShapeDtypeStruct((L, DK), BF16),
              jax.ShapeDtypeStruct((L, DG), BF16), jax.ShapeDtypeStruct((L, LR_BLK), BF16)]
    if with_prev:
        in_specs += tiles + [col(DC, 0)] * 4
        args += list(prev)
        tiles, shapes = [col(D_INP, 0)], [jax.ShapeDtypeStruct((L, D_INP), BF16)]
    return pl.pallas_call(
        body, name=name, grid=(nb,), in_specs=in_specs,
        out_specs=tiles + [pl.BlockSpec((LR_BLK, DK), lambda i: (0, 0)), pl.BlockSpec((1, DK), lambda i: (0, 0))],
        out_shape=shapes + [jax.ShapeDtypeStruct((LR_BLK, DK), F32), jax.ShapeDtypeStruct((1, DK), F32)],
        scratch_shapes=[pltpu.VMEM((DG, DK), F32)],
        compiler_params=_params(("arbitrary",)),
    )(*args)


def _mixer_out(p, conv_a, gh, l, o_tot, name):
    L = p.shape[0]
    tm = min(TILE_TOKENS, L)
    n = L // tm
    pmap, nmap = _halo_maps(tm, L)

    def body(gb_ref, gc_ref, gcp_ref, gcn_ref, gv_ref, gvp_ref, gvn_ref, go_ref, cw_ref, o_ref, gh_ref, y_ref):
        ps, ns = _edge_scales(pl.program_id(0), n)
        z = gc_ref[...].astype(F32) * gv_ref[...].astype(F32)
        zp = gcp_ref[...].astype(F32) * gvp_ref[...].astype(F32) * ps
        zn = gcn_ref[...].astype(F32) * gvn_ref[...].astype(F32) * ns
        z_dn, z_up = _shift(z, zp, zn)
        conv = cw_ref[0:1, :] * z_dn + cw_ref[1:2, :] * z + cw_ref[2:3, :] * z_up
        y_ref[:, 0:DC] = (gb_ref[...].astype(F32) * conv).astype(BF16)
        o = o_ref[...]
        go = go_ref[...].astype(F32)
        for h in range(HEADS):
            oh = o[:, h * HV:(h + 1) * HV]
            on = oh * _rstd(oh) * gh_ref[...]
            act, _ = _silu_parts(go[:, h * HV:(h + 1) * HV])
            y_ref[:, DC + h * HV:DC + (h + 1) * HV] = (act * on).astype(BF16)

    def main(c):
        return pl.BlockSpec((tm, DC), lambda i: (i, c))

    def halo(c, imap):
        return pl.BlockSpec((HB, DC), lambda i: (imap(i), c))

    return pl.pallas_call(
        body, name=name, grid=(n,),
        in_specs=[main(COL_GB), main(COL_GC), halo(COL_GC, pmap), halo(COL_GC, nmap),
                  main(COL_GV), halo(COL_GV, pmap), halo(COL_GV, nmap), main(COL_GO),
                  pl.BlockSpec((None, 3, DC), lambda i: (l, 0, 0)), pl.BlockSpec((tm, DG), lambda i: (i, 0)),
                  _gain_spec(l, HV)],
        out_specs=pl.BlockSpec((tm, D), lambda i: (i, 0)), out_shape=jax.ShapeDtypeStruct((L, D), BF16),
        compiler_params=_params(("parallel",)),
    )(p, p, p, p, p, p, p, p, conv_a, o_tot, gh)


def _mixer_out_bwd(p, conv_a, gh, l, o_tot, dy, name):
    L = p.shape[0]
    tm = min(TILE_TOKENS, L)
    n = L // tm
    pmap, nmap = _halo_maps(tm, L)

    def body(gb_ref, gbp_ref, gbn_ref, gc_ref, gcp_ref, gcn_ref, gv_ref, gvp_ref, gvn_ref, go_ref, cw_ref, o_ref,
             gh_ref, dy_ref, dyp_ref, dyn_ref, dgb_ref, dgc_ref, dgv_ref, dgo_ref, do_ref, dcw_ref, dgh_ref):
        i = pl.program_id(0)
        ps, ns = _edge_scales(i, n)
        gb = gb_ref[...].astype(F32)
        gc = gc_ref[...].astype(F32)
        gv = gv_ref[...].astype(F32)
        z = gc * gv
        zp = gcp_ref[...].astype(F32) * gvp_ref[...].astype(F32) * ps
        zn = gcn_ref[...].astype(F32) * gvn_ref[...].astype(F32) * ns
        z_dn, z_up = _shift(z, zp, zn)
        w0, w1, w2 = cw_ref[0:1, :], cw_ref[1:2, :], cw_ref[2:3, :]
        conv = w0 * z_dn + w1 * z + w2 * z_up
        dya = dy_ref[:, 0:DC].astype(F32)
        dgb_ref[...] = (dya * conv).astype(BF16)
        dc = dya * gb
        dcp = dyp_ref[...].astype(F32) * gbp_ref[...].astype(F32) * ps
        dcn = dyn_ref[...].astype(F32) * gbn_ref[...].astype(F32) * ns
        dc_dn, dc_up = _shift(dc, dcp, dcn)
        dz = w0 * dc_up + w1 * dc + w2 * dc_dn
        dgc_ref[...] = (dz * gv).astype(BF16)
        dgv_ref[...] = (dz * gc).astype(BF16)
        dcw = [jnp.sum(zs * dc, axis=0, keepdims=True) for zs in (z_dn, z, z_up)]

        o = o_ref[...]
        go = go_ref[...].astype(F32)
        dgh = jnp.zeros((1, HV), F32)
        for h in range(HEADS):
            sl = slice(h * HV, (h + 1) * HV)
            oh = o[:, sl]
            r = _rstd(oh)
            act, sg = _silu_parts(go[:, sl])
            dyb = dy_ref[:, DC + h * HV:DC + (h + 1) * HV].astype(F32)
            on = oh * r * gh_ref[...]
            dgo_ref[:, sl] = (dyb * on * (sg + act * (1.0 - sg))).astype(BF16)
            don = dyb * act
            zz = don * gh_ref[...]
            do_ref[:, sl] = (r * zz - oh * (r * r * r) * jnp.mean(oh * zz, axis=-1, keepdims=True)).astype(BF16)
            dgh = dgh + jnp.sum(don * oh * r, axis=0, keepdims=True)

        @pl.when(i == 0)
        def _():
            dcw_ref[...] = jnp.zeros_like(dcw_ref)
            dgh_ref[...] = jnp.zeros_like(dgh_ref)

        for kk in range(3):
            dcw_ref[kk:kk + 1, :] += dcw[kk]
        dgh_ref[...] += dgh

    def main(c):
        return pl.BlockSpec((tm, DC), lambda i: (i, c))

    def halo(c, imap):
        return pl.BlockSpec((HB, DC), lambda i: (imap(i), c))

    tile = pl.BlockSpec((tm, DC), lambda i: (i, 0))
    return pl.pallas_call(
        body, name=name, grid=(n,),
        in_specs=[main(COL_GB), halo(COL_GB, pmap), halo(COL_GB, nmap), main(COL_GC), halo(COL_GC, pmap),
                  halo(COL_GC, nmap), main(COL_GV), halo(COL_GV, pmap), halo(COL_GV, nmap), main(COL_GO),
                  pl.BlockSpec((None, 3, DC), lambda i: (l, 0, 0)), tile, _gain_spec(l, HV),
                  pl.BlockSpec((tm, D), lambda i: (i, 0)), halo(0, pmap), halo(0, nmap)],
        out_specs=[tile, tile, tile, tile, tile, pl.BlockSpec((3, DC), lambda i: (0, 0)),
                   pl.BlockSpec((1, HV), lambda i: (0, 0))],
        out_shape=[jax.ShapeDtypeStruct((L, DC), BF16)] * 5
        + [jax.ShapeDtypeStruct((3, DC), F32), jax.ShapeDtypeStruct((1, HV), F32)],
        compiler_params=_params(("arbitrary",)),
    )(p, p, p, p, p, p, p, p, p, p, conv_a, o_tot, gh, dy, dy, dy)


def _ffn_specs(tm, L, l, row_axis, sh_axis):
    pmap, nmap = _halo_maps(tm, L)

    def u(off, imap=None, rows=tm):
        if imap is None:
            return pl.BlockSpec((None, rows, SH_FF), lambda *g: (g[sh_axis] + off, g[row_axis], 0))
        return pl.BlockSpec((None, rows, SH_FF), lambda *g: (g[sh_axis] + off, imap(g[row_axis]), 0))

    def cw(off):
        return pl.BlockSpec((None, None, 3, SH_FF), lambda *g: (l, g[sh_axis] + off, 0, 0))

    u_specs = [u(0), u(0, pmap, HB), u(0, nmap, HB), u(FF_HALF), u(FF_HALF, pmap, HB), u(FF_HALF, nmap, HB)]
    return u_specs, [cw(0), cw(FF_HALF)]


def _conv3(x_ref, xp_ref, xn_ref, cw_ref, ps, ns):
    x = x_ref[...].astype(F32)
    x_dn, x_up = _shift(x, xp_ref[...].astype(F32) * ps, xn_ref[...].astype(F32) * ns)
    return cw_ref[0:1, :] * x_dn + cw_ref[1:2, :] * x + cw_ref[2:3, :] * x_up, (x_dn, x, x_up)


def _carry(body, *, name, grid, args, in_specs, out_specs, out_shape, scratch, comm):
    n_in, n_out = len(args), len(out_shape)
    n_ci = len(comm.ins) if comm else 0
    n_co = len(comm.out_shapes) if comm else 0

    def wrapped(*refs):
        ins, refs = refs[:n_in], refs[n_in:]
        ci, refs = refs[:n_ci], refs[n_ci:]
        outs, refs = refs[:n_out], refs[n_out:]
        co, refs = refs[:n_co], refs[n_co:]
        if comm:
            sems, refs = refs[:3], refs[3:]
            step = 0
            for ax, size in enumerate(grid):
                step = step * size + pl.program_id(ax)

            @pl.when(step == 0)
            def _():
                comm.start(ci, co, *sems)

        body(ins, outs, refs)
        if comm:
            @pl.when(step == math.prod(grid) - 1)
            def _():
                comm.wait(ci, co, *sems)

    return pl.pallas_call(
        wrapped, name=name, grid=grid, in_specs=list(in_specs) + [HBM_SPEC] * n_ci,
        out_specs=list(out_specs) + [HBM_SPEC] * n_co,
        out_shape=list(out_shape) + list(comm.out_shapes if comm else ()),
        scratch_shapes=([pltpu.SemaphoreType.DMA((s,)) for s in comm.sems] if comm else []) + list(scratch),
        input_output_aliases={n_in + i: n_out + o for i, o in comm.aliases.items()} if comm else {},
        compiler_params=_params(("arbitrary",) * len(grid)),
    )(*args, *(comm.ins if comm else ()))


def _ffn_act(u8, cw, l, name):
    L = u8.shape[1]
    tm = min(TILE_FFN, L)
    n = L // tm
    u_specs, cw_specs = _ffn_specs(tm, L, l, 0, 1)

    def body(ins, outs, scratch):
        g_ref, gp_ref, gn_ref, v_ref, vp_ref, vn_ref, cwg_ref, cwv_ref = ins
        ps, ns = _edge_scales(pl.program_id(0), n)
        gate, _ = _conv3(g_ref, gp_ref, gn_ref, cwg_ref, ps, ns)
        val, _ = _conv3(v_ref, vp_ref, vn_ref, cwv_ref, ps, ns)
        act, _ = _silu_parts(gate)
        outs[0][...] = (act * val).astype(BF16)

    return _carry(
        body, name=name, grid=(n, FF_HALF), args=(u8, u8, u8, u8, u8, u8, cw, cw), in_specs=u_specs + cw_specs,
        out_specs=[pl.BlockSpec((None, tm, SH_FF), lambda i, d: (d, i, 0))],
        out_shape=[jax.ShapeDtypeStruct((FF_HALF, L, SH_FF), BF16)], scratch=[], comm=None)[0]


def _ffn_act_bwd(u8, cw, l, da, name):
    L = u8.shape[1]
    tm = min(TILE_FFN, L)
    n = L // tm
    u_specs, cw_specs = _ffn_specs(tm, L, l, 1, 0)

    def body(ins, outs, scratch):
        g_ref, gp_ref, gn_ref, v_ref, vp_ref, vn_ref, cwg_ref, cwv_ref, da_ref = ins
        du_ref, dcw_ref = outs
        i = pl.program_id(1)
        ps, ns = _edge_scales(i, n)
        gate, g_sh = _conv3(g_ref, gp_ref, gn_ref, cwg_ref, ps, ns)
        val, v_sh = _conv3(v_ref, vp_ref, vn_ref, cwv_ref, ps, ns)
        act, sg = _silu_parts(gate)
        da_f = da_ref[...].astype(F32)
        dgate = da_f * val * (sg + act * (1.0 - sg))
        dval = da_f * act
        du_ref[0] = dgate.astype(BF16)
        du_ref[1] = dval.astype(BF16)

        @pl.when(i == 0)
        def _():
            dcw_ref[...] = jnp.zeros_like(dcw_ref)

        for kk in range(3):
            dcw_ref[0, kk:kk + 1, :] += jnp.sum(g_sh[kk] * dgate, axis=0, keepdims=True)
            dcw_ref[1, kk:kk + 1, :] += jnp.sum(v_sh[kk] * dval, axis=0, keepdims=True)

    return _carry(
        body, name=name, grid=(FF_HALF, n), args=(u8, u8, u8, u8, u8, u8, cw, cw, da),
        in_specs=u_specs + cw_specs + [pl.BlockSpec((None, tm, SH_FF), lambda d, i: (d, i, 0))],
        out_specs=[pl.BlockSpec((2, None, tm, SH_FF), lambda d, i: (0, d, i, 0)),
                   pl.BlockSpec((2, None, 3, SH_FF), lambda d, i: (0, d, 0, 0))],
        out_shape=[jax.ShapeDtypeStruct((2, FF_HALF, L, SH_FF), BF16),
                   jax.ShapeDtypeStruct((2, FF_HALF, 3, SH_FF), F32)],
        scratch=[], comm=None)


def _ffn_conv_t(du8, cw, l, name):
    L = du8.shape[1]
    tm = min(TILE_FFN, L)
    n = L // tm
    pmap, nmap = _halo_maps(tm, L)

    def body(ins, outs, scratch):
        x_ref, xp_ref, xn_ref, cw_ref = ins
        ps, ns = _edge_scales(pl.program_id(1), n)
        x = x_ref[...].astype(F32)
        x_dn, x_up = _shift(x, xp_ref[...].astype(F32) * ps, xn_ref[...].astype(F32) * ns)
        outs[0][...] = (cw_ref[0:1, :] * x_up + cw_ref[1:2, :] * x + cw_ref[2:3, :] * x_dn).astype(BF16)

    return _carry(
        body, name=name, grid=(N_DEV, n), args=(du8, du8, du8, cw),
        in_specs=[pl.BlockSpec((None, tm, SH_FF), lambda d, i: (d, i, 0)),
                  pl.BlockSpec((None, HB, SH_FF), lambda d, i: (d, pmap(i), 0)),
                  pl.BlockSpec((None, HB, SH_FF), lambda d, i: (d, nmap(i), 0)),
                  pl.BlockSpec((None, None, 3, SH_FF), lambda d, i: (l, d, 0, 0))],
        out_specs=[pl.BlockSpec((None, tm, SH_FF), lambda d, i: (d, i, 0))],
        out_shape=[jax.ShapeDtypeStruct((N_DEV, L, SH_FF), BF16)], scratch=[], comm=None)[0]


def _loss_grad(xl, target, name):
    L = xl.shape[0]
    tm = min(TILE_NORM, L)

    def body(x_ref, t_ref, dx_ref, sq_ref):
        i = pl.program_id(0)
        err = x_ref[...] - t_ref[...]
        dx_ref[...] = err * (1.0 / D)
        part = jnp.sum(err * err, axis=0, keepdims=True)

        @pl.when(i == 0)
        def _():
            sq_ref[...] = part

        @pl.when(i > 0)
        def _():
            sq_ref[...] += part

    tile = pl.BlockSpec((tm, D), lambda i: (i, 0))
    return pl.pallas_call(
        body, name=name, grid=(L // tm,), in_specs=[tile, tile],
        out_specs=[tile, pl.BlockSpec((1, D), lambda i: (0, 0))],
        out_shape=[jax.ShapeDtypeStruct((L, D), F32), jax.ShapeDtypeStruct((1, D), F32)],
        compiler_params=_params(("arbitrary",)),
    )(xl, target)


MESH = pl.DeviceIdType.MESH
HBM_SPEC = pl.BlockSpec(memory_space=pltpu.HBM)


def _position():
    return lax.axis_index("x"), lax.axis_index("y"), lax.axis_index("c")


def _other_chips(x, y):
    return [(1 - x, y), (x, 1 - y), (1 - x, 1 - y)]


def _all_gather(shards, name):
    n = len(shards)

    def body(*refs):
        x_refs, out_refs = refs[:n], refs[n:2 * n]
        send_sems, recv_sems, local_sems = refs[2 * n:]
        x, y, c = _position()
        me, sibling = (x, y, c), (x, y, 1 - c)
        chips = _other_chips(x, y)

        def slot(t, px, py, pc):
            return out_refs[t].at[:, 4 * px + 2 * py + pc]

        def copy(t, k, block, to, from_input=False):
            return pltpu.make_async_remote_copy(
                src_ref=x_refs[t] if from_input else slot(t, *block), dst_ref=slot(t, *block),
                send_sem=send_sems.at[k * n + t], recv_sem=recv_sems.at[k * n + t], device_id=to, device_id_type=MESH)

        mine = [pltpu.make_async_copy(x_refs[t], slot(t, *me), local_sems.at[t]) for t in range(n)]
        for cp in mine:
            cp.start()
        first = [copy(t, 0, me, sibling, True) for t in range(n)]
        first += [copy(t, 1 + j, me, (*chip, c), True) for j, chip in enumerate(chips) for t in range(n)]
        for cp in first:
            cp.start()
        passed = []
        for j, chip in enumerate(chips):
            for t in range(n):
                copy(t, 1 + j, (*chip, c), me).wait_recv()
                passed.append(copy(t, 4 + j, (*chip, c), sibling))
                passed[-1].start()
        for t in range(n):
            copy(t, 0, sibling, me).wait_recv()
        for j, chip in enumerate(chips):
            for t in range(n):
                copy(t, 4 + j, (*chip, 1 - c), me).wait_recv()
        for cp in first + passed:
            cp.wait_send()
        for cp in mine:
            cp.wait()

    return pl.pallas_call(
        body, name=name,
        out_shape=[jax.ShapeDtypeStruct((s.shape[0], N_DEV) + s.shape[1:], s.dtype) for s in shards],
        in_specs=[HBM_SPEC] * n, out_specs=[HBM_SPEC] * n,
        scratch_shapes=[pltpu.SemaphoreType.DMA((7 * n,)), pltpu.SemaphoreType.DMA((7 * n,)),
                        pltpu.SemaphoreType.DMA((n,))],
    )(*shards)


def _exchange_sibling(grads, name):
    n = len(grads)

    def body(*refs):
        g_refs, out_refs, send_sems, recv_sems = refs[:n], refs[n:2 * n], refs[2 * n], refs[2 * n + 1]
        x, y, c = _position()
        copies = [pltpu.make_async_remote_copy(
            src_ref=g_refs[t].at[:, 2 * k + (1 - c)], dst_ref=out_refs[t].at[:, k], send_sem=send_sems.at[k * n + t],
            recv_sem=recv_sems.at[k * n + t], device_id=(x, y, 1 - c), device_id_type=MESH)
            for k in range(N_CHIP) for t in range(n)]
        for cp in copies:
            cp.start()
        for cp in copies:
            cp.wait()

    return pl.pallas_call(
        body, name=name,
        out_shape=[jax.ShapeDtypeStruct((g.shape[0], N_CHIP) + g.shape[2:], g.dtype) for g in grads],
        in_specs=[HBM_SPEC] * n, out_specs=[HBM_SPEC] * n,
        scratch_shapes=[pltpu.SemaphoreType.DMA((N_CHIP * n,)), pltpu.SemaphoreType.DMA((N_CHIP * n,))],
    )(*grads)


class _Comm(NamedTuple):
    ins: tuple
    out_shapes: tuple
    aliases: dict
    sems: tuple
    start: Callable
    wait: Callable


def _comm_of(ins, out_shapes, aliases, sems, copies):
    def start(ci, co, send, recv, local):
        for cp in copies(ci, co, send, recv, local):
            cp.start()

    def wait(ci, co, send, recv, local):
        for cp in copies(ci, co, send, recv, local):
            cp.wait()

    return _Comm(tuple(ins), tuple(out_shapes), aliases, sems, start, wait)


def _remote(src, dst, send, recv, idx, to):
    return pltpu.make_async_remote_copy(src_ref=src, dst_ref=dst, send_sem=send.at[idx], recv_sem=recv.at[idx],
                                        device_id=to, device_id_type=MESH)


def _gather_ici_comm(shards):
    n = len(shards)

    def copies(ci, co, send, recv, local):
        x, y, c = _position()
        me = 4 * x + 2 * y + c
        mine = [pltpu.make_async_copy(ci[t], co[t].at[:, me], local.at[t]) for t in range(n)]
        return mine + [_remote(ci[t], co[t].at[:, me], send, recv, j * n + t, (cx, cy, c))
                       for j, (cx, cy) in enumerate(_other_chips(x, y)) for t in range(n)]

    outs = [jax.ShapeDtypeStruct((1, N_DEV) + s.shape[1:], s.dtype) for s in shards]
    return _comm_of(shards, outs, {}, (3 * n, 3 * n, n), copies)


def _gather_d2d_comm(partials):
    n = len(partials)

    def copies(ci, co, send, recv, local):
        x, y, c = _position()
        return [_remote(co[t].at[:, 4 * cx + 2 * cy + c], co[t].at[:, 4 * cx + 2 * cy + c], send, recv, k * n + t,
                        (x, y, 1 - c))
                for k, (cx, cy) in enumerate([(x, y)] + _other_chips(x, y)) for t in range(n)]

    outs = [jax.ShapeDtypeStruct(p.shape, p.dtype) for p in partials]
    return _comm_of(partials, outs, {t: t for t in range(n)}, (N_CHIP * n, N_CHIP * n, 1), copies)


def _grads_d2d_comm(grads):
    n = len(grads)

    def copies(ci, co, send, recv, local):
        x, y, c = _position()
        return [_remote(ci[t].at[:, 2 * k + (1 - c)], co[t].at[:, k], send, recv, k * n + t, (x, y, 1 - c))
                for k in range(N_CHIP) for t in range(n)]

    outs = [jax.ShapeDtypeStruct((g.shape[0], N_CHIP) + g.shape[2:], g.dtype) for g in grads]
    return _comm_of(grads, outs, {}, (N_CHIP * n, N_CHIP * n, 1), copies)


def _grads_ici_comm(parts):
    n = len(parts)

    def copies(ci, co, send, recv, local):
        x, y, c = _position()
        my_chip = 2 * x + y
        mine = [pltpu.make_async_copy(ci[t].at[:, my_chip], co[t].at[:, my_chip], local.at[t]) for t in range(n)]
        return mine + [_remote(ci[t].at[:, 2 * cx + cy], co[t].at[:, my_chip], send, recv, j * n + t, (cx, cy, c))
                       for j, (cx, cy) in enumerate(_other_chips(x, y)) for t in range(n)]

    outs = [jax.ShapeDtypeStruct(p.shape, p.dtype) for p in parts]
    return _comm_of(parts, outs, {}, (3 * n, 3 * n, n), copies)


def _row_tile(rows):
    return 256 if rows % 256 == 0 else rows


def _pair_sum(g, recv, c_idx, out_dtype, name):
    lay, _, rows, cols = g.shape
    tr = _row_tile(rows)

    def body(c_ref, g_ref, r_ref, o_ref):
        o_ref[...] = (g_ref[...] + r_ref[...]).astype(o_ref.dtype)

    def spec(blk_of):
        return pl.BlockSpec((None, None, tr, cols), lambda l, k, r, c_ref: (l, blk_of(k, c_ref), r, 0))

    return pl.pallas_call(
        body, name=name,
        grid_spec=pltpu.PrefetchScalarGridSpec(
            num_scalar_prefetch=1, grid=(lay, N_CHIP, rows // tr),
            in_specs=[spec(lambda k, c_ref: 2 * k + c_ref[0]), spec(lambda k, c_ref: k)],
            out_specs=spec(lambda k, c_ref: k)),
        out_shape=jax.ShapeDtypeStruct((lay, N_CHIP, rows, cols), out_dtype),
        compiler_params=_params(("parallel", "parallel", "parallel")),
    )(c_idx, g, recv)


def _exchange_chips(parts, name):
    n = len(parts)

    def body(*refs):
        p_refs, out_refs = refs[:n], refs[n:2 * n]
        send_sems, recv_sems, local_sems = refs[2 * n:]
        x, y, c = _position()
        my_chip = 2 * x + y
        mine = [pltpu.make_async_copy(p_refs[t].at[:, my_chip], out_refs[t].at[:, my_chip], local_sems.at[t])
                for t in range(n)]
        for cp in mine:
            cp.start()
        copies = [pltpu.make_async_remote_copy(
            src_ref=p_refs[t].at[:, 2 * cx + cy], dst_ref=out_refs[t].at[:, my_chip], send_sem=send_sems.at[j * n + t],
            recv_sem=recv_sems.at[j * n + t], device_id=(cx, cy, c), device_id_type=MESH)
            for j, (cx, cy) in enumerate(_other_chips(x, y)) for t in range(n)]
        for cp in copies:
            cp.start()
        for cp in copies:
            cp.wait()
        for cp in mine:
            cp.wait()

    return pl.pallas_call(
        body, name=name, out_shape=[jax.ShapeDtypeStruct(p.shape, p.dtype) for p in parts],
        in_specs=[HBM_SPEC] * n, out_specs=[HBM_SPEC] * n,
        scratch_shapes=[pltpu.SemaphoreType.DMA((3 * n,)), pltpu.SemaphoreType.DMA((3 * n,)),
                        pltpu.SemaphoreType.DMA((n,))],
    )(*parts)


def _sum_adamw(parts, w, m, v, name):
    lay, rows, cols = w.shape
    tr = _row_tile(rows)

    def body(p_ref, w_ref, m_ref, v_ref, g_ref, d_ref, nm_ref, nv_ref):
        g = ((p_ref[0].astype(F32) + p_ref[1].astype(F32)) + p_ref[2].astype(F32)) + p_ref[3].astype(F32)
        g_ref[...] = g
        nm = ADAM_B1 * m_ref[...] + (1.0 - ADAM_B1) * g
        nv = ADAM_B2 * v_ref[...] + (1.0 - ADAM_B2) * (g * g)
        nm_ref[...] = nm
        nv_ref[...] = nv
        m_hat = nm / (1.0 - ADAM_B1 ** ADAM_STEP)
        v_hat = nv / (1.0 - ADAM_B2 ** ADAM_STEP)
        d_ref[...] = -ADAM_LR * (m_hat / (jnp.sqrt(v_hat) + ADAM_EPS) + ADAM_WD * w_ref[...])

    tile = pl.BlockSpec((None, tr, cols), lambda l, r: (l, r, 0))
    return pl.pallas_call(
        body, name=name, grid=(lay, rows // tr),
        in_specs=[pl.BlockSpec((None, N_CHIP, tr, cols), lambda l, r: (l, 0, r, 0)), tile, tile, tile],
        out_specs=[tile] * 4, out_shape=[jax.ShapeDtypeStruct((lay, rows, cols), F32)] * 4,
        compiler_params=_params(("parallel", "parallel")),
    )(parts, w, m, v)


def _pad_rows(flat, rows):
    return jnp.pad(flat, (0, rows * LANES - flat.shape[0])).reshape(rows, LANES)


def _pack_small(tree):
    sh = jnp.concatenate([tree[n].reshape(-1) for n, _, _ in SMALL_SHARDED])
    rep = jnp.concatenate([tree[n].reshape(-1) for n, _ in REPLICATED])
    return jnp.concatenate([_pad_rows(sh, ROWS_SSH), _pad_rows(rep, ROWS_REP)], axis=0)


def _unpack_small(buf):
    out = {}
    for flat, items in ((buf[:ROWS_SSH].reshape(-1), [(n, s) for n, s, _ in SMALL_SHARDED]),
                        (buf[ROWS_SSH:].reshape(-1), REPLICATED)):
        off = 0
        for n, s in items:
            out[n] = flat[off:off + math.prod(s)].reshape(s)
            off += math.prod(s)
    return out


def _full_from_blocks(blocks, s, ax):
    return jnp.concatenate([blocks[d] for d in range(N_DEV)], axis=ax)


def _blocks_from_full(full, s, ax):
    return jnp.stack([lax.slice_in_dim(full, d * s[ax], (d + 1) * s[ax], axis=ax) for d in range(N_DEV)])


def _pack_small_grads(sharded_blocks, replicated):
    sh = jnp.concatenate([sharded_blocks[n].reshape(N_DEV, -1) for n, _, _ in SMALL_SHARDED], axis=1)
    sh = jnp.pad(sh, ((0, 0), (0, ROWS_SSH * LANES - sh.shape[1]))).reshape(N_DEV, ROWS_SSH, LANES)
    rep = _pad_rows(jnp.concatenate([replicated[n].reshape(-1) for n, _ in REPLICATED]), ROWS_REP)
    return jnp.concatenate([sh, jnp.broadcast_to(rep[None], (N_DEV, ROWS_REP, LANES))], axis=1)


def _layer_fwd(x, h1, wts, big, l, l_next, next_shards):
    L = x.shape[0]
    tm = min(TILE_MM, L)
    nt = L // tm
    rest_next = [s for n, s in zip(BIG, next_shards) if n != "w_up"] if next_shards else None
    p = _mm(h1, big["w_in"], dims=NN, grid=(nt, D_INP // 640, 1),
            a_spec=((tm, D), lambda i, j, k: (i, 0)), b_spec=((None, D, 640), lambda i, j, k: (0, 0, j)),
            o_spec=((tm, 640), lambda i, j, k: (i, j)), out_shape=jax.ShapeDtypeStruct((L, D_INP), BF16),
            tile=(tm, 640), name="proj_in", comm=_gather_ici_comm(rest_next) if next_shards else None)
    if next_shards:
        p, rest_next = p[0], p[1:]
    o_f, sp_f = _gla_fwd(p, wts["gpad_f"], wts["bias_f"], l, None, True, "gla_fwd_f")
    o_tot, sp_b = _gla_fwd(p, wts["gpad_b"], wts["bias_b"], l, o_f, False, "gla_fwd_b")
    y_cat = _mixer_out(p, wts["conv_a"], wts["gh"], l, o_tot, "mixer_out")
    y = _mm(y_cat, big["w_out"], dims=NN, grid=(nt, 1, 1),
            a_spec=((tm, D), lambda i, j, k: (i, 0)), b_spec=((None, D, D), lambda i, j, k: (0, 0, 0)),
            o_spec=((tm, D), lambda i, j, k: (i, 0)), out_shape=jax.ShapeDtypeStruct((L, D), BF16),
            tile=(tm, D), name="proj_out")
    x1, h2 = _post_pre(x, y, wts["g2"], l, wts["g3"], l, "post_pre_mix")
    u8 = _mm(h2, big["w_up"], dims=NN, grid=(nt, N_DEV, 1),
             a_spec=((tm, D), lambda i, j, k: (i, 0)), b_spec=((None, None, D, SH_FF), lambda i, j, k: (0, j, 0, 0)),
             o_spec=((None, tm, SH_FF), lambda i, j, k: (j, i, 0)),
             out_shape=jax.ShapeDtypeStruct((N_DEV, L, SH_FF), BF16), tile=(tm, SH_FF), name="ffn_up",
             comm=_gather_ici_comm([next_shards[BIG.index("w_up")]]) if next_shards else None)
    gathered = None
    if next_shards:
        u8, up_next = u8[0], u8[1]
        gathered = [rest_next[0], rest_next[1], up_next, rest_next[2]]
    a = _ffn_act(u8, wts["cw"], l, "ffn_act")
    tm1 = min(TILE_MM_KIN, L)
    y2 = _mm(a, big["w_down"], dims=NN, grid=(L // tm1, 1, 1), kin=FF_HALF,
             a_spec=((FF_HALF, tm1, SH_FF), lambda i, j, k: (0, i, 0)),
             b_spec=((None, FF_HALF, SH_FF, D), lambda i, j, k: (0, 0, 0, 0)),
             o_spec=((tm1, D), lambda i, j, k: (i, 0)), out_shape=jax.ShapeDtypeStruct((L, D), BF16),
             tile=(tm1, D), name="ffn_down", comm=_gather_d2d_comm(gathered) if next_shards else None)
    if next_shards:
        y2, gathered = y2[0], y2[1:]
    x2, h1_next = _post_pre(x1, y2, wts["g4"], l, wts["g1"], l_next, "post_pre_ffn")
    saved = dict(x=x, h1=h1, p=p, o_tot=o_tot, sp_f=sp_f, sp_b=sp_b, y_cat=y_cat, y=y, x1=x1, h2=h2, u8=u8, a=a, y2=y2,
                 big=big)
    return x2, h1_next, saved, gathered


def _layer_bwd(dx2, wts, s, l, pending, c_idx, early_ffn):
    L = dx2.shape[0]
    big = s["big"]
    tm = min(TILE_MM, L)
    nt = L // tm
    tm1 = min(TILE_MM_KIN, L)
    tk = min(TILE_MM_TOKENS, L)
    nkt = L // tk
    dy2, dg4 = _norm_bwd(s["y2"], wts["g4"], l, dx2, None, "norm_bwd_ffn_post")
    da = _mm(dy2, big["w_down"], dims=NT, grid=(nt, FF_HALF, 1),
             a_spec=((tm, D), lambda i, j, k: (i, 0)), b_spec=((None, None, SH_FF, D), lambda i, j, k: (0, j, 0, 0)),
             o_spec=((None, tm, SH_FF), lambda i, j, k: (j, i, 0)),
             out_shape=jax.ShapeDtypeStruct((FF_HALF, L, SH_FF), BF16), tile=(tm, SH_FF), name="ffn_down_dx",
             comm=_grads_d2d_comm(pending) if pending else None)
    pairs = None
    if pending:
        da, from_sibling = da[0], da[1:]
        pairs = [_pair_sum(g, r, c_idx, BF16, "grads_pair_sum") for g, r in zip(pending, from_sibling)]
    dw_down = _mm(s["a"], dy2, dims=TN, grid=(FF_HALF, 1, nkt),
                  a_spec=((None, tk, SH_FF), lambda i, j, k: (i, k, 0)), b_spec=((tk, D), lambda i, j, k: (k, 0)),
                  o_spec=((SH_FF, D), lambda i, j, k: (i, 0)), out_shape=jax.ShapeDtypeStruct((DFF, D), F32),
                  tile=(SH_FF, D), name="ffn_down_dw")
    du, dcw = _ffn_act_bwd(s["u8"], wts["cw"], l, da, "ffn_act_bwd")
    d_u8 = _ffn_conv_t(du.reshape(N_DEV, L, SH_FF), wts["cw"], l, "ffn_conv_t")
    dh2 = _mm(d_u8, big["w_up"], dims=NT, grid=(L // tm1, 1, N_DEV // FF_HALF), kin=FF_HALF,
              a_spec=((FF_HALF, tm1, SH_FF), lambda i, j, k: (k, i, 0)),
              b_spec=((None, FF_HALF, D, SH_FF), lambda i, j, k: (0, k, 0, 0)),
              o_spec=((tm1, D), lambda i, j, k: (i, 0)), out_shape=jax.ShapeDtypeStruct((L, D), BF16),
              tile=(tm1, D), name="ffn_up_dx",
              comm=_grads_ici_comm([q for n, q in zip(BIG, pairs) if n != "w_up"]) if pending else None)
    if pending:
        dh2, rest_parts = dh2[0], dh2[1:]
    dw_up = _mm(s["h2"], d_u8, dims=TN, grid=(1, N_DEV, nkt),
                a_spec=((tk, D), lambda i, j, k: (k, 0)), b_spec=((None, tk, SH_FF), lambda i, j, k: (j, k, 0)),
                o_spec=((None, D, SH_FF), lambda i, j, k: (j, 0, 0)),
                out_shape=jax.ShapeDtypeStruct((N_DEV, D, SH_FF), F32), tile=(D, SH_FF), name="ffn_up_dw",
                comm=_grads_ici_comm([pairs[BIG.index("w_up")]]) if pending else None)
    parts = None
    if pending:
        dw_up, up_part = dw_up[0], dw_up[1]
        parts = [rest_parts[0], rest_parts[1], up_part, rest_parts[2]]
    dx1, dg3 = _norm_bwd(s["x1"], wts["g3"], l, dh2, dx2, "norm_bwd_ffn_pre")
    dy, dg2 = _norm_bwd(s["y"], wts["g2"], l, dx1, None, "norm_bwd_mix_post")
    dy_cat = _mm(dy, big["w_out"], dims=NT, grid=(nt, 1, 1),
                 a_spec=((tm, D), lambda i, j, k: (i, 0)), b_spec=((None, D, D), lambda i, j, k: (0, 0, 0)),
                 o_spec=((tm, D), lambda i, j, k: (i, 0)), out_shape=jax.ShapeDtypeStruct((L, D), BF16),
                 tile=(tm, D), name="proj_out_dx")
    dw_out = _mm(s["y_cat"], dy, dims=TN, grid=(1, 1, nkt),
                 a_spec=((tk, D), lambda i, j, k: (k, 0)), b_spec=((tk, D), lambda i, j, k: (k, 0)),
                 o_spec=((D, D), lambda i, j, k: (0, 0)), out_shape=jax.ShapeDtypeStruct((D, D), F32),
                 tile=(D, D), name="proj_out_dw")
    dgb, dgc, dgv, dgo, d_o, dconv_a, dgh = _mixer_out_bwd(s["p"], wts["conv_a"], wts["gh"], l, s["o_tot"], dy_cat,
                                                          "mixer_out_bwd")
    part_f = _gla_bwd(s["p"], wts["gpad_f"], wts["bias_f"], l, s["sp_f"], d_o, None, True, "gla_bwd_f")
    dp, dgp_b, dbias_b = _gla_bwd(s["p"], wts["gpad_b"], wts["bias_b"], l, s["sp_b"], d_o,
                                  list(part_f[:4]) + [dgb, dgc, dgv, dgo], False, "gla_bwd_b")
    early = [dw_up[None], dw_down.reshape(1, N_DEV, DFF // N_DEV, D)] if early_ffn else None
    dh1 = _mm(dp, big["w_in"], dims=NT, grid=(L // tm1, 1, 1),
              a_spec=((tm1, D_INP), lambda i, j, k: (i, 0)), b_spec=((None, D, D_INP), lambda i, j, k: (0, 0, 0)),
              o_spec=((tm1, D), lambda i, j, k: (i, 0)), out_shape=jax.ShapeDtypeStruct((L, D), BF16),
              tile=(tm1, D), name="proj_in_dx", comm=_grads_d2d_comm(early) if early_ffn else None)
    if early_ffn:
        dh1, early_sibling = dh1[0], dh1[1:]
        early = [_pair_sum(g, r, c_idx, BF16, "grads_pair_sum") for g, r in zip(early, early_sibling)]
    dw_in = _mm(s["h1"], dp, dims=TN, grid=(1, D_INP // 640, nkt),
                a_spec=((tk, D), lambda i, j, k: (k, 0)), b_spec=((tk, 640), lambda i, j, k: (k, j)),
                o_spec=((D, 640), lambda i, j, k: (0, j)), out_shape=jax.ShapeDtypeStruct((D, D_INP), F32),
                tile=(D, 640), name="proj_in_dw", comm=_grads_ici_comm(early) if early_ffn else None)
    if early_ffn:
        dw_in, early = dw_in[0], dw_in[1:]
    dx0, dg1 = _norm_bwd(s["x"], wts["g1"], l, dh1, dx1, "norm_bwd_mix_pre")
    grads = dict(
        norm_mix_pre=dg1[0], norm_mix_post=dg2[0], norm_ffn_pre=dg3[0], norm_ffn_post=dg4[0],
        gate_bias_fwd=part_f[5][0], gate_bias_bwd=dbias_b[0], gla_head_norm=dgh[0],
        w_in=_blocks_from_full(dw_in, (D, SH_IN), 1), w_out=dw_out.reshape(N_DEV, D // N_DEV, D), w_up=dw_up,
        w_down=dw_down.reshape(N_DEV, DFF // N_DEV, D),
        conv_a=_blocks_from_full(dconv_a, (3, DC // N_DEV), 1),
        gate_up_fwd=_blocks_from_full(part_f[4][:RANK], (RANK, DK // N_DEV), 1),
        gate_up_bwd=_blocks_from_full(dgp_b[RANK:2 * RANK], (RANK, DK // N_DEV), 1),
        conv_ffn=dcw.reshape(N_DEV, 3, SH_FF))
    return dx0, grads, parts, early


def _matmul_weights(g_in, g_out, g_up, g_down):
    w_in = jnp.concatenate([g_in[:, d] for d in range(N_DEV)] + [jnp.zeros((1, D, D_INP - D_IN), BF16)], axis=2)
    return dict(w_in=w_in, w_out=g_out.reshape(1, D, D), w_up=g_up, w_down=g_down.reshape(1, FF_HALF, SH_FF, D))


def _small_weights(g_small, rep):
    small = {n: jnp.moveaxis(t, 0, 1) for n, t in jax.vmap(_unpack_small)(
        jnp.concatenate([g_small[0], jnp.zeros((N_DEV, ROWS_REP, LANES), F32)], axis=1)).items()
        if n in [s[0] for s in SMALL_SHARDED]}
    conv_a = jnp.concatenate([small["conv_a"][:, d] for d in range(N_DEV)], axis=2)
    gate_f = jnp.concatenate([small["gate_up_fwd"][:, d] for d in range(N_DEV)], axis=2).astype(BF16)
    gate_b = jnp.concatenate([small["gate_up_bwd"][:, d] for d in range(N_DEV)], axis=2).astype(BF16)
    zeros = jnp.zeros((DEPTH, LR_BLK, DK), BF16)
    return dict(
        conv_a=conv_a, cw=small["conv_ffn"],
        gpad_f=zeros.at[:, :RANK].set(gate_f), gpad_b=zeros.at[:, RANK:2 * RANK].set(gate_b),
        bias_f=rep["gate_bias_fwd"][:, None, :], bias_b=rep["gate_bias_bwd"][:, None, :],
        gh=rep["gla_head_norm"][:, None, :],
        g1=rep["norm_mix_pre"][:, None, :], g2=rep["norm_mix_post"][:, None, :],
        g3=rep["norm_ffn_pre"][:, None, :], g4=rep["norm_ffn_post"][:, None, :])


def kernel(x, norm_mix_pre, norm_mix_post, norm_ffn_pre, norm_ffn_post, w_in, conv_a, gate_up_fwd, gate_bias_fwd, gate_up_bwd, gate_bias_bwd, gla_head_norm, w_out, w_up, conv_ffn, w_down, loss_target, m_norm_mix_pre, m_norm_mix_post, m_norm_ffn_pre, m_norm_ffn_post, m_w_in, m_conv_a, m_gate_up_fwd, m_gate_bias_fwd, m_gate_up_bwd, m_gate_bias_bwd, m_gla_head_norm, m_w_out, m_w_up, m_conv_ffn, m_w_down, v_norm_mix_pre, v_norm_mix_post, v_norm_ffn_pre, v_norm_ffn_post, v_w_in, v_conv_a, v_gate_up_fwd, v_gate_bias_fwd, v_gate_up_bwd, v_gate_bias_bwd, v_gla_head_norm, v_w_out, v_w_up, v_conv_ffn, v_w_down):
    w = dict(norm_mix_pre=norm_mix_pre, norm_mix_post=norm_mix_post, norm_ffn_pre=norm_ffn_pre,
             norm_ffn_post=norm_ffn_post, w_in=w_in, conv_a=conv_a, gate_up_fwd=gate_up_fwd,
             gate_bias_fwd=gate_bias_fwd, gate_up_bwd=gate_up_bwd, gate_bias_bwd=gate_bias_bwd,
             gla_head_norm=gla_head_norm, w_out=w_out, w_up=w_up, conv_ffn=conv_ffn, w_down=w_down)
    m = dict(norm_mix_pre=m_norm_mix_pre, norm_mix_post=m_norm_mix_post, norm_ffn_pre=m_norm_ffn_pre,
             norm_ffn_post=m_norm_ffn_post, w_in=m_w_in, conv_a=m_conv_a, gate_up_fwd=m_gate_up_fwd,
             gate_bias_fwd=m_gate_bias_fwd, gate_up_bwd=m_gate_up_bwd, gate_bias_bwd=m_gate_bias_bwd,
             gla_head_norm=m_gla_head_norm, w_out=m_w_out, w_up=m_w_up, conv_ffn=m_conv_ffn, w_down=m_w_down)
    v = dict(norm_mix_pre=v_norm_mix_pre, norm_mix_post=v_norm_mix_post, norm_ffn_pre=v_norm_ffn_pre,
             norm_ffn_post=v_norm_ffn_post, w_in=v_w_in, conv_a=v_conv_a, gate_up_fwd=v_gate_up_fwd,
             gate_bias_fwd=v_gate_bias_fwd, gate_up_bwd=v_gate_up_bwd, gate_bias_bwd=v_gate_bias_bwd,
             gla_head_norm=v_gla_head_norm, w_out=v_w_out, w_up=v_w_up, conv_ffn=v_conv_ffn, w_down=v_w_down)
    axes = ("x", "y", "c")
    L = x.shape[1]
    x0 = x.reshape(L, D)
    target = loss_target.reshape(L, D)

    w_small = _pack_small(w)
    w16 = {n: w[n].astype(BF16) for n in BIG}
    gathered = _all_gather([w16[n][0:1] for n in BIG] + [w_small[None, :ROWS_SSH]], "gather_weights")
    wts = _small_weights(gathered[-1], w)
    big = _matmul_weights(*gathered[:-1])

    h1 = _norm_cast(x0, wts["g1"], 0, "norm_first")
    xl, saved = x0, []
    for l in range(DEPTH):
        nxt = [w16[n][l + 1:l + 2] for n in BIG] if l + 1 < DEPTH else None
        xl, h1, s, gathered = _layer_fwd(xl, h1, wts, big, l, min(l + 1, DEPTH - 1), nxt)
        saved.append(s)
        if nxt:
            big = _matmul_weights(*gathered)
    dx, sq = _loss_grad(xl, target, "loss_grad")
    loss = lax.psum(0.5 * jnp.sum(sq) / D, axes)

    c_idx = lax.axis_index("c").astype(jnp.int32).reshape(1)
    layer_grads, layer_parts, pending = [None] * DEPTH, [None] * DEPTH, None
    for l in reversed(range(DEPTH)):
        dx, layer_grads[l], done, early = _layer_bwd(dx, wts, saved[l], l, pending, c_idx, l == 0)
        if pending:
            layer_parts[l + 1] = done
        pending = [layer_grads[l][n][None] for n in BIG]
    small_names = [n for n, _, _ in SMALL_SHARDED] + [n for n, _ in REPLICATED]
    stacked = {n: jnp.stack([g[n] for g in layer_grads]) for n in small_names}
    g_small = _pack_small_grads({n: jnp.moveaxis(stacked[n], 0, 1) for n, _, _ in SMALL_SHARDED}, stacked)
    last = [g for n, g in zip(BIG, pending) if n in ("w_in", "w_out")] + [g_small[None]]
    from_sibling = _exchange_sibling(last, "grads_to_sibling")
    pairs = [_pair_sum(g, r, c_idx, BF16 if i < 2 else F32, "grads_pair_sum")
             for i, (g, r) in enumerate(zip(last, from_sibling))]
    parts = _exchange_chips(pairs, "grads_to_chips")
    layer_parts[0] = [parts[0], parts[1], early[0], early[1]]

    results = {}
    for i, n in enumerate(BIG):
        part = jnp.concatenate([layer_parts[l][i] for l in range(DEPTH)], axis=0)
        results[n] = _sum_adamw(part, w[n], m[n], v[n], "sum_adamw")
    small = _sum_adamw(parts[-1], w_small[None], _pack_small(m)[None], _pack_small(v)[None], "sum_adamw_small")
    small = [_unpack_small(buf[0]) for buf in small]
    outs = [loss, dx.reshape(x.shape)]
    for i in range(4):
        outs += [results[n][i] if n in BIG else small[i][n] for n in WEIGHT_ORDER]
    return tuple(outs)
```

```python
import math
from typing import Callable, NamedTuple

import jax
import jax.numpy as jnp
from jax import lax
from jax.experimental import pallas as pl
from jax.experimental.pallas import tpu as pltpu

F32 = jnp.float32
BF16 = jnp.bfloat16

DEPTH = 4
D = 1024
DC = 512
DG = 512
HEADS = 4
HV = 128
HK = 64
DK = 256
RANK = 16
CH = 64
DFF = 2816
D_IN = 3104
D_INP = 3200
LR_BLK = 128
EPS = 1e-6
HB = 16
N_DEV = 8
N_CHIP = 4
LANES = 1024
SH_IN = D_IN // N_DEV
SH_FF = 2 * DFF // N_DEV
FF_HALF = N_DEV // 2

ADAM_LR, ADAM_B1, ADAM_B2, ADAM_EPS, ADAM_WD, ADAM_STEP = 0.001, 0.9, 0.999, 1e-08, 0.01, 10

VMEM_LIMIT = 48 * 1024 * 1024
TILE_TOKENS = 512
TILE_GLA = 512
TILE_FFN = 1024
TILE_NORM = 1024
TILE_MM = 2048
TILE_MM_KIN = 1024
TILE_MM_TOKENS = 2048

COL_GB, COL_GC, COL_GV = 0, 1, 2
COL_Q, COL_K = 6, 7
COL_V, COL_GO = 4, 5
COL_LR = 24

BIG = ("w_in", "w_out", "w_up", "w_down")
SMALL_SHARDED = (
    ("conv_a", (DEPTH, 3, DC // N_DEV), 2),
    ("gate_up_fwd", (DEPTH, RANK, DK // N_DEV), 2),
    ("gate_up_bwd", (DEPTH, RANK, DK // N_DEV), 2),
    ("conv_ffn", (DEPTH, 3, SH_FF), 2),
)
REPLICATED = (
    ("norm_mix_pre", (DEPTH, D)), ("norm_mix_post", (DEPTH, D)), ("norm_ffn_pre", (DEPTH, D)),
    ("norm_ffn_post", (DEPTH, D)), ("gate_bias_fwd", (DEPTH, DK)), ("gate_bias_bwd", (DEPTH, DK)),
    ("gla_head_norm", (DEPTH, HV)),
)
WEIGHT_ORDER = ("norm_mix_pre", "norm_mix_post", "norm_ffn_pre", "norm_ffn_post", "w_in", "conv_a", "gate_up_fwd",
                "gate_bias_fwd", "gate_up_bwd", "gate_bias_bwd", "gla_head_norm", "w_out", "w_up", "conv_ffn", "w_down")


def _rows_for(n_elems):
    return (-(-n_elems // LANES) + 7) // 8 * 8


ROWS_SSH = _rows_for(sum(math.prod(s) for _, s, _ in SMALL_SHARDED))
ROWS_REP = _rows_for(sum(math.prod(s) for _, s in REPLICATED))
ROWS_SMALL = ROWS_SSH + ROWS_REP


def _params(sem):
    return pltpu.CompilerParams(dimension_semantics=sem, vmem_limit_bytes=VMEM_LIMIT)


def _silu_parts(x):
    s = 1.0 / (1.0 + jnp.exp(-x))
    return x * s, s


def _rstd(xf):
    return lax.rsqrt(jnp.mean(xf * xf, axis=-1, keepdims=True) + EPS)


NN, NT, TN = ((1,), (0,)), ((1,), (1,)), ((0,), (0,))


def _dot(a, b, dims):
    return lax.dot_general(a, b, (dims, ((), ())), preferred_element_type=F32)


def _mm(a, b, *, dims, grid, a_spec, b_spec, o_spec, out_shape, tile, name, kin=0, comm=None):
    nk = grid[2]
    n_ci = len(comm.ins) if comm else 0
    n_co = len(comm.out_shapes) if comm else 0

    def body(*refs):
        a_ref, b_ref = refs[:2]
        ci = refs[2:2 + n_ci]
        o_ref = refs[2 + n_ci]
        co = refs[3 + n_ci:3 + n_ci + n_co]
        rest = refs[3 + n_ci + n_co:]
        if comm:
            sems, rest = rest[:3], rest[3:]
            step = (pl.program_id(0) * grid[1] + pl.program_id(1)) * grid[2] + pl.program_id(2)

            @pl.when(step == 0)
            def _():
                comm.start(ci, co, *sems)

        if kin:
            prod = _dot(a_ref[0], b_ref[0], dims)
            for d in range(1, kin):
                prod = prod + _dot(a_ref[d], b_ref[d], dims)
        else:
            prod = _dot(a_ref[...], b_ref[...], dims)
        if nk == 1:
            o_ref[...] = prod.astype(o_ref.dtype)
        else:
            acc_ref = rest[0]
            k = pl.program_id(2)

            @pl.when(k == 0)
            def _():
                acc_ref[...] = prod

            @pl.when(k > 0)
            def _():
                acc_ref[...] += prod

            @pl.when(k == nk - 1)
            def _():
                o_ref[...] = acc_ref[...].astype(o_ref.dtype)

        if comm:
            @pl.when(step == grid[0] * grid[1] * grid[2] - 1)
            def _():
                comm.wait(ci, co, *sems)

    acc = [pltpu.VMEM(tile, F32)] if nk > 1 else []
    if not comm:
        return pl.pallas_call(
            body, name=name, grid=grid, in_specs=[pl.BlockSpec(*a_spec), pl.BlockSpec(*b_spec)],
            out_specs=pl.BlockSpec(*o_spec), out_shape=out_shape, scratch_shapes=acc,
            compiler_params=_params(("parallel", "parallel", "arbitrary")),
        )(a, b)
    return pl.pallas_call(
        body, name=name, grid=grid,
        in_specs=[pl.BlockSpec(*a_spec), pl.BlockSpec(*b_spec)] + [HBM_SPEC] * n_ci,
        out_specs=[pl.BlockSpec(*o_spec)] + [HBM_SPEC] * n_co, out_shape=[out_shape] + list(comm.out_shapes),
        scratch_shapes=[pltpu.SemaphoreType.DMA((s,)) for s in comm.sems] + acc,
        input_output_aliases={2 + i: 1 + o for i, o in comm.aliases.items()},
        compiler_params=_params(("arbitrary", "arbitrary", "arbitrary")),
    )(a, b, *comm.ins)


def _halo_maps(tm, n_rows):
    r, last = tm // HB, n_rows // HB - 1
    return (lambda i: jnp.maximum(i * r - 1, 0)), (lambda i: jnp.minimum((i + 1) * r, last))


def _shift(x, prev_blk, next_blk):
    tm = x.shape[0]
    xs = jnp.concatenate([prev_blk, x, next_blk], axis=0)
    n = xs.shape[0]
    down = pltpu.roll(xs, 1, 0)[HB:HB + tm]
    up = pltpu.roll(xs, n - 1, 0)[HB:HB + tm]
    return down, up


def _edge_scales(i, n):
    return jnp.where(i > 0, 1.0, 0.0).astype(F32), jnp.where(i < n - 1, 1.0, 0.0).astype(F32)


def _gain_spec(l, width=D):
    return pl.BlockSpec((None, 1, width), lambda *_: (l, 0, 0))


def _norm_cast(x, g, l, name):
    L = x.shape[0]
    tm = min(TILE_NORM, L)

    def body(x_ref, g_ref, o_ref):
        xf = x_ref[...]
        o_ref[...] = (xf * _rstd(xf) * g_ref[...]).astype(BF16)

    return pl.pallas_call(
        body, name=name, grid=(L // tm,), in_specs=[pl.BlockSpec((tm, D), lambda i: (i, 0)), _gain_spec(l)],
        out_specs=pl.BlockSpec((tm, D), lambda i: (i, 0)), out_shape=jax.ShapeDtypeStruct((L, D), BF16),
        compiler_params=_params(("parallel",)),
    )(x, g)


def _post_pre(x, y, g_post, l_post, g_pre, l_pre, name):
    L = x.shape[0]
    tm = min(TILE_NORM, L)

    def body(x_ref, y_ref, gp_ref, gn_ref, x1_ref, h_ref):
        yf = y_ref[...].astype(F32)
        x1 = x_ref[...] + yf * _rstd(yf) * gp_ref[...]
        x1_ref[...] = x1
        h_ref[...] = (x1 * _rstd(x1) * gn_ref[...]).astype(BF16)

    tile = pl.BlockSpec((tm, D), lambda i: (i, 0))
    return pl.pallas_call(
        body, name=name, grid=(L // tm,), in_specs=[tile, tile, _gain_spec(l_post), _gain_spec(l_pre)],
        out_specs=[tile, tile],
        out_shape=[jax.ShapeDtypeStruct((L, D), F32), jax.ShapeDtypeStruct((L, D), BF16)],
        compiler_params=_params(("parallel",)),
    )(x, y, g_post, g_pre)


def _norm_bwd(yin, g, l, dout, dres, name):
    L = yin.shape[0]
    tm = min(TILE_NORM, L)
    with_res = dres is not None

    def body(*refs):
        if with_res:
            y_ref, g_ref, do_ref, dr_ref, din_ref, dg_ref = refs
        else:
            y_ref, g_ref, do_ref, din_ref, dg_ref = refs
        i = pl.program_id(0)
        y = y_ref[...].astype(F32)
        r = _rstd(y)
        do = do_ref[...].astype(F32)
        z = do * g_ref[...]
        din = r * z - y * (r * r * r) * jnp.mean(y * z, axis=-1, keepdims=True)
        if with_res:
            din = din + dr_ref[...]
        din_ref[...] = din.astype(din_ref.dtype)
        part = jnp.sum(do * y * r, axis=0, keepdims=True)

        @pl.when(i == 0)
        def _():
            dg_ref[...] = part

        @pl.when(i > 0)
        def _():
            dg_ref[...] += part

    tile = pl.BlockSpec((tm, D), lambda i: (i, 0))
    args = (yin, g, dout) + ((dres,) if with_res else ())
    return pl.pallas_call(
        body, name=name, grid=(L // tm,), in_specs=[tile, _gain_spec(l), tile] + ([tile] if with_res else []),
        out_specs=[tile, pl.BlockSpec((1, D), lambda i: (0, 0))],
        out_shape=[jax.ShapeDtypeStruct((L, D), F32 if with_res else BF16), jax.ShapeDtypeStruct((1, D), F32)],
        compiler_params=_params(("arbitrary",)),
    )(*args)


def _gla_consts(fwd, tb):
    row = lax.broadcasted_iota(jnp.int32, (tb, tb), 0)
    col = lax.broadcasted_iota(jnp.int32, (tb, tb), 1)
    same = (row // CH) == (col // CH)
    tri = same & ((col <= row) if fwd else (col >= row))
    tri_t = same & ((col >= row) if fwd else (col <= row))
    row_st = lax.broadcasted_iota(jnp.int32, (HEADS * CH, CH), 0) & (CH - 1)
    col_st = lax.broadcasted_iota(jnp.int32, (HEADS * CH, CH), 1)
    tri_st = (col_st <= row_st) if fwd else (col_st >= row_st)
    lane_head = lax.broadcasted_iota(jnp.int32, (1, DK), 1) // HK
    head_masks = [lane_head == h for h in range(HEADS)]
    srow = lax.broadcasted_iota(jnp.int32, (DG, DK), 0) // HV
    scol = lax.broadcasted_iota(jnp.int32, (DG, DK), 1) // HK
    return tri.astype(BF16), tri_t.astype(BF16), same.astype(BF16), tri_st, head_masks, srow == scol


def _dot_hilo(tri_b, x):
    hi = x.astype(BF16)
    lo = (x - hi.astype(F32)).astype(BF16)
    return _dot(tri_b, hi, NN) + _dot(tri_b, lo, NN)


def _gla_block_terms(q_ref, k_ref, lr_ref, gp_ref, bias_ref, tri_b, same_b):
    pre = _dot(lr_ref[...], gp_ref[...], NN) + bias_ref[...]
    sig_neg = 1.0 / (1.0 + jnp.exp(pre))
    a = (jnp.minimum(pre, 0.0) - jnp.log(1.0 + jnp.exp(-jnp.abs(pre)))) * (1.0 / 16.0)
    cum = _dot_hilo(tri_b, a)
    cl = _dot_hilo(same_b, a)
    e = jnp.exp(cum)
    einv = jnp.exp(-cum)
    eout = jnp.exp(cl - cum)
    q_in = q_ref[...].astype(F32) * e * (HK ** -0.5)
    k = k_ref[...].astype(F32)
    return dict(sig_neg=sig_neg, e=e, einv=einv, eout=eout, decay=jnp.exp(cl), q_in=q_in, k_in=k * einv,
                k_out=k * eout)


def _gla_chunk(t, c, head_masks):
    sl = slice(c * CH, (c + 1) * CH)
    tc = {n: x[sl] for n, x in t.items() if n != "decay"}
    tc["decay"] = jnp.max(t["decay"][c * CH:c * CH + 8], axis=0, keepdims=True)
    tc["q_st"] = jnp.concatenate([jnp.where(mh, tc["q_in"], 0.0) for mh in head_masks], axis=0).astype(BF16)
    return tc


def _gate_specs(l):
    return [pl.BlockSpec((None, LR_BLK, DK), lambda i: (l, 0, 0)), pl.BlockSpec((None, 1, DK), lambda i: (l, 0, 0))]


def _gla_fwd(p, gpad, bias, l, o_prev, fwd, name):
    L = p.shape[0]
    tb = min(TILE_GLA, L)
    nb, ncb, nch = L // tb, tb // CH, L // CH
    blk = (lambda i: i) if fwd else (lambda i: nb - 1 - i)
    with_prev = o_prev is not None

    def body(*refs):
        if with_prev:
            q_ref, k_ref, v_ref, lr_ref, gp_ref, bias_ref, op_ref, o_ref, sp_ref, s_ref = refs
        else:
            q_ref, k_ref, v_ref, lr_ref, gp_ref, bias_ref, o_ref, sp_ref, s_ref = refs
        i = pl.program_id(0)

        @pl.when(i == 0)
        def _():
            s_ref[...] = jnp.zeros_like(s_ref)

        tri_b, _, same_b, tri_st, head_masks, blockmask = _gla_consts(fwd, tb)
        terms = _gla_block_terms(q_ref, k_ref, lr_ref, gp_ref, bias_ref, tri_b, same_b)
        for c in (range(ncb) if fwd else reversed(range(ncb))):
            rows = pl.ds(c * CH, CH)
            t = _gla_chunk(terms, c, head_masks)
            v = v_ref[rows, :]
            scores = _dot(t["q_st"], t["k_in"].astype(BF16), NT)
            a_st = jnp.where(tri_st, scores, 0.0).astype(BF16)
            r = _dot(a_st, v, NN)
            o_intra = jnp.concatenate([r[h * CH:(h + 1) * CH, h * HV:(h + 1) * HV] for h in range(HEADS)], axis=1)
            s_b = s_ref[...].astype(BF16)
            sp_ref[c] = s_b
            o = o_intra + _dot(t["q_in"].astype(BF16), s_b, NT)
            if with_prev:
                o = o + op_ref[rows, :]
            o_ref[rows, :] = o
            kv_t = _dot(v, t["k_out"].astype(BF16), TN)
            s_ref[...] = s_ref[...] * t["decay"] + jnp.where(blockmask, kv_t, 0.0)

    def col(width, c):
        return pl.BlockSpec((tb, width), lambda i: (blk(i), c))

    in_specs = [col(DK, COL_Q), col(DK, COL_K), col(DG, COL_V), col(LR_BLK, COL_LR)] + _gate_specs(l)
    args = [p, p, p, p, gpad, bias]
    if with_prev:
        in_specs.append(pl.BlockSpec((tb, DG), lambda i: (blk(i), 0)))
        args.append(o_prev)
    return pl.pallas_call(
        body, name=name, grid=(nb,), in_specs=in_specs,
        out_specs=[pl.BlockSpec((tb, DG), lambda i: (blk(i), 0)), pl.BlockSpec((ncb, DG, DK), lambda i: (blk(i), 0, 0))],
        out_shape=[jax.ShapeDtypeStruct((L, DG), F32), jax.ShapeDtypeStruct((nch, DG, DK), BF16)],
        scratch_shapes=[pltpu.VMEM((DG, DK), F32)],
        compiler_params=_params(("arbitrary",)),
    )(*args)


P_COLS = dict(gb=(0, DC), gc=(DC, DC), gv=(2 * DC, DC), q=(3 * DC, DK), k=(3 * DC + DK, DK), v=(3 * DC + 2 * DK, DG),
              go=(3 * DC + 2 * DK + DG, DG), lr=(3 * DC + 2 * DK + 2 * DG, LR_BLK))


def _gla_bwd(p, gpad, bias, l, sprev, d_o, prev, fwd, name):
    L = p.shape[0]
    tb = min(TILE_GLA, L)
    nb, ncb = L // tb, tb // CH
    blk = (lambda i: nb - 1 - i) if fwd else (lambda i: i)
    with_prev = prev is not None

    def body(*refs):
        q_ref, k_ref, v_ref, lr_ref, gp_ref, bias_ref, sp_ref, do_ref = refs[:8]
        rest = refs[8:]
        if with_prev:
            pq_ref, pk_ref, pv_ref, plr_ref, dgb_ref, dgc_ref, dgv_ref, dgo_ref = rest[:8]
            dp_ref, dg_ref, db_ref, ds_ref = rest[8:]

            def put(what, rows, val):
                c0, width = P_COLS[what]
                dp_ref[rows, c0:c0 + width] = val

            for what, ref in (("gb", dgb_ref), ("gc", dgc_ref), ("gv", dgv_ref), ("go", dgo_ref)):
                put(what, slice(None), ref[...])
        else:
            dq_ref, dk_ref, dv_ref, dlr_ref, dg_ref, db_ref, ds_ref = rest
            out_of = dict(q=dq_ref, k=dk_ref, v=dv_ref, lr=dlr_ref)

            def put(what, rows, val):
                out_of[what][rows, :] = val

        i = pl.program_id(0)

        @pl.when(i == 0)
        def _():
            ds_ref[...] = jnp.zeros_like(ds_ref)
            dg_ref[...] = jnp.zeros_like(dg_ref)
            db_ref[...] = jnp.zeros_like(db_ref)

        tri_b, tri_t_b, same_b, tri_st, head_masks, blockmask = _gla_consts(fwd, tb)
        terms = _gla_block_terms(q_ref, k_ref, lr_ref, gp_ref, bias_ref, tri_b, same_b)
        dcum_of, dcl_of = [None] * ncb, [None] * ncb
        for c in (reversed(range(ncb)) if fwd else range(ncb)):
            rows = pl.ds(c * CH, CH)
            t = _gla_chunk(terms, c, head_masks)
            v = v_ref[rows, :]
            do = do_ref[rows, :]
            q_in, k_in, k_out = t["q_in"], t["k_in"], t["k_out"]
            q_b, k_in_b, k_out_b = q_in.astype(BF16), k_in.astype(BF16), k_out.astype(BF16)
            scores = _dot(t["q_st"], k_in_b, NT)
            a_st = jnp.where(tri_st, scores, 0.0).astype(BF16)
            s_prev = sp_ref[c]
            ds = ds_ref[...]
            ds_b = ds.astype(BF16)

            da_heads = [_dot(do[:, h * HV:(h + 1) * HV], v[:, h * HV:(h + 1) * HV], NT) for h in range(HEADS)]
            da_st = jnp.where(tri_st, jnp.concatenate(da_heads, axis=0), 0.0).astype(BF16)

            dv_heads = [_dot(a_st[h * CH:(h + 1) * CH, :], do[:, h * HV:(h + 1) * HV], TN) for h in range(HEADS)]
            dv = jnp.concatenate(dv_heads, axis=1) + _dot(k_out_b, ds_b, NT)

            x = _dot(da_st, k_in_b, NN)
            dq_in = _dot(do, s_prev, NN)
            for h in range(HEADS):
                dq_in = dq_in + jnp.where(head_masks[h], x[h * CH:(h + 1) * CH, :], 0.0)
            dk_in = _dot(da_st, t["q_st"], TN)
            dk_out = _dot(v, ds_b, NN)
            d_decay = jnp.sum(ds * s_prev.astype(F32), axis=0, keepdims=True)
            ds_ref[...] = ds * t["decay"] + jnp.where(blockmask, _dot(do, q_b, TN), 0.0)

            dq = dq_in * t["e"] * (HK ** -0.5)
            dk = dk_in * t["einv"] + dk_out * t["eout"]
            dko_ko = dk_out * k_out
            dcum_of[c] = dq_in * q_in - dk_in * k_in - dko_ko
            dcl = jnp.sum(dko_ko, axis=0, keepdims=True) + d_decay * t["decay"]
            dcl_of[c] = jnp.broadcast_to(dcl, (CH, DK))
            if with_prev:
                dq = dq + pq_ref[rows, :].astype(F32)
                dk = dk + pk_ref[rows, :].astype(F32)
                dv = dv + pv_ref[rows, :].astype(F32)
            put("q", rows, dq.astype(BF16))
            put("k", rows, dk.astype(BF16))
            put("v", rows, dv.astype(BF16))

        da = _dot_hilo(tri_t_b, jnp.concatenate(dcum_of, axis=0)) + jnp.concatenate(dcl_of, axis=0)
        dpre = da * terms["sig_neg"] * (1.0 / 16.0)
        dpre_b = dpre.astype(BF16)
        dlr = _dot(dpre_b, gp_ref[...], NT)
        dg_ref[...] += _dot(lr_ref[...], dpre_b, TN)
        db_ref[...] += jnp.sum(dpre, axis=0, keepdims=True)
        if with_prev:
            dlr = dlr + plr_ref[...].astype(F32)
        put("lr", slice(None), dlr.astype(BF16))

    def col(width, c):
        return pl.BlockSpec((tb, width), lambda i: (blk(i), c))

    in_specs = [col(DK, COL_Q), col(DK, COL_K), col(DG, COL_V), col(LR_BLK, COL_LR)] + _gate_specs(l) + [
        pl.BlockSpec((ncb, DG, DK), lambda i: (blk(i), 0, 0)), col(DG, 0)]
    args = [p, p, p, p, gpad, bias, sprev, d_o]
    tiles = [col(DK, 0), col(DK, 0), col(DG, 0), col(LR_BLK, 0)]
    shapes = [jax.ShapeDtypeStruct((L, DK), BF16), jax.ShapeDtypeStruct((L, DK), BF16),
              jax.ShapeDtypeStruct((L, DG), BF16), jax.ShapeDtypeStruct((L, LR_BLK), BF16)]
    if with_prev:
        in_specs += tiles + [col(DC, 0)] * 4
        args += list(prev)
        tiles, shapes = [col(D_INP, 0)], [jax.ShapeDtypeStruct((L, D_INP), BF16)]
    return pl.pallas_call(
        body, name=name, grid=(nb,), in_specs=in_specs,
        out_specs=tiles + [pl.BlockSpec((LR_BLK, DK), lambda i: (0, 0)), pl.BlockSpec((1, DK), lambda i: (0, 0))],
        out_shape=shapes + [jax.ShapeDtypeStruct((LR_BLK, DK), F32), jax.ShapeDtypeStruct((1, DK), F32)],
        scratch_shapes=[pltpu.VMEM((DG, DK), F32)],
        compiler_params=_params(("arbitrary",)),
    )(*args)


def _mixer_out(p, conv_a, gh, l, o_tot, name):
    L = p.shape[0]
    tm = min(TILE_TOKENS, L)
    n = L // tm
    pmap, nmap = _halo_maps(tm, L)

    def body(gb_ref, gc_ref, gcp_ref, gcn_ref, gv_ref, gvp_ref, gvn_ref, go_ref, cw_ref, o_ref, gh_ref, y_ref):
        ps, ns = _edge_scales(pl.program_id(0), n)
        z = gc_ref[...].astype(F32) * gv_ref[...].astype(F32)
        zp = gcp_ref[...].astype(F32) * gvp_ref[...].astype(F32) * ps
        zn = gcn_ref[...].astype(F32) * gvn_ref[...].astype(F32) * ns
        z_dn, z_up = _shift(z, zp, zn)
        conv = cw_ref[0:1, :] * z_dn + cw_ref[1:2, :] * z + cw_ref[2:3, :] * z_up
        y_ref[:, 0:DC] = (gb_ref[...].astype(F32) * conv).astype(BF16)
        o = o_ref[...]
        go = go_ref[...].astype(F32)
        for h in range(HEADS):
            oh = o[:, h * HV:(h + 1) * HV]
            on = oh * _rstd(oh) * gh_ref[...]
            act, _ = _silu_parts(go[:, h * HV:(h + 1) * HV])
            y_ref[:, DC + h * HV:DC + (h + 1) * HV] = (act * on).astype(BF16)

    def main(c):
        return pl.BlockSpec((tm, DC), lambda i: (i, c))

    def halo(c, imap):
        return pl.BlockSpec((HB, DC), lambda i: (imap(i), c))

    return pl.pallas_call(
        body, name=name, grid=(n,),
        in_specs=[main(COL_GB), main(COL_GC), halo(COL_GC, pmap), halo(COL_GC, nmap),
                  main(COL_GV), halo(COL_GV, pmap), halo(COL_GV, nmap), main(COL_GO),
                  pl.BlockSpec((None, 3, DC), lambda i: (l, 0, 0)), pl.BlockSpec((tm, DG), lambda i: (i, 0)),
                  _gain_spec(l, HV)],
        out_specs=pl.BlockSpec((tm, D), lambda i: (i, 0)), out_shape=jax.ShapeDtypeStruct((L, D), BF16),
        compiler_params=_params(("parallel",)),
    )(p, p, p, p, p, p, p, p, conv_a, o_tot, gh)


def _mixer_out_bwd(p, conv_a, gh, l, o_tot, dy, name):
    L = p.shape[0]
    tm = min(TILE_TOKENS, L)
    n = L // tm
    pmap, nmap = _halo_maps(tm, L)

    def body(gb_ref, gbp_ref, gbn_ref, gc_ref, gcp_ref, gcn_ref, gv_ref, gvp_ref, gvn_ref, go_ref, cw_ref, o_ref,
             gh_ref, dy_ref, dyp_ref, dyn_ref, dgb_ref, dgc_ref, dgv_ref, dgo_ref, do_ref, dcw_ref, dgh_ref):
        i = pl.program_id(0)
        ps, ns = _edge_scales(i, n)
        gb = gb_ref[...].astype(F32)
        gc = gc_ref[...].astype(F32)
        gv = gv_ref[...].astype(F32)
        z = gc * gv
        zp = gcp_ref[...].astype(F32) * gvp_ref[...].astype(F32) * ps
        zn = gcn_ref[...].astype(F32) * gvn_ref[...].astype(F32) * ns
        z_dn, z_up = _shift(z, zp, zn)
        w0, w1, w2 = cw_ref[0:1, :], cw_ref[1:2, :], cw_ref[2:3, :]
        conv = w0 * z_dn + w1 * z + w2 * z_up
        dya = dy_ref[:, 0:DC].astype(F32)
        dgb_ref[...] = (dya * conv).astype(BF16)
        dc = dya * gb
        dcp = dyp_ref[...].astype(F32) * gbp_ref[...].astype(F32) * ps
        dcn = dyn_ref[...].astype(F32) * gbn_ref[...].astype(F32) * ns
        dc_dn, dc_up = _shift(dc, dcp, dcn)
        dz = w0 * dc_up + w1 * dc + w2 * dc_dn
        dgc_ref[...] = (dz * gv).astype(BF16)
        dgv_ref[...] = (dz * gc).astype(BF16)
        dcw = [jnp.sum(zs * dc, axis=0, keepdims=True) for zs in (z_dn, z, z_up)]

        o = o_ref[...]
        go = go_ref[...].astype(F32)
        dgh = jnp.zeros((1, HV), F32)
        for h in range(HEADS):
            sl = slice(h * HV, (h + 1) * HV)
            oh = o[:, sl]
            r = _rstd(oh)
            act, sg = _silu_parts(go[:, sl])
            dyb = dy_ref[:, DC + h * HV:DC + (h + 1) * HV].astype(F32)
            on = oh * r * gh_ref[...]
            dgo_ref[:, sl] = (dyb * on * (sg + act * (1.0 - sg))).astype(BF16)
            don = dyb * act
            zz = don * gh_ref[...]
            do_ref[:, sl] = (r * zz - oh * (r * r * r) * jnp.mean(oh * zz, axis=-1, keepdims=True)).astype(BF16)
            dgh = dgh + jnp.sum(don * oh * r, axis=0, keepdims=True)

        @pl.when(i == 0)
        def _():
            dcw_ref[...] = jnp.zeros_like(dcw_ref)
            dgh_ref[...] = jnp.zeros_like(dgh_ref)

        for kk in range(3):
            dcw_ref[kk:kk + 1, :] += dcw[kk]
        dgh_ref[...] += dgh

    def main(c):
        return pl.BlockSpec((tm, DC), lambda i: (i, c))

    def halo(c, imap):
        return pl.BlockSpec((HB, DC), lambda i: (imap(i), c))

    tile = pl.BlockSpec((tm, DC), lambda i: (i, 0))
    return pl.pallas_call(
        body, name=name, grid=(n,),
        in_specs=[main(COL_GB), halo(COL_GB, pmap), halo(COL_GB, nmap), main(COL_GC), halo(COL_GC, pmap),
                  halo(COL_GC, nmap), main(COL_GV), halo(COL_GV, pmap), halo(COL_GV, nmap), main(COL_GO),
                  pl.BlockSpec((None, 3, DC), lambda i: (l, 0, 0)), tile, _gain_spec(l, HV),
                  pl.BlockSpec((tm, D), lambda i: (i, 0)), halo(0, pmap), halo(0, nmap)],
        out_specs=[tile, tile, tile, tile, tile, pl.BlockSpec((3, DC), lambda i: (0, 0)),
                   pl.BlockSpec((1, HV), lambda i: (0, 0))],
        out_shape=[jax.ShapeDtypeStruct((L, DC), BF16)] * 5
        + [jax.ShapeDtypeStruct((3, DC), F32), jax.ShapeDtypeStruct((1, HV), F32)],
        compiler_params=_params(("arbitrary",)),
    )(p, p, p, p, p, p, p, p, p, p, conv_a, o_tot, gh, dy, dy, dy)


def _ffn_specs(tm, L, l, row_axis, sh_axis):
    pmap, nmap = _halo_maps(tm, L)

    def u(off, imap=None, rows=tm):
        if imap is None:
            return pl.BlockSpec((None, rows, SH_FF), lambda *g: (g[sh_axis] + off, g[row_axis], 0))
        return pl.BlockSpec((None, rows, SH_FF), lambda *g: (g[sh_axis] + off, imap(g[row_axis]), 0))

    def cw(off):
        return pl.BlockSpec((None, None, 3, SH_FF), lambda *g: (l, g[sh_axis] + off, 0, 0))

    u_specs = [u(0), u(0, pmap, HB), u(0, nmap, HB), u(FF_HALF), u(FF_HALF, pmap, HB), u(FF_HALF, nmap, HB)]
    return u_specs, [cw(0), cw(FF_HALF)]


def _conv3(x_ref, xp_ref, xn_ref, cw_ref, ps, ns):
    x = x_ref[...].astype(F32)
    x_dn, x_up = _shift(x, xp_ref[...].astype(F32) * ps, xn_ref[...].astype(F32) * ns)
    return cw_ref[0:1, :] * x_dn + cw_ref[1:2, :] * x + cw_ref[2:3, :] * x_up, (x_dn, x, x_up)


def _carry(body, *, name, grid, args, in_specs, out_specs, out_shape, scratch, comm):
    n_in, n_out = len(args), len(out_shape)
    n_ci = len(comm.ins) if comm else 0
    n_co = len(comm.out_shapes) if comm else 0

    def wrapped(*refs):
        ins, refs = refs[:n_in], refs[n_in:]
        ci, refs = refs[:n_ci], refs[n_ci:]
        outs, refs = refs[:n_out], refs[n_out:]
        co, refs = refs[:n_co], refs[n_co:]
        if comm:
            sems, refs = refs[:3], refs[3:]
            step = 0
            for ax, size in enumerate(grid):
                step = step * size + pl.program_id(ax)

            @pl.when(step == 0)
            def _():
                comm.start(ci, co, *sems)

        body(ins, outs, refs)
        if comm:
            @pl.when(step == math.prod(grid) - 1)
            def _():
                comm.wait(ci, co, *sems)

    return pl.pallas_call(
        wrapped, name=name, grid=grid, in_specs=list(in_specs) + [HBM_SPEC] * n_ci,
        out_specs=list(out_specs) + [HBM_SPEC] * n_co,
        out_shape=list(out_shape) + list(comm.out_shapes if comm else ()),
        scratch_shapes=([pltpu.SemaphoreType.DMA((s,)) for s in comm.sems] if comm else []) + list(scratch),
        input_output_aliases={n_in + i: n_out + o for i, o in comm.aliases.items()} if comm else {},
        compiler_params=_params(("arbitrary",) * len(grid)),
    )(*args, *(comm.ins if comm else ()))


def _ffn_act(u8, cw, l, name):
    L = u8.shape[1]
    tm = min(TILE_FFN, L)
    n = L // tm
    u_specs, cw_specs = _ffn_specs(tm, L, l, 0, 1)

    def body(ins, outs, scratch):
        g_ref, gp_ref, gn_ref, v_ref, vp_ref, vn_ref, cwg_ref, cwv_ref = ins
        ps, ns = _edge_scales(pl.program_id(0), n)
        gate, _ = _conv3(g_ref, gp_ref, gn_ref, cwg_ref, ps, ns)
        val, _ = _conv3(v_ref, vp_ref, vn_ref, cwv_ref, ps, ns)
        act, _ = _silu_parts(gate)
        outs[0][...] = (act * val).astype(BF16)
        outs[1][0] = gate.astype(BF16)
        outs[1][1] = val.astype(BF16)

    pair = pl.BlockSpec((2, None, tm, SH_FF), lambda i, d: (0, d, i, 0))
    return _carry(
        body, name=name, grid=(n, FF_HALF), args=(u8, u8, u8, u8, u8, u8, cw, cw), in_specs=u_specs + cw_specs,
        out_specs=[pl.BlockSpec((None, tm, SH_FF), lambda i, d: (d, i, 0)), pair],
        out_shape=[jax.ShapeDtypeStruct((FF_HALF, L, SH_FF), BF16),
                   jax.ShapeDtypeStruct((2, FF_HALF, L, SH_FF), BF16)], scratch=[], comm=None)


def _ffn_act_bwd(conv, da, name):
    L = conv.shape[2]
    tm = min(TILE_FFN, L)

    def body(ins, outs, scratch):
        c_ref, da_ref = ins
        act, sg = _silu_parts(c_ref[0].astype(F32))
        da_f = da_ref[...].astype(F32)
        outs[0][0] = (da_f * c_ref[1].astype(F32) * (sg + act * (1.0 - sg))).astype(BF16)
        outs[0][1] = (da_f * act).astype(BF16)

    pair = pl.BlockSpec((2, None, tm, SH_FF), lambda d, i: (0, d, i, 0))
    return _carry(
        body, name=name, grid=(FF_HALF, L // tm), args=(conv, da),
        in_specs=[pair, pl.BlockSpec((None, tm, SH_FF), lambda d, i: (d, i, 0))], out_specs=[pair],
        out_shape=[jax.ShapeDtypeStruct((2, FF_HALF, L, SH_FF), BF16)], scratch=[], comm=None)[0]


def _ffn_conv_t(du8, u8, cw, l, name):
    L = du8.shape[1]
    tm = min(TILE_FFN, L)
    n = L // tm
    pmap, nmap = _halo_maps(tm, L)

    def body(ins, outs, scratch):
        x_ref, xp_ref, xn_ref, u_ref, cw_ref = ins
        d_u_ref, dcw_ref = outs
        i = pl.program_id(1)
        ps, ns = _edge_scales(i, n)
        x = x_ref[...].astype(F32)
        x_dn, x_up = _shift(x, xp_ref[...].astype(F32) * ps, xn_ref[...].astype(F32) * ns)
        d_u_ref[...] = (cw_ref[0:1, :] * x_up + cw_ref[1:2, :] * x + cw_ref[2:3, :] * x_dn).astype(BF16)

        @pl.when(i == 0)
        def _():
            dcw_ref[...] = jnp.zeros_like(dcw_ref)

        u = u_ref[...].astype(F32)
        for kk, xs in enumerate((x_up, x, x_dn)):
            dcw_ref[kk:kk + 1, :] += jnp.sum(u * xs, axis=0, keepdims=True)

    tile = pl.BlockSpec((None, tm, SH_FF), lambda d, i: (d, i, 0))
    return _carry(
        body, name=name, grid=(N_DEV, n), args=(du8, du8, du8, u8, cw),
        in_specs=[tile, pl.BlockSpec((None, HB, SH_FF), lambda d, i: (d, pmap(i), 0)),
                  pl.BlockSpec((None, HB, SH_FF), lambda d, i: (d, nmap(i), 0)), tile,
                  pl.BlockSpec((None, None, 3, SH_FF), lambda d, i: (l, d, 0, 0))],
        out_specs=[tile, pl.BlockSpec((None, 3, SH_FF), lambda d, i: (d, 0, 0))],
        out_shape=[jax.ShapeDtypeStruct((N_DEV, L, SH_FF), BF16), jax.ShapeDtypeStruct((N_DEV, 3, SH_FF), F32)],
        scratch=[], comm=None)


def _loss_grad(xl, target, name):
    L = xl.shape[0]
    tm = min(TILE_NORM, L)

    def body(x_ref, t_ref, dx_ref, sq_ref):
        i = pl.program_id(0)
        err = x_ref[...] - t_ref[...]
        dx_ref[...] = err * (1.0 / D)
        part = jnp.sum(err * err, axis=0, keepdims=True)

        @pl.when(i == 0)
        def _():
            sq_ref[...] = part

        @pl.when(i > 0)
        def _():
            sq_ref[...] += part

    tile = pl.BlockSpec((tm, D), lambda i: (i, 0))
    return pl.pallas_call(
        body, name=name, grid=(L // tm,), in_specs=[tile, tile],
        out_specs=[tile, pl.BlockSpec((1, D), lambda i: (0, 0))],
        out_shape=[jax.ShapeDtypeStruct((L, D), F32), jax.ShapeDtypeStruct((1, D), F32)],
        compiler_params=_params(("arbitrary",)),
    )(xl, target)


MESH = pl.DeviceIdType.MESH
HBM_SPEC = pl.BlockSpec(memory_space=pltpu.HBM)


def _position():
    return lax.axis_index("x"), lax.axis_index("y"), lax.axis_index("c")


def _other_chips(x, y):
    return [(1 - x, y), (x, 1 - y), (1 - x, 1 - y)]


def _all_gather(shards, name):
    n = len(shards)

    def body(*refs):
        x_refs, out_refs = refs[:n], refs[n:2 * n]
        send_sems, recv_sems, local_sems = refs[2 * n:]
        x, y, c = _position()
        me, sibling = (x, y, c), (x, y, 1 - c)
        chips = _other_chips(x, y)

        def slot(t, px, py, pc):
            return out_refs[t].at[:, 4 * px + 2 * py + pc]

        def copy(t, k, block, to, from_input=False):
            return pltpu.make_async_remote_copy(
                src_ref=x_refs[t] if from_input else slot(t, *block), dst_ref=slot(t, *block),
                send_sem=send_sems.at[k * n + t], recv_sem=recv_sems.at[k * n + t], device_id=to, device_id_type=MESH)

        mine = [pltpu.make_async_copy(x_refs[t], slot(t, *me), local_sems.at[t]) for t in range(n)]
        for cp in mine:
            cp.start()
        first = [copy(t, 0, me, sibling, True) for t in range(n)]
        first += [copy(t, 1 + j, me, (*chip, c), True) for j, chip in enumerate(chips) for t in range(n)]
        for cp in first:
            cp.start()
        passed = []
        for j, chip in enumerate(chips):
            for t in range(n):
                copy(t, 1 + j, (*chip, c), me).wait_recv()
                passed.append(copy(t, 4 + j, (*chip, c), sibling))
                passed[-1].start()
        for t in range(n):
            copy(t, 0, sibling, me).wait_recv()
        for j, chip in enumerate(chips):
            for t in range(n):
                copy(t, 4 + j, (*chip, 1 - c), me).wait_recv()
        for cp in first + passed:
            cp.wait_send()
        for cp in mine:
            cp.wait()

    return pl.pallas_call(
        body, name=name,
        out_shape=[jax.ShapeDtypeStruct((s.shape[0], N_DEV) + s.shape[1:], s.dtype) for s in shards],
        in_specs=[HBM_SPEC] * n, out_specs=[HBM_SPEC] * n,
        scratch_shapes=[pltpu.SemaphoreType.DMA((7 * n,)), pltpu.SemaphoreType.DMA((7 * n,)),
                        pltpu.SemaphoreType.DMA((n,))],
    )(*shards)


def _exchange_sibling(grads, name):
    n = len(grads)

    def body(*refs):
        g_refs, out_refs, send_sems, recv_sems = refs[:n], refs[n:2 * n], refs[2 * n], refs[2 * n + 1]
        x, y, c = _position()
        copies = [pltpu.make_async_remote_copy(
            src_ref=g_refs[t].at[:, 2 * k + (1 - c)], dst_ref=out_refs[t].at[:, k], send_sem=send_sems.at[k * n + t],
            recv_sem=recv_sems.at[k * n + t], device_id=(x, y, 1 - c), device_id_type=MESH)
            for k in range(N_CHIP) for t in range(n)]
        for cp in copies:
            cp.start()
        for cp in copies:
            cp.wait()

    return pl.pallas_call(
        body, name=name,
        out_shape=[jax.ShapeDtypeStruct((g.shape[0], N_CHIP) + g.shape[2:], g.dtype) for g in grads],
        in_specs=[HBM_SPEC] * n, out_specs=[HBM_SPEC] * n,
        scratch_shapes=[pltpu.SemaphoreType.DMA((N_CHIP * n,)), pltpu.SemaphoreType.DMA((N_CHIP * n,))],
    )(*grads)


class _Comm(NamedTuple):
    ins: tuple
    out_shapes: tuple
    aliases: dict
    sems: tuple
    start: Callable
    wait: Callable


def _comm_of(ins, out_shapes, aliases, sems, copies):
    def start(ci, co, send, recv, local):
        for cp in copies(ci, co, send, recv, local):
            cp.start()

    def wait(ci, co, send, recv, local):
        for cp in copies(ci, co, send, recv, local):
            cp.wait()

    return _Comm(tuple(ins), tuple(out_shapes), aliases, sems, start, wait)


def _remote(src, dst, send, recv, idx, to):
    return pltpu.make_async_remote_copy(src_ref=src, dst_ref=dst, send_sem=send.at[idx], recv_sem=recv.at[idx],
                                        device_id=to, device_id_type=MESH)


def _gather_ici_comm(shards):
    n = len(shards)

    def copies(ci, co, send, recv, local):
        x, y, c = _position()
        me = 4 * x + 2 * y + c
        mine = [pltpu.make_async_copy(ci[t], co[t].at[:, me], local.at[t]) for t in range(n)]
        return mine + [_remote(ci[t], co[t].at[:, me], send, recv, j * n + t, (cx, cy, c))
                       for j, (cx, cy) in enumerate(_other_chips(x, y)) for t in range(n)]

    outs = [jax.ShapeDtypeStruct((1, N_DEV) + s.shape[1:], s.dtype) for s in shards]
    return _comm_of(shards, outs, {}, (3 * n, 3 * n, n), copies)


def _gather_d2d_comm(partials):
    n = len(partials)

    def copies(ci, co, send, recv, local):
        x, y, c = _position()
        return [_remote(co[t].at[:, 4 * cx + 2 * cy + c], co[t].at[:, 4 * cx + 2 * cy + c], send, recv, k * n + t,
                        (x, y, 1 - c))
                for k, (cx, cy) in enumerate([(x, y)] + _other_chips(x, y)) for t in range(n)]

    outs = [jax.ShapeDtypeStruct(p.shape, p.dtype) for p in partials]
    return _comm_of(partials, outs, {t: t for t in range(n)}, (N_CHIP * n, N_CHIP * n, 1), copies)


def _grads_d2d_comm(grads):
    n = len(grads)

    def copies(ci, co, send, recv, local):
        x, y, c = _position()
        return [_remote(ci[t].at[:, 2 * k + (1 - c)], co[t].at[:, k], send, recv, k * n + t, (x, y, 1 - c))
                for k in range(N_CHIP) for t in range(n)]

    outs = [jax.ShapeDtypeStruct((g.shape[0], N_CHIP) + g.shape[2:], g.dtype) for g in grads]
    return _comm_of(grads, outs, {}, (N_CHIP * n, N_CHIP * n, 1), copies)


def _grads_ici_comm(parts):
    n = len(parts)

    def copies(ci, co, send, recv, local):
        x, y, c = _position()
        my_chip = 2 * x + y
        mine = [pltpu.make_async_copy(ci[t].at[:, my_chip], co[t].at[:, my_chip], local.at[t]) for t in range(n)]
        return mine + [_remote(ci[t].at[:, 2 * cx + cy], co[t].at[:, my_chip], send, recv, j * n + t, (cx, cy, c))
                       for j, (cx, cy) in enumerate(_other_chips(x, y)) for t in range(n)]

    outs = [jax.ShapeDtypeStruct(p.shape, p.dtype) for p in parts]
    return _comm_of(parts, outs, {}, (3 * n, 3 * n, n), copies)


def _row_tile(rows):
    return 256 if rows % 256 == 0 else rows


def _pair_sum(g, recv, c_idx, out_dtype, name):
    lay, _, rows, cols = g.shape
    tr = _row_tile(rows)

    def body(c_ref, g_ref, r_ref, o_ref):
        o_ref[...] = (g_ref[...] + r_ref[...]).astype(o_ref.dtype)

    def spec(blk_of):
        return pl.BlockSpec((None, None, tr, cols), lambda l, k, r, c_ref: (l, blk_of(k, c_ref), r, 0))

    return pl.pallas_call(
        body, name=name,
        grid_spec=pltpu.PrefetchScalarGridSpec(
            num_scalar_prefetch=1, grid=(lay, N_CHIP, rows // tr),
            in_specs=[spec(lambda k, c_ref: 2 * k + c_ref[0]), spec(lambda k, c_ref: k)],
            out_specs=spec(lambda k, c_ref: k)),
        out_shape=jax.ShapeDtypeStruct((lay, N_CHIP, rows, cols), out_dtype),
        compiler_params=_params(("parallel", "parallel", "parallel")),
    )(c_idx, g, recv)


def _exchange_chips(parts, name):
    n = len(parts)

    def body(*refs):
        p_refs, out_refs = refs[:n], refs[n:2 * n]
        send_sems, recv_sems, local_sems = refs[2 * n:]
        x, y, c = _position()
        my_chip = 2 * x + y
        mine = [pltpu.make_async_copy(p_refs[t].at[:, my_chip], out_refs[t].at[:, my_chip], local_sems.at[t])
                for t in range(n)]
        for cp in mine:
            cp.start()
        copies = [pltpu.make_async_remote_copy(
            src_ref=p_refs[t].at[:, 2 * cx + cy], dst_ref=out_refs[t].at[:, my_chip], send_sem=send_sems.at[j * n + t],
            recv_sem=recv_sems.at[j * n + t], device_id=(cx, cy, c), device_id_type=MESH)
            for j, (cx, cy) in enumerate(_other_chips(x, y)) for t in range(n)]
        for cp in copies:
            cp.start()
        for cp in copies:
            cp.wait()
        for cp in mine:
            cp.wait()

    return pl.pallas_call(
        body, name=name, out_shape=[jax.ShapeDtypeStruct(p.shape, p.dtype) for p in parts],
        in_specs=[HBM_SPEC] * n, out_specs=[HBM_SPEC] * n,
        scratch_shapes=[pltpu.SemaphoreType.DMA((3 * n,)), pltpu.SemaphoreType.DMA((3 * n,)),
                        pltpu.SemaphoreType.DMA((n,))],
    )(*parts)


def _sum_adamw(parts, w, m, v, name):
    lay, rows, cols = w.shape
    tr = _row_tile(rows)

    def body(p_ref, w_ref, m_ref, v_ref, g_ref, d_ref, nm_ref, nv_ref):
        g = ((p_ref[0].astype(F32) + p_ref[1].astype(F32)) + p_ref[2].astype(F32)) + p_ref[3].astype(F32)
        g_ref[...] = g
        nm = ADAM_B1 * m_ref[...] + (1.0 - ADAM_B1) * g
        nv = ADAM_B2 * v_ref[...] + (1.0 - ADAM_B2) * (g * g)
        nm_ref[...] = nm
        nv_ref[...] = nv
        m_hat = nm / (1.0 - ADAM_B1 ** ADAM_STEP)
        v_hat = nv / (1.0 - ADAM_B2 ** ADAM_STEP)
        d_ref[...] = -ADAM_LR * (m_hat / (jnp.sqrt(v_hat) + ADAM_EPS) + ADAM_WD * w_ref[...])

    tile = pl.BlockSpec((None, tr, cols), lambda l, r: (l, r, 0))
    return pl.pallas_call(
        body, name=name, grid=(lay, rows // tr),
        in_specs=[pl.BlockSpec((None, N_CHIP, tr, cols), lambda l, r: (l, 0, r, 0)), tile, tile, tile],
        out_specs=[tile] * 4, out_shape=[jax.ShapeDtypeStruct((lay, rows, cols), F32)] * 4,
        compiler_params=_params(("parallel", "parallel")),
    )(parts, w, m, v)


def _pad_rows(flat, rows):
    return jnp.pad(flat, (0, rows * LANES - flat.shape[0])).reshape(rows, LANES)


def _pack_small(tree):
    sh = jnp.concatenate([tree[n].reshape(-1) for n, _, _ in SMALL_SHARDED])
    rep = jnp.concatenate([tree[n].reshape(-1) for n, _ in REPLICATED])
    return jnp.concatenate([_pad_rows(sh, ROWS_SSH), _pad_rows(rep, ROWS_REP)], axis=0)


def _unpack_small(buf):
    out = {}
    for flat, items in ((buf[:ROWS_SSH].reshape(-1), [(n, s) for n, s, _ in SMALL_SHARDED]),
                        (buf[ROWS_SSH:].reshape(-1), REPLICATED)):
        off = 0
        for n, s in items:
            out[n] = flat[off:off + math.prod(s)].reshape(s)
            off += math.prod(s)
    return out


def _full_from_blocks(blocks, s, ax):
    return jnp.concatenate([blocks[d] for d in range(N_DEV)], axis=ax)


def _blocks_from_full(full, s, ax):
    return jnp.stack([lax.slice_in_dim(full, d * s[ax], (d + 1) * s[ax], axis=ax) for d in range(N_DEV)])


def _pack_small_grads(sharded_blocks, replicated):
    sh = jnp.concatenate([sharded_blocks[n].reshape(N_DEV, -1) for n, _, _ in SMALL_SHARDED], axis=1)
    sh = jnp.pad(sh, ((0, 0), (0, ROWS_SSH * LANES - sh.shape[1]))).reshape(N_DEV, ROWS_SSH, LANES)
    rep = _pad_rows(jnp.concatenate([replicated[n].reshape(-1) for n, _ in REPLICATED]), ROWS_REP)
    return jnp.concatenate([sh, jnp.broadcast_to(rep[None], (N_DEV, ROWS_REP, LANES))], axis=1)


def _layer_fwd(x, h1, wts, big, l, l_next, next_shards):
    L = x.shape[0]
    tm = min(TILE_MM, L)
    nt = L // tm
    rest_next = [s for n, s in zip(BIG, next_shards) if n != "w_up"] if next_shards else None
    p = _mm(h1, big["w_in"], dims=NN, grid=(nt, D_INP // 640, 1),
            a_spec=((tm, D), lambda i, j, k: (i, 0)), b_spec=((None, D, 640), lambda i, j, k: (0, 0, j)),
            o_spec=((tm, 640), lambda i, j, k: (i, j)), out_shape=jax.ShapeDtypeStruct((L, D_INP), BF16),
            tile=(tm, 640), name="proj_in", comm=_gather_ici_comm(rest_next) if next_shards else None)
    if next_shards:
        p, rest_next = p[0], p[1:]
    o_f, sp_f = _gla_fwd(p, wts["gpad_f"], wts["bias_f"], l, None, True, "gla_fwd_f")
    o_tot, sp_b = _gla_fwd(p, wts["gpad_b"], wts["bias_b"], l, o_f, False, "gla_fwd_b")
    y_cat = _mixer_out(p, wts["conv_a"], wts["gh"], l, o_tot, "mixer_out")
    y = _mm(y_cat, big["w_out"], dims=NN, grid=(nt, 1, 1),
            a_spec=((tm, D), lambda i, j, k: (i, 0)), b_spec=((None, D, D), lambda i, j, k: (0, 0, 0)),
            o_spec=((tm, D), lambda i, j, k: (i, 0)), out_shape=jax.ShapeDtypeStruct((L, D), BF16),
            tile=(tm, D), name="proj_out")
    x1, h2 = _post_pre(x, y, wts["g2"], l, wts["g3"], l, "post_pre_mix")
    u8 = _mm(h2, big["w_up"], dims=NN, grid=(nt, N_DEV, 1),
             a_spec=((tm, D), lambda i, j, k: (i, 0)), b_spec=((None, None, D, SH_FF), lambda i, j, k: (0, j, 0, 0)),
             o_spec=((None, tm, SH_FF), lambda i, j, k: (j, i, 0)),
             out_shape=jax.ShapeDtypeStruct((N_DEV, L, SH_FF), BF16), tile=(tm, SH_FF), name="ffn_up",
             comm=_gather_ici_comm([next_shards[BIG.index("w_up")]]) if next_shards else None)
    gathered = None
    if next_shards:
        u8, up_next = u8[0], u8[1]
        gathered = [rest_next[0], rest_next[1], up_next, rest_next[2]]
    a, conv = _ffn_act(u8, wts["cw"], l, "ffn_act")
    tm1 = min(TILE_MM_KIN, L)
    y2 = _mm(a, big["w_down"], dims=NN, grid=(L // tm1, 1, 1), kin=FF_HALF,
             a_spec=((FF_HALF, tm1, SH_FF), lambda i, j, k: (0, i, 0)),
             b_spec=((None, FF_HALF, SH_FF, D), lambda i, j, k: (0, 0, 0, 0)),
             o_spec=((tm1, D), lambda i, j, k: (i, 0)), out_shape=jax.ShapeDtypeStruct((L, D), BF16),
             tile=(tm1, D), name="ffn_down", comm=_gather_d2d_comm(gathered) if next_shards else None)
    if next_shards:
        y2, gathered = y2[0], y2[1:]
    x2, h1_next = _post_pre(x1, y2, wts["g4"], l, wts["g1"], l_next, "post_pre_ffn")
    saved = dict(x=x, h1=h1, p=p, o_tot=o_tot, sp_f=sp_f, sp_b=sp_b, y_cat=y_cat, y=y, x1=x1, h2=h2, u8=u8, a=a, y2=y2,
                 big=big, conv=conv)
    return x2, h1_next, saved, gathered


def _layer_bwd(dx2, wts, s, l, pending, c_idx, early_ffn):
    L = dx2.shape[0]
    big = s["big"]
    tm = min(TILE_MM, L)
    nt = L // tm
    tm1 = min(TILE_MM_KIN, L)
    tk = min(TILE_MM_TOKENS, L)
    nkt = L // tk
    dy2, dg4 = _norm_bwd(s["y2"], wts["g4"], l, dx2, None, "norm_bwd_ffn_post")
    da = _mm(dy2, big["w_down"], dims=NT, grid=(nt, FF_HALF, 1),
             a_spec=((tm, D), lambda i, j, k: (i, 0)), b_spec=((None, None, SH_FF, D), lambda i, j, k: (0, j, 0, 0)),
             o_spec=((None, tm, SH_FF), lambda i, j, k: (j, i, 0)),
             out_shape=jax.ShapeDtypeStruct((FF_HALF, L, SH_FF), BF16), tile=(tm, SH_FF), name="ffn_down_dx",
             comm=_grads_d2d_comm(pending) if pending else None)
    pairs = None
    if pending:
        da, from_sibling = da[0], da[1:]
        pairs = [_pair_sum(g, r, c_idx, BF16, "grads_pair_sum") for g, r in zip(pending, from_sibling)]
    dw_down = _mm(s["a"], dy2, dims=TN, grid=(FF_HALF, 1, nkt),
                  a_spec=((None, tk, SH_FF), lambda i, j, k: (i, k, 0)), b_spec=((tk, D), lambda i, j, k: (k, 0)),
                  o_spec=((SH_FF, D), lambda i, j, k: (i, 0)), out_shape=jax.ShapeDtypeStruct((DFF, D), F32),
                  tile=(SH_FF, D), name="ffn_down_dw")
    du = _ffn_act_bwd(s["conv"], da, "ffn_act_bwd")
    d_u8, dcw = _ffn_conv_t(du.reshape(N_DEV, L, SH_FF), s["u8"], wts["cw"], l, "ffn_conv_t")
    dh2 = _mm(d_u8, big["w_up"], dims=NT, grid=(L // tm1, 1, N_DEV // FF_HALF), kin=FF_HALF,
              a_spec=((FF_HALF, tm1, SH_FF), lambda i, j, k: (k, i, 0)),
              b_spec=((None, FF_HALF, D, SH_FF), lambda i, j, k: (0, k, 0, 0)),
              o_spec=((tm1, D), lambda i, j, k: (i, 0)), out_shape=jax.ShapeDtypeStruct((L, D), BF16),
              tile=(tm1, D), name="ffn_up_dx",
              comm=_grads_ici_comm([q for n, q in zip(BIG, pairs) if n != "w_up"]) if pending else None)
    if pending:
        dh2, rest_parts = dh2[0], dh2[1:]
    dw_up = _mm(s["h2"], d_u8, dims=TN, grid=(1, N_DEV, nkt),
                a_spec=((tk, D), lambda i, j, k: (k, 0)), b_spec=((None, tk, SH_FF), lambda i, j, k: (j, k, 0)),
                o_spec=((None, D, SH_FF), lambda i, j, k: (j, 0, 0)),
                out_shape=jax.ShapeDtypeStruct((N_DEV, D, SH_FF), F32), tile=(D, SH_FF), name="ffn_up_dw",
                comm=_grads_ici_comm([pairs[BIG.index("w_up")]]) if pending else None)
    parts = None
    if pending:
        dw_up, up_part = dw_up[0], dw_up[1]
        parts = [rest_parts[0], rest_parts[1], up_part, rest_parts[2]]
    dx1, dg3 = _norm_bwd(s["x1"], wts["g3"], l, dh2, dx2, "norm_bwd_ffn_pre")
    dy, dg2 = _norm_bwd(s["y"], wts["g2"], l, dx1, None, "norm_bwd_mix_post")
    dy_cat = _mm(dy, big["w_out"], dims=NT, grid=(nt, 1, 1),
                 a_spec=((tm, D), lambda i, j, k: (i, 0)), b_spec=((None, D, D), lambda i, j, k: (0, 0, 0)),
                 o_spec=((tm, D), lambda i, j, k: (i, 0)), out_shape=jax.ShapeDtypeStruct((L, D), BF16),
                 tile=(tm, D), name="proj_out_dx")
    dw_out = _mm(s["y_cat"], dy, dims=TN, grid=(1, 1, nkt),
                 a_spec=((tk, D), lambda i, j, k: (k, 0)), b_spec=((tk, D), lambda i, j, k: (k, 0)),
                 o_spec=((D, D), lambda i, j, k: (0, 0)), out_shape=jax.ShapeDtypeStruct((D, D), F32),
                 tile=(D, D), name="proj_out_dw")
    dgb, dgc, dgv, dgo, d_o, dconv_a, dgh = _mixer_out_bwd(s["p"], wts["conv_a"], wts["gh"], l, s["o_tot"], dy_cat,
                                                          "mixer_out_bwd")
    part_f = _gla_bwd(s["p"], wts["gpad_f"], wts["bias_f"], l, s["sp_f"], d_o, None, True, "gla_bwd_f")
    dp, dgp_b, dbias_b = _gla_bwd(s["p"], wts["gpad_b"], wts["bias_b"], l, s["sp_b"], d_o,
                                  list(part_f[:4]) + [dgb, dgc, dgv, dgo], False, "gla_bwd_b")
    early = [dw_up[None], dw_down.reshape(1, N_DEV, DFF // N_DEV, D)] if early_ffn else None
    dh1 = _mm(dp, big["w_in"], dims=NT, grid=(L // tm1, 1, 1),
              a_spec=((tm1, D_INP), lambda i, j, k: (i, 0)), b_spec=((None, D, D_INP), lambda i, j, k: (0, 0, 0)),
              o_spec=((tm1, D), lambda i, j, k: (i, 0)), out_shape=jax.ShapeDtypeStruct((L, D), BF16),
              tile=(tm1, D), name="proj_in_dx", comm=_grads_d2d_comm(early) if early_ffn else None)
    if early_ffn:
        dh1, early_sibling = dh1[0], dh1[1:]
        early = [_pair_sum(g, r, c_idx, BF16, "grads_pair_sum") for g, r in zip(early, early_sibling)]
    dw_in = _mm(s["h1"], dp, dims=TN, grid=(1, D_INP // 640, nkt),
                a_spec=((tk, D), lambda i, j, k: (k, 0)), b_spec=((tk, 640), lambda i, j, k: (k, j)),
                o_spec=((D, 640), lambda i, j, k: (0, j)), out_shape=jax.ShapeDtypeStruct((D, D_INP), F32),
                tile=(D, 640), name="proj_in_dw", comm=_grads_ici_comm(early) if early_ffn else None)
    if early_ffn:
        dw_in, early = dw_in[0], dw_in[1:]
    dx0, dg1 = _norm_bwd(s["x"], wts["g1"], l, dh1, dx1, "norm_bwd_mix_pre")
    grads = dict(
        norm_mix_pre=dg1[0], norm_mix_post=dg2[0], norm_ffn_pre=dg3[0], norm_ffn_post=dg4[0],
        gate_bias_fwd=part_f[5][0], gate_bias_bwd=dbias_b[0], gla_head_norm=dgh[0],
        w_in=_blocks_from_full(dw_in, (D, SH_IN), 1), w_out=dw_out.reshape(N_DEV, D // N_DEV, D), w_up=dw_up,
        w_down=dw_down.reshape(N_DEV, DFF // N_DEV, D),
        conv_a=_blocks_from_full(dconv_a, (3, DC // N_DEV), 1),
        gate_up_fwd=_blocks_from_full(part_f[4][:RANK], (RANK, DK // N_DEV), 1),
        gate_up_bwd=_blocks_from_full(dgp_b[RANK:2 * RANK], (RANK, DK // N_DEV), 1),
        conv_ffn=dcw.reshape(N_DEV, 3, SH_FF))
    return dx0, grads, parts, early


def _matmul_weights(g_in, g_out, g_up, g_down):
    w_in = jnp.concatenate([g_in[:, d] for d in range(N_DEV)] + [jnp.zeros((1, D, D_INP - D_IN), BF16)], axis=2)
    return dict(w_in=w_in, w_out=g_out.reshape(1, D, D), w_up=g_up, w_down=g_down.reshape(1, FF_HALF, SH_FF, D))


def _small_weights(g_small, rep):
    small = {n: jnp.moveaxis(t, 0, 1) for n, t in jax.vmap(_unpack_small)(
        jnp.concatenate([g_small[0], jnp.zeros((N_DEV, ROWS_REP, LANES), F32)], axis=1)).items()
        if n in [s[0] for s in SMALL_SHARDED]}
    conv_a = jnp.concatenate([small["conv_a"][:, d] for d in range(N_DEV)], axis=2)
    gate_f = jnp.concatenate([small["gate_up_fwd"][:, d] for d in range(N_DEV)], axis=2).astype(BF16)
    gate_b = jnp.concatenate([small["gate_up_bwd"][:, d] for d in range(N_DEV)], axis=2).astype(BF16)
    zeros = jnp.zeros((DEPTH, LR_BLK, DK), BF16)
    return dict(
        conv_a=conv_a, cw=small["conv_ffn"],
        gpad_f=zeros.at[:, :RANK].set(gate_f), gpad_b=zeros.at[:, RANK:2 * RANK].set(gate_b),
        bias_f=rep["gate_bias_fwd"][:, None, :], bias_b=rep["gate_bias_bwd"][:, None, :],
        gh=rep["gla_head_norm"][:, None, :],
        g1=rep["norm_mix_pre"][:, None, :], g2=rep["norm_mix_post"][:, None, :],
        g3=rep["norm_ffn_pre"][:, None, :], g4=rep["norm_ffn_post"][:, None, :])


def kernel(x, norm_mix_pre, norm_mix_post, norm_ffn_pre, norm_ffn_post, w_in, conv_a, gate_up_fwd, gate_bias_fwd, gate_up_bwd, gate_bias_bwd, gla_head_norm, w_out, w_up, conv_ffn, w_down, loss_target, m_norm_mix_pre, m_norm_mix_post, m_norm_ffn_pre, m_norm_ffn_post, m_w_in, m_conv_a, m_gate_up_fwd, m_gate_bias_fwd, m_gate_up_bwd, m_gate_bias_bwd, m_gla_head_norm, m_w_out, m_w_up, m_conv_ffn, m_w_down, v_norm_mix_pre, v_norm_mix_post, v_norm_ffn_pre, v_norm_ffn_post, v_w_in, v_conv_a, v_gate_up_fwd, v_gate_bias_fwd, v_gate_up_bwd, v_gate_bias_bwd, v_gla_head_norm, v_w_out, v_w_up, v_conv_ffn, v_w_down):
    w = dict(norm_mix_pre=norm_mix_pre, norm_mix_post=norm_mix_post, norm_ffn_pre=norm_ffn_pre,
             norm_ffn_post=norm_ffn_post, w_in=w_in, conv_a=conv_a, gate_up_fwd=gate_up_fwd,
             gate_bias_fwd=gate_bias_fwd, gate_up_bwd=gate_up_bwd, gate_bias_bwd=gate_bias_bwd,
             gla_head_norm=gla_head_norm, w_out=w_out, w_up=w_up, conv_ffn=conv_ffn, w_down=w_down)
    m = dict(norm_mix_pre=m_norm_mix_pre, norm_mix_post=m_norm_mix_post, norm_ffn_pre=m_norm_ffn_pre,
             norm_ffn_post=m_norm_ffn_post, w_in=m_w_in, conv_a=m_conv_a, gate_up_fwd=m_gate_up_fwd,
             gate_bias_fwd=m_gate_bias_fwd, gate_up_bwd=m_gate_up_bwd, gate_bias_bwd=m_gate_bias_bwd,
             gla_head_norm=m_gla_head_norm, w_out=m_w_out, w_up=m_w_up, conv_ffn=m_conv_ffn, w_down=m_w_down)
    v = dict(norm_mix_pre=v_norm_mix_pre, norm_mix_post=v_norm_mix_post, norm_ffn_pre=v_norm_ffn_pre,
             norm_ffn_post=v_norm_ffn_post, w_in=v_w_in, conv_a=v_conv_a, gate_up_fwd=v_gate_up_fwd,
             gate_bias_fwd=v_gate_bias_fwd, gate_up_bwd=v_gate_up_bwd, gate_bias_bwd=v_gate_bias_bwd,
             gla_head_norm=v_gla_head_norm, w_out=v_w_out, w_up=v_w_up, conv_ffn=v_conv_ffn, w_down=v_w_down)
    axes = ("x", "y", "c")
    L = x.shape[1]
    x0 = x.reshape(L, D)
    target = loss_target.reshape(L, D)

    w_small = _pack_small(w)
    w16 = {n: w[n].astype(BF16) for n in BIG}
    gathered = _all_gather([w16[n][0:1] for n in BIG] + [w_small[None, :ROWS_SSH]], "gather_weights")
    wts = _small_weights(gathered[-1], w)
    big = _matmul_weights(*gathered[:-1])

    h1 = _norm_cast(x0, wts["g1"], 0, "norm_first")
    xl, saved = x0, []
    for l in range(DEPTH):
        nxt = [w16[n][l + 1:l + 2] for n in BIG] if l + 1 < DEPTH else None
        xl, h1, s, gathered = _layer_fwd(xl, h1, wts, big, l, min(l + 1, DEPTH - 1), nxt)
        saved.append(s)
        if nxt:
            big = _matmul_weights(*gathered)
    dx, sq = _loss_grad(xl, target, "loss_grad")
    loss = lax.psum(0.5 * jnp.sum(sq) / D, axes)

    c_idx = lax.axis_index("c").astype(jnp.int32).reshape(1)
    layer_grads, layer_parts, pending = [None] * DEPTH, [None] * DEPTH, None
    for l in reversed(range(DEPTH)):
        dx, layer_grads[l], done, early = _layer_bwd(dx, wts, saved[l], l, pending, c_idx, l == 0)
        if pending:
            layer_parts[l + 1] = done
        pending = [layer_grads[l][n][None] for n in BIG]
    small_names = [n for n, _, _ in SMALL_SHARDED] + [n for n, _ in REPLICATED]
    stacked = {n: jnp.stack([g[n] for g in layer_grads]) for n in small_names}
    g_small = _pack_small_grads({n: jnp.moveaxis(stacked[n], 0, 1) for n, _, _ in SMALL_SHARDED}, stacked)
    last = [g for n, g in zip(BIG, pending) if n in ("w_in", "w_out")] + [g_small[None]]
    from_sibling = _exchange_sibling(last, "grads_to_sibling")
    pairs = [_pair_sum(g, r, c_idx, BF16 if i < 2 else F32, "grads_pair_sum")
             for i, (g, r) in enumerate(zip(last, from_sibling))]
    parts = _exchange_chips(pairs, "grads_to_chips")
    layer_parts[0] = [parts[0], parts[1], early[0], early[1]]

    results = {}
    for i, n in enumerate(BIG):
        part = jnp.concatenate([layer_parts[l][i] for l in range(DEPTH)], axis=0)
        results[n] = _sum_adamw(part, w[n], m[n], v[n], "sum_adamw")
    small = _sum_adamw(parts[-1], w_small[None], _pack_small(m)[None], _pack_small(v)[None], "sum_adamw_small")
    small = [_unpack_small(buf[0]) for buf in small]
    outs = [loss, dx.reshape(x.shape)]
    for i in range(4):
        outs += [results[n][i] if n in BIG else small[i][n] for n in WEIGHT_ORDER]
    return tuple(outs)
```

```python
import math
from typing import Callable, NamedTuple

import jax
import jax.numpy as jnp
from jax import lax
from jax.experimental import pallas as pl
from jax.experimental.pallas import tpu as pltpu

F32 = jnp.float32
BF16 = jnp.bfloat16

DEPTH = 4
D = 1024
DC = 512
DG = 512
HEADS = 4
HV = 128
HK = 64
DK = 256
RANK = 16
CH = 64
DFF = 2816
D_IN = 3104
D_INP = 3200
LR_BLK = 128
EPS = 1e-6
HB = 16
N_DEV = 8
N_CHIP = 4
LANES = 1024
SH_IN = D_IN // N_DEV
SH_FF = 2 * DFF // N_DEV
FF_HALF = N_DEV // 2

ADAM_LR, ADAM_B1, ADAM_B2, ADAM_EPS, ADAM_WD, ADAM_STEP = 0.001, 0.9, 0.999, 1e-08, 0.01, 10

VMEM_LIMIT = 48 * 1024 * 1024
TILE_TOKENS = 512
TILE_GLA = 512
TILE_FFN = 1024
TILE_NORM = 1024
TILE_MM = 2048
TILE_MM_KIN = 1024
TILE_MM_TOKENS = 2048

COL_GB, COL_GC, COL_GV = 0, 1, 2
COL_Q, COL_K = 6, 7
COL_V, COL_GO = 4, 5
COL_LR = 24

BIG = ("w_in", "w_out", "w_up", "w_down")
SMALL_SHARDED = (
    ("conv_a", (DEPTH, 3, DC // N_DEV), 2),
    ("gate_up_fwd", (DEPTH, RANK, DK // N_DEV), 2),
    ("gate_up_bwd", (DEPTH, RANK, DK // N_DEV), 2),
    ("conv_ffn", (DEPTH, 3, SH_FF), 2),
)
REPLICATED = (
    ("norm_mix_pre", (DEPTH, D)), ("norm_mix_post", (DEPTH, D)), ("norm_ffn_pre", (DEPTH, D)),
    ("norm_ffn_post", (DEPTH, D)), ("gate_bias_fwd", (DEPTH, DK)), ("gate_bias_bwd", (DEPTH, DK)),
    ("gla_head_norm", (DEPTH, HV)),
)
WEIGHT_ORDER = ("norm_mix_pre", "norm_mix_post", "norm_ffn_pre", "norm_ffn_post", "w_in", "conv_a", "gate_up_fwd",
                "gate_bias_fwd", "gate_up_bwd", "gate_bias_bwd", "gla_head_norm", "w_out", "w_up", "conv_ffn", "w_down")


def _rows_for(n_elems):
    return (-(-n_elems // LANES) + 7) // 8 * 8


ROWS_SSH = _rows_for(sum(math.prod(s) for _, s, _ in SMALL_SHARDED))
ROWS_REP = _rows_for(sum(math.prod(s) for _, s in REPLICATED))
ROWS_SMALL = ROWS_SSH + ROWS_REP


def _params(sem):
    return pltpu.CompilerParams(dimension_semantics=sem, vmem_limit_bytes=VMEM_LIMIT)


def _silu_parts(x):
    s = 1.0 / (1.0 + jnp.exp(-x))
    return x * s, s


def _rstd(xf):
    return lax.rsqrt(jnp.mean(xf * xf, axis=-1, keepdims=True) + EPS)


NN, NT, TN = ((1,), (0,)), ((1,), (1,)), ((0,), (0,))


def _dot(a, b, dims):
    return lax.dot_general(a, b, (dims, ((), ())), preferred_element_type=F32)


def _mm(a, b, *, dims, grid, a_spec, b_spec, o_spec, out_shape, tile, name, kin=0, comm=None):
    nk = grid[2]
    n_ci = len(comm.ins) if comm else 0
    n_co = len(comm.out_shapes) if comm else 0

    def body(*refs):
        a_ref, b_ref = refs[:2]
        ci = refs[2:2 + n_ci]
        o_ref = refs[2 + n_ci]
        co = refs[3 + n_ci:3 + n_ci + n_co]
        rest = refs[3 + n_ci + n_co:]
        if comm:
            sems, rest = rest[:3], rest[3:]
            step = (pl.program_id(0) * grid[1] + pl.program_id(1)) * grid[2] + pl.program_id(2)

            @pl.when(step == 0)
            def _():
                comm.start(ci, co, *sems)

        if kin:
            prod = _dot(a_ref[0], b_ref[0], dims)
            for d in range(1, kin):
                prod = prod + _dot(a_ref[d], b_ref[d], dims)
        else:
            prod = _dot(a_ref[...], b_ref[...], dims)
        if nk == 1:
            o_ref[...] = prod.astype(o_ref.dtype)
        else:
            acc_ref = rest[0]
            k = pl.program_id(2)

            @pl.when(k == 0)
            def _():
                acc_ref[...] = prod

            @pl.when(k > 0)
            def _():
                acc_ref[...] += prod

            @pl.when(k == nk - 1)
            def _():
                o_ref[...] = acc_ref[...].astype(o_ref.dtype)

        if comm:
            @pl.when(step == grid[0] * grid[1] * grid[2] - 1)
            def _():
                comm.wait(ci, co, *sems)

    acc = [pltpu.VMEM(tile, F32)] if nk > 1 else []
    if not comm:
        return pl.pallas_call(
            body, name=name, grid=grid, in_specs=[pl.BlockSpec(*a_spec), pl.BlockSpec(*b_spec)],
            out_specs=pl.BlockSpec(*o_spec), out_shape=out_shape, scratch_shapes=acc,
            compiler_params=_params(("parallel", "parallel", "arbitrary")),
        )(a, b)
    return pl.pallas_call(
        body, name=name, grid=grid,
        in_specs=[pl.BlockSpec(*a_spec), pl.BlockSpec(*b_spec)] + [HBM_SPEC] * n_ci,
        out_specs=[pl.BlockSpec(*o_spec)] + [HBM_SPEC] * n_co, out_shape=[out_shape] + list(comm.out_shapes),
        scratch_shapes=[pltpu.SemaphoreType.DMA((s,)) for s in comm.sems] + acc,
        input_output_aliases={2 + i: 1 + o for i, o in comm.aliases.items()},
        compiler_params=_params(("arbitrary", "arbitrary", "arbitrary")),
    )(a, b, *comm.ins)


def _halo_maps(tm, n_rows):
    r, last = tm // HB, n_rows // HB - 1
    return (lambda i: jnp.maximum(i * r - 1, 0)), (lambda i: jnp.minimum((i + 1) * r, last))


def _shift(x, prev_blk, next_blk):
    tm = x.shape[0]
    xs = jnp.concatenate([prev_blk, x, next_blk], axis=0)
    n = xs.shape[0]
    down = pltpu.roll(xs, 1, 0)[HB:HB + tm]
    up = pltpu.roll(xs, n - 1, 0)[HB:HB + tm]
    return down, up


def _edge_scales(i, n):
    return jnp.where(i > 0, 1.0, 0.0).astype(F32), jnp.where(i < n - 1, 1.0, 0.0).astype(F32)


def _gain_spec(l, width=D):
    return pl.BlockSpec((None, 1, width), lambda *_: (l, 0, 0))


def _norm_cast(x, g, l, name):
    L = x.shape[0]
    tm = min(TILE_NORM, L)

    def body(x_ref, g_ref, o_ref):
        xf = x_ref[...]
        o_ref[...] = (xf * _rstd(xf) * g_ref[...]).astype(BF16)

    return pl.pallas_call(
        body, name=name, grid=(L // tm,), in_specs=[pl.BlockSpec((tm, D), lambda i: (i, 0)), _gain_spec(l)],
        out_specs=pl.BlockSpec((tm, D), lambda i: (i, 0)), out_shape=jax.ShapeDtypeStruct((L, D), BF16),
        compiler_params=_params(("parallel",)),
    )(x, g)


def _post_pre(x, y, g_post, l_post, g_pre, l_pre, name):
    L = x.shape[0]
    tm = min(TILE_NORM, L)

    def body(x_ref, y_ref, gp_ref, gn_ref, x1_ref, h_ref):
        yf = y_ref[...].astype(F32)
        x1 = x_ref[...] + yf * _rstd(yf) * gp_ref[...]
        x1_ref[...] = x1
        h_ref[...] = (x1 * _rstd(x1) * gn_ref[...]).astype(BF16)

    tile = pl.BlockSpec((tm, D), lambda i: (i, 0))
    return pl.pallas_call(
        body, name=name, grid=(L // tm,), in_specs=[tile, tile, _gain_spec(l_post), _gain_spec(l_pre)],
        out_specs=[tile, tile],
        out_shape=[jax.ShapeDtypeStruct((L, D), F32), jax.ShapeDtypeStruct((L, D), BF16)],
        compiler_params=_params(("parallel",)),
    )(x, y, g_post, g_pre)


def _norm_bwd(yin, g, l, dout, dres, name):
    L = yin.shape[0]
    tm = min(TILE_NORM, L)
    with_res = dres is not None

    def body(*refs):
        if with_res:
            y_ref, g_ref, do_ref, dr_ref, din_ref, dg_ref = refs
        else:
            y_ref, g_ref, do_ref, din_ref, dg_ref = refs
        i = pl.program_id(0)
        y = y_ref[...].astype(F32)
        r = _rstd(y)
        do = do_ref[...].astype(F32)
        z = do * g_ref[...]
        din = r * z - y * (r * r * r) * jnp.mean(y * z, axis=-1, keepdims=True)
        if with_res:
            din = din + dr_ref[...]
        din_ref[...] = din.astype(din_ref.dtype)
        part = jnp.sum(do * y * r, axis=0, keepdims=True)

        @pl.when(i == 0)
        def _():
            dg_ref[...] = part

        @pl.when(i > 0)
        def _():
            dg_ref[...] += part

    tile = pl.BlockSpec((tm, D), lambda i: (i, 0))
    args = (yin, g, dout) + ((dres,) if with_res else ())
    return pl.pallas_call(
        body, name=name, grid=(L // tm,), in_specs=[tile, _gain_spec(l), tile] + ([tile] if with_res else []),
        out_specs=[tile, pl.BlockSpec((1, D), lambda i: (0, 0))],
        out_shape=[jax.ShapeDtypeStruct((L, D), F32 if with_res else BF16), jax.ShapeDtypeStruct((1, D), F32)],
        compiler_params=_params(("arbitrary",)),
    )(*args)


GLA_SUB = 256


def _gla_consts(fwd, tb):
    sub = min(GLA_SUB, tb)
    row = lax.broadcasted_iota(jnp.int32, (sub, sub), 0)
    col = lax.broadcasted_iota(jnp.int32, (sub, sub), 1)
    same = (row // CH) == (col // CH)
    tri = same & ((col <= row) if fwd else (col >= row))
    tri_t = same & ((col >= row) if fwd else (col <= row))
    row_st = lax.broadcasted_iota(jnp.int32, (HEADS * CH, CH), 0) & (CH - 1)
    col_st = lax.broadcasted_iota(jnp.int32, (HEADS * CH, CH), 1)
    tri_st = (col_st <= row_st) if fwd else (col_st >= row_st)
    lane_head = lax.broadcasted_iota(jnp.int32, (1, DK), 1) // HK
    head_masks = [lane_head == h for h in range(HEADS)]
    srow = lax.broadcasted_iota(jnp.int32, (DG, DK), 0) // HV
    scol = lax.broadcasted_iota(jnp.int32, (DG, DK), 1) // HK
    return tri.astype(BF16), tri_t.astype(BF16), tri_st, head_masks, srow == scol


def _dot_hilo(tri_b, x):
    hi = x.astype(BF16)
    lo = (x - hi.astype(F32)).astype(BF16)
    sub = tri_b.shape[0]
    return jnp.concatenate([_dot(tri_b, hi[r:r + sub], NN) + _dot(tri_b, lo[r:r + sub], NN)
                            for r in range(0, x.shape[0], sub)], axis=0)


def _gla_block_terms(q_ref, k_ref, lr_ref, gp_ref, bias_ref, tri_b):
    pre = _dot(lr_ref[...], gp_ref[...], NN) + bias_ref[...]
    sig_neg = 1.0 / (1.0 + jnp.exp(pre))
    a = (jnp.minimum(pre, 0.0) - jnp.log(1.0 + jnp.exp(-jnp.abs(pre)))) * (1.0 / 16.0)
    cum = _dot_hilo(tri_b, a)
    cl = jnp.concatenate([jnp.broadcast_to(jnp.min(cum[r:r + CH], axis=0, keepdims=True), (CH, DK))
                          for r in range(0, cum.shape[0], CH)], axis=0)
    e = jnp.exp(cum)
    einv = jnp.exp(-cum)
    eout = jnp.exp(cl - cum)
    q_in = q_ref[...].astype(F32) * e * (HK ** -0.5)
    k = k_ref[...].astype(F32)
    return dict(sig_neg=sig_neg, e=e, einv=einv, eout=eout, decay=jnp.exp(cl), q_in=q_in, k_in=k * einv,
                k_out=k * eout)


def _gla_chunk(t, c, head_masks):
    sl = slice(c * CH, (c + 1) * CH)
    tc = {n: x[sl] for n, x in t.items() if n != "decay"}
    tc["decay"] = jnp.max(t["decay"][c * CH:c * CH + 8], axis=0, keepdims=True)
    tc["q_st"] = jnp.concatenate([jnp.where(mh, tc["q_in"], 0.0) for mh in head_masks], axis=0).astype(BF16)
    return tc


def _gate_specs(l):
    return [pl.BlockSpec((None, LR_BLK, DK), lambda i: (l, 0, 0)), pl.BlockSpec((None, 1, DK), lambda i: (l, 0, 0))]


def _gla_fwd(p, gpad, bias, l, o_prev, fwd, name):
    L = p.shape[0]
    tb = min(TILE_GLA, L)
    nb, ncb, nch = L // tb, tb // CH, L // CH
    blk = (lambda i: i) if fwd else (lambda i: nb - 1 - i)
    with_prev = o_prev is not None

    def body(*refs):
        if with_prev:
            q_ref, k_ref, v_ref, lr_ref, gp_ref, bias_ref, op_ref, o_ref, sp_ref, s_ref = refs
        else:
            q_ref, k_ref, v_ref, lr_ref, gp_ref, bias_ref, o_ref, sp_ref, s_ref = refs
        i = pl.program_id(0)

        @pl.when(i == 0)
        def _():
            s_ref[...] = jnp.zeros_like(s_ref)

        tri_b, _, tri_st, head_masks, blockmask = _gla_consts(fwd, tb)
        terms = _gla_block_terms(q_ref, k_ref, lr_ref, gp_ref, bias_ref, tri_b)
        for c in (range(ncb) if fwd else reversed(range(ncb))):
            rows = pl.ds(c * CH, CH)
            t = _gla_chunk(terms, c, head_masks)
            v = v_ref[rows, :]
            scores = _dot(t["q_st"], t["k_in"].astype(BF16), NT)
            a_st = jnp.where(tri_st, scores, 0.0).astype(BF16)
            r = _dot(a_st, v, NN)
            o_intra = jnp.concatenate([r[h * CH:(h + 1) * CH, h * HV:(h + 1) * HV] for h in range(HEADS)], axis=1)
            s_b = s_ref[...].astype(BF16)
            sp_ref[c] = s_b
            o = o_intra + _dot(t["q_in"].astype(BF16), s_b, NT)
            if with_prev:
                o = o + op_ref[rows, :]
            o_ref[rows, :] = o
            kv_t = _dot(v, t["k_out"].astype(BF16), TN)
            s_ref[...] = s_ref[...] * t["decay"] + jnp.where(blockmask, kv_t, 0.0)

    def col(width, c):
        return pl.BlockSpec((tb, width), lambda i: (blk(i), c))

    in_specs = [col(DK, COL_Q), col(DK, COL_K), col(DG, COL_V), col(LR_BLK, COL_LR)] + _gate_specs(l)
    args = [p, p, p, p, gpad, bias]
    if with_prev:
        in_specs.append(pl.BlockSpec((tb, DG), lambda i: (blk(i), 0)))
        args.append(o_prev)
    return pl.pallas_call(
        body, name=name, grid=(nb,), in_specs=in_specs,
        out_specs=[pl.BlockSpec((tb, DG), lambda i: (blk(i), 0)), pl.BlockSpec((ncb, DG, DK), lambda i: (blk(i), 0, 0))],
        out_shape=[jax.ShapeDtypeStruct((L, DG), F32), jax.ShapeDtypeStruct((nch, DG, DK), BF16)],
        scratch_shapes=[pltpu.VMEM((DG, DK), F32)],
        compiler_params=_params(("arbitrary",)),
    )(*args)


P_COLS = dict(gb=(0, DC), gc=(DC, DC), gv=(2 * DC, DC), q=(3 * DC, DK), k=(3 * DC + DK, DK), v=(3 * DC + 2 * DK, DG),
              go=(3 * DC + 2 * DK + DG, DG), lr=(3 * DC + 2 * DK + 2 * DG, LR_BLK))


def _gla_bwd(p, gpad, bias, l, sprev, d_o, prev, fwd, name):
    L = p.shape[0]
    tb = min(TILE_GLA, L)
    nb, ncb = L // tb, tb // CH
    blk = (lambda i: nb - 1 - i) if fwd else (lambda i: i)
    with_prev = prev is not None

    def body(*refs):
        q_ref, k_ref, v_ref, lr_ref, gp_ref, bias_ref, sp_ref, do_ref = refs[:8]
        rest = refs[8:]
        if with_prev:
            pq_ref, pk_ref, pv_ref, plr_ref, dgb_ref, dgc_ref, dgv_ref, dgo_ref = rest[:8]
            dp_ref, dg_ref, db_ref, ds_ref = rest[8:]

            def put(what, rows, val):
                c0, width = P_COLS[what]
                dp_ref[rows, c0:c0 + width] = val

            for what, ref in (("gb", dgb_ref), ("gc", dgc_ref), ("gv", dgv_ref), ("go", dgo_ref)):
                put(what, slice(None), ref[...])
        else:
            dq_ref, dk_ref, dv_ref, dlr_ref, dg_ref, db_ref, ds_ref = rest
            out_of = dict(q=dq_ref, k=dk_ref, v=dv_ref, lr=dlr_ref)

            def put(what, rows, val):
                out_of[what][rows, :] = val

        i = pl.program_id(0)

        @pl.when(i == 0)
        def _():
            ds_ref[...] = jnp.zeros_like(ds_ref)
            dg_ref[...] = jnp.zeros_like(dg_ref)
            db_ref[...] = jnp.zeros_like(db_ref)

        tri_b, tri_t_b, tri_st, head_masks, blockmask = _gla_consts(fwd, tb)
        terms = _gla_block_terms(q_ref, k_ref, lr_ref, gp_ref, bias_ref, tri_b)
        dcum_of, dcl_of = [None] * ncb, [None] * ncb
        for c in (reversed(range(ncb)) if fwd else range(ncb)):
            rows = pl.ds(c * CH, CH)
            t = _gla_chunk(terms, c, head_masks)
            v = v_ref[rows, :]
            do = do_ref[rows, :]
            q_in, k_in, k_out = t["q_in"], t["k_in"], t["k_out"]
            q_b, k_in_b, k_out_b = q_in.astype(BF16), k_in.astype(BF16), k_out.astype(BF16)
            scores = _dot(t["q_st"], k_in_b, NT)
            a_st = jnp.where(tri_st, scores, 0.0).astype(BF16)
            s_prev = sp_ref[c]
            ds = ds_ref[...]
            ds_b = ds.astype(BF16)

            da_heads = [_dot(do[:, h * HV:(h + 1) * HV], v[:, h * HV:(h + 1) * HV], NT) for h in range(HEADS)]
            da_st = jnp.where(tri_st, jnp.concatenate(da_heads, axis=0), 0.0).astype(BF16)

            dv_heads = [_dot(a_st[h * CH:(h + 1) * CH, :], do[:, h * HV:(h + 1) * HV], TN) for h in range(HEADS)]
            dv = jnp.concatenate(dv_heads, axis=1) + _dot(k_out_b, ds_b, NT)

            x = _dot(da_st, k_in_b, NN)
            dq_in = _dot(do, s_prev, NN)
            for h in range(HEADS):
                dq_in = dq_in + jnp.where(head_masks[h], x[h * CH:(h + 1) * CH, :], 0.0)
            dk_in = _dot(da_st, t["q_st"], TN)
            dk_out = _dot(v, ds_b, NN)
            d_decay = jnp.sum(ds * s_prev.astype(F32), axis=0, keepdims=True)
            ds_ref[...] = ds * t["decay"] + jnp.where(blockmask, _dot(do, q_b, TN), 0.0)

            dq = dq_in * t["e"] * (HK ** -0.5)
            dk = dk_in * t["einv"] + dk_out * t["eout"]
            dko_ko = dk_out * k_out
            dcum_of[c] = dq_in * q_in - dk_in * k_in - dko_ko
            dcl = jnp.sum(dko_ko, axis=0, keepdims=True) + d_decay * t["decay"]
            dcl_of[c] = jnp.broadcast_to(dcl, (CH, DK))
            if with_prev:
                dq = dq + pq_ref[rows, :].astype(F32)
                dk = dk + pk_ref[rows, :].astype(F32)
                dv = dv + pv_ref[rows, :].astype(F32)
            put("q", rows, dq.astype(BF16))
            put("k", rows, dk.astype(BF16))
            put("v", rows, dv.astype(BF16))

        da = _dot_hilo(tri_t_b, jnp.concatenate(dcum_of, axis=0)) + jnp.concatenate(dcl_of, axis=0)
        dpre = da * terms["sig_neg"] * (1.0 / 16.0)
        dpre_b = dpre.astype(BF16)
        dlr = _dot(dpre_b, gp_ref[...], NT)
        dg_ref[...] += _dot(lr_ref[...], dpre_b, TN)
        db_ref[...] += jnp.sum(dpre, axis=0, keepdims=True)
        if with_prev:
            dlr = dlr + plr_ref[...].astype(F32)
        put("lr", slice(None), dlr.astype(BF16))

    def col(width, c):
        return pl.BlockSpec((tb, width), lambda i: (blk(i), c))

    in_specs = [col(DK, COL_Q), col(DK, COL_K), col(DG, COL_V), col(LR_BLK, COL_LR)] + _gate_specs(l) + [
        pl.BlockSpec((ncb, DG, DK), lambda i: (blk(i), 0, 0)), col(DG, 0)]
    args = [p, p, p, p, gpad, bias, sprev, d_o]
    tiles = [col(DK, 0), col(DK, 0), col(DG, 0), col(LR_BLK, 0)]
    shapes = [jax.ShapeDtypeStruct((L, DK), BF16), jax.ShapeDtypeStruct((L, DK), BF16),
              jax.ShapeDtypeStruct((L, DG), BF16), jax.ShapeDtypeStruct((L, LR_BLK), BF16)]
    if with_prev:
        in_specs += tiles + [col(DC, 0)] * 4
        args += list(prev)
        tiles, shapes = [col(D_INP, 0)], [jax.ShapeDtypeStruct((L, D_INP), BF16)]
    return pl.pallas_call(
        body, name=name, grid=(nb,), in_specs=in_specs,
        out_specs=tiles + [pl.BlockSpec((LR_BLK, DK), lambda i: (0, 0)), pl.BlockSpec((1, DK), lambda i: (0, 0))],
        out_shape=shapes + [jax.ShapeDtypeStruct((LR_BLK, DK), F32), jax.ShapeDtypeStruct((1, DK), F32)],
        scratch_shapes=[pltpu.VMEM((DG, DK), F32)],
        compiler_params=_params(("arbitrary",)),
    )(*args)


def _mixer_out(p, conv_a, gh, l, o_tot, name):
    L = p.shape[0]
    tm = min(TILE_TOKENS, L)
    n = L // tm
    pmap, nmap = _halo_maps(tm, L)

    def body(gb_ref, gc_ref, gcp_ref, gcn_ref, gv_ref, gvp_ref, gvn_ref, go_ref, cw_ref, o_ref, gh_ref, y_ref):
        ps, ns = _edge_scales(pl.program_id(0), n)
        z = gc_ref[...].astype(F32) * gv_ref[...].astype(F32)
        zp = gcp_ref[...].astype(F32) * gvp_ref[...].astype(F32) * ps
        zn = gcn_ref[...].astype(F32) * gvn_ref[...].astype(F32) * ns
        z_dn, z_up = _shift(z, zp, zn)
        conv = cw_ref[0:1, :] * z_dn + cw_ref[1:2, :] * z + cw_ref[2:3, :] * z_up
        y_ref[:, 0:DC] = (gb_ref[...].astype(F32) * conv).astype(BF16)
        o = o_ref[...]
        go = go_ref[...].astype(F32)
        for h in range(HEADS):
            oh = o[:, h * HV:(h + 1) * HV]
            on = oh * _rstd(oh) * gh_ref[...]
            act, _ = _silu_parts(go[:, h * HV:(h + 1) * HV])
            y_ref[:, DC + h * HV:DC + (h + 1) * HV] = (act * on).astype(BF16)

    def main(c):
        return pl.BlockSpec((tm, DC), lambda i: (i, c))

    def halo(c, imap):
        return pl.BlockSpec((HB, DC), lambda i: (imap(i), c))

    return pl.pallas_call(
        body, name=name, grid=(n,),
        in_specs=[main(COL_GB), main(COL_GC), halo(COL_GC, pmap), halo(COL_GC, nmap),
                  main(COL_GV), halo(COL_GV, pmap), halo(COL_GV, nmap), main(COL_GO),
                  pl.BlockSpec((None, 3, DC), lambda i: (l, 0, 0)), pl.BlockSpec((tm, DG), lambda i: (i, 0)),
                  _gain_spec(l, HV)],
        out_specs=pl.BlockSpec((tm, D), lambda i: (i, 0)), out_shape=jax.ShapeDtypeStruct((L, D), BF16),
        compiler_params=_params(("parallel",)),
    )(p, p, p, p, p, p, p, p, conv_a, o_tot, gh)


def _mixer_out_bwd(p, conv_a, gh, l, o_tot, dy, name):
    L = p.shape[0]
    tm = min(TILE_TOKENS, L)
    n = L // tm
    pmap, nmap = _halo_maps(tm, L)

    def body(gb_ref, gbp_ref, gbn_ref, gc_ref, gcp_ref, gcn_ref, gv_ref, gvp_ref, gvn_ref, go_ref, cw_ref, o_ref,
             gh_ref, dy_ref, dyp_ref, dyn_ref, dgb_ref, dgc_ref, dgv_ref, dgo_ref, do_ref, dcw_ref, dgh_ref):
        i = pl.program_id(0)
        ps, ns = _edge_scales(i, n)
        gb = gb_ref[...].astype(F32)
        gc = gc_ref[...].astype(F32)
        gv = gv_ref[...].astype(F32)
        z = gc * gv
        zp = gcp_ref[...].astype(F32) * gvp_ref[...].astype(F32) * ps
        zn = gcn_ref[...].astype(F32) * gvn_ref[...].astype(F32) * ns
        z_dn, z_up = _shift(z, zp, zn)
        w0, w1, w2 = cw_ref[0:1, :], cw_ref[1:2, :], cw_ref[2:3, :]
        conv = w0 * z_dn + w1 * z + w2 * z_up
        dya = dy_ref[:, 0:DC].astype(F32)
        dgb_ref[...] = (dya * conv).astype(BF16)
        dc = dya * gb
        dcp = dyp_ref[...].astype(F32) * gbp_ref[...].astype(F32) * ps
        dcn = dyn_ref[...].astype(F32) * gbn_ref[...].astype(F32) * ns
        dc_dn, dc_up = _shift(dc, dcp, dcn)
        dz = w0 * dc_up + w1 * dc + w2 * dc_dn
        dgc_ref[...] = (dz * gv).astype(BF16)
        dgv_ref[...] = (dz * gc).astype(BF16)
        dcw = [jnp.sum(zs * dc, axis=0, keepdims=True) for zs in (z_dn, z, z_up)]

        o = o_ref[...]
        go = go_ref[...].astype(F32)
        dgh = jnp.zeros((1, HV), F32)
        for h in range(HEADS):
            sl = slice(h * HV, (h + 1) * HV)
            oh = o[:, sl]
            r = _rstd(oh)
            act, sg = _silu_parts(go[:, sl])
            dyb = dy_ref[:, DC + h * HV:DC + (h + 1) * HV].astype(F32)
            on = oh * r * gh_ref[...]
            dgo_ref[:, sl] = (dyb * on * (sg + act * (1.0 - sg))).astype(BF16)
            don = dyb * act
            zz = don * gh_ref[...]
            do_ref[:, sl] = (r * zz - oh * (r * r * r) * jnp.mean(oh * zz, axis=-1, keepdims=True)).astype(BF16)
            dgh = dgh + jnp.sum(don * oh * r, axis=0, keepdims=True)

        @pl.when(i == 0)
        def _():
            dcw_ref[...] = jnp.zeros_like(dcw_ref)
            dgh_ref[...] = jnp.zeros_like(dgh_ref)

        for kk in range(3):
            dcw_ref[kk:kk + 1, :] += dcw[kk]
        dgh_ref[...] += dgh

    def main(c):
        return pl.BlockSpec((tm, DC), lambda i: (i, c))

    def halo(c, imap):
        return pl.BlockSpec((HB, DC), lambda i: (imap(i), c))

    tile = pl.BlockSpec((tm, DC), lambda i: (i, 0))
    return pl.pallas_call(
        body, name=name, grid=(n,),
        in_specs=[main(COL_GB), halo(COL_GB, pmap), halo(COL_GB, nmap), main(COL_GC), halo(COL_GC, pmap),
                  halo(COL_GC, nmap), main(COL_GV), halo(COL_GV, pmap), halo(COL_GV, nmap), main(COL_GO),
                  pl.BlockSpec((None, 3, DC), lambda i: (l, 0, 0)), tile, _gain_spec(l, HV),
                  pl.BlockSpec((tm, D), lambda i: (i, 0)), halo(0, pmap), halo(0, nmap)],
        out_specs=[tile, tile, tile, tile, tile, pl.BlockSpec((3, DC), lambda i: (0, 0)),
                   pl.BlockSpec((1, HV), lambda i: (0, 0))],
        out_shape=[jax.ShapeDtypeStruct((L, DC), BF16)] * 5
        + [jax.ShapeDtypeStruct((3, DC), F32), jax.ShapeDtypeStruct((1, HV), F32)],
        compiler_params=_params(("arbitrary",)),
    )(p, p, p, p, p, p, p, p, p, p, conv_a, o_tot, gh, dy, dy, dy)


def _ffn_specs(tm, L, l, row_axis, sh_axis):
    pmap, nmap = _halo_maps(tm, L)

    def u(off, imap=None, rows=tm):
        if imap is None:
            return pl.BlockSpec((None, rows, SH_FF), lambda *g: (g[sh_axis] + off, g[row_axis], 0))
        return pl.BlockSpec((None, rows, SH_FF), lambda *g: (g[sh_axis] + off, imap(g[row_axis]), 0))

    def cw(off):
        return pl.BlockSpec((None, None, 3, SH_FF), lambda *g: (l, g[sh_axis] + off, 0, 0))

    u_specs = [u(0), u(0, pmap, HB), u(0, nmap, HB), u(FF_HALF), u(FF_HALF, pmap, HB), u(FF_HALF, nmap, HB)]
    return u_specs, [cw(0), cw(FF_HALF)]


def _conv3(x_ref, xp_ref, xn_ref, cw_ref, ps, ns):
    x = x_ref[...].astype(F32)
    x_dn, x_up = _shift(x, xp_ref[...].astype(F32) * ps, xn_ref[...].astype(F32) * ns)
    return cw_ref[0:1, :] * x_dn + cw_ref[1:2, :] * x + cw_ref[2:3, :] * x_up, (x_dn, x, x_up)


def _carry(body, *, name, grid, args, in_specs, out_specs, out_shape, scratch, comm):
    n_in, n_out = len(args), len(out_shape)
    n_ci = len(comm.ins) if comm else 0
    n_co = len(comm.out_shapes) if comm else 0

    def wrapped(*refs):
        ins, refs = refs[:n_in], refs[n_in:]
        ci, refs = refs[:n_ci], refs[n_ci:]
        outs, refs = refs[:n_out], refs[n_out:]
        co, refs = refs[:n_co], refs[n_co:]
        if comm:
            sems, refs = refs[:3], refs[3:]
            step = 0
            for ax, size in enumerate(grid):
                step = step * size + pl.program_id(ax)

            @pl.when(step == 0)
            def _():
                comm.start(ci, co, *sems)

        body(ins, outs, refs)
        if comm:
            @pl.when(step == math.prod(grid) - 1)
            def _():
                comm.wait(ci, co, *sems)

    return pl.pallas_call(
        wrapped, name=name, grid=grid, in_specs=list(in_specs) + [HBM_SPEC] * n_ci,
        out_specs=list(out_specs) + [HBM_SPEC] * n_co,
        out_shape=list(out_shape) + list(comm.out_shapes if comm else ()),
        scratch_shapes=([pltpu.SemaphoreType.DMA((s,)) for s in comm.sems] if comm else []) + list(scratch),
        input_output_aliases={n_in + i: n_out + o for i, o in comm.aliases.items()} if comm else {},
        compiler_params=_params(("arbitrary",) * len(grid)),
    )(*args, *(comm.ins if comm else ()))


def _ffn_act(u8, cw, l, name):
    L = u8.shape[1]
    tm = min(TILE_FFN, L)
    n = L // tm
    u_specs, cw_specs = _ffn_specs(tm, L, l, 0, 1)

    def body(ins, outs, scratch):
        g_ref, gp_ref, gn_ref, v_ref, vp_ref, vn_ref, cwg_ref, cwv_ref = ins
        ps, ns = _edge_scales(pl.program_id(0), n)
        gate, _ = _conv3(g_ref, gp_ref, gn_ref, cwg_ref, ps, ns)
        val, _ = _conv3(v_ref, vp_ref, vn_ref, cwv_ref, ps, ns)
        act, _ = _silu_parts(gate)
        outs[0][...] = (act * val).astype(BF16)
        outs[1][0] = gate.astype(BF16)
        outs[1][1] = val.astype(BF16)

    pair = pl.BlockSpec((2, None, tm, SH_FF), lambda i, d: (0, d, i, 0))
    return _carry(
        body, name=name, grid=(n, FF_HALF), args=(u8, u8, u8, u8, u8, u8, cw, cw), in_specs=u_specs + cw_specs,
        out_specs=[pl.BlockSpec((None, tm, SH_FF), lambda i, d: (d, i, 0)), pair],
        out_shape=[jax.ShapeDtypeStruct((FF_HALF, L, SH_FF), BF16),
                   jax.ShapeDtypeStruct((2, FF_HALF, L, SH_FF), BF16)], scratch=[], comm=None)


def _ffn_act_bwd(conv, da, name):
    L = conv.shape[2]
    tm = min(TILE_FFN, L)

    def body(ins, outs, scratch):
        c_ref, da_ref = ins
        act, sg = _silu_parts(c_ref[0].astype(F32))
        da_f = da_ref[...].astype(F32)
        outs[0][0] = (da_f * c_ref[1].astype(F32) * (sg + act * (1.0 - sg))).astype(BF16)
        outs[0][1] = (da_f * act).astype(BF16)

    pair = pl.BlockSpec((2, None, tm, SH_FF), lambda d, i: (0, d, i, 0))
    return _carry(
        body, name=name, grid=(FF_HALF, L // tm), args=(conv, da),
        in_specs=[pair, pl.BlockSpec((None, tm, SH_FF), lambda d, i: (d, i, 0))], out_specs=[pair],
        out_shape=[jax.ShapeDtypeStruct((2, FF_HALF, L, SH_FF), BF16)], scratch=[], comm=None)[0]


def _ffn_conv_t(du8, u8, cw, l, name):
    L = du8.shape[1]
    tm = min(TILE_FFN, L)
    n = L // tm
    pmap, nmap = _halo_maps(tm, L)

    def body(ins, outs, scratch):
        x_ref, xp_ref, xn_ref, u_ref, cw_ref = ins
        d_u_ref, dcw_ref = outs
        i = pl.program_id(1)
        ps, ns = _edge_scales(i, n)
        x = x_ref[...].astype(F32)
        x_dn, x_up = _shift(x, xp_ref[...].astype(F32) * ps, xn_ref[...].astype(F32) * ns)
        d_u_ref[...] = (cw_ref[0:1, :] * x_up + cw_ref[1:2, :] * x + cw_ref[2:3, :] * x_dn).astype(BF16)

        @pl.when(i == 0)
        def _():
            dcw_ref[...] = jnp.zeros_like(dcw_ref)

        u = u_ref[...].astype(F32)
        for kk, xs in enumerate((x_up, x, x_dn)):
            dcw_ref[kk:kk + 1, :] += jnp.sum(u * xs, axis=0, keepdims=True)

    tile = pl.BlockSpec((None, tm, SH_FF), lambda d, i: (d, i, 0))
    return _carry(
        body, name=name, grid=(N_DEV, n), args=(du8, du8, du8, u8, cw),
        in_specs=[tile, pl.BlockSpec((None, HB, SH_FF), lambda d, i: (d, pmap(i), 0)),
                  pl.BlockSpec((None, HB, SH_FF), lambda d, i: (d, nmap(i), 0)), tile,
                  pl.BlockSpec((None, None, 3, SH_FF), lambda d, i: (l, d, 0, 0))],
        out_specs=[tile, pl.BlockSpec((None, 3, SH_FF), lambda d, i: (d, 0, 0))],
        out_shape=[jax.ShapeDtypeStruct((N_DEV, L, SH_FF), BF16), jax.ShapeDtypeStruct((N_DEV, 3, SH_FF), F32)],
        scratch=[], comm=None)


def _loss_grad(xl, target, name):
    L = xl.shape[0]
    tm = min(TILE_NORM, L)

    def body(x_ref, t_ref, dx_ref, sq_ref):
        i = pl.program_id(0)
        err = x_ref[...] - t_ref[...]
        dx_ref[...] = err * (1.0 / D)
        part = jnp.sum(err * err, axis=0, keepdims=True)

        @pl.when(i == 0)
        def _():
            sq_ref[...] = part

        @pl.when(i > 0)
        def _():
            sq_ref[...] += part

    tile = pl.BlockSpec((tm, D), lambda i: (i, 0))
    return pl.pallas_call(
        body, name=name, grid=(L // tm,), in_specs=[tile, tile],
        out_specs=[tile, pl.BlockSpec((1, D), lambda i: (0, 0))],
        out_shape=[jax.ShapeDtypeStruct((L, D), F32), jax.ShapeDtypeStruct((1, D), F32)],
        compiler_params=_params(("arbitrary",)),
    )(xl, target)


MESH = pl.DeviceIdType.MESH
HBM_SPEC = pl.BlockSpec(memory_space=pltpu.HBM)


def _position():
    return lax.axis_index("x"), lax.axis_index("y"), lax.axis_index("c")


def _other_chips(x, y):
    return [(1 - x, y), (x, 1 - y), (1 - x, 1 - y)]


def _all_gather(shards, name):
    n = len(shards)

    def body(*refs):
        x_refs, out_refs = refs[:n], refs[n:2 * n]
        send_sems, recv_sems, local_sems = refs[2 * n:]
        x, y, c = _position()
        me, sibling = (x, y, c), (x, y, 1 - c)
        chips = _other_chips(x, y)

        def slot(t, px, py, pc):
            return out_refs[t].at[:, 4 * px + 2 * py + pc]

        def copy(t, k, block, to, from_input=False):
            return pltpu.make_async_remote_copy(
                src_ref=x_refs[t] if from_input else slot(t, *block), dst_ref=slot(t, *block),
                send_sem=send_sems.at[k * n + t], recv_sem=recv_sems.at[k * n + t], device_id=to, device_id_type=MESH)

        mine = [pltpu.make_async_copy(x_refs[t], slot(t, *me), local_sems.at[t]) for t in range(n)]
        for cp in mine:
            cp.start()
        first = [copy(t, 0, me, sibling, True) for t in range(n)]
        first += [copy(t, 1 + j, me, (*chip, c), True) for j, chip in enumerate(chips) for t in range(n)]
        for cp in first:
            cp.start()
        passed = []
        for j, chip in enumerate(chips):
            for t in range(n):
                copy(t, 1 + j, (*chip, c), me).wait_recv()
                passed.append(copy(t, 4 + j, (*chip, c), sibling))
                passed[-1].start()
        for t in range(n):
            copy(t, 0, sibling, me).wait_recv()
        for j, chip in enumerate(chips):
            for t in range(n):
                copy(t, 4 + j, (*chip, 1 - c), me).wait_recv()
        for cp in first + passed:
            cp.wait_send()
        for cp in mine:
            cp.wait()

    return pl.pallas_call(
        body, name=name,
        out_shape=[jax.ShapeDtypeStruct((s.shape[0], N_DEV) + s.shape[1:], s.dtype) for s in shards],
        in_specs=[HBM_SPEC] * n, out_specs=[HBM_SPEC] * n,
        scratch_shapes=[pltpu.SemaphoreType.DMA((7 * n,)), pltpu.SemaphoreType.DMA((7 * n,)),
                        pltpu.SemaphoreType.DMA((n,))],
    )(*shards)


def _exchange_sibling(grads, name):
    n = len(grads)

    def body(*refs):
        g_refs, out_refs, send_sems, recv_sems = refs[:n], refs[n:2 * n], refs[2 * n], refs[2 * n + 1]
        x, y, c = _position()
        copies = [pltpu.make_async_remote_copy(
            src_ref=g_refs[t].at[:, 2 * k + (1 - c)], dst_ref=out_refs[t].at[:, k], send_sem=send_sems.at[k * n + t],
            recv_sem=recv_sems.at[k * n + t], device_id=(x, y, 1 - c), device_id_type=MESH)
            for k in range(N_CHIP) for t in range(n)]
        for cp in copies:
            cp.start()
        for cp in copies:
            cp.wait()

    return pl.pallas_call(
        body, name=name,
        out_shape=[jax.ShapeDtypeStruct((g.shape[0], N_CHIP) + g.shape[2:], g.dtype) for g in grads],
        in_specs=[HBM_SPEC] * n, out_specs=[HBM_SPEC] * n,
        scratch_shapes=[pltpu.SemaphoreType.DMA((N_CHIP * n,)), pltpu.SemaphoreType.DMA((N_CHIP * n,))],
    )(*grads)


class _Comm(NamedTuple):
    ins: tuple
    out_shapes: tuple
    aliases: dict
    sems: tuple
    start: Callable
    wait: Callable


def _comm_of(ins, out_shapes, aliases, sems, copies):
    def start(ci, co, send, recv, local):
        for cp in copies(ci, co, send, recv, local):
            cp.start()

    def wait(ci, co, send, recv, local):
        for cp in copies(ci, co, send, recv, local):
            cp.wait()

    return _Comm(tuple(ins), tuple(out_shapes), aliases, sems, start, wait)


def _remote(src, dst, send, recv, idx, to):
    return pltpu.make_async_remote_copy(src_ref=src, dst_ref=dst, send_sem=send.at[idx], recv_sem=recv.at[idx],
                                        device_id=to, device_id_type=MESH)


def _gather_ici_comm(shards):
    n = len(shards)

    def copies(ci, co, send, recv, local):
        x, y, c = _position()
        me = 4 * x + 2 * y + c
        mine = [pltpu.make_async_copy(ci[t], co[t].at[:, me], local.at[t]) for t in range(n)]
        return mine + [_remote(ci[t], co[t].at[:, me], send, recv, j * n + t, (cx, cy, c))
                       for j, (cx, cy) in enumerate(_other_chips(x, y)) for t in range(n)]

    outs = [jax.ShapeDtypeStruct((1, N_DEV) + s.shape[1:], s.dtype) for s in shards]
    return _comm_of(shards, outs, {}, (3 * n, 3 * n, n), copies)


def _gather_d2d_comm(partials):
    n = len(partials)

    def copies(ci, co, send, recv, local):
        x, y, c = _position()
        return [_remote(co[t].at[:, 4 * cx + 2 * cy + c], co[t].at[:, 4 * cx + 2 * cy + c], send, recv, k * n + t,
                        (x, y, 1 - c))
                for k, (cx, cy) in enumerate([(x, y)] + _other_chips(x, y)) for t in range(n)]

    outs = [jax.ShapeDtypeStruct(p.shape, p.dtype) for p in partials]
    return _comm_of(partials, outs, {t: t for t in range(n)}, (N_CHIP * n, N_CHIP * n, 1), copies)


def _grads_d2d_comm(grads):
    n = len(grads)

    def copies(ci, co, send, recv, local):
        x, y, c = _position()
        return [_remote(ci[t].at[:, 2 * k + (1 - c)], co[t].at[:, k], send, recv, k * n + t, (x, y, 1 - c))
                for k in range(N_CHIP) for t in range(n)]

    outs = [jax.ShapeDtypeStruct((g.shape[0], N_CHIP) + g.shape[2:], g.dtype) for g in grads]
    return _comm_of(grads, outs, {}, (N_CHIP * n, N_CHIP * n, 1), copies)


def _grads_ici_comm(parts):
    n = len(parts)

    def copies(ci, co, send, recv, local):
        x, y, c = _position()
        my_chip = 2 * x + y
        mine = [pltpu.make_async_copy(ci[t].at[:, my_chip], co[t].at[:, my_chip], local.at[t]) for t in range(n)]
        return mine + [_remote(ci[t].at[:, 2 * cx + cy], co[t].at[:, my_chip], send, recv, j * n + t, (cx, cy, c))
                       for j, (cx, cy) in enumerate(_other_chips(x, y)) for t in range(n)]

    outs = [jax.ShapeDtypeStruct(p.shape, p.dtype) for p in parts]
    return _comm_of(parts, outs, {}, (3 * n, 3 * n, n), copies)


def _row_tile(rows):
    return 256 if rows % 256 == 0 else rows


def _pair_sum(g, recv, c_idx, out_dtype, name):
    lay, _, rows, cols = g.shape
    tr = _row_tile(rows)

    def body(c_ref, g_ref, r_ref, o_ref):
        o_ref[...] = (g_ref[...] + r_ref[...]).astype(o_ref.dtype)

    def spec(blk_of):
        return pl.BlockSpec((None, None, tr, cols), lambda l, k, r, c_ref: (l, blk_of(k, c_ref), r, 0))

    return pl.pallas_call(
        body, name=name,
        grid_spec=pltpu.PrefetchScalarGridSpec(
            num_scalar_prefetch=1, grid=(lay, N_CHIP, rows // tr),
            in_specs=[spec(lambda k, c_ref: 2 * k + c_ref[0]), spec(lambda k, c_ref: k)],
            out_specs=spec(lambda k, c_ref: k)),
        out_shape=jax.ShapeDtypeStruct((lay, N_CHIP, rows, cols), out_dtype),
        compiler_params=_params(("parallel", "parallel", "parallel")),
    )(c_idx, g, recv)


def _exchange_chips(parts, name):
    n = len(parts)

    def body(*refs):
        p_refs, out_refs = refs[:n], refs[n:2 * n]
        send_sems, recv_sems, local_sems = refs[2 * n:]
        x, y, c = _position()
        my_chip = 2 * x + y
        mine = [pltpu.make_async_copy(p_refs[t].at[:, my_chip], out_refs[t].at[:, my_chip], local_sems.at[t])
                for t in range(n)]
        for cp in mine:
            cp.start()
        copies = [pltpu.make_async_remote_copy(
            src_ref=p_refs[t].at[:, 2 * cx + cy], dst_ref=out_refs[t].at[:, my_chip], send_sem=send_sems.at[j * n + t],
            recv_sem=recv_sems.at[j * n + t], device_id=(cx, cy, c), device_id_type=MESH)
            for j, (cx, cy) in enumerate(_other_chips(x, y)) for t in range(n)]
        for cp in copies:
            cp.start()
        for cp in copies:
            cp.wait()
        for cp in mine:
            cp.wait()

    return pl.pallas_call(
        body, name=name, out_shape=[jax.ShapeDtypeStruct(p.shape, p.dtype) for p in parts],
        in_specs=[HBM_SPEC] * n, out_specs=[HBM_SPEC] * n,
        scratch_shapes=[pltpu.SemaphoreType.DMA((3 * n,)), pltpu.SemaphoreType.DMA((3 * n,)),
                        pltpu.SemaphoreType.DMA((n,))],
    )(*parts)


def _sum_adamw(parts, w, m, v, name):
    lay, rows, cols = w.shape
    tr = _row_tile(rows)

    def body(p_ref, w_ref, m_ref, v_ref, g_ref, d_ref, nm_ref, nv_ref):
        g = ((p_ref[0].astype(F32) + p_ref[1].astype(F32)) + p_ref[2].astype(F32)) + p_ref[3].astype(F32)
        g_ref[...] = g
        nm = ADAM_B1 * m_ref[...] + (1.0 - ADAM_B1) * g
        nv = ADAM_B2 * v_ref[...] + (1.0 - ADAM_B2) * (g * g)
        nm_ref[...] = nm
        nv_ref[...] = nv
        m_hat = nm / (1.0 - ADAM_B1 ** ADAM_STEP)
        v_hat = nv / (1.0 - ADAM_B2 ** ADAM_STEP)
        d_ref[...] = -ADAM_LR * (m_hat / (jnp.sqrt(v_hat) + ADAM_EPS) + ADAM_WD * w_ref[...])

    tile = pl.BlockSpec((None, tr, cols), lambda l, r: (l, r, 0))
    return pl.pallas_call(
        body, name=name, grid=(lay, rows // tr),
        in_specs=[pl.BlockSpec((None, N_CHIP, tr, cols), lambda l, r: (l, 0, r, 0)), tile, tile, tile],
        out_specs=[tile] * 4, out_shape=[jax.ShapeDtypeStruct((lay, rows, cols), F32)] * 4,
        compiler_params=_params(("parallel", "parallel")),
    )(parts, w, m, v)


def _pad_rows(flat, rows):
    return jnp.pad(flat, (0, rows * LANES - flat.shape[0])).reshape(rows, LANES)


def _pack_small(tree):
    sh = jnp.concatenate([tree[n].reshape(-1) for n, _, _ in SMALL_SHARDED])
    rep = jnp.concatenate([tree[n].reshape(-1) for n, _ in REPLICATED])
    return jnp.concatenate([_pad_rows(sh, ROWS_SSH), _pad_rows(rep, ROWS_REP)], axis=0)


def _unpack_small(buf):
    out = {}
    for flat, items in ((buf[:ROWS_SSH].reshape(-1), [(n, s) for n, s, _ in SMALL_SHARDED]),
                        (buf[ROWS_SSH:].reshape(-1), REPLICATED)):
        off = 0
        for n, s in items:
            out[n] = flat[off:off + math.prod(s)].reshape(s)
            off += math.prod(s)
    return out


def _full_from_blocks(blocks, s, ax):
    return jnp.concatenate([blocks[d] for d in range(N_DEV)], axis=ax)


def _blocks_from_full(full, s, ax):
    return jnp.stack([lax.slice_in_dim(full, d * s[ax], (d + 1) * s[ax], axis=ax) for d in range(N_DEV)])


def _pack_small_grads(sharded_blocks, replicated):
    sh = jnp.concatenate([sharded_blocks[n].reshape(N_DEV, -1) for n, _, _ in SMALL_SHARDED], axis=1)
    sh = jnp.pad(sh, ((0, 0), (0, ROWS_SSH * LANES - sh.shape[1]))).reshape(N_DEV, ROWS_SSH, LANES)
    rep = _pad_rows(jnp.concatenate([replicated[n].reshape(-1) for n, _ in REPLICATED]), ROWS_REP)
    return jnp.concatenate([sh, jnp.broadcast_to(rep[None], (N_DEV, ROWS_REP, LANES))], axis=1)


def _layer_fwd(x, h1, wts, big, l, l_next, next_shards, own_ffn):
    L = x.shape[0]
    tm = min(TILE_MM, L)
    nt = L // tm
    nxt = dict(zip(BIG, next_shards)) if next_shards else {}
    if own_ffn:
        at_proj_in, at_ffn_up = {"own_up": own_ffn[0], "own_down": own_ffn[1]}, nxt
    else:
        at_proj_in = {n: nxt[n] for n in ("w_in", "w_out") if n in nxt}
        at_ffn_up = {n: nxt[n] for n in ("w_up", "w_down") if n in nxt}

    def carried(result, comm):
        return (result[0], list(result[1:])) if comm else (result, [])

    comm = _gather_ici_comm(list(at_proj_in.values())) if at_proj_in else None
    p, got = carried(_mm(h1, big["w_in"], dims=NN, grid=(nt, D_INP // 640, 1),
                         a_spec=((tm, D), lambda i, j, k: (i, 0)), b_spec=((None, D, 640), lambda i, j, k: (0, 0, j)),
                         o_spec=((tm, 640), lambda i, j, k: (i, j)), out_shape=jax.ShapeDtypeStruct((L, D_INP), BF16),
                         tile=(tm, 640), name="proj_in", comm=comm), comm)
    stage1 = dict(zip(at_proj_in, got))
    o_f, sp_f = _gla_fwd(p, wts["gpad_f"], wts["bias_f"], l, None, True, "gla_fwd_f")
    o_tot, sp_b = _gla_fwd(p, wts["gpad_b"], wts["bias_b"], l, o_f, False, "gla_fwd_b")
    y_cat = _mixer_out(p, wts["conv_a"], wts["gh"], l, o_tot, "mixer_out")
    comm = _gather_d2d_comm([stage1["own_up"], stage1["own_down"]]) if own_ffn else None
    y, got = carried(_mm(y_cat, big["w_out"], dims=NN, grid=(nt, 1, 1),
                         a_spec=((tm, D), lambda i, j, k: (i, 0)), b_spec=((None, D, D), lambda i, j, k: (0, 0, 0)),
                         o_spec=((tm, D), lambda i, j, k: (i, 0)), out_shape=jax.ShapeDtypeStruct((L, D), BF16),
                         tile=(tm, D), name="proj_out", comm=comm), comm)
    if own_ffn:
        big = dict(big, w_up=got[0], w_down=got[1].reshape(1, FF_HALF, SH_FF, D))
    x1, h2 = _post_pre(x, y, wts["g2"], l, wts["g3"], l, "post_pre_mix")
    comm = _gather_ici_comm(list(at_ffn_up.values())) if at_ffn_up else None
    u8, got = carried(_mm(h2, big["w_up"], dims=NN, grid=(nt, N_DEV, 1),
                          a_spec=((tm, D), lambda i, j, k: (i, 0)),
                          b_spec=((None, None, D, SH_FF), lambda i, j, k: (0, j, 0, 0)),
                          o_spec=((None, tm, SH_FF), lambda i, j, k: (j, i, 0)),
                          out_shape=jax.ShapeDtypeStruct((N_DEV, L, SH_FF), BF16), tile=(tm, SH_FF), name="ffn_up",
                          comm=comm), comm)
    stage1.update(zip(at_ffn_up, got))
    a, conv = _ffn_act(u8, wts["cw"], l, "ffn_act")
    tm1 = min(TILE_MM_KIN, L)
    comm = _gather_d2d_comm([stage1[n] for n in BIG]) if next_shards else None
    y2, gathered = carried(_mm(a, big["w_down"], dims=NN, grid=(L // tm1, 1, 1), kin=FF_HALF,
                               a_spec=((FF_HALF, tm1, SH_FF), lambda i, j, k: (0, i, 0)),
                               b_spec=((None, FF_HALF, SH_FF, D), lambda i, j, k: (0, 0, 0, 0)),
                               o_spec=((tm1, D), lambda i, j, k: (i, 0)),
                               out_shape=jax.ShapeDtypeStruct((L, D), BF16), tile=(tm1, D), name="ffn_down",
                               comm=comm), comm)
    x2, h1_next = _post_pre(x1, y2, wts["g4"], l, wts["g1"], l_next, "post_pre_ffn")
    saved = dict(x=x, h1=h1, p=p, o_tot=o_tot, sp_f=sp_f, sp_b=sp_b, y_cat=y_cat, y=y, x1=x1, h2=h2, u8=u8, a=a, y2=y2,
                 big=big, conv=conv)
    return x2, h1_next, saved, gathered


def _layer_bwd(dx2, wts, s, l, pending, c_idx, early_ffn):
    L = dx2.shape[0]
    big = s["big"]
    tm = min(TILE_MM, L)
    nt = L // tm
    tm1 = min(TILE_MM_KIN, L)
    tk = min(TILE_MM_TOKENS, L)
    nkt = L // tk
    dy2, dg4 = _norm_bwd(s["y2"], wts["g4"], l, dx2, None, "norm_bwd_ffn_post")
    da = _mm(dy2, big["w_down"], dims=NT, grid=(nt, FF_HALF, 1),
             a_spec=((tm, D), lambda i, j, k: (i, 0)), b_spec=((None, None, SH_FF, D), lambda i, j, k: (0, j, 0, 0)),
             o_spec=((None, tm, SH_FF), lambda i, j, k: (j, i, 0)),
             out_shape=jax.ShapeDtypeStruct((FF_HALF, L, SH_FF), BF16), tile=(tm, SH_FF), name="ffn_down_dx",
             comm=_grads_d2d_comm(pending) if pending else None)
    pairs = None
    if pending:
        da, from_sibling = da[0], da[1:]
        pairs = [_pair_sum(g, r, c_idx, BF16, "grads_pair_sum") for g, r in zip(pending, from_sibling)]
    dw_down = _mm(s["a"], dy2, dims=TN, grid=(FF_HALF, 1, nkt),
                  a_spec=((None, tk, SH_FF), lambda i, j, k: (i, k, 0)), b_spec=((tk, D), lambda i, j, k: (k, 0)),
                  o_spec=((SH_FF, D), lambda i, j, k: (i, 0)), out_shape=jax.ShapeDtypeStruct((DFF, D), F32),
                  tile=(SH_FF, D), name="ffn_down_dw")
    du = _ffn_act_bwd(s["conv"], da, "ffn_act_bwd")
    d_u8, dcw = _ffn_conv_t(du.reshape(N_DEV, L, SH_FF), s["u8"], wts["cw"], l, "ffn_conv_t")
    dh2 = _mm(d_u8, big["w_up"], dims=NT, grid=(L // tm1, 1, N_DEV // FF_HALF), kin=FF_HALF,
              a_spec=((FF_HALF, tm1, SH_FF), lambda i, j, k: (k, i, 0)),
              b_spec=((None, FF_HALF, D, SH_FF), lambda i, j, k: (0, k, 0, 0)),
              o_spec=((tm1, D), lambda i, j, k: (i, 0)), out_shape=jax.ShapeDtypeStruct((L, D), BF16),
              tile=(tm1, D), name="ffn_up_dx",
              comm=_grads_ici_comm([q for n, q in zip(BIG, pairs) if n != "w_up"]) if pending else None)
    if pending:
        dh2, rest_parts = dh2[0], dh2[1:]
    dw_up = _mm(s["h2"], d_u8, dims=TN, grid=(1, N_DEV, nkt),
                a_spec=((tk, D), lambda i, j, k: (k, 0)), b_spec=((None, tk, SH_FF), lambda i, j, k: (j, k, 0)),
                o_spec=((None, D, SH_FF), lambda i, j, k: (j, 0, 0)),
                out_shape=jax.ShapeDtypeStruct((N_DEV, D, SH_FF), F32), tile=(D, SH_FF), name="ffn_up_dw",
                comm=_grads_ici_comm([pairs[BIG.index("w_up")]]) if pending else None)
    parts = None
    if pending:
        dw_up, up_part = dw_up[0], dw_up[1]
        parts = [rest_parts[0], rest_parts[1], up_part, rest_parts[2]]
    dx1, dg3 = _norm_bwd(s["x1"], wts["g3"], l, dh2, dx2, "norm_bwd_ffn_pre")
    dy, dg2 = _norm_bwd(s["y"], wts["g2"], l, dx1, None, "norm_bwd_mix_post")
    dy_cat = _mm(dy, big["w_out"], dims=NT, grid=(nt, 1, 1),
                 a_spec=((tm, D), lambda i, j, k: (i, 0)), b_spec=((None, D, D), lambda i, j, k: (0, 0, 0)),
                 o_spec=((tm, D), lambda i, j, k: (i, 0)), out_shape=jax.ShapeDtypeStruct((L, D), BF16),
                 tile=(tm, D), name="proj_out_dx")
    dw_out = _mm(s["y_cat"], dy, dims=TN, grid=(1, 1, nkt),
                 a_spec=((tk, D), lambda i, j, k: (k, 0)), b_spec=((tk, D), lambda i, j, k: (k, 0)),
                 o_spec=((D, D), lambda i, j, k: (0, 0)), out_shape=jax.ShapeDtypeStruct((D, D), F32),
                 tile=(D, D), name="proj_out_dw")
    dgb, dgc, dgv, dgo, d_o, dconv_a, dgh = _mixer_out_bwd(s["p"], wts["conv_a"], wts["gh"], l, s["o_tot"], dy_cat,
                                                          "mixer_out_bwd")
    part_f = _gla_bwd(s["p"], wts["gpad_f"], wts["bias_f"], l, s["sp_f"], d_o, None, True, "gla_bwd_f")
    dp, dgp_b, dbias_b = _gla_bwd(s["p"], wts["gpad_b"], wts["bias_b"], l, s["sp_b"], d_o,
                                  list(part_f[:4]) + [dgb, dgc, dgv, dgo], False, "gla_bwd_b")
    early = [dw_up[None], dw_down.reshape(1, N_DEV, DFF // N_DEV, D)] if early_ffn else None
    dh1 = _mm(dp, big["w_in"], dims=NT, grid=(L // tm1, 1, 1),
              a_spec=((tm1, D_INP), lambda i, j, k: (i, 0)), b_spec=((None, D, D_INP), lambda i, j, k: (0, 0, 0)),
              o_spec=((tm1, D), lambda i, j, k: (i, 0)), out_shape=jax.ShapeDtypeStruct((L, D), BF16),
              tile=(tm1, D), name="proj_in_dx", comm=_grads_d2d_comm(early) if early_ffn else None)
    if early_ffn:
        dh1, early_sibling = dh1[0], dh1[1:]
        early = [_pair_sum(g, r, c_idx, BF16, "grads_pair_sum") for g, r in zip(early, early_sibling)]
    dw_in = _mm(s["h1"], dp, dims=TN, grid=(1, D_INP // 640, nkt),
                a_spec=((tk, D), lambda i, j, k: (k, 0)), b_spec=((tk, 640), lambda i, j, k: (k, j)),
                o_spec=((D, 640), lambda i, j, k: (0, j)), out_shape=jax.ShapeDtypeStruct((D, D_INP), F32),
                tile=(D, 640), name="proj_in_dw", comm=_grads_ici_comm(early) if early_ffn else None)
    if early_ffn:
        dw_in, early = dw_in[0], dw_in[1:]
    dx0, dg1 = _norm_bwd(s["x"], wts["g1"], l, dh1, dx1, "norm_bwd_mix_pre")
    grads = dict(
        norm_mix_pre=dg1[0], norm_mix_post=dg2[0], norm_ffn_pre=dg3[0], norm_ffn_post=dg4[0],
        gate_bias_fwd=part_f[5][0], gate_bias_bwd=dbias_b[0], gla_head_norm=dgh[0],
        w_in=_blocks_from_full(dw_in, (D, SH_IN), 1), w_out=dw_out.reshape(N_DEV, D // N_DEV, D), w_up=dw_up,
        w_down=dw_down.reshape(N_DEV, DFF // N_DEV, D),
        conv_a=_blocks_from_full(dconv_a, (3, DC // N_DEV), 1),
        gate_up_fwd=_blocks_from_full(part_f[4][:RANK], (RANK, DK // N_DEV), 1),
        gate_up_bwd=_blocks_from_full(dgp_b[RANK:2 * RANK], (RANK, DK // N_DEV), 1),
        conv_ffn=dcw.reshape(N_DEV, 3, SH_FF))
    return dx0, grads, parts, early


def _matmul_weights(g_in, g_out, g_up, g_down):
    w_in = jnp.concatenate([g_in[:, d] for d in range(N_DEV)] + [jnp.zeros((1, D, D_INP - D_IN), BF16)], axis=2)
    return dict(w_in=w_in, w_out=g_out.reshape(1, D, D), w_up=g_up,
                w_down=None if g_down is None else g_down.reshape(1, FF_HALF, SH_FF, D))


def _small_weights(g_small, rep):
    small = {n: jnp.moveaxis(t, 0, 1) for n, t in jax.vmap(_unpack_small)(
        jnp.concatenate([g_small[0], jnp.zeros((N_DEV, ROWS_REP, LANES), F32)], axis=1)).items()
        if n in [s[0] for s in SMALL_SHARDED]}
    conv_a = jnp.concatenate([small["conv_a"][:, d] for d in range(N_DEV)], axis=2)
    gate_f = jnp.concatenate([small["gate_up_fwd"][:, d] for d in range(N_DEV)], axis=2).astype(BF16)
    gate_b = jnp.concatenate([small["gate_up_bwd"][:, d] for d in range(N_DEV)], axis=2).astype(BF16)
    zeros = jnp.zeros((DEPTH, LR_BLK, DK), BF16)
    return dict(
        conv_a=conv_a, cw=small["conv_ffn"],
        gpad_f=zeros.at[:, :RANK].set(gate_f), gpad_b=zeros.at[:, RANK:2 * RANK].set(gate_b),
        bias_f=rep["gate_bias_fwd"][:, None, :], bias_b=rep["gate_bias_bwd"][:, None, :],
        gh=rep["gla_head_norm"][:, None, :],
        g1=rep["norm_mix_pre"][:, None, :], g2=rep["norm_mix_post"][:, None, :],
        g3=rep["norm_ffn_pre"][:, None, :], g4=rep["norm_ffn_post"][:, None, :])


def kernel(x, norm_mix_pre, norm_mix_post, norm_ffn_pre, norm_ffn_post, w_in, conv_a, gate_up_fwd, gate_bias_fwd, gate_up_bwd, gate_bias_bwd, gla_head_norm, w_out, w_up, conv_ffn, w_down, loss_target, m_norm_mix_pre, m_norm_mix_post, m_norm_ffn_pre, m_norm_ffn_post, m_w_in, m_conv_a, m_gate_up_fwd, m_gate_bias_fwd, m_gate_up_bwd, m_gate_bias_bwd, m_gla_head_norm, m_w_out, m_w_up, m_conv_ffn, m_w_down, v_norm_mix_pre, v_norm_mix_post, v_norm_ffn_pre, v_norm_ffn_post, v_w_in, v_conv_a, v_gate_up_fwd, v_gate_bias_fwd, v_gate_up_bwd, v_gate_bias_bwd, v_gla_head_norm, v_w_out, v_w_up, v_conv_ffn, v_w_down):
    w = dict(norm_mix_pre=norm_mix_pre, norm_mix_post=norm_mix_post, norm_ffn_pre=norm_ffn_pre,
             norm_ffn_post=norm_ffn_post, w_in=w_in, conv_a=conv_a, gate_up_fwd=gate_up_fwd,
             gate_bias_fwd=gate_bias_fwd, gate_up_bwd=gate_up_bwd, gate_bias_bwd=gate_bias_bwd,
             gla_head_norm=gla_head_norm, w_out=w_out, w_up=w_up, conv_ffn=conv_ffn, w_down=w_down)
    m = dict(norm_mix_pre=m_norm_mix_pre, norm_mix_post=m_norm_mix_post, norm_ffn_pre=m_norm_ffn_pre,
             norm_ffn_post=m_norm_ffn_post, w_in=m_w_in, conv_a=m_conv_a, gate_up_fwd=m_gate_up_fwd,
             gate_bias_fwd=m_gate_bias_fwd, gate_up_bwd=m_gate_up_bwd, gate_bias_bwd=m_gate_bias_bwd,
             gla_head_norm=m_gla_head_norm, w_out=m_w_out, w_up=m_w_up, conv_ffn=m_conv_ffn, w_down=m_w_down)
    v = dict(norm_mix_pre=v_norm_mix_pre, norm_mix_post=v_norm_mix_post, norm_ffn_pre=v_norm_ffn_pre,
             norm_ffn_post=v_norm_ffn_post, w_in=v_w_in, conv_a=v_conv_a, gate_up_fwd=v_gate_up_fwd,
             gate_bias_fwd=v_gate_bias_fwd, gate_up_bwd=v_gate_up_bwd, gate_bias_bwd=v_gate_bias_bwd,
             gla_head_norm=v_gla_head_norm, w_out=v_w_out, w_up=v_w_up, conv_ffn=v_conv_ffn, w_down=v_w_down)
    axes = ("x", "y", "c")
    L = x.shape[1]
    x0 = x.reshape(L, D)
    target = loss_target.reshape(L, D)

    w_small = _pack_small(w)
    w16 = {n: w[n].astype(BF16) for n in BIG}
    g_in, g_out, g_small = _all_gather([w16["w_in"][0:1], w16["w_out"][0:1], w_small[None, :ROWS_SSH]],
                                       "gather_weights")
    wts = _small_weights(g_small, w)
    big = {n: t for n, t in _matmul_weights(g_in, g_out, None, None).items() if t is not None}

    h1 = _norm_cast(x0, wts["g1"], 0, "norm_first")
    xl, saved = x0, []
    for l in range(DEPTH):
        nxt = [w16[n][l + 1:l + 2] for n in BIG] if l + 1 < DEPTH else None
        own = [w16["w_up"][0:1], w16["w_down"][0:1]] if l == 0 else None
        xl, h1, s, gathered = _layer_fwd(xl, h1, wts, big, l, min(l + 1, DEPTH - 1), nxt, own)
        saved.append(s)
        if nxt:
            big = _matmul_weights(*gathered)
    dx, sq = _loss_grad(xl, target, "loss_grad")
    loss = lax.psum(0.5 * jnp.sum(sq) / D, axes)

    c_idx = lax.axis_index("c").astype(jnp.int32).reshape(1)
    layer_grads, layer_parts, pending = [None] * DEPTH, [None] * DEPTH, None
    for l in reversed(range(DEPTH)):
        dx, layer_grads[l], done, early = _layer_bwd(dx, wts, saved[l], l, pending, c_idx, l == 0)
        if pending:
            layer_parts[l + 1] = done
        pending = [layer_grads[l][n][None] for n in BIG]
    small_names = [n for n, _, _ in SMALL_SHARDED] + [n for n, _ in REPLICATED]
    stacked = {n: jnp.stack([g[n] for g in layer_grads]) for n in small_names}
    g_small = _pack_small_grads({n: jnp.moveaxis(stacked[n], 0, 1) for n, _, _ in SMALL_SHARDED}, stacked)
    last = [g for n, g in zip(BIG, pending) if n in ("w_in", "w_out")] + [g_small[None]]
    from_sibling = _exchange_sibling(last, "grads_to_sibling")
    pairs = [_pair_sum(g, r, c_idx, BF16 if i < 2 else F32, "grads_pair_sum")
             for i, (g, r) in enumerate(zip(last, from_sibling))]
    parts = _exchange_chips(pairs, "grads_to_chips")
    layer_parts[0] = [parts[0], parts[1], early[0], early[1]]

    results = {}
    for i, n in enumerate(BIG):
        part = jnp.concatenate([layer_parts[l][i] for l in range(DEPTH)], axis=0)
        results[n] = _sum_adamw(part, w[n], m[n], v[n], "sum_adamw")
    small = _sum_adamw(parts[-1], w_small[None], _pack_small(m)[None], _pack_small(v)[None], "sum_adamw_small")
    small = [_unpack_small(buf[0]) for buf in small]
    outs = [loss, dx.reshape(x.shape)]
    for i in range(4):
        outs += [results[n][i] if n in BIG else small[i][n] for n in WEIGHT_ORDER]
    return tuple(outs)
```

```python
import math
from typing import Callable, NamedTuple

import jax
import jax.numpy as jnp
from jax import lax
from jax.experimental import pallas as pl
from jax.experimental.pallas import tpu as pltpu

F32 = jnp.float32
BF16 = jnp.bfloat16

DEPTH = 4
D = 1024
DC = 512
DG = 512
HEADS = 4
HV = 128
HK = 64
DK = 256
RANK = 16
CH = 64
DFF = 2816
D_IN = 3104
D_INP = 3200
LR_BLK = 128
EPS = 1e-6
HB = 16
N_DEV = 8
N_CHIP = 4
LANES = 1024
SH_IN = D_IN // N_DEV
SH_FF = 2 * DFF // N_DEV
FF_HALF = N_DEV // 2

ADAM_LR, ADAM_B1, ADAM_B2, ADAM_EPS, ADAM_WD, ADAM_STEP = 0.001, 0.9, 0.999, 1e-08, 0.01, 10

VMEM_LIMIT = 48 * 1024 * 1024
TILE_TOKENS = 512
TILE_GLA = 512
TILE_FFN = 1024
TILE_NORM = 512
TILE_MM = 2048
TILE_MM_KIN = 1024
TILE_MM_TOKENS = 2048

COL_GB, COL_GC, COL_GV = 0, 1, 2
COL_Q, COL_K = 6, 7
COL_V, COL_GO = 4, 5
COL_LR = 24

BIG = ("w_in", "w_out", "w_up", "w_down")
SMALL_SHARDED = (
    ("conv_a", (DEPTH, 3, DC // N_DEV), 2),
    ("gate_up_fwd", (DEPTH, RANK, DK // N_DEV), 2),
    ("gate_up_bwd", (DEPTH, RANK, DK // N_DEV), 2),
    ("conv_ffn", (DEPTH, 3, SH_FF), 2),
)
REPLICATED = (
    ("norm_mix_pre", (DEPTH, D)), ("norm_mix_post", (DEPTH, D)), ("norm_ffn_pre", (DEPTH, D)),
    ("norm_ffn_post", (DEPTH, D)), ("gate_bias_fwd", (DEPTH, DK)), ("gate_bias_bwd", (DEPTH, DK)),
    ("gla_head_norm", (DEPTH, HV)),
)
WEIGHT_ORDER = ("norm_mix_pre", "norm_mix_post", "norm_ffn_pre", "norm_ffn_post", "w_in", "conv_a", "gate_up_fwd",
                "gate_bias_fwd", "gate_up_bwd", "gate_bias_bwd", "gla_head_norm", "w_out", "w_up", "conv_ffn", "w_down")


def _rows_for(n_elems):
    return (-(-n_elems // LANES) + 7) // 8 * 8


ROWS_SSH = _rows_for(sum(math.prod(s) for _, s, _ in SMALL_SHARDED))
ROWS_REP = _rows_for(sum(math.prod(s) for _, s in REPLICATED))
ROWS_SMALL = ROWS_SSH + ROWS_REP


def _params(sem):
    return pltpu.CompilerParams(dimension_semantics=sem, vmem_limit_bytes=VMEM_LIMIT)


def _silu_parts(x):
    s = 1.0 / (1.0 + jnp.exp(-x))
    return x * s, s


def _rstd(xf):
    return lax.rsqrt(jnp.mean(xf * xf, axis=-1, keepdims=True) + EPS)


NN, NT, TN = ((1,), (0,)), ((1,), (1,)), ((0,), (0,))


def _dot(a, b, dims):
    return lax.dot_general(a, b, (dims, ((), ())), preferred_element_type=F32)


def _mm(a, b, *, dims, grid, a_spec, b_spec, o_spec, out_shape, tile, name, kin=0, comm=None):
    nk = grid[2]
    n_ci = len(comm.ins) if comm else 0
    n_co = len(comm.out_shapes) if comm else 0

    def body(*refs):
        a_ref, b_ref = refs[:2]
        ci = refs[2:2 + n_ci]
        o_ref = refs[2 + n_ci]
        co = refs[3 + n_ci:3 + n_ci + n_co]
        rest = refs[3 + n_ci + n_co:]
        if comm:
            sems, rest = rest[:3], rest[3:]
            step = (pl.program_id(0) * grid[1] + pl.program_id(1)) * grid[2] + pl.program_id(2)

            @pl.when(step == 0)
            def _():
                comm.start(ci, co, *sems)

        if kin:
            prod = _dot(a_ref[0], b_ref[0], dims)
            for d in range(1, kin):
                prod = prod + _dot(a_ref[d], b_ref[d], dims)
        else:
            prod = _dot(a_ref[...], b_ref[...], dims)
        if nk == 1:
            o_ref[...] = prod.astype(o_ref.dtype)
        else:
            acc_ref = rest[0]
            k = pl.program_id(2)

            @pl.when(k == 0)
            def _():
                acc_ref[...] = prod

            @pl.when(k > 0)
            def _():
                acc_ref[...] += prod

            @pl.when(k == nk - 1)
            def _():
                o_ref[...] = acc_ref[...].astype(o_ref.dtype)

        if comm:
            @pl.when(step == grid[0] * grid[1] * grid[2] - 1)
            def _():
                comm.wait(ci, co, *sems)

    acc = [pltpu.VMEM(tile, F32)] if nk > 1 else []
    if not comm:
        return pl.pallas_call(
            body, name=name, grid=grid, in_specs=[pl.BlockSpec(*a_spec), pl.BlockSpec(*b_spec)],
            out_specs=pl.BlockSpec(*o_spec), out_shape=out_shape, scratch_shapes=acc,
            compiler_params=_params(("parallel", "parallel", "arbitrary")),
        )(a, b)
    return pl.pallas_call(
        body, name=name, grid=grid,
        in_specs=[pl.BlockSpec(*a_spec), pl.BlockSpec(*b_spec)] + [HBM_SPEC] * n_ci,
        out_specs=[pl.BlockSpec(*o_spec)] + [HBM_SPEC] * n_co, out_shape=[out_shape] + list(comm.out_shapes),
        scratch_shapes=[pltpu.SemaphoreType.DMA((s,)) for s in comm.sems] + acc,
        input_output_aliases={2 + i: 1 + o for i, o in comm.aliases.items()},
        compiler_params=_params(("arbitrary", "arbitrary", "arbitrary")),
    )(a, b, *comm.ins)


def _halo_maps(tm, n_rows):
    r, last = tm // HB, n_rows // HB - 1
    return (lambda i: jnp.maximum(i * r - 1, 0)), (lambda i: jnp.minimum((i + 1) * r, last))


def _shift(x, prev_blk, next_blk):
    tm = x.shape[0]
    xs = jnp.concatenate([prev_blk, x, next_blk], axis=0)
    n = xs.shape[0]
    down = pltpu.roll(xs, 1, 0)[HB:HB + tm]
    up = pltpu.roll(xs, n - 1, 0)[HB:HB + tm]
    return down, up


def _edge_scales(i, n):
    return jnp.where(i > 0, 1.0, 0.0).astype(F32), jnp.where(i < n - 1, 1.0, 0.0).astype(F32)


def _gain_spec(l, width=D):
    return pl.BlockSpec((None, 1, width), lambda *_: (l, 0, 0))


def _norm_cast(x, g, l, name):
    L = x.shape[0]
    tm = min(TILE_NORM, L)

    def body(x_ref, g_ref, o_ref):
        xf = x_ref[...]
        o_ref[...] = (xf * _rstd(xf) * g_ref[...]).astype(BF16)

    return pl.pallas_call(
        body, name=name, grid=(L // tm,), in_specs=[pl.BlockSpec((tm, D), lambda i: (i, 0)), _gain_spec(l)],
        out_specs=pl.BlockSpec((tm, D), lambda i: (i, 0)), out_shape=jax.ShapeDtypeStruct((L, D), BF16),
        compiler_params=_params(("parallel",)),
    )(x, g)


def _post_pre(x, y, g_post, l_post, g_pre, l_pre, name):
    L = x.shape[0]
    tm = min(TILE_NORM, L)

    def body(x_ref, y_ref, gp_ref, gn_ref, x1_ref, h_ref):
        yf = y_ref[...].astype(F32)
        x1 = x_ref[...] + yf * _rstd(yf) * gp_ref[...]
        x1_ref[...] = x1
        h_ref[...] = (x1 * _rstd(x1) * gn_ref[...]).astype(BF16)

    tile = pl.BlockSpec((tm, D), lambda i: (i, 0))
    return pl.pallas_call(
        body, name=name, grid=(L // tm,), in_specs=[tile, tile, _gain_spec(l_post), _gain_spec(l_pre)],
        out_specs=[tile, tile],
        out_shape=[jax.ShapeDtypeStruct((L, D), F32), jax.ShapeDtypeStruct((L, D), BF16)],
        compiler_params=_params(("parallel",)),
    )(x, y, g_post, g_pre)


def _norm_bwd(yin, g, l, dout, dres, name, then=None):
    L = yin.shape[0]
    tm = min(TILE_NORM, L)
    with_res = dres is not None

    def norm_vjp(y_ref, g_ref, do):
        y = y_ref[...].astype(F32)
        r = _rstd(y)
        z = do * g_ref[...]
        return (r * z - y * (r * r * r) * jnp.mean(y * z, axis=-1, keepdims=True),
                jnp.sum(do * y * r, axis=0, keepdims=True))

    def body(*refs):
        y_ref, g_ref, do_ref = refs[:3]
        refs = refs[3:]
        if with_res:
            dr_ref, refs = refs[0], refs[1:]
        if then:
            y2_ref, g2_ref, refs = refs[0], refs[1], refs[2:]
        din_ref, dg_ref = refs[:2]
        i = pl.program_id(0)
        din, part = norm_vjp(y_ref, g_ref, do_ref[...].astype(F32))
        if with_res:
            din = din + dr_ref[...]
        din_ref[...] = din.astype(din_ref.dtype)
        parts = [(dg_ref, part)]
        if then:
            d2, part2 = norm_vjp(y2_ref, g2_ref, din)
            refs[2][...] = d2.astype(BF16)
            parts.append((refs[3], part2))

        @pl.when(i == 0)
        def _():
            for ref, val in parts:
                ref[...] = val

        @pl.when(i > 0)
        def _():
            for ref, val in parts:
                ref[...] += val

    tile = pl.BlockSpec((tm, D), lambda i: (i, 0))
    gain_out = pl.BlockSpec((1, D), lambda i: (0, 0))
    args = (yin, g, dout) + ((dres,) if with_res else ()) + ((then[0], then[1]) if then else ())
    return pl.pallas_call(
        body, name=name, grid=(L // tm,),
        in_specs=[tile, _gain_spec(l), tile] + ([tile] if with_res else []) + ([tile, _gain_spec(then[2])] if then else []),
        out_specs=[tile, gain_out] + ([tile, gain_out] if then else []),
        out_shape=[jax.ShapeDtypeStruct((L, D), F32 if with_res else BF16), jax.ShapeDtypeStruct((1, D), F32)]
        + ([jax.ShapeDtypeStruct((L, D), BF16), jax.ShapeDtypeStruct((1, D), F32)] if then else []),
        compiler_params=_params(("arbitrary",)),
    )(*args)


GLA_SUB = 256


def _gla_consts(fwd, tb):
    sub = min(GLA_SUB, tb)
    row = lax.broadcasted_iota(jnp.int32, (sub, sub), 0)
    col = lax.broadcasted_iota(jnp.int32, (sub, sub), 1)
    same = (row // CH) == (col // CH)
    tri = same & ((col <= row) if fwd else (col >= row))
    tri_t = same & ((col >= row) if fwd else (col <= row))
    row_st = lax.broadcasted_iota(jnp.int32, (HEADS * CH, CH), 0) & (CH - 1)
    col_st = lax.broadcasted_iota(jnp.int32, (HEADS * CH, CH), 1)
    tri_st = (col_st <= row_st) if fwd else (col_st >= row_st)
    lane_head = lax.broadcasted_iota(jnp.int32, (1, DK), 1) // HK
    head_masks = [lane_head == h for h in range(HEADS)]
    srow = lax.broadcasted_iota(jnp.int32, (DG, DK), 0) // HV
    scol = lax.broadcasted_iota(jnp.int32, (DG, DK), 1) // HK
    return tri.astype(BF16), tri_t.astype(BF16), tri_st, head_masks, srow == scol


def _dot_hilo(tri_b, x):
    hi = x.astype(BF16)
    lo = (x - hi.astype(F32)).astype(BF16)
    sub = tri_b.shape[0]
    return jnp.concatenate([_dot(tri_b, hi[r:r + sub], NN) + _dot(tri_b, lo[r:r + sub], NN)
                            for r in range(0, x.shape[0], sub)], axis=0)


def _gla_block_terms(q_ref, k_ref, lr_ref, gp_ref, bias_ref, tri_b):
    pre = _dot(lr_ref[...], gp_ref[...], NN) + bias_ref[...]
    sig_neg = 1.0 / (1.0 + jnp.exp(pre))
    a = (jnp.minimum(pre, 0.0) - jnp.log(1.0 + jnp.exp(-jnp.abs(pre)))) * (1.0 / 16.0)
    cum = _dot_hilo(tri_b, a)
    cl = jnp.concatenate([jnp.broadcast_to(jnp.min(cum[r:r + CH], axis=0, keepdims=True), (CH, DK))
                          for r in range(0, cum.shape[0], CH)], axis=0)
    e = jnp.exp(cum)
    einv = jnp.exp(-cum)
    eout = jnp.exp(cl - cum)
    q_in = q_ref[...].astype(F32) * e * (HK ** -0.5)
    k = k_ref[...].astype(F32)
    return dict(sig_neg=sig_neg, e=e, einv=einv, eout=eout, decay=jnp.exp(cl), q_in=q_in, k_in=k * einv,
                k_out=k * eout)


def _gla_chunk(t, c, head_masks):
    sl = slice(c * CH, (c + 1) * CH)
    tc = {n: x[sl] for n, x in t.items() if n != "decay"}
    tc["decay"] = jnp.max(t["decay"][c * CH:c * CH + 8], axis=0, keepdims=True)
    tc["q_st"] = jnp.concatenate([jnp.where(mh, tc["q_in"], 0.0) for mh in head_masks], axis=0).astype(BF16)
    return tc


def _gate_specs(l):
    return [pl.BlockSpec((None, LR_BLK, DK), lambda i: (l, 0, 0)), pl.BlockSpec((None, 1, DK), lambda i: (l, 0, 0))]


def _gla_fwd(p, gpad, bias, l, o_prev, fwd, name):
    L = p.shape[0]
    tb = min(TILE_GLA, L)
    nb, ncb, nch = L // tb, tb // CH, L // CH
    blk = (lambda i: i) if fwd else (lambda i: nb - 1 - i)
    with_prev = o_prev is not None

    def body(*refs):
        if with_prev:
            q_ref, k_ref, v_ref, lr_ref, gp_ref, bias_ref, op_ref, o_ref, sp_ref, s_ref = refs
        else:
            q_ref, k_ref, v_ref, lr_ref, gp_ref, bias_ref, o_ref, sp_ref, s_ref = refs
        i = pl.program_id(0)

        @pl.when(i == 0)
        def _():
            s_ref[...] = jnp.zeros_like(s_ref)

        tri_b, _, tri_st, head_masks, blockmask = _gla_consts(fwd, tb)
        terms = _gla_block_terms(q_ref, k_ref, lr_ref, gp_ref, bias_ref, tri_b)
        for c in (range(ncb) if fwd else reversed(range(ncb))):
            rows = pl.ds(c * CH, CH)
            t = _gla_chunk(terms, c, head_masks)
            v = v_ref[rows, :]
            scores = _dot(t["q_st"], t["k_in"].astype(BF16), NT)
            a_st = jnp.where(tri_st, scores, 0.0).astype(BF16)
            r = _dot(a_st, v, NN)
            o_intra = jnp.concatenate([r[h * CH:(h + 1) * CH, h * HV:(h + 1) * HV] for h in range(HEADS)], axis=1)
            s_b = s_ref[...].astype(BF16)
            sp_ref[c] = s_b
            o = o_intra + _dot(t["q_in"].astype(BF16), s_b, NT)
            if with_prev:
                o = o + op_ref[rows, :]
            o_ref[rows, :] = o
            kv_t = _dot(v, t["k_out"].astype(BF16), TN)
            s_ref[...] = s_ref[...] * t["decay"] + jnp.where(blockmask, kv_t, 0.0)

    def col(width, c):
        return pl.BlockSpec((tb, width), lambda i: (blk(i), c))

    in_specs = [col(DK, COL_Q), col(DK, COL_K), col(DG, COL_V), col(LR_BLK, COL_LR)] + _gate_specs(l)
    args = [p, p, p, p, gpad, bias]
    if with_prev:
        in_specs.append(pl.BlockSpec((tb, DG), lambda i: (blk(i), 0)))
        args.append(o_prev)
    return pl.pallas_call(
        body, name=name, grid=(nb,), in_specs=in_specs,
        out_specs=[pl.BlockSpec((tb, DG), lambda i: (blk(i), 0)), pl.BlockSpec((ncb, DG, DK), lambda i: (blk(i), 0, 0))],
        out_shape=[jax.ShapeDtypeStruct((L, DG), F32), jax.ShapeDtypeStruct((nch, DG, DK), BF16)],
        scratch_shapes=[pltpu.VMEM((DG, DK), F32)],
        compiler_params=_params(("arbitrary",)),
    )(*args)


P_COLS = dict(gb=(0, DC), gc=(DC, DC), gv=(2 * DC, DC), q=(3 * DC, DK), k=(3 * DC + DK, DK), v=(3 * DC + 2 * DK, DG),
              go=(3 * DC + 2 * DK + DG, DG), lr=(3 * DC + 2 * DK + 2 * DG, LR_BLK))


def _gla_bwd(p, gpad, bias, l, sprev, d_o, prev, fwd, name):
    L = p.shape[0]
    tb = min(TILE_GLA, L)
    nb, ncb = L // tb, tb // CH
    blk = (lambda i: nb - 1 - i) if fwd else (lambda i: i)
    with_prev = prev is not None

    def body(*refs):
        q_ref, k_ref, v_ref, lr_ref, gp_ref, bias_ref, sp_ref, do_ref = refs[:8]
        rest = refs[8:]
        if with_prev:
            pq_ref, pk_ref, pv_ref, plr_ref, dgb_ref, dgc_ref, dgv_ref, dgo_ref = rest[:8]
            dp_ref, dg_ref, db_ref, ds_ref = rest[8:]

            def put(what, rows, val):
                c0, width = P_COLS[what]
                dp_ref[rows, c0:c0 + width] = val

            for what, ref in (("gb", dgb_ref), ("gc", dgc_ref), ("gv", dgv_ref), ("go", dgo_ref)):
                put(what, slice(None), ref[...])
        else:
            dq_ref, dk_ref, dv_ref, dlr_ref, dg_ref, db_ref, ds_ref = rest
            out_of = dict(q=dq_ref, k=dk_ref, v=dv_ref, lr=dlr_ref)

            def put(what, rows, val):
                out_of[what][rows, :] = val

        i = pl.program_id(0)

        @pl.when(i == 0)
        def _():
            ds_ref[...] = jnp.zeros_like(ds_ref)
            dg_ref[...] = jnp.zeros_like(dg_ref)
            db_ref[...] = jnp.zeros_like(db_ref)

        tri_b, tri_t_b, tri_st, head_masks, blockmask = _gla_consts(fwd, tb)
        terms = _gla_block_terms(q_ref, k_ref, lr_ref, gp_ref, bias_ref, tri_b)
        dcum_of, dcl_of = [None] * ncb, [None] * ncb
        for c in (reversed(range(ncb)) if fwd else range(ncb)):
            rows = pl.ds(c * CH, CH)
            t = _gla_chunk(terms, c, head_masks)
            v = v_ref[rows, :]
            do = do_ref[rows, :]
            q_in, k_in, k_out = t["q_in"], t["k_in"], t["k_out"]
            q_b, k_in_b, k_out_b = q_in.astype(BF16), k_in.astype(BF16), k_out.astype(BF16)
            scores = _dot(t["q_st"], k_in_b, NT)
            a_st = jnp.where(tri_st, scores, 0.0).astype(BF16)
            s_prev = sp_ref[c]
            ds = ds_ref[...]
            ds_b = ds.astype(BF16)

            da_heads = [_dot(do[:, h * HV:(h + 1) * HV], v[:, h * HV:(h + 1) * HV], NT) for h in range(HEADS)]
            da_st = jnp.where(tri_st, jnp.concatenate(da_heads, axis=0), 0.0).astype(BF16)

            dv_heads = [_dot(a_st[h * CH:(h + 1) * CH, :], do[:, h * HV:(h + 1) * HV], TN) for h in range(HEADS)]
            dv = jnp.concatenate(dv_heads, axis=1) + _dot(k_out_b, ds_b, NT)

            x = _dot(da_st, k_in_b, NN)
            dq_in = _dot(do, s_prev, NN)
            for h in range(HEADS):
                dq_in = dq_in + jnp.where(head_masks[h], x[h * CH:(h + 1) * CH, :], 0.0)
            dk_in = _dot(da_st, t["q_st"], TN)
            dk_out = _dot(v, ds_b, NN)
            d_decay = jnp.sum(ds * s_prev.astype(F32), axis=0, keepdims=True)
            ds_ref[...] = ds * t["decay"] + jnp.where(blockmask, _dot(do, q_b, TN), 0.0)

            dq = dq_in * t["e"] * (HK ** -0.5)
            dk = dk_in * t["einv"] + dk_out * t["eout"]
            dko_ko = dk_out * k_out
            dcum_of[c] = dq_in * q_in - dk_in * k_in - dko_ko
            dcl = jnp.sum(dko_ko, axis=0, keepdims=True) + d_decay * t["decay"]
            dcl_of[c] = jnp.broadcast_to(dcl, (CH, DK))
            if with_prev:
                dq = dq + pq_ref[rows, :].astype(F32)
                dk = dk + pk_ref[rows, :].astype(F32)
                dv = dv + pv_ref[rows, :].astype(F32)
            put("q", rows, dq.astype(BF16))
            put("k", rows, dk.astype(BF16))
            put("v", rows, dv.astype(BF16))

        da = _dot_hilo(tri_t_b, jnp.concatenate(dcum_of, axis=0)) + jnp.concatenate(dcl_of, axis=0)
        dpre = da * terms["sig_neg"] * (1.0 / 16.0)
        dpre_b = dpre.astype(BF16)
        dlr = _dot(dpre_b, gp_ref[...], NT)
        dg_ref[...] += _dot(lr_ref[...], dpre_b, TN)
        db_ref[...] += jnp.sum(dpre, axis=0, keepdims=True)
        if with_prev:
            dlr = dlr + plr_ref[...].astype(F32)
        put("lr", slice(None), dlr.astype(BF16))

    def col(width, c):
        return pl.BlockSpec((tb, width), lambda i: (blk(i), c))

    in_specs = [col(DK, COL_Q), col(DK, COL_K), col(DG, COL_V), col(LR_BLK, COL_LR)] + _gate_specs(l) + [
        pl.BlockSpec((ncb, DG, DK), lambda i: (blk(i), 0, 0)), col(DG, 0)]
    args = [p, p, p, p, gpad, bias, sprev, d_o]
    tiles = [col(DK, 0), col(DK, 0), col(DG, 0), col(LR_BLK, 0)]
    shapes = [jax.ShapeDtypeStruct((L, DK), BF16), jax.ShapeDtypeStruct((L, DK), BF16),
              jax.ShapeDtypeStruct((L, DG), BF16), jax.ShapeDtypeStruct((L, LR_BLK), BF16)]
    if with_prev:
        in_specs += tiles + [col(DC, 0)] * 4
        args += list(prev)
        tiles, shapes = [col(D_INP, 0)], [jax.ShapeDtypeStruct((L, D_INP), BF16)]
    return pl.pallas_call(
        body, name=name, grid=(nb,), in_specs=in_specs,
        out_specs=tiles + [pl.BlockSpec((LR_BLK, DK), lambda i: (0, 0)), pl.BlockSpec((1, DK), lambda i: (0, 0))],
        out_shape=shapes + [jax.ShapeDtypeStruct((LR_BLK, DK), F32), jax.ShapeDtypeStruct((1, DK), F32)],
        scratch_shapes=[pltpu.VMEM((DG, DK), F32)],
        compiler_params=_params(("arbitrary",)),
    )(*args)


def _mixer_out(p, conv_a, gh, l, o_tot, name):
    L = p.shape[0]
    tm = min(TILE_TOKENS, L)
    n = L // tm
    pmap, nmap = _halo_maps(tm, L)

    def body(gb_ref, gc_ref, gcp_ref, gcn_ref, gv_ref, gvp_ref, gvn_ref, go_ref, cw_ref, o_ref, gh_ref, y_ref):
        ps, ns = _edge_scales(pl.program_id(0), n)
        z = gc_ref[...].astype(F32) * gv_ref[...].astype(F32)
        zp = gcp_ref[...].astype(F32) * gvp_ref[...].astype(F32) * ps
        zn = gcn_ref[...].astype(F32) * gvn_ref[...].astype(F32) * ns
        z_dn, z_up = _shift(z, zp, zn)
        conv = cw_ref[0:1, :] * z_dn + cw_ref[1:2, :] * z + cw_ref[2:3, :] * z_up
        y_ref[:, 0:DC] = (gb_ref[...].astype(F32) * conv).astype(BF16)
        o = o_ref[...]
        go = go_ref[...].astype(F32)
        for h in range(HEADS):
            oh = o[:, h * HV:(h + 1) * HV]
            on = oh * _rstd(oh) * gh_ref[...]
            act, _ = _silu_parts(go[:, h * HV:(h + 1) * HV])
            y_ref[:, DC + h * HV:DC + (h + 1) * HV] = (act * on).astype(BF16)

    def main(c):
        return pl.BlockSpec((tm, DC), lambda i: (i, c))

    def halo(c, imap):
        return pl.BlockSpec((HB, DC), lambda i: (imap(i), c))

    return pl.pallas_call(
        body, name=name, grid=(n,),
        in_specs=[main(COL_GB), main(COL_GC), halo(COL_GC, pmap), halo(COL_GC, nmap),
                  main(COL_GV), halo(COL_GV, pmap), halo(COL_GV, nmap), main(COL_GO),
                  pl.BlockSpec((None, 3, DC), lambda i: (l, 0, 0)), pl.BlockSpec((tm, DG), lambda i: (i, 0)),
                  _gain_spec(l, HV)],
        out_specs=pl.BlockSpec((tm, D), lambda i: (i, 0)), out_shape=jax.ShapeDtypeStruct((L, D), BF16),
        compiler_params=_params(("parallel",)),
    )(p, p, p, p, p, p, p, p, conv_a, o_tot, gh)


def _mixer_out_bwd(p, conv_a, gh, l, o_tot, dy, name):
    L = p.shape[0]
    tm = min(TILE_TOKENS, L)
    n = L // tm
    pmap, nmap = _halo_maps(tm, L)

    def body(gb_ref, gbp_ref, gbn_ref, gc_ref, gcp_ref, gcn_ref, gv_ref, gvp_ref, gvn_ref, go_ref, cw_ref, o_ref,
             gh_ref, dy_ref, dyp_ref, dyn_ref, dgb_ref, dgc_ref, dgv_ref, dgo_ref, do_ref, dcw_ref, dgh_ref):
        i = pl.program_id(0)
        ps, ns = _edge_scales(i, n)
        gb = gb_ref[...].astype(F32)
        gc = gc_ref[...].astype(F32)
        gv = gv_ref[...].astype(F32)
        z = gc * gv
        zp = gcp_ref[...].astype(F32) * gvp_ref[...].astype(F32) * ps
        zn = gcn_ref[...].astype(F32) * gvn_ref[...].astype(F32) * ns
        z_dn, z_up = _shift(z, zp, zn)
        w0, w1, w2 = cw_ref[0:1, :], cw_ref[1:2, :], cw_ref[2:3, :]
        conv = w0 * z_dn + w1 * z + w2 * z_up
        dya = dy_ref[:, 0:DC].astype(F32)
        dgb_ref[...] = (dya * conv).astype(BF16)
        dc = dya * gb
        dcp = dyp_ref[...].astype(F32) * gbp_ref[...].astype(F32) * ps
        dcn = dyn_ref[...].astype(F32) * gbn_ref[...].astype(F32) * ns
        dc_dn, dc_up = _shift(dc, dcp, dcn)
        dz = w0 * dc_up + w1 * dc + w2 * dc_dn
        dgc_ref[...] = (dz * gv).astype(BF16)
        dgv_ref[...] = (dz * gc).astype(BF16)
        dcw = [jnp.sum(zs * dc, axis=0, keepdims=True) for zs in (z_dn, z, z_up)]

        o = o_ref[...]
        go = go_ref[...].astype(F32)
        dgh = jnp.zeros((1, HV), F32)
        for h in range(HEADS):
            sl = slice(h * HV, (h + 1) * HV)
            oh = o[:, sl]
            r = _rstd(oh)
            act, sg = _silu_parts(go[:, sl])
            dyb = dy_ref[:, DC + h * HV:DC + (h + 1) * HV].astype(F32)
            on = oh * r * gh_ref[...]
            dgo_ref[:, sl] = (dyb * on * (sg + act * (1.0 - sg))).astype(BF16)
            don = dyb * act
            zz = don * gh_ref[...]
            do_ref[:, sl] = (r * zz - oh * (r * r * r) * jnp.mean(oh * zz, axis=-1, keepdims=True)).astype(BF16)
            dgh = dgh + jnp.sum(don * oh * r, axis=0, keepdims=True)

        @pl.when(i == 0)
        def _():
            dcw_ref[...] = jnp.zeros_like(dcw_ref)
            dgh_ref[...] = jnp.zeros_like(dgh_ref)

        for kk in range(3):
            dcw_ref[kk:kk + 1, :] += dcw[kk]
        dgh_ref[...] += dgh

    def main(c):
        return pl.BlockSpec((tm, DC), lambda i: (i, c))

    def halo(c, imap):
        return pl.BlockSpec((HB, DC), lambda i: (imap(i), c))

    tile = pl.BlockSpec((tm, DC), lambda i: (i, 0))
    return pl.pallas_call(
        body, name=name, grid=(n,),
        in_specs=[main(COL_GB), halo(COL_GB, pmap), halo(COL_GB, nmap), main(COL_GC), halo(COL_GC, pmap),
                  halo(COL_GC, nmap), main(COL_GV), halo(COL_GV, pmap), halo(COL_GV, nmap), main(COL_GO),
                  pl.BlockSpec((None, 3, DC), lambda i: (l, 0, 0)), tile, _gain_spec(l, HV),
                  pl.BlockSpec((tm, D), lambda i: (i, 0)), halo(0, pmap), halo(0, nmap)],
        out_specs=[tile, tile, tile, tile, tile, pl.BlockSpec((3, DC), lambda i: (0, 0)),
                   pl.BlockSpec((1, HV), lambda i: (0, 0))],
        out_shape=[jax.ShapeDtypeStruct((L, DC), BF16)] * 5
        + [jax.ShapeDtypeStruct((3, DC), F32), jax.ShapeDtypeStruct((1, HV), F32)],
        compiler_params=_params(("arbitrary",)),
    )(p, p, p, p, p, p, p, p, p, p, conv_a, o_tot, gh, dy, dy, dy)


def _ffn_specs(tm, L, l, row_axis, sh_axis):
    pmap, nmap = _halo_maps(tm, L)

    def u(off, imap=None, rows=tm):
        if imap is None:
            return pl.BlockSpec((None, rows, SH_FF), lambda *g: (g[sh_axis] + off, g[row_axis], 0))
        return pl.BlockSpec((None, rows, SH_FF), lambda *g: (g[sh_axis] + off, imap(g[row_axis]), 0))

    def cw(off):
        return pl.BlockSpec((None, None, 3, SH_FF), lambda *g: (l, g[sh_axis] + off, 0, 0))

    u_specs = [u(0), u(0, pmap, HB), u(0, nmap, HB), u(FF_HALF), u(FF_HALF, pmap, HB), u(FF_HALF, nmap, HB)]
    return u_specs, [cw(0), cw(FF_HALF)]


def _conv3(x_ref, xp_ref, xn_ref, cw_ref, ps, ns):
    x = x_ref[...].astype(F32)
    x_dn, x_up = _shift(x, xp_ref[...].astype(F32) * ps, xn_ref[...].astype(F32) * ns)
    return cw_ref[0:1, :] * x_dn + cw_ref[1:2, :] * x + cw_ref[2:3, :] * x_up, (x_dn, x, x_up)


def _carry(body, *, name, grid, args, in_specs, out_specs, out_shape, scratch, comm):
    n_in, n_out = len(args), len(out_shape)
    n_ci = len(comm.ins) if comm else 0
    n_co = len(comm.out_shapes) if comm else 0

    def wrapped(*refs):
        ins, refs = refs[:n_in], refs[n_in:]
        ci, refs = refs[:n_ci], refs[n_ci:]
        outs, refs = refs[:n_out], refs[n_out:]
        co, refs = refs[:n_co], refs[n_co:]
        if comm:
            sems, refs = refs[:3], refs[3:]
            step = 0
            for ax, size in enumerate(grid):
                step = step * size + pl.program_id(ax)

            @pl.when(step == 0)
            def _():
                comm.start(ci, co, *sems)

        body(ins, outs, refs)
        if comm:
            @pl.when(step == math.prod(grid) - 1)
            def _():
                comm.wait(ci, co, *sems)

    return pl.pallas_call(
        wrapped, name=name, grid=grid, in_specs=list(in_specs) + [HBM_SPEC] * n_ci,
        out_specs=list(out_specs) + [HBM_SPEC] * n_co,
        out_shape=list(out_shape) + list(comm.out_shapes if comm else ()),
        scratch_shapes=([pltpu.SemaphoreType.DMA((s,)) for s in comm.sems] if comm else []) + list(scratch),
        input_output_aliases={n_in + i: n_out + o for i, o in comm.aliases.items()} if comm else {},
        compiler_params=_params(("arbitrary",) * len(grid)),
    )(*args, *(comm.ins if comm else ()))


def _ffn_act(u8, cw, l, name):
    L = u8.shape[1]
    tm = min(TILE_FFN, L)
    n = L // tm
    u_specs, cw_specs = _ffn_specs(tm, L, l, 0, 1)

    def body(ins, outs, scratch):
        g_ref, gp_ref, gn_ref, v_ref, vp_ref, vn_ref, cwg_ref, cwv_ref = ins
        ps, ns = _edge_scales(pl.program_id(0), n)
        gate, _ = _conv3(g_ref, gp_ref, gn_ref, cwg_ref, ps, ns)
        val, _ = _conv3(v_ref, vp_ref, vn_ref, cwv_ref, ps, ns)
        act, _ = _silu_parts(gate)
        outs[0][...] = (act * val).astype(BF16)
        outs[1][0] = gate.astype(BF16)
        outs[1][1] = val.astype(BF16)

    pair = pl.BlockSpec((2, None, tm, SH_FF), lambda i, d: (0, d, i, 0))
    return _carry(
        body, name=name, grid=(n, FF_HALF), args=(u8, u8, u8, u8, u8, u8, cw, cw), in_specs=u_specs + cw_specs,
        out_specs=[pl.BlockSpec((None, tm, SH_FF), lambda i, d: (d, i, 0)), pair],
        out_shape=[jax.ShapeDtypeStruct((FF_HALF, L, SH_FF), BF16),
                   jax.ShapeDtypeStruct((2, FF_HALF, L, SH_FF), BF16)], scratch=[], comm=None)


def _ffn_act_bwd(conv, da, name):
    L = conv.shape[2]
    tm = min(TILE_FFN, L)

    def body(ins, outs, scratch):
        c_ref, da_ref = ins
        act, sg = _silu_parts(c_ref[0].astype(F32))
        da_f = da_ref[...].astype(F32)
        outs[0][0] = (da_f * c_ref[1].astype(F32) * (sg + act * (1.0 - sg))).astype(BF16)
        outs[0][1] = (da_f * act).astype(BF16)

    pair = pl.BlockSpec((2, None, tm, SH_FF), lambda d, i: (0, d, i, 0))
    return _carry(
        body, name=name, grid=(FF_HALF, L // tm), args=(conv, da),
        in_specs=[pair, pl.BlockSpec((None, tm, SH_FF), lambda d, i: (d, i, 0))], out_specs=[pair],
        out_shape=[jax.ShapeDtypeStruct((2, FF_HALF, L, SH_FF), BF16)], scratch=[], comm=None)[0]


def _ffn_conv_t(du8, u8, cw, l, name):
    L = du8.shape[1]
    tm = min(TILE_FFN, L)
    n = L // tm
    pmap, nmap = _halo_maps(tm, L)

    def body(ins, outs, scratch):
        x_ref, xp_ref, xn_ref, u_ref, cw_ref = ins
        d_u_ref, dcw_ref = outs
        i = pl.program_id(1)
        ps, ns = _edge_scales(i, n)
        x = x_ref[...].astype(F32)
        x_dn, x_up = _shift(x, xp_ref[...].astype(F32) * ps, xn_ref[...].astype(F32) * ns)
        d_u_ref[...] = (cw_ref[0:1, :] * x_up + cw_ref[1:2, :] * x + cw_ref[2:3, :] * x_dn).astype(BF16)

        @pl.when(i == 0)
        def _():
            dcw_ref[...] = jnp.zeros_like(dcw_ref)

        u = u_ref[...].astype(F32)
        for kk, xs in enumerate((x_up, x, x_dn)):
            dcw_ref[kk:kk + 1, :] += jnp.sum(u * xs, axis=0, keepdims=True)

    tile = pl.BlockSpec((None, tm, SH_FF), lambda d, i: (d, i, 0))
    return _carry(
        body, name=name, grid=(N_DEV, n), args=(du8, du8, du8, u8, cw),
        in_specs=[tile, pl.BlockSpec((None, HB, SH_FF), lambda d, i: (d, pmap(i), 0)),
                  pl.BlockSpec((None, HB, SH_FF), lambda d, i: (d, nmap(i), 0)), tile,
                  pl.BlockSpec((None, None, 3, SH_FF), lambda d, i: (l, d, 0, 0))],
        out_specs=[tile, pl.BlockSpec((None, 3, SH_FF), lambda d, i: (d, 0, 0))],
        out_shape=[jax.ShapeDtypeStruct((N_DEV, L, SH_FF), BF16), jax.ShapeDtypeStruct((N_DEV, 3, SH_FF), F32)],
        scratch=[], comm=None)


def _loss_grad(xl, target, name):
    L = xl.shape[0]
    tm = min(TILE_NORM, L)

    def body(x_ref, t_ref, dx_ref, sq_ref):
        i = pl.program_id(0)
        err = x_ref[...] - t_ref[...]
        dx_ref[...] = err * (1.0 / D)
        part = jnp.sum(err * err, axis=0, keepdims=True)

        @pl.when(i == 0)
        def _():
            sq_ref[...] = part

        @pl.when(i > 0)
        def _():
            sq_ref[...] += part

    tile = pl.BlockSpec((tm, D), lambda i: (i, 0))
    return pl.pallas_call(
        body, name=name, grid=(L // tm,), in_specs=[tile, tile],
        out_specs=[tile, pl.BlockSpec((1, D), lambda i: (0, 0))],
        out_shape=[jax.ShapeDtypeStruct((L, D), F32), jax.ShapeDtypeStruct((1, D), F32)],
        compiler_params=_params(("arbitrary",)),
    )(xl, target)


MESH = pl.DeviceIdType.MESH
HBM_SPEC = pl.BlockSpec(memory_space=pltpu.HBM)


def _position():
    return lax.axis_index("x"), lax.axis_index("y"), lax.axis_index("c")


def _other_chips(x, y):
    return [(1 - x, y), (x, 1 - y), (1 - x, 1 - y)]


def _all_gather(shards, name):
    n = len(shards)

    def body(*refs):
        x_refs, out_refs = refs[:n], refs[n:2 * n]
        send_sems, recv_sems, local_sems = refs[2 * n:]
        x, y, c = _position()
        me, sibling = (x, y, c), (x, y, 1 - c)
        chips = _other_chips(x, y)

        def slot(t, px, py, pc):
            return out_refs[t].at[:, 4 * px + 2 * py + pc]

        def copy(t, k, block, to, from_input=False):
            return pltpu.make_async_remote_copy(
                src_ref=x_refs[t] if from_input else slot(t, *block), dst_ref=slot(t, *block),
                send_sem=send_sems.at[k * n + t], recv_sem=recv_sems.at[k * n + t], device_id=to, device_id_type=MESH)

        mine = [pltpu.make_async_copy(x_refs[t], slot(t, *me), local_sems.at[t]) for t in range(n)]
        for cp in mine:
            cp.start()
        first = [copy(t, 0, me, sibling, True) for t in range(n)]
        first += [copy(t, 1 + j, me, (*chip, c), True) for j, chip in enumerate(chips) for t in range(n)]
        for cp in first:
            cp.start()
        passed = []
        for j, chip in enumerate(chips):
            for t in range(n):
                copy(t, 1 + j, (*chip, c), me).wait_recv()
                passed.append(copy(t, 4 + j, (*chip, c), sibling))
                passed[-1].start()
        for t in range(n):
            copy(t, 0, sibling, me).wait_recv()
        for j, chip in enumerate(chips):
            for t in range(n):
                copy(t, 4 + j, (*chip, 1 - c), me).wait_recv()
        for cp in first + passed:
            cp.wait_send()
        for cp in mine:
            cp.wait()

    return pl.pallas_call(
        body, name=name,
        out_shape=[jax.ShapeDtypeStruct((s.shape[0], N_DEV) + s.shape[1:], s.dtype) for s in shards],
        in_specs=[HBM_SPEC] * n, out_specs=[HBM_SPEC] * n,
        scratch_shapes=[pltpu.SemaphoreType.DMA((7 * n,)), pltpu.SemaphoreType.DMA((7 * n,)),
                        pltpu.SemaphoreType.DMA((n,))],
    )(*shards)


def _exchange_sibling(grads, name):
    n = len(grads)

    def body(*refs):
        g_refs, out_refs, send_sems, recv_sems = refs[:n], refs[n:2 * n], refs[2 * n], refs[2 * n + 1]
        x, y, c = _position()
        copies = [pltpu.make_async_remote_copy(
            src_ref=g_refs[t].at[:, 2 * k + (1 - c)], dst_ref=out_refs[t].at[:, k], send_sem=send_sems.at[k * n + t],
            recv_sem=recv_sems.at[k * n + t], device_id=(x, y, 1 - c), device_id_type=MESH)
            for k in range(N_CHIP) for t in range(n)]
        for cp in copies:
            cp.start()
        for cp in copies:
            cp.wait()

    return pl.pallas_call(
        body, name=name,
        out_shape=[jax.ShapeDtypeStruct((g.shape[0], N_CHIP) + g.shape[2:], g.dtype) for g in grads],
        in_specs=[HBM_SPEC] * n, out_specs=[HBM_SPEC] * n,
        scratch_shapes=[pltpu.SemaphoreType.DMA((N_CHIP * n,)), pltpu.SemaphoreType.DMA((N_CHIP * n,))],
    )(*grads)


class _Comm(NamedTuple):
    ins: tuple
    out_shapes: tuple
    aliases: dict
    sems: tuple
    start: Callable
    wait: Callable


def _comm_of(ins, out_shapes, aliases, sems, copies):
    def start(ci, co, send, recv, local):
        for cp in copies(ci, co, send, recv, local):
            cp.start()

    def wait(ci, co, send, recv, local):
        for cp in copies(ci, co, send, recv, local):
            cp.wait()

    return _Comm(tuple(ins), tuple(out_shapes), aliases, sems, start, wait)


def _remote(src, dst, send, recv, idx, to):
    return pltpu.make_async_remote_copy(src_ref=src, dst_ref=dst, send_sem=send.at[idx], recv_sem=recv.at[idx],
                                        device_id=to, device_id_type=MESH)


def _gather_ici_comm(shards):
    n = len(shards)

    def copies(ci, co, send, recv, local):
        x, y, c = _position()
        me = 4 * x + 2 * y + c
        mine = [pltpu.make_async_copy(ci[t], co[t].at[:, me], local.at[t]) for t in range(n)]
        return mine + [_remote(ci[t], co[t].at[:, me], send, recv, j * n + t, (cx, cy, c))
                       for j, (cx, cy) in enumerate(_other_chips(x, y)) for t in range(n)]

    outs = [jax.ShapeDtypeStruct((1, N_DEV) + s.shape[1:], s.dtype) for s in shards]
    return _comm_of(shards, outs, {}, (3 * n, 3 * n, n), copies)


def _gather_d2d_comm(partials):
    n = len(partials)

    def copies(ci, co, send, recv, local):
        x, y, c = _position()
        return [_remote(co[t].at[:, 4 * cx + 2 * cy + c], co[t].at[:, 4 * cx + 2 * cy + c], send, recv, k * n + t,
                        (x, y, 1 - c))
                for k, (cx, cy) in enumerate([(x, y)] + _other_chips(x, y)) for t in range(n)]

    outs = [jax.ShapeDtypeStruct(p.shape, p.dtype) for p in partials]
    return _comm_of(partials, outs, {t: t for t in range(n)}, (N_CHIP * n, N_CHIP * n, 1), copies)


def _grads_d2d_comm(grads):
    n = len(grads)

    def copies(ci, co, send, recv, local):
        x, y, c = _position()
        return [_remote(ci[t].at[:, 2 * k + (1 - c)], co[t].at[:, k], send, recv, k * n + t, (x, y, 1 - c))
                for k in range(N_CHIP) for t in range(n)]

    outs = [jax.ShapeDtypeStruct((g.shape[0], N_CHIP) + g.shape[2:], g.dtype) for g in grads]
    return _comm_of(grads, outs, {}, (N_CHIP * n, N_CHIP * n, 1), copies)


def _grads_ici_comm(parts):
    n = len(parts)

    def copies(ci, co, send, recv, local):
        x, y, c = _position()
        my_chip = 2 * x + y
        mine = [pltpu.make_async_copy(ci[t].at[:, my_chip], co[t].at[:, my_chip], local.at[t]) for t in range(n)]
        return mine + [_remote(ci[t].at[:, 2 * cx + cy], co[t].at[:, my_chip], send, recv, j * n + t, (cx, cy, c))
                       for j, (cx, cy) in enumerate(_other_chips(x, y)) for t in range(n)]

    outs = [jax.ShapeDtypeStruct(p.shape, p.dtype) for p in parts]
    return _comm_of(parts, outs, {}, (3 * n, 3 * n, n), copies)


def _row_tile(rows):
    return 256 if rows % 256 == 0 else rows


def _pair_sum(g, recv, c_idx, out_dtype, name):
    lay, _, rows, cols = g.shape
    tr = _row_tile(rows)

    def body(c_ref, g_ref, r_ref, o_ref):
        o_ref[...] = (g_ref[...] + r_ref[...]).astype(o_ref.dtype)

    def spec(blk_of):
        return pl.BlockSpec((None, None, tr, cols), lambda l, k, r, c_ref: (l, blk_of(k, c_ref), r, 0))

    return pl.pallas_call(
        body, name=name,
        grid_spec=pltpu.PrefetchScalarGridSpec(
            num_scalar_prefetch=1, grid=(lay, N_CHIP, rows // tr),
            in_specs=[spec(lambda k, c_ref: 2 * k + c_ref[0]), spec(lambda k, c_ref: k)],
            out_specs=spec(lambda k, c_ref: k)),
        out_shape=jax.ShapeDtypeStruct((lay, N_CHIP, rows, cols), out_dtype),
        compiler_params=_params(("parallel", "parallel", "parallel")),
    )(c_idx, g, recv)


def _exchange_chips(parts, name):
    n = len(parts)

    def body(*refs):
        p_refs, out_refs = refs[:n], refs[n:2 * n]
        send_sems, recv_sems, local_sems = refs[2 * n:]
        x, y, c = _position()
        my_chip = 2 * x + y
        mine = [pltpu.make_async_copy(p_refs[t].at[:, my_chip], out_refs[t].at[:, my_chip], local_sems.at[t])
                for t in range(n)]
        for cp in mine:
            cp.start()
        copies = [pltpu.make_async_remote_copy(
            src_ref=p_refs[t].at[:, 2 * cx + cy], dst_ref=out_refs[t].at[:, my_chip], send_sem=send_sems.at[j * n + t],
            recv_sem=recv_sems.at[j * n + t], device_id=(cx, cy, c), device_id_type=MESH)
            for j, (cx, cy) in enumerate(_other_chips(x, y)) for t in range(n)]
        for cp in copies:
            cp.start()
        for cp in copies:
            cp.wait()
        for cp in mine:
            cp.wait()

    return pl.pallas_call(
        body, name=name, out_shape=[jax.ShapeDtypeStruct(p.shape, p.dtype) for p in parts],
        in_specs=[HBM_SPEC] * n, out_specs=[HBM_SPEC] * n,
        scratch_shapes=[pltpu.SemaphoreType.DMA((3 * n,)), pltpu.SemaphoreType.DMA((3 * n,)),
                        pltpu.SemaphoreType.DMA((n,))],
    )(*parts)


def _sum_adamw(parts, w, m, v, name):
    lay, rows, cols = w.shape
    tr = _row_tile(rows)

    def body(p_ref, w_ref, m_ref, v_ref, g_ref, d_ref, nm_ref, nv_ref):
        g = ((p_ref[0].astype(F32) + p_ref[1].astype(F32)) + p_ref[2].astype(F32)) + p_ref[3].astype(F32)
        g_ref[...] = g
        nm = ADAM_B1 * m_ref[...] + (1.0 - ADAM_B1) * g
        nv = ADAM_B2 * v_ref[...] + (1.0 - ADAM_B2) * (g * g)
        nm_ref[...] = nm
        nv_ref[...] = nv
        m_hat = nm / (1.0 - ADAM_B1 ** ADAM_STEP)
        v_hat = nv / (1.0 - ADAM_B2 ** ADAM_STEP)
        d_ref[...] = -ADAM_LR * (m_hat / (jnp.sqrt(v_hat) + ADAM_EPS) + ADAM_WD * w_ref[...])

    tile = pl.BlockSpec((None, tr, cols), lambda l, r: (l, r, 0))
    return pl.pallas_call(
        body, name=name, grid=(lay, rows // tr),
        in_specs=[pl.BlockSpec((None, N_CHIP, tr, cols), lambda l, r: (l, 0, r, 0)), tile, tile, tile],
        out_specs=[tile] * 4, out_shape=[jax.ShapeDtypeStruct((lay, rows, cols), F32)] * 4,
        compiler_params=_params(("parallel", "parallel")),
    )(parts, w, m, v)


def _pad_rows(flat, rows):
    return jnp.pad(flat, (0, rows * LANES - flat.shape[0])).reshape(rows, LANES)


def _pack_small(tree):
    sh = jnp.concatenate([tree[n].reshape(-1) for n, _, _ in SMALL_SHARDED])
    rep = jnp.concatenate([tree[n].reshape(-1) for n, _ in REPLICATED])
    return jnp.concatenate([_pad_rows(sh, ROWS_SSH), _pad_rows(rep, ROWS_REP)], axis=0)


def _unpack_small(buf):
    out = {}
    for flat, items in ((buf[:ROWS_SSH].reshape(-1), [(n, s) for n, s, _ in SMALL_SHARDED]),
                        (buf[ROWS_SSH:].reshape(-1), REPLICATED)):
        off = 0
        for n, s in items:
            out[n] = flat[off:off + math.prod(s)].reshape(s)
            off += math.prod(s)
    return out


def _full_from_blocks(blocks, s, ax):
    return jnp.concatenate([blocks[d] for d in range(N_DEV)], axis=ax)


def _blocks_from_full(full, s, ax):
    return jnp.stack([lax.slice_in_dim(full, d * s[ax], (d + 1) * s[ax], axis=ax) for d in range(N_DEV)])


def _pack_small_grads(sharded_blocks, replicated):
    sh = jnp.concatenate([sharded_blocks[n].reshape(N_DEV, -1) for n, _, _ in SMALL_SHARDED], axis=1)
    sh = jnp.pad(sh, ((0, 0), (0, ROWS_SSH * LANES - sh.shape[1]))).reshape(N_DEV, ROWS_SSH, LANES)
    rep = _pad_rows(jnp.concatenate([replicated[n].reshape(-1) for n, _ in REPLICATED]), ROWS_REP)
    return jnp.concatenate([sh, jnp.broadcast_to(rep[None], (N_DEV, ROWS_REP, LANES))], axis=1)


def _layer_fwd(x, h1, wts, big, l, l_next, next_shards, own_ffn):
    L = x.shape[0]
    tm = min(TILE_MM, L)
    nt = L // tm
    nxt = dict(zip(BIG, next_shards)) if next_shards else {}
    if own_ffn:
        at_proj_in, at_ffn_up = {"own_up": own_ffn[0], "own_down": own_ffn[1]}, nxt
    else:
        at_proj_in = {n: nxt[n] for n in ("w_in", "w_out") if n in nxt}
        at_ffn_up = {n: nxt[n] for n in ("w_up", "w_down") if n in nxt}

    def carried(result, comm):
        return (result[0], list(result[1:])) if comm else (result, [])

    comm = _gather_ici_comm(list(at_proj_in.values())) if at_proj_in else None
    p, got = carried(_mm(h1, big["w_in"], dims=NN, grid=(nt, D_INP // 640, 1),
                         a_spec=((tm, D), lambda i, j, k: (i, 0)), b_spec=((None, D, 640), lambda i, j, k: (0, 0, j)),
                         o_spec=((tm, 640), lambda i, j, k: (i, j)), out_shape=jax.ShapeDtypeStruct((L, D_INP), BF16),
                         tile=(tm, 640), name="proj_in", comm=comm), comm)
    stage1 = dict(zip(at_proj_in, got))
    o_f, sp_f = _gla_fwd(p, wts["gpad_f"], wts["bias_f"], l, None, True, "gla_fwd_f")
    o_tot, sp_b = _gla_fwd(p, wts["gpad_b"], wts["bias_b"], l, o_f, False, "gla_fwd_b")
    y_cat = _mixer_out(p, wts["conv_a"], wts["gh"], l, o_tot, "mixer_out")
    comm = _gather_d2d_comm([stage1["own_up"], stage1["own_down"]]) if own_ffn else None
    y, got = carried(_mm(y_cat, big["w_out"], dims=NN, grid=(nt, 1, 1),
                         a_spec=((tm, D), lambda i, j, k: (i, 0)), b_spec=((None, D, D), lambda i, j, k: (0, 0, 0)),
                         o_spec=((tm, D), lambda i, j, k: (i, 0)), out_shape=jax.ShapeDtypeStruct((L, D), BF16),
                         tile=(tm, D), name="proj_out", comm=comm), comm)
    if own_ffn:
        big = dict(big, w_up=got[0], w_down=got[1].reshape(1, FF_HALF, SH_FF, D))
    x1, h2 = _post_pre(x, y, wts["g2"], l, wts["g3"], l, "post_pre_mix")
    comm = _gather_ici_comm(list(at_ffn_up.values())) if at_ffn_up else None
    u8, got = carried(_mm(h2, big["w_up"], dims=NN, grid=(nt, N_DEV, 1),
                          a_spec=((tm, D), lambda i, j, k: (i, 0)),
                          b_spec=((None, None, D, SH_FF), lambda i, j, k: (0, j, 0, 0)),
                          o_spec=((None, tm, SH_FF), lambda i, j, k: (j, i, 0)),
                          out_shape=jax.ShapeDtypeStruct((N_DEV, L, SH_FF), BF16), tile=(tm, SH_FF), name="ffn_up",
                          comm=comm), comm)
    stage1.update(zip(at_ffn_up, got))
    a, conv = _ffn_act(u8, wts["cw"], l, "ffn_act")
    tm1 = min(TILE_MM_KIN, L)
    comm = _gather_d2d_comm([stage1[n] for n in BIG]) if next_shards else None
    y2, gathered = carried(_mm(a, big["w_down"], dims=NN, grid=(L // tm1, 1, 1), kin=FF_HALF,
                               a_spec=((FF_HALF, tm1, SH_FF), lambda i, j, k: (0, i, 0)),
                               b_spec=((None, FF_HALF, SH_FF, D), lambda i, j, k: (0, 0, 0, 0)),
                               o_spec=((tm1, D), lambda i, j, k: (i, 0)),
                               out_shape=jax.ShapeDtypeStruct((L, D), BF16), tile=(tm1, D), name="ffn_down",
                               comm=comm), comm)
    x2, h1_next = _post_pre(x1, y2, wts["g4"], l, wts["g1"], l_next, "post_pre_ffn")
    saved = dict(x=x, h1=h1, p=p, o_tot=o_tot, sp_f=sp_f, sp_b=sp_b, y_cat=y_cat, y=y, x1=x1, h2=h2, u8=u8, a=a, y2=y2,
                 big=big, conv=conv)
    return x2, h1_next, saved, gathered


def _layer_bwd(dx2, wts, s, l, pending, c_idx, early_ffn, post, below_y2):
    L = dx2.shape[0]
    big = s["big"]
    tm = min(TILE_MM, L)
    nt = L // tm
    tm1 = min(TILE_MM_KIN, L)
    tk = min(TILE_MM_TOKENS, L)
    nkt = L // tk
    dy2, dg4 = post if post else _norm_bwd(s["y2"], wts["g4"], l, dx2, None, "norm_bwd_ffn_post")
    da = _mm(dy2, big["w_down"], dims=NT, grid=(nt, FF_HALF, 1),
             a_spec=((tm, D), lambda i, j, k: (i, 0)), b_spec=((None, None, SH_FF, D), lambda i, j, k: (0, j, 0, 0)),
             o_spec=((None, tm, SH_FF), lambda i, j, k: (j, i, 0)),
             out_shape=jax.ShapeDtypeStruct((FF_HALF, L, SH_FF), BF16), tile=(tm, SH_FF), name="ffn_down_dx",
             comm=_grads_d2d_comm(pending) if pending else None)
    pairs = None
    if pending:
        da, from_sibling = da[0], da[1:]
        pairs = [_pair_sum(g, r, c_idx, BF16, "grads_pair_sum") for g, r in zip(pending, from_sibling)]
    dw_down = _mm(s["a"], dy2, dims=TN, grid=(FF_HALF, 1, nkt),
                  a_spec=((None, tk, SH_FF), lambda i, j, k: (i, k, 0)), b_spec=((tk, D), lambda i, j, k: (k, 0)),
                  o_spec=((SH_FF, D), lambda i, j, k: (i, 0)), out_shape=jax.ShapeDtypeStruct((DFF, D), F32),
                  tile=(SH_FF, D), name="ffn_down_dw")
    du = _ffn_act_bwd(s["conv"], da, "ffn_act_bwd")
    d_u8, dcw = _ffn_conv_t(du.reshape(N_DEV, L, SH_FF), s["u8"], wts["cw"], l, "ffn_conv_t")
    dh2 = _mm(d_u8, big["w_up"], dims=NT, grid=(L // tm1, 1, N_DEV // FF_HALF), kin=FF_HALF,
              a_spec=((FF_HALF, tm1, SH_FF), lambda i, j, k: (k, i, 0)),
              b_spec=((None, FF_HALF, D, SH_FF), lambda i, j, k: (0, k, 0, 0)),
              o_spec=((tm1, D), lambda i, j, k: (i, 0)), out_shape=jax.ShapeDtypeStruct((L, D), BF16),
              tile=(tm1, D), name="ffn_up_dx",
              comm=_grads_ici_comm([q for n, q in zip(BIG, pairs) if n != "w_up"]) if pending else None)
    if pending:
        dh2, rest_parts = dh2[0], dh2[1:]
    dw_up = _mm(s["h2"], d_u8, dims=TN, grid=(1, N_DEV, nkt),
                a_spec=((tk, D), lambda i, j, k: (k, 0)), b_spec=((None, tk, SH_FF), lambda i, j, k: (j, k, 0)),
                o_spec=((None, D, SH_FF), lambda i, j, k: (j, 0, 0)),
                out_shape=jax.ShapeDtypeStruct((N_DEV, D, SH_FF), F32), tile=(D, SH_FF), name="ffn_up_dw",
                comm=_grads_ici_comm([pairs[BIG.index("w_up")]]) if pending else None)
    parts = None
    if pending:
        dw_up, up_part = dw_up[0], dw_up[1]
        parts = [rest_parts[0], rest_parts[1], up_part, rest_parts[2]]
    dx1, dg3, dy, dg2 = _norm_bwd(s["x1"], wts["g3"], l, dh2, dx2, "norm_bwd_ffn_pre_mix_post",
                                  then=(s["y"], wts["g2"], l))
    dy_cat = _mm(dy, big["w_out"], dims=NT, grid=(nt, 1, 1),
                 a_spec=((tm, D), lambda i, j, k: (i, 0)), b_spec=((None, D, D), lambda i, j, k: (0, 0, 0)),
                 o_spec=((tm, D), lambda i, j, k: (i, 0)), out_shape=jax.ShapeDtypeStruct((L, D), BF16),
                 tile=(tm, D), name="proj_out_dx")
    dw_out = _mm(s["y_cat"], dy, dims=TN, grid=(1, 1, nkt),
                 a_spec=((tk, D), lambda i, j, k: (k, 0)), b_spec=((tk, D), lambda i, j, k: (k, 0)),
                 o_spec=((D, D), lambda i, j, k: (0, 0)), out_shape=jax.ShapeDtypeStruct((D, D), F32),
                 tile=(D, D), name="proj_out_dw")
    dgb, dgc, dgv, dgo, d_o, dconv_a, dgh = _mixer_out_bwd(s["p"], wts["conv_a"], wts["gh"], l, s["o_tot"], dy_cat,
                                                          "mixer_out_bwd")
    part_f = _gla_bwd(s["p"], wts["gpad_f"], wts["bias_f"], l, s["sp_f"], d_o, None, True, "gla_bwd_f")
    dp, dgp_b, dbias_b = _gla_bwd(s["p"], wts["gpad_b"], wts["bias_b"], l, s["sp_b"], d_o,
                                  list(part_f[:4]) + [dgb, dgc, dgv, dgo], False, "gla_bwd_b")
    early = [dw_up[None], dw_down.reshape(1, N_DEV, DFF // N_DEV, D)] if early_ffn else None
    dh1 = _mm(dp, big["w_in"], dims=NT, grid=(L // tm1, 1, 1),
              a_spec=((tm1, D_INP), lambda i, j, k: (i, 0)), b_spec=((None, D, D_INP), lambda i, j, k: (0, 0, 0)),
              o_spec=((tm1, D), lambda i, j, k: (i, 0)), out_shape=jax.ShapeDtypeStruct((L, D), BF16),
              tile=(tm1, D), name="proj_in_dx", comm=_grads_d2d_comm(early) if early_ffn else None)
    if early_ffn:
        dh1, early_sibling = dh1[0], dh1[1:]
        early = [_pair_sum(g, r, c_idx, BF16, "grads_pair_sum") for g, r in zip(early, early_sibling)]
    dw_in = _mm(s["h1"], dp, dims=TN, grid=(1, D_INP // 640, nkt),
                a_spec=((tk, D), lambda i, j, k: (k, 0)), b_spec=((tk, 640), lambda i, j, k: (k, j)),
                o_spec=((D, 640), lambda i, j, k: (0, j)), out_shape=jax.ShapeDtypeStruct((D, D_INP), F32),
                tile=(D, 640), name="proj_in_dw", comm=_grads_ici_comm(early) if early_ffn else None)
    if early_ffn:
        dw_in, early = dw_in[0], dw_in[1:]
    if below_y2 is None:
        dx0, dg1 = _norm_bwd(s["x"], wts["g1"], l, dh1, dx1, "norm_bwd_mix_pre")
        post_below = None
    else:
        dx0, dg1, *post_below = _norm_bwd(s["x"], wts["g1"], l, dh1, dx1, "norm_bwd_mix_pre_ffn_post",
                                          then=(below_y2, wts["g4"], l - 1))
    grads = dict(
        norm_mix_pre=dg1[0], norm_mix_post=dg2[0], norm_ffn_pre=dg3[0], norm_ffn_post=dg4[0],
        gate_bias_fwd=part_f[5][0], gate_bias_bwd=dbias_b[0], gla_head_norm=dgh[0],
        w_in=_blocks_from_full(dw_in, (D, SH_IN), 1), w_out=dw_out.reshape(N_DEV, D // N_DEV, D), w_up=dw_up,
        w_down=dw_down.reshape(N_DEV, DFF // N_DEV, D),
        conv_a=_blocks_from_full(dconv_a, (3, DC // N_DEV), 1),
        gate_up_fwd=_blocks_from_full(part_f[4][:RANK], (RANK, DK // N_DEV), 1),
        gate_up_bwd=_blocks_from_full(dgp_b[RANK:2 * RANK], (RANK, DK // N_DEV), 1),
        conv_ffn=dcw.reshape(N_DEV, 3, SH_FF))
    return dx0, grads, parts, early, post_below


def _matmul_weights(g_in, g_out, g_up, g_down):
    w_in = jnp.concatenate([g_in[:, d] for d in range(N_DEV)] + [jnp.zeros((1, D, D_INP - D_IN), BF16)], axis=2)
    return dict(w_in=w_in, w_out=g_out.reshape(1, D, D), w_up=g_up,
                w_down=None if g_down is None else g_down.reshape(1, FF_HALF, SH_FF, D))


def _small_weights(g_small, rep):
    small = {n: jnp.moveaxis(t, 0, 1) for n, t in jax.vmap(_unpack_small)(
        jnp.concatenate([g_small[0], jnp.zeros((N_DEV, ROWS_REP, LANES), F32)], axis=1)).items()
        if n in [s[0] for s in SMALL_SHARDED]}
    conv_a = jnp.concatenate([small["conv_a"][:, d] for d in range(N_DEV)], axis=2)
    gate_f = jnp.concatenate([small["gate_up_fwd"][:, d] for d in range(N_DEV)], axis=2).astype(BF16)
    gate_b = jnp.concatenate([small["gate_up_bwd"][:, d] for d in range(N_DEV)], axis=2).astype(BF16)
    zeros = jnp.zeros((DEPTH, LR_BLK, DK), BF16)
    return dict(
        conv_a=conv_a, cw=small["conv_ffn"],
        gpad_f=zeros.at[:, :RANK].set(gate_f), gpad_b=zeros.at[:, RANK:2 * RANK].set(gate_b),
        bias_f=rep["gate_bias_fwd"][:, None, :], bias_b=rep["gate_bias_bwd"][:, None, :],
        gh=rep["gla_head_norm"][:, None, :],
        g1=rep["norm_mix_pre"][:, None, :], g2=rep["norm_mix_post"][:, None, :],
        g3=rep["norm_ffn_pre"][:, None, :], g4=rep["norm_ffn_post"][:, None, :])


def kernel(x, norm_mix_pre, norm_mix_post, norm_ffn_pre, norm_ffn_post, w_in, conv_a, gate_up_fwd, gate_bias_fwd, gate_up_bwd, gate_bias_bwd, gla_head_norm, w_out, w_up, conv_ffn, w_down, loss_target, m_norm_mix_pre, m_norm_mix_post, m_norm_ffn_pre, m_norm_ffn_post, m_w_in, m_conv_a, m_gate_up_fwd, m_gate_bias_fwd, m_gate_up_bwd, m_gate_bias_bwd, m_gla_head_norm, m_w_out, m_w_up, m_conv_ffn, m_w_down, v_norm_mix_pre, v_norm_mix_post, v_norm_ffn_pre, v_norm_ffn_post, v_w_in, v_conv_a, v_gate_up_fwd, v_gate_bias_fwd, v_gate_up_bwd, v_gate_bias_bwd, v_gla_head_norm, v_w_out, v_w_up, v_conv_ffn, v_w_down):
    w = dict(norm_mix_pre=norm_mix_pre, norm_mix_post=norm_mix_post, norm_ffn_pre=norm_ffn_pre,
             norm_ffn_post=norm_ffn_post, w_in=w_in, conv_a=conv_a, gate_up_fwd=gate_up_fwd,
             gate_bias_fwd=gate_bias_fwd, gate_up_bwd=gate_up_bwd, gate_bias_bwd=gate_bias_bwd,
             gla_head_norm=gla_head_norm, w_out=w_out, w_up=w_up, conv_ffn=conv_ffn, w_down=w_down)
    m = dict(norm_mix_pre=m_norm_mix_pre, norm_mix_post=m_norm_mix_post, norm_ffn_pre=m_norm_ffn_pre,
             norm_ffn_post=m_norm_ffn_post, w_in=m_w_in, conv_a=m_conv_a, gate_up_fwd=m_gate_up_fwd,
             gate_bias_fwd=m_gate_bias_fwd, gate_up_bwd=m_gate_up_bwd, gate_bias_bwd=m_gate_bias_bwd,
             gla_head_norm=m_gla_head_norm, w_out=m_w_out, w_up=m_w_up, conv_ffn=m_conv_ffn, w_down=m_w_down)
    v = dict(norm_mix_pre=v_norm_mix_pre, norm_mix_post=v_norm_mix_post, norm_ffn_pre=v_norm_ffn_pre,
             norm_ffn_post=v_norm_ffn_post, w_in=v_w_in, conv_a=v_conv_a, gate_up_fwd=v_gate_up_fwd,
             gate_bias_fwd=v_gate_bias_fwd, gate_up_bwd=v_gate_up_bwd, gate_bias_bwd=v_gate_bias_bwd,
             gla_head_norm=v_gla_head_norm, w_out=v_w_out, w_up=v_w_up, conv_ffn=v_conv_ffn, w_down=v_w_down)
    axes = ("x", "y", "c")
    L = x.shape[1]
    x0 = x.reshape(L, D)
    target = loss_target.reshape(L, D)

    w_small = _pack_small(w)
    w16 = {n: w[n].astype(BF16) for n in BIG}
    g_in, g_out, g_small = _all_gather([w16["w_in"][0:1], w16["w_out"][0:1], w_small[None, :ROWS_SSH]],
                                       "gather_weights")
    wts = _small_weights(g_small, w)
    big = {n: t for n, t in _matmul_weights(g_in, g_out, None, None).items() if t is not None}

    h1 = _norm_cast(x0, wts["g1"], 0, "norm_first")
    xl, saved = x0, []
    for l in range(DEPTH):
        nxt = [w16[n][l + 1:l + 2] for n in BIG] if l + 1 < DEPTH else None
        own = [w16["w_up"][0:1], w16["w_down"][0:1]] if l == 0 else None
        xl, h1, s, gathered = _layer_fwd(xl, h1, wts, big, l, min(l + 1, DEPTH - 1), nxt, own)
        saved.append(s)
        if nxt:
            big = _matmul_weights(*gathered)
    dx, sq = _loss_grad(xl, target, "loss_grad")
    loss = lax.psum(0.5 * jnp.sum(sq) / D, axes)

    c_idx = lax.axis_index("c").astype(jnp.int32).reshape(1)
    layer_grads, layer_parts, pending = [None] * DEPTH, [None] * DEPTH, None
    post = None
    for l in reversed(range(DEPTH)):
        dx, layer_grads[l], done, early, post = _layer_bwd(
            dx, wts, saved[l], l, pending, c_idx, l == 0, post, saved[l - 1]["y2"] if l > 0 else None)
        if pending:
            layer_parts[l + 1] = done
        pending = [layer_grads[l][n][None] for n in BIG]
    small_names = [n for n, _, _ in SMALL_SHARDED] + [n for n, _ in REPLICATED]
    stacked = {n: jnp.stack([g[n] for g in layer_grads]) for n in small_names}
    g_small = _pack_small_grads({n: jnp.moveaxis(stacked[n], 0, 1) for n, _, _ in SMALL_SHARDED}, stacked)
    last = [g for n, g in zip(BIG, pending) if n in ("w_in", "w_out")] + [g_small[None]]
    from_sibling = _exchange_sibling(last, "grads_to_sibling")
    pairs = [_pair_sum(g, r, c_idx, BF16 if i < 2 else F32, "grads_pair_sum")
             for i, (g, r) in enumerate(zip(last, from_sibling))]
    parts = _exchange_chips(pairs, "grads_to_chips")
    layer_parts[0] = [parts[0], parts[1], early[0], early[1]]

    results = {}
    for i, n in enumerate(BIG):
        part = jnp.concatenate([layer_parts[l][i] for l in range(DEPTH)], axis=0)
        results[n] = _sum_adamw(part, w[n], m[n], v[n], "sum_adamw")
    small = _sum_adamw(parts[-1], w_small[None], _pack_small(m)[None], _pack_small(v)[None], "sum_adamw_small")
    small = [_unpack_small(buf[0]) for buf in small]
    outs = [loss, dx.reshape(x.shape)]
    for i in range(4):
        outs += [results[n][i] if n in BIG else small[i][n] for n in WEIGHT_ORDER]
    return tuple(outs)
```

```python
import math
from typing import Callable, NamedTuple

import jax
import jax.numpy as jnp
from jax import lax
from jax.experimental import pallas as pl
from jax.experimental.pallas import tpu as pltpu

F32 = jnp.float32
BF16 = jnp.bfloat16

DEPTH = 4
D = 1024
DC = 512
DG = 512
HEADS = 4
HV = 128
HK = 64
DK = 256
RANK = 16
CH = 64
DFF = 2816
D_IN = 3104
D_INP = 3200
LR_BLK = 128
EPS = 1e-6
HB = 16
N_DEV = 8
N_CHIP = 4
LANES = 1024
SH_IN = D_IN // N_DEV
SH_FF = 2 * DFF // N_DEV
FF_HALF = N_DEV // 2

ADAM_LR, ADAM_B1, ADAM_B2, ADAM_EPS, ADAM_WD, ADAM_STEP = 0.001, 0.9, 0.999, 1e-08, 0.01, 10

VMEM_LIMIT = 48 * 1024 * 1024
TILE_TOKENS = 512
TILE_GLA = 512
TILE_FFN = 1024
TILE_NORM = 512
TILE_MM = 2048
TILE_MM_KIN = 1024
TILE_MM_TOKENS = 2048

COL_GB, COL_GC, COL_GV = 0, 1, 2
COL_Q, COL_K = 6, 7
COL_V, COL_GO = 4, 5
COL_LR = 24

BIG = ("w_in", "w_out", "w_up", "w_down")
TRANSPOSED = ("w_in", "w_up")
SMALL_SHARDED = (
    ("conv_a", (DEPTH, 3, DC // N_DEV), 2),
    ("gate_up_fwd", (DEPTH, RANK, DK // N_DEV), 2),
    ("gate_up_bwd", (DEPTH, RANK, DK // N_DEV), 2),
    ("conv_ffn", (DEPTH, 3, SH_FF), 2),
)
REPLICATED = (
    ("norm_mix_pre", (DEPTH, D)), ("norm_mix_post", (DEPTH, D)), ("norm_ffn_pre", (DEPTH, D)),
    ("norm_ffn_post", (DEPTH, D)), ("gate_bias_fwd", (DEPTH, DK)), ("gate_bias_bwd", (DEPTH, DK)),
    ("gla_head_norm", (DEPTH, HV)),
)
WEIGHT_ORDER = ("norm_mix_pre", "norm_mix_post", "norm_ffn_pre", "norm_ffn_post", "w_in", "conv_a", "gate_up_fwd",
                "gate_bias_fwd", "gate_up_bwd", "gate_bias_bwd", "gla_head_norm", "w_out", "w_up", "conv_ffn", "w_down")


def _rows_for(n_elems):
    return (-(-n_elems // LANES) + 7) // 8 * 8


ROWS_SSH = _rows_for(sum(math.prod(s) for _, s, _ in SMALL_SHARDED))
ROWS_REP = _rows_for(sum(math.prod(s) for _, s in REPLICATED))
ROWS_SMALL = ROWS_SSH + ROWS_REP


def _params(sem):
    return pltpu.CompilerParams(dimension_semantics=sem, vmem_limit_bytes=VMEM_LIMIT)


def _silu_parts(x):
    s = 1.0 / (1.0 + jnp.exp(-x))
    return x * s, s


def _rstd(xf):
    return lax.rsqrt(jnp.mean(xf * xf, axis=-1, keepdims=True) + EPS)


NN, NT, TN = ((1,), (0,)), ((1,), (1,)), ((0,), (0,))


def _dot(a, b, dims):
    return lax.dot_general(a, b, (dims, ((), ())), preferred_element_type=F32)


def _mm(a, b, *, dims, grid, a_spec, b_spec, o_spec, out_shape, tile, name, kin=0, comm=None):
    nk = grid[2]
    n_ci = len(comm.ins) if comm else 0
    n_co = len(comm.out_shapes) if comm else 0

    def body(*refs):
        a_ref, b_ref = refs[:2]
        ci = refs[2:2 + n_ci]
        o_ref = refs[2 + n_ci]
        co = refs[3 + n_ci:3 + n_ci + n_co]
        rest = refs[3 + n_ci + n_co:]
        if comm:
            sems, rest = rest[:3], rest[3:]
            step = (pl.program_id(0) * grid[1] + pl.program_id(1)) * grid[2] + pl.program_id(2)

            @pl.when(step == 0)
            def _():
                comm.start(ci, co, *sems)

        if kin:
            prod = _dot(a_ref[0], b_ref[0], dims)
            for d in range(1, kin):
                prod = prod + _dot(a_ref[d], b_ref[d], dims)
        else:
            prod = _dot(a_ref[...], b_ref[...], dims)
        if nk == 1:
            o_ref[...] = prod.astype(o_ref.dtype)
        else:
            acc_ref = rest[0]
            k = pl.program_id(2)

            @pl.when(k == 0)
            def _():
                acc_ref[...] = prod

            @pl.when(k > 0)
            def _():
                acc_ref[...] += prod

            @pl.when(k == nk - 1)
            def _():
                o_ref[...] = acc_ref[...].astype(o_ref.dtype)

        if comm:
            @pl.when(step == grid[0] * grid[1] * grid[2] - 1)
            def _():
                comm.wait(ci, co, *sems)

    acc = [pltpu.VMEM(tile, F32)] if nk > 1 else []
    if not comm:
        return pl.pallas_call(
            body, name=name, grid=grid, in_specs=[pl.BlockSpec(*a_spec), pl.BlockSpec(*b_spec)],
            out_specs=pl.BlockSpec(*o_spec), out_shape=out_shape, scratch_shapes=acc,
            compiler_params=_params(("parallel", "parallel", "arbitrary")),
        )(a, b)
    return pl.pallas_call(
        body, name=name, grid=grid,
        in_specs=[pl.BlockSpec(*a_spec), pl.BlockSpec(*b_spec)] + [HBM_SPEC] * n_ci,
        out_specs=[pl.BlockSpec(*o_spec)] + [HBM_SPEC] * n_co, out_shape=[out_shape] + list(comm.out_shapes),
        scratch_shapes=[pltpu.SemaphoreType.DMA((s,)) for s in comm.sems] + acc,
        input_output_aliases={2 + i: 1 + o for i, o in comm.aliases.items()},
        compiler_params=_params(("arbitrary", "arbitrary", "arbitrary")),
    )(a, b, *comm.ins)


def _halo_maps(tm, n_rows):
    r, last = tm // HB, n_rows // HB - 1
    return (lambda i: jnp.maximum(i * r - 1, 0)), (lambda i: jnp.minimum((i + 1) * r, last))


def _shift(x, prev_blk, next_blk):
    tm = x.shape[0]
    xs = jnp.concatenate([prev_blk, x, next_blk], axis=0)
    n = xs.shape[0]
    down = pltpu.roll(xs, 1, 0)[HB:HB + tm]
    up = pltpu.roll(xs, n - 1, 0)[HB:HB + tm]
    return down, up


def _edge_scales(i, n):
    return jnp.where(i > 0, 1.0, 0.0).astype(F32), jnp.where(i < n - 1, 1.0, 0.0).astype(F32)


def _gain_spec(l, width=D):
    return pl.BlockSpec((None, 1, width), lambda *_: (l, 0, 0))


def _norm_cast(x, g, l, name):
    L = x.shape[0]
    tm = min(TILE_NORM, L)

    def body(x_ref, g_ref, o_ref):
        xf = x_ref[...]
        o_ref[...] = (xf * _rstd(xf) * g_ref[...]).astype(BF16)

    return pl.pallas_call(
        body, name=name, grid=(L // tm,), in_specs=[pl.BlockSpec((tm, D), lambda i: (i, 0)), _gain_spec(l)],
        out_specs=pl.BlockSpec((tm, D), lambda i: (i, 0)), out_shape=jax.ShapeDtypeStruct((L, D), BF16),
        compiler_params=_params(("parallel",)),
    )(x, g)


def _post_pre(x, y, g_post, l_post, g_pre, l_pre, name):
    L = x.shape[0]
    tm = min(TILE_NORM, L)

    def body(x_ref, y_ref, gp_ref, gn_ref, x1_ref, h_ref):
        yf = y_ref[...].astype(F32)
        x1 = x_ref[...] + yf * _rstd(yf) * gp_ref[...]
        x1_ref[...] = x1
        h_ref[...] = (x1 * _rstd(x1) * gn_ref[...]).astype(BF16)

    tile = pl.BlockSpec((tm, D), lambda i: (i, 0))
    return pl.pallas_call(
        body, name=name, grid=(L // tm,), in_specs=[tile, tile, _gain_spec(l_post), _gain_spec(l_pre)],
        out_specs=[tile, tile],
        out_shape=[jax.ShapeDtypeStruct((L, D), F32), jax.ShapeDtypeStruct((L, D), BF16)],
        compiler_params=_params(("parallel",)),
    )(x, y, g_post, g_pre)


def _norm_bwd(yin, g, l, dout, dres, name, then=None):
    L = yin.shape[0]
    tm = min(TILE_NORM, L)
    with_res = dres is not None

    def norm_vjp(y_ref, g_ref, do):
        y = y_ref[...].astype(F32)
        r = _rstd(y)
        z = do * g_ref[...]
        return (r * z - y * (r * r * r) * jnp.mean(y * z, axis=-1, keepdims=True),
                jnp.sum(do * y * r, axis=0, keepdims=True))

    def body(*refs):
        y_ref, g_ref, do_ref = refs[:3]
        refs = refs[3:]
        if with_res:
            dr_ref, refs = refs[0], refs[1:]
        if then:
            y2_ref, g2_ref, refs = refs[0], refs[1], refs[2:]
        din_ref, dg_ref = refs[:2]
        i = pl.program_id(0)
        din, part = norm_vjp(y_ref, g_ref, do_ref[...].astype(F32))
        if with_res:
            din = din + dr_ref[...]
        din_ref[...] = din.astype(din_ref.dtype)
        parts = [(dg_ref, part)]
        if then:
            d2, part2 = norm_vjp(y2_ref, g2_ref, din)
            refs[2][...] = d2.astype(BF16)
            parts.append((refs[3], part2))

        @pl.when(i == 0)
        def _():
            for ref, val in parts:
                ref[...] = val

        @pl.when(i > 0)
        def _():
            for ref, val in parts:
                ref[...] += val

    tile = pl.BlockSpec((tm, D), lambda i: (i, 0))
    gain_out = pl.BlockSpec((1, D), lambda i: (0, 0))
    args = (yin, g, dout) + ((dres,) if with_res else ()) + ((then[0], then[1]) if then else ())
    return pl.pallas_call(
        body, name=name, grid=(L // tm,),
        in_specs=[tile, _gain_spec(l), tile] + ([tile] if with_res else []) + ([tile, _gain_spec(then[2])] if then else []),
        out_specs=[tile, gain_out] + ([tile, gain_out] if then else []),
        out_shape=[jax.ShapeDtypeStruct((L, D), F32 if with_res else BF16), jax.ShapeDtypeStruct((1, D), F32)]
        + ([jax.ShapeDtypeStruct((L, D), BF16), jax.ShapeDtypeStruct((1, D), F32)] if then else []),
        compiler_params=_params(("arbitrary",)),
    )(*args)


GLA_SUB = 256


def _gla_consts(fwd, tb):
    sub = min(GLA_SUB, tb)
    row = lax.broadcasted_iota(jnp.int32, (sub, sub), 0)
    col = lax.broadcasted_iota(jnp.int32, (sub, sub), 1)
    same = (row // CH) == (col // CH)
    tri = same & ((col <= row) if fwd else (col >= row))
    tri_t = same & ((col >= row) if fwd else (col <= row))
    row_st = lax.broadcasted_iota(jnp.int32, (HEADS * CH, CH), 0) & (CH - 1)
    col_st = lax.broadcasted_iota(jnp.int32, (HEADS * CH, CH), 1)
    tri_st = (col_st <= row_st) if fwd else (col_st >= row_st)
    lane_head = lax.broadcasted_iota(jnp.int32, (1, DK), 1) // HK
    head_masks = [lane_head == h for h in range(HEADS)]
    srow = lax.broadcasted_iota(jnp.int32, (DG, DK), 0) // HV
    scol = lax.broadcasted_iota(jnp.int32, (DG, DK), 1) // HK
    return tri.astype(BF16), tri_t.astype(BF16), tri_st, head_masks, srow == scol


def _dot_hilo(tri_b, x):
    hi = x.astype(BF16)
    lo = (x - hi.astype(F32)).astype(BF16)
    sub = tri_b.shape[0]
    return jnp.concatenate([_dot(tri_b, hi[r:r + sub], NN) + _dot(tri_b, lo[r:r + sub], NN)
                            for r in range(0, x.shape[0], sub)], axis=0)


def _gla_block_terms(q_ref, k_ref, lr_ref, gp_ref, bias_ref, tri_b):
    pre = _dot(lr_ref[...], gp_ref[...], NN) + bias_ref[...]
    sig_neg = 1.0 / (1.0 + jnp.exp(pre))
    a = (jnp.minimum(pre, 0.0) - jnp.log(1.0 + jnp.exp(-jnp.abs(pre)))) * (1.0 / 16.0)
    cum = _dot_hilo(tri_b, a)
    cl = jnp.concatenate([jnp.broadcast_to(jnp.min(cum[r:r + CH], axis=0, keepdims=True), (CH, DK))
                          for r in range(0, cum.shape[0], CH)], axis=0)
    e = jnp.exp(cum)
    einv = jnp.exp(-cum)
    eout = jnp.exp(cl - cum)
    q_in = q_ref[...].astype(F32) * e * (HK ** -0.5)
    k = k_ref[...].astype(F32)
    return dict(sig_neg=sig_neg, e=e, einv=einv, eout=eout, decay=jnp.exp(cl), q_in=q_in, k_in=k * einv,
                k_out=k * eout)


def _gla_chunk(t, c, head_masks):
    sl = slice(c * CH, (c + 1) * CH)
    tc = {n: x[sl] for n, x in t.items() if n != "decay"}
    tc["decay"] = jnp.max(t["decay"][c * CH:c * CH + 8], axis=0, keepdims=True)
    tc["q_st"] = jnp.concatenate([jnp.where(mh, tc["q_in"], 0.0) for mh in head_masks], axis=0).astype(BF16)
    return tc


def _gate_specs(l):
    return [pl.BlockSpec((None, LR_BLK, DK), lambda i: (l, 0, 0)), pl.BlockSpec((None, 1, DK), lambda i: (l, 0, 0))]


def _gla_fwd(p, gpad, bias, l, o_prev, fwd, name):
    L = p.shape[0]
    tb = min(TILE_GLA, L)
    nb, ncb, nch = L // tb, tb // CH, L // CH
    blk = (lambda i: i) if fwd else (lambda i: nb - 1 - i)
    with_prev = o_prev is not None

    def body(*refs):
        if with_prev:
            q_ref, k_ref, v_ref, lr_ref, gp_ref, bias_ref, op_ref, o_ref, sp_ref, s_ref = refs
        else:
            q_ref, k_ref, v_ref, lr_ref, gp_ref, bias_ref, o_ref, sp_ref, s_ref = refs
        i = pl.program_id(0)

        @pl.when(i == 0)
        def _():
            s_ref[...] = jnp.zeros_like(s_ref)

        tri_b, _, tri_st, head_masks, blockmask = _gla_consts(fwd, tb)
        terms = _gla_block_terms(q_ref, k_ref, lr_ref, gp_ref, bias_ref, tri_b)
        for c in (range(ncb) if fwd else reversed(range(ncb))):
            rows = pl.ds(c * CH, CH)
            t = _gla_chunk(terms, c, head_masks)
            v = v_ref[rows, :]
            scores = _dot(t["q_st"], t["k_in"].astype(BF16), NT)
            a_st = jnp.where(tri_st, scores, 0.0).astype(BF16)
            r = _dot(a_st, v, NN)
            o_intra = jnp.concatenate([r[h * CH:(h + 1) * CH, h * HV:(h + 1) * HV] for h in range(HEADS)], axis=1)
            s_b = s_ref[...].astype(BF16)
            sp_ref[c] = s_b
            o = o_intra + _dot(t["q_in"].astype(BF16), s_b, NT)
            if with_prev:
                o = o + op_ref[rows, :]
            o_ref[rows, :] = o
            kv_t = _dot(v, t["k_out"].astype(BF16), TN)
            s_ref[...] = s_ref[...] * t["decay"] + jnp.where(blockmask, kv_t, 0.0)

    def col(width, c):
        return pl.BlockSpec((tb, width), lambda i: (blk(i), c))

    in_specs = [col(DK, COL_Q), col(DK, COL_K), col(DG, COL_V), col(LR_BLK, COL_LR)] + _gate_specs(l)
    args = [p, p, p, p, gpad, bias]
    if with_prev:
        in_specs.append(pl.BlockSpec((tb, DG), lambda i: (blk(i), 0)))
        args.append(o_prev)
    return pl.pallas_call(
        body, name=name, grid=(nb,), in_specs=in_specs,
        out_specs=[pl.BlockSpec((tb, DG), lambda i: (blk(i), 0)), pl.BlockSpec((ncb, DG, DK), lambda i: (blk(i), 0, 0))],
        out_shape=[jax.ShapeDtypeStruct((L, DG), F32), jax.ShapeDtypeStruct((nch, DG, DK), BF16)],
        scratch_shapes=[pltpu.VMEM((DG, DK), F32)],
        compiler_params=_params(("arbitrary",)),
    )(*args)


P_COLS = dict(gb=(0, DC), gc=(DC, DC), gv=(2 * DC, DC), q=(3 * DC, DK), k=(3 * DC + DK, DK), v=(3 * DC + 2 * DK, DG),
              go=(3 * DC + 2 * DK + DG, DG), lr=(3 * DC + 2 * DK + 2 * DG, LR_BLK))


def _gla_bwd(p, gpad, bias, l, sprev, d_o, prev, fwd, name):
    L = p.shape[0]
    tb = min(TILE_GLA, L)
    nb, ncb = L // tb, tb // CH
    blk = (lambda i: nb - 1 - i) if fwd else (lambda i: i)
    with_prev = prev is not None

    def body(*refs):
        q_ref, k_ref, v_ref, lr_ref, gp_ref, bias_ref, sp_ref, do_ref = refs[:8]
        rest = refs[8:]
        if with_prev:
            pq_ref, pk_ref, pv_ref, plr_ref, dgb_ref, dgc_ref, dgv_ref, dgo_ref = rest[:8]
            dp_ref, dg_ref, db_ref, ds_ref = rest[8:]

            def put(what, rows, val):
                c0, width = P_COLS[what]
                dp_ref[rows, c0:c0 + width] = val

            for what, ref in (("gb", dgb_ref), ("gc", dgc_ref), ("gv", dgv_ref), ("go", dgo_ref)):
                put(what, slice(None), ref[...])
        else:
            dq_ref, dk_ref, dv_ref, dlr_ref, dg_ref, db_ref, ds_ref = rest
            out_of = dict(q=dq_ref, k=dk_ref, v=dv_ref, lr=dlr_ref)

            def put(what, rows, val):
                out_of[what][rows, :] = val

        i = pl.program_id(0)

        @pl.when(i == 0)
        def _():
            ds_ref[...] = jnp.zeros_like(ds_ref)
            dg_ref[...] = jnp.zeros_like(dg_ref)
            db_ref[...] = jnp.zeros_like(db_ref)

        tri_b, tri_t_b, tri_st, head_masks, blockmask = _gla_consts(fwd, tb)
        terms = _gla_block_terms(q_ref, k_ref, lr_ref, gp_ref, bias_ref, tri_b)
        dcum_of, dcl_of = [None] * ncb, [None] * ncb
        for c in (reversed(range(ncb)) if fwd else range(ncb)):
            rows = pl.ds(c * CH, CH)
            t = _gla_chunk(terms, c, head_masks)
            v = v_ref[rows, :]
            do = do_ref[rows, :]
            q_in, k_in, k_out = t["q_in"], t["k_in"], t["k_out"]
            q_b, k_in_b, k_out_b = q_in.astype(BF16), k_in.astype(BF16), k_out.astype(BF16)
            scores = _dot(t["q_st"], k_in_b, NT)
            a_st = jnp.where(tri_st, scores, 0.0).astype(BF16)
            s_prev = sp_ref[c]
            ds = ds_ref[...]
            ds_b = ds.astype(BF16)

            da_heads = [_dot(do[:, h * HV:(h + 1) * HV], v[:, h * HV:(h + 1) * HV], NT) for h in range(HEADS)]
            da_st = jnp.where(tri_st, jnp.concatenate(da_heads, axis=0), 0.0).astype(BF16)

            dv_heads = [_dot(a_st[h * CH:(h + 1) * CH, :], do[:, h * HV:(h + 1) * HV], TN) for h in range(HEADS)]
            dv = jnp.concatenate(dv_heads, axis=1) + _dot(k_out_b, ds_b, NT)

            x = _dot(da_st, k_in_b, NN)
            dq_in = _dot(do, s_prev, NN)
            for h in range(HEADS):
                dq_in = dq_in + jnp.where(head_masks[h], x[h * CH:(h + 1) * CH, :], 0.0)
            dk_in = _dot(da_st, t["q_st"], TN)
            dk_out = _dot(v, ds_b, NN)
            d_decay = jnp.sum(ds * s_prev.astype(F32), axis=0, keepdims=True)
            ds_ref[...] = ds * t["decay"] + jnp.where(blockmask, _dot(do, q_b, TN), 0.0)

            dq = dq_in * t["e"] * (HK ** -0.5)
            dk = dk_in * t["einv"] + dk_out * t["eout"]
            dko_ko = dk_out * k_out
            dcum_of[c] = dq_in * q_in - dk_in * k_in - dko_ko
            dcl = jnp.sum(dko_ko, axis=0, keepdims=True) + d_decay * t["decay"]
            dcl_of[c] = jnp.broadcast_to(dcl, (CH, DK))
            if with_prev:
                dq = dq + pq_ref[rows, :].astype(F32)
                dk = dk + pk_ref[rows, :].astype(F32)
                dv = dv + pv_ref[rows, :].astype(F32)
            put("q", rows, dq.astype(BF16))
            put("k", rows, dk.astype(BF16))
            put("v", rows, dv.astype(BF16))

        da = _dot_hilo(tri_t_b, jnp.concatenate(dcum_of, axis=0)) + jnp.concatenate(dcl_of, axis=0)
        dpre = da * terms["sig_neg"] * (1.0 / 16.0)
        dpre_b = dpre.astype(BF16)
        dlr = _dot(dpre_b, gp_ref[...], NT)
        dg_ref[...] += _dot(lr_ref[...], dpre_b, TN)
        db_ref[...] += jnp.sum(dpre, axis=0, keepdims=True)
        if with_prev:
            dlr = dlr + plr_ref[...].astype(F32)
        put("lr", slice(None), dlr.astype(BF16))

    def col(width, c):
        return pl.BlockSpec((tb, width), lambda i: (blk(i), c))

    in_specs = [col(DK, COL_Q), col(DK, COL_K), col(DG, COL_V), col(LR_BLK, COL_LR)] + _gate_specs(l) + [
        pl.BlockSpec((ncb, DG, DK), lambda i: (blk(i), 0, 0)), col(DG, 0)]
    args = [p, p, p, p, gpad, bias, sprev, d_o]
    tiles = [col(DK, 0), col(DK, 0), col(DG, 0), col(LR_BLK, 0)]
    shapes = [jax.ShapeDtypeStruct((L, DK), BF16), jax.ShapeDtypeStruct((L, DK), BF16),
              jax.ShapeDtypeStruct((L, DG), BF16), jax.ShapeDtypeStruct((L, LR_BLK), BF16)]
    if with_prev:
        in_specs += tiles + [col(DC, 0)] * 4
        args += list(prev)
        tiles, shapes = [col(D_INP, 0)], [jax.ShapeDtypeStruct((L, D_INP), BF16)]
    return pl.pallas_call(
        body, name=name, grid=(nb,), in_specs=in_specs,
        out_specs=tiles + [pl.BlockSpec((LR_BLK, DK), lambda i: (0, 0)), pl.BlockSpec((1, DK), lambda i: (0, 0))],
        out_shape=shapes + [jax.ShapeDtypeStruct((LR_BLK, DK), F32), jax.ShapeDtypeStruct((1, DK), F32)],
        scratch_shapes=[pltpu.VMEM((DG, DK), F32)],
        compiler_params=_params(("arbitrary",)),
    )(*args)


def _mixer_out(p, conv_a, gh, l, o_tot, name):
    L = p.shape[0]
    tm = min(TILE_TOKENS, L)
    n = L // tm
    pmap, nmap = _halo_maps(tm, L)

    def body(gb_ref, gc_ref, gcp_ref, gcn_ref, gv_ref, gvp_ref, gvn_ref, go_ref, cw_ref, o_ref, gh_ref, y_ref):
        ps, ns = _edge_scales(pl.program_id(0), n)
        z = gc_ref[...].astype(F32) * gv_ref[...].astype(F32)
        zp = gcp_ref[...].astype(F32) * gvp_ref[...].astype(F32) * ps
        zn = gcn_ref[...].astype(F32) * gvn_ref[...].astype(F32) * ns
        z_dn, z_up = _shift(z, zp, zn)
        conv = cw_ref[0:1, :] * z_dn + cw_ref[1:2, :] * z + cw_ref[2:3, :] * z_up
        y_ref[:, 0:DC] = (gb_ref[...].astype(F32) * conv).astype(BF16)
        o = o_ref[...]
        go = go_ref[...].astype(F32)
        for h in range(HEADS):
            oh = o[:, h * HV:(h + 1) * HV]
            on = oh * _rstd(oh) * gh_ref[...]
            act, _ = _silu_parts(go[:, h * HV:(h + 1) * HV])
            y_ref[:, DC + h * HV:DC + (h + 1) * HV] = (act * on).astype(BF16)

    def main(c):
        return pl.BlockSpec((tm, DC), lambda i: (i, c))

    def halo(c, imap):
        return pl.BlockSpec((HB, DC), lambda i: (imap(i), c))

    return pl.pallas_call(
        body, name=name, grid=(n,),
        in_specs=[main(COL_GB), main(COL_GC), halo(COL_GC, pmap), halo(COL_GC, nmap),
                  main(COL_GV), halo(COL_GV, pmap), halo(COL_GV, nmap), main(COL_GO),
                  pl.BlockSpec((None, 3, DC), lambda i: (l, 0, 0)), pl.BlockSpec((tm, DG), lambda i: (i, 0)),
                  _gain_spec(l, HV)],
        out_specs=pl.BlockSpec((tm, D), lambda i: (i, 0)), out_shape=jax.ShapeDtypeStruct((L, D), BF16),
        compiler_params=_params(("parallel",)),
    )(p, p, p, p, p, p, p, p, conv_a, o_tot, gh)


def _mixer_out_bwd(p, conv_a, gh, l, o_tot, dy, name):
    L = p.shape[0]
    tm = min(TILE_TOKENS, L)
    n = L // tm
    pmap, nmap = _halo_maps(tm, L)

    def body(gb_ref, gbp_ref, gbn_ref, gc_ref, gcp_ref, gcn_ref, gv_ref, gvp_ref, gvn_ref, go_ref, cw_ref, o_ref,
             gh_ref, dy_ref, dyp_ref, dyn_ref, dgb_ref, dgc_ref, dgv_ref, dgo_ref, do_ref, dcw_ref, dgh_ref):
        i = pl.program_id(0)
        ps, ns = _edge_scales(i, n)
        gb = gb_ref[...].astype(F32)
        gc = gc_ref[...].astype(F32)
        gv = gv_ref[...].astype(F32)
        z = gc * gv
        zp = gcp_ref[...].astype(F32) * gvp_ref[...].astype(F32) * ps
        zn = gcn_ref[...].astype(F32) * gvn_ref[...].astype(F32) * ns
        z_dn, z_up = _shift(z, zp, zn)
        w0, w1, w2 = cw_ref[0:1, :], cw_ref[1:2, :], cw_ref[2:3, :]
        conv = w0 * z_dn + w1 * z + w2 * z_up
        dya = dy_ref[:, 0:DC].astype(F32)
        dgb_ref[...] = (dya * conv).astype(BF16)
        dc = dya * gb
        dcp = dyp_ref[...].astype(F32) * gbp_ref[...].astype(F32) * ps
        dcn = dyn_ref[...].astype(F32) * gbn_ref[...].astype(F32) * ns
        dc_dn, dc_up = _shift(dc, dcp, dcn)
        dz = w0 * dc_up + w1 * dc + w2 * dc_dn
        dgc_ref[...] = (dz * gv).astype(BF16)
        dgv_ref[...] = (dz * gc).astype(BF16)
        dcw = [jnp.sum(zs * dc, axis=0, keepdims=True) for zs in (z_dn, z, z_up)]

        o = o_ref[...]
        go = go_ref[...].astype(F32)
        dgh = jnp.zeros((1, HV), F32)
        for h in range(HEADS):
            sl = slice(h * HV, (h + 1) * HV)
            oh = o[:, sl]
            r = _rstd(oh)
            act, sg = _silu_parts(go[:, sl])
            dyb = dy_ref[:, DC + h * HV:DC + (h + 1) * HV].astype(F32)
            on = oh * r * gh_ref[...]
            dgo_ref[:, sl] = (dyb * on * (sg + act * (1.0 - sg))).astype(BF16)
            don = dyb * act
            zz = don * gh_ref[...]
            do_ref[:, sl] = (r * zz - oh * (r * r * r) * jnp.mean(oh * zz, axis=-1, keepdims=True)).astype(BF16)
            dgh = dgh + jnp.sum(don * oh * r, axis=0, keepdims=True)

        @pl.when(i == 0)
        def _():
            dcw_ref[...] = jnp.zeros_like(dcw_ref)
            dgh_ref[...] = jnp.zeros_like(dgh_ref)

        for kk in range(3):
            dcw_ref[kk:kk + 1, :] += dcw[kk]
        dgh_ref[...] += dgh

    def main(c):
        return pl.BlockSpec((tm, DC), lambda i: (i, c))

    def halo(c, imap):
        return pl.BlockSpec((HB, DC), lambda i: (imap(i), c))

    tile = pl.BlockSpec((tm, DC), lambda i: (i, 0))
    return pl.pallas_call(
        body, name=name, grid=(n,),
        in_specs=[main(COL_GB), halo(COL_GB, pmap), halo(COL_GB, nmap), main(COL_GC), halo(COL_GC, pmap),
                  halo(COL_GC, nmap), main(COL_GV), halo(COL_GV, pmap), halo(COL_GV, nmap), main(COL_GO),
                  pl.BlockSpec((None, 3, DC), lambda i: (l, 0, 0)), tile, _gain_spec(l, HV),
                  pl.BlockSpec((tm, D), lambda i: (i, 0)), halo(0, pmap), halo(0, nmap)],
        out_specs=[tile, tile, tile, tile, tile, pl.BlockSpec((3, DC), lambda i: (0, 0)),
                   pl.BlockSpec((1, HV), lambda i: (0, 0))],
        out_shape=[jax.ShapeDtypeStruct((L, DC), BF16)] * 5
        + [jax.ShapeDtypeStruct((3, DC), F32), jax.ShapeDtypeStruct((1, HV), F32)],
        compiler_params=_params(("arbitrary",)),
    )(p, p, p, p, p, p, p, p, p, p, conv_a, o_tot, gh, dy, dy, dy)


def _ffn_specs(tm, L, l, row_axis, sh_axis):
    pmap, nmap = _halo_maps(tm, L)

    def u(off, imap=None, rows=tm):
        if imap is None:
            return pl.BlockSpec((None, rows, SH_FF), lambda *g: (g[sh_axis] + off, g[row_axis], 0))
        return pl.BlockSpec((None, rows, SH_FF), lambda *g: (g[sh_axis] + off, imap(g[row_axis]), 0))

    def cw(off):
        return pl.BlockSpec((None, None, 3, SH_FF), lambda *g: (l, g[sh_axis] + off, 0, 0))

    u_specs = [u(0), u(0, pmap, HB), u(0, nmap, HB), u(FF_HALF), u(FF_HALF, pmap, HB), u(FF_HALF, nmap, HB)]
    return u_specs, [cw(0), cw(FF_HALF)]


def _conv3(x_ref, xp_ref, xn_ref, cw_ref, ps, ns):
    x = x_ref[...].astype(F32)
    x_dn, x_up = _shift(x, xp_ref[...].astype(F32) * ps, xn_ref[...].astype(F32) * ns)
    return cw_ref[0:1, :] * x_dn + cw_ref[1:2, :] * x + cw_ref[2:3, :] * x_up, (x_dn, x, x_up)


def _carry(body, *, name, grid, args, in_specs, out_specs, out_shape, scratch, comm):
    n_in, n_out = len(args), len(out_shape)
    n_ci = len(comm.ins) if comm else 0
    n_co = len(comm.out_shapes) if comm else 0

    def wrapped(*refs):
        ins, refs = refs[:n_in], refs[n_in:]
        ci, refs = refs[:n_ci], refs[n_ci:]
        outs, refs = refs[:n_out], refs[n_out:]
        co, refs = refs[:n_co], refs[n_co:]
        if comm:
            sems, refs = refs[:3], refs[3:]
            step = 0
            for ax, size in enumerate(grid):
                step = step * size + pl.program_id(ax)

            @pl.when(step == 0)
            def _():
                comm.start(ci, co, *sems)

        body(ins, outs, refs)
        if comm:
            @pl.when(step == math.prod(grid) - 1)
            def _():
                comm.wait(ci, co, *sems)

    return pl.pallas_call(
        wrapped, name=name, grid=grid, in_specs=list(in_specs) + [HBM_SPEC] * n_ci,
        out_specs=list(out_specs) + [HBM_SPEC] * n_co,
        out_shape=list(out_shape) + list(comm.out_shapes if comm else ()),
        scratch_shapes=([pltpu.SemaphoreType.DMA((s,)) for s in comm.sems] if comm else []) + list(scratch),
        input_output_aliases={n_in + i: n_out + o for i, o in comm.aliases.items()} if comm else {},
        compiler_params=_params(("arbitrary",) * len(grid)),
    )(*args, *(comm.ins if comm else ()))


def _ffn_act(u8, cw, l, name):
    L = u8.shape[1]
    tm = min(TILE_FFN, L)
    n = L // tm
    u_specs, cw_specs = _ffn_specs(tm, L, l, 0, 1)

    def body(ins, outs, scratch):
        g_ref, gp_ref, gn_ref, v_ref, vp_ref, vn_ref, cwg_ref, cwv_ref = ins
        ps, ns = _edge_scales(pl.program_id(0), n)
        gate, _ = _conv3(g_ref, gp_ref, gn_ref, cwg_ref, ps, ns)
        val, _ = _conv3(v_ref, vp_ref, vn_ref, cwv_ref, ps, ns)
        act, _ = _silu_parts(gate)
        outs[0][...] = (act * val).astype(BF16)
        outs[1][0] = gate.astype(BF16)
        outs[1][1] = val.astype(BF16)

    pair = pl.BlockSpec((2, None, tm, SH_FF), lambda i, d: (0, d, i, 0))
    return _carry(
        body, name=name, grid=(n, FF_HALF), args=(u8, u8, u8, u8, u8, u8, cw, cw), in_specs=u_specs + cw_specs,
        out_specs=[pl.BlockSpec((None, tm, SH_FF), lambda i, d: (d, i, 0)), pair],
        out_shape=[jax.ShapeDtypeStruct((FF_HALF, L, SH_FF), BF16),
                   jax.ShapeDtypeStruct((2, FF_HALF, L, SH_FF), BF16)], scratch=[], comm=None)


def _ffn_act_bwd(conv, da, name):
    L = conv.shape[2]
    tm = min(TILE_FFN, L)

    def body(ins, outs, scratch):
        c_ref, da_ref = ins
        act, sg = _silu_parts(c_ref[0].astype(F32))
        da_f = da_ref[...].astype(F32)
        outs[0][0] = (da_f * c_ref[1].astype(F32) * (sg + act * (1.0 - sg))).astype(BF16)
        outs[0][1] = (da_f * act).astype(BF16)

    pair = pl.BlockSpec((2, None, tm, SH_FF), lambda d, i: (0, d, i, 0))
    return _carry(
        body, name=name, grid=(FF_HALF, L // tm), args=(conv, da),
        in_specs=[pair, pl.BlockSpec((None, tm, SH_FF), lambda d, i: (d, i, 0))], out_specs=[pair],
        out_shape=[jax.ShapeDtypeStruct((2, FF_HALF, L, SH_FF), BF16)], scratch=[], comm=None)[0]


def _ffn_conv_t(du8, u8, cw, l, name):
    L = du8.shape[1]
    tm = min(TILE_FFN, L)
    n = L // tm
    pmap, nmap = _halo_maps(tm, L)

    def body(ins, outs, scratch):
        x_ref, xp_ref, xn_ref, u_ref, cw_ref = ins
        d_u_ref, dcw_ref = outs
        i = pl.program_id(1)
        ps, ns = _edge_scales(i, n)
        x = x_ref[...].astype(F32)
        x_dn, x_up = _shift(x, xp_ref[...].astype(F32) * ps, xn_ref[...].astype(F32) * ns)
        d_u_ref[...] = (cw_ref[0:1, :] * x_up + cw_ref[1:2, :] * x + cw_ref[2:3, :] * x_dn).astype(BF16)

        @pl.when(i == 0)
        def _():
            dcw_ref[...] = jnp.zeros_like(dcw_ref)

        u = u_ref[...].astype(F32)
        for kk, xs in enumerate((x_up, x, x_dn)):
            dcw_ref[kk:kk + 1, :] += jnp.sum(u * xs, axis=0, keepdims=True)

    tile = pl.BlockSpec((None, tm, SH_FF), lambda d, i: (d, i, 0))
    return _carry(
        body, name=name, grid=(N_DEV, n), args=(du8, du8, du8, u8, cw),
        in_specs=[tile, pl.BlockSpec((None, HB, SH_FF), lambda d, i: (d, pmap(i), 0)),
                  pl.BlockSpec((None, HB, SH_FF), lambda d, i: (d, nmap(i), 0)), tile,
                  pl.BlockSpec((None, None, 3, SH_FF), lambda d, i: (l, d, 0, 0))],
        out_specs=[tile, pl.BlockSpec((None, 3, SH_FF), lambda d, i: (d, 0, 0))],
        out_shape=[jax.ShapeDtypeStruct((N_DEV, L, SH_FF), BF16), jax.ShapeDtypeStruct((N_DEV, 3, SH_FF), F32)],
        scratch=[], comm=None)


def _loss_grad(xl, target, name):
    L = xl.shape[0]
    tm = min(TILE_NORM, L)

    def body(x_ref, t_ref, dx_ref, sq_ref):
        i = pl.program_id(0)
        err = x_ref[...] - t_ref[...]
        dx_ref[...] = err * (1.0 / D)
        part = jnp.sum(err * err, axis=0, keepdims=True)

        @pl.when(i == 0)
        def _():
            sq_ref[...] = part

        @pl.when(i > 0)
        def _():
            sq_ref[...] += part

    tile = pl.BlockSpec((tm, D), lambda i: (i, 0))
    return pl.pallas_call(
        body, name=name, grid=(L // tm,), in_specs=[tile, tile],
        out_specs=[tile, pl.BlockSpec((1, D), lambda i: (0, 0))],
        out_shape=[jax.ShapeDtypeStruct((L, D), F32), jax.ShapeDtypeStruct((1, D), F32)],
        compiler_params=_params(("arbitrary",)),
    )(xl, target)


MESH = pl.DeviceIdType.MESH
HBM_SPEC = pl.BlockSpec(memory_space=pltpu.HBM)


def _position():
    return lax.axis_index("x"), lax.axis_index("y"), lax.axis_index("c")


def _other_chips(x, y):
    return [(1 - x, y), (x, 1 - y), (1 - x, 1 - y)]


def _all_gather(shards, name):
    n = len(shards)

    def body(*refs):
        x_refs, out_refs = refs[:n], refs[n:2 * n]
        send_sems, recv_sems, local_sems = refs[2 * n:]
        x, y, c = _position()
        me, sibling = (x, y, c), (x, y, 1 - c)
        chips = _other_chips(x, y)

        def slot(t, px, py, pc):
            return out_refs[t].at[:, 4 * px + 2 * py + pc]

        def copy(t, k, block, to, from_input=False):
            return pltpu.make_async_remote_copy(
                src_ref=x_refs[t] if from_input else slot(t, *block), dst_ref=slot(t, *block),
                send_sem=send_sems.at[k * n + t], recv_sem=recv_sems.at[k * n + t], device_id=to, device_id_type=MESH)

        mine = [pltpu.make_async_copy(x_refs[t], slot(t, *me), local_sems.at[t]) for t in range(n)]
        for cp in mine:
            cp.start()
        first = [copy(t, 0, me, sibling, True) for t in range(n)]
        first += [copy(t, 1 + j, me, (*chip, c), True) for j, chip in enumerate(chips) for t in range(n)]
        for cp in first:
            cp.start()
        passed = []
        for j, chip in enumerate(chips):
            for t in range(n):
                copy(t, 1 + j, (*chip, c), me).wait_recv()
                passed.append(copy(t, 4 + j, (*chip, c), sibling))
                passed[-1].start()
        for t in range(n):
            copy(t, 0, sibling, me).wait_recv()
        for j, chip in enumerate(chips):
            for t in range(n):
                copy(t, 4 + j, (*chip, 1 - c), me).wait_recv()
        for cp in first + passed:
            cp.wait_send()
        for cp in mine:
            cp.wait()

    return pl.pallas_call(
        body, name=name,
        out_shape=[jax.ShapeDtypeStruct((s.shape[0], N_DEV) + s.shape[1:], s.dtype) for s in shards],
        in_specs=[HBM_SPEC] * n, out_specs=[HBM_SPEC] * n,
        scratch_shapes=[pltpu.SemaphoreType.DMA((7 * n,)), pltpu.SemaphoreType.DMA((7 * n,)),
                        pltpu.SemaphoreType.DMA((n,))],
    )(*shards)


def _exchange_sibling(grads, name):
    n = len(grads)

    def body(*refs):
        g_refs, out_refs, send_sems, recv_sems = refs[:n], refs[n:2 * n], refs[2 * n], refs[2 * n + 1]
        x, y, c = _position()
        copies = [pltpu.make_async_remote_copy(
            src_ref=g_refs[t].at[:, 2 * k + (1 - c)], dst_ref=out_refs[t].at[:, k], send_sem=send_sems.at[k * n + t],
            recv_sem=recv_sems.at[k * n + t], device_id=(x, y, 1 - c), device_id_type=MESH)
            for k in range(N_CHIP) for t in range(n)]
        for cp in copies:
            cp.start()
        for cp in copies:
            cp.wait()

    return pl.pallas_call(
        body, name=name,
        out_shape=[jax.ShapeDtypeStruct((g.shape[0], N_CHIP) + g.shape[2:], g.dtype) for g in grads],
        in_specs=[HBM_SPEC] * n, out_specs=[HBM_SPEC] * n,
        scratch_shapes=[pltpu.SemaphoreType.DMA((N_CHIP * n,)), pltpu.SemaphoreType.DMA((N_CHIP * n,))],
    )(*grads)


class _Comm(NamedTuple):
    ins: tuple
    out_shapes: tuple
    aliases: dict
    sems: tuple
    start: Callable
    wait: Callable


def _comm_of(ins, out_shapes, aliases, sems, copies):
    def start(ci, co, send, recv, local):
        for cp in copies(ci, co, send, recv, local):
            cp.start()

    def wait(ci, co, send, recv, local):
        for cp in copies(ci, co, send, recv, local):
            cp.wait()

    return _Comm(tuple(ins), tuple(out_shapes), aliases, sems, start, wait)


def _remote(src, dst, send, recv, idx, to):
    return pltpu.make_async_remote_copy(src_ref=src, dst_ref=dst, send_sem=send.at[idx], recv_sem=recv.at[idx],
                                        device_id=to, device_id_type=MESH)


def _gather_ici_comm(shards):
    n = len(shards)

    def copies(ci, co, send, recv, local):
        x, y, c = _position()
        me = 4 * x + 2 * y + c
        mine = [pltpu.make_async_copy(ci[t], co[t].at[:, me], local.at[t]) for t in range(n)]
        return mine + [_remote(ci[t], co[t].at[:, me], send, recv, j * n + t, (cx, cy, c))
                       for j, (cx, cy) in enumerate(_other_chips(x, y)) for t in range(n)]

    outs = [jax.ShapeDtypeStruct((1, N_DEV) + s.shape[1:], s.dtype) for s in shards]
    return _comm_of(shards, outs, {}, (3 * n, 3 * n, n), copies)


def _gather_d2d_comm(partials):
    n = len(partials)

    def copies(ci, co, send, recv, local):
        x, y, c = _position()
        return [_remote(co[t].at[:, 4 * cx + 2 * cy + c], co[t].at[:, 4 * cx + 2 * cy + c], send, recv, k * n + t,
                        (x, y, 1 - c))
                for k, (cx, cy) in enumerate([(x, y)] + _other_chips(x, y)) for t in range(n)]

    outs = [jax.ShapeDtypeStruct(p.shape, p.dtype) for p in partials]
    return _comm_of(partials, outs, {t: t for t in range(n)}, (N_CHIP * n, N_CHIP * n, 1), copies)


def _grads_d2d_comm(grads):
    n = len(grads)

    def copies(ci, co, send, recv, local):
        x, y, c = _position()
        return [_remote(ci[t].at[:, 2 * k + (1 - c)], co[t].at[:, k], send, recv, k * n + t, (x, y, 1 - c))
                for k in range(N_CHIP) for t in range(n)]

    outs = [jax.ShapeDtypeStruct((g.shape[0], N_CHIP) + g.shape[2:], g.dtype) for g in grads]
    return _comm_of(grads, outs, {}, (N_CHIP * n, N_CHIP * n, 1), copies)


def _grads_ici_comm(parts):
    n = len(parts)

    def copies(ci, co, send, recv, local):
        x, y, c = _position()
        my_chip = 2 * x + y
        mine = [pltpu.make_async_copy(ci[t].at[:, my_chip], co[t].at[:, my_chip], local.at[t]) for t in range(n)]
        return mine + [_remote(ci[t].at[:, 2 * cx + cy], co[t].at[:, my_chip], send, recv, j * n + t, (cx, cy, c))
                       for j, (cx, cy) in enumerate(_other_chips(x, y)) for t in range(n)]

    outs = [jax.ShapeDtypeStruct(p.shape, p.dtype) for p in parts]
    return _comm_of(parts, outs, {}, (3 * n, 3 * n, n), copies)


def _row_tile(rows):
    fits = [t for t in range(HB, 257, HB) if rows % t == 0]
    return max(fits) if fits else rows


def _pair_sum(g, recv, c_idx, out_dtype, name):
    lay, _, rows, cols = g.shape
    tr = _row_tile(rows)

    def body(c_ref, g_ref, r_ref, o_ref):
        o_ref[...] = (g_ref[...] + r_ref[...]).astype(o_ref.dtype)

    def spec(blk_of):
        return pl.BlockSpec((None, None, tr, cols), lambda l, k, r, c_ref: (l, blk_of(k, c_ref), r, 0))

    return pl.pallas_call(
        body, name=name,
        grid_spec=pltpu.PrefetchScalarGridSpec(
            num_scalar_prefetch=1, grid=(lay, N_CHIP, rows // tr),
            in_specs=[spec(lambda k, c_ref: 2 * k + c_ref[0]), spec(lambda k, c_ref: k)],
            out_specs=spec(lambda k, c_ref: k)),
        out_shape=jax.ShapeDtypeStruct((lay, N_CHIP, rows, cols), out_dtype),
        compiler_params=_params(("parallel", "parallel", "parallel")),
    )(c_idx, g, recv)


def _exchange_chips(parts, name):
    n = len(parts)

    def body(*refs):
        p_refs, out_refs = refs[:n], refs[n:2 * n]
        send_sems, recv_sems, local_sems = refs[2 * n:]
        x, y, c = _position()
        my_chip = 2 * x + y
        mine = [pltpu.make_async_copy(p_refs[t].at[:, my_chip], out_refs[t].at[:, my_chip], local_sems.at[t])
                for t in range(n)]
        for cp in mine:
            cp.start()
        copies = [pltpu.make_async_remote_copy(
            src_ref=p_refs[t].at[:, 2 * cx + cy], dst_ref=out_refs[t].at[:, my_chip], send_sem=send_sems.at[j * n + t],
            recv_sem=recv_sems.at[j * n + t], device_id=(cx, cy, c), device_id_type=MESH)
            for j, (cx, cy) in enumerate(_other_chips(x, y)) for t in range(n)]
        for cp in copies:
            cp.start()
        for cp in copies:
            cp.wait()
        for cp in mine:
            cp.wait()

    return pl.pallas_call(
        body, name=name, out_shape=[jax.ShapeDtypeStruct(p.shape, p.dtype) for p in parts],
        in_specs=[HBM_SPEC] * n, out_specs=[HBM_SPEC] * n,
        scratch_shapes=[pltpu.SemaphoreType.DMA((3 * n,)), pltpu.SemaphoreType.DMA((3 * n,)),
                        pltpu.SemaphoreType.DMA((n,))],
    )(*parts)


def _sum_adamw(parts, w, m, v, name):
    lay, rows, cols = w.shape
    tr = _row_tile(rows)

    def body(p_ref, w_ref, m_ref, v_ref, g_ref, d_ref, nm_ref, nv_ref):
        g = ((p_ref[0].astype(F32) + p_ref[1].astype(F32)) + p_ref[2].astype(F32)) + p_ref[3].astype(F32)
        g_ref[...] = g
        nm = ADAM_B1 * m_ref[...] + (1.0 - ADAM_B1) * g
        nv = ADAM_B2 * v_ref[...] + (1.0 - ADAM_B2) * (g * g)
        nm_ref[...] = nm
        nv_ref[...] = nv
        m_hat = nm / (1.0 - ADAM_B1 ** ADAM_STEP)
        v_hat = nv / (1.0 - ADAM_B2 ** ADAM_STEP)
        d_ref[...] = -ADAM_LR * (m_hat / (jnp.sqrt(v_hat) + ADAM_EPS) + ADAM_WD * w_ref[...])

    tile = pl.BlockSpec((None, tr, cols), lambda l, r: (l, r, 0))
    return pl.pallas_call(
        body, name=name, grid=(lay, rows // tr),
        in_specs=[pl.BlockSpec((None, N_CHIP, tr, cols), lambda l, r: (l, 0, r, 0)), tile, tile, tile],
        out_specs=[tile] * 4, out_shape=[jax.ShapeDtypeStruct((lay, rows, cols), F32)] * 4,
        compiler_params=_params(("parallel", "parallel")),
    )(parts, w, m, v)


def _pad_rows(flat, rows):
    return jnp.pad(flat, (0, rows * LANES - flat.shape[0])).reshape(rows, LANES)


def _pack_small(tree):
    sh = jnp.concatenate([tree[n].reshape(-1) for n, _, _ in SMALL_SHARDED])
    rep = jnp.concatenate([tree[n].reshape(-1) for n, _ in REPLICATED])
    return jnp.concatenate([_pad_rows(sh, ROWS_SSH), _pad_rows(rep, ROWS_REP)], axis=0)


def _unpack_small(buf):
    out = {}
    for flat, items in ((buf[:ROWS_SSH].reshape(-1), [(n, s) for n, s, _ in SMALL_SHARDED]),
                        (buf[ROWS_SSH:].reshape(-1), REPLICATED)):
        off = 0
        for n, s in items:
            out[n] = flat[off:off + math.prod(s)].reshape(s)
            off += math.prod(s)
    return out


def _full_from_blocks(blocks, s, ax):
    return jnp.concatenate([blocks[d] for d in range(N_DEV)], axis=ax)


def _blocks_from_full(full, s, ax):
    return jnp.stack([lax.slice_in_dim(full, d * s[ax], (d + 1) * s[ax], axis=ax) for d in range(N_DEV)])


def _pack_small_grads(sharded_blocks, replicated):
    sh = jnp.concatenate([sharded_blocks[n].reshape(N_DEV, -1) for n, _, _ in SMALL_SHARDED], axis=1)
    sh = jnp.pad(sh, ((0, 0), (0, ROWS_SSH * LANES - sh.shape[1]))).reshape(N_DEV, ROWS_SSH, LANES)
    rep = _pad_rows(jnp.concatenate([replicated[n].reshape(-1) for n, _ in REPLICATED]), ROWS_REP)
    return jnp.concatenate([sh, jnp.broadcast_to(rep[None], (N_DEV, ROWS_REP, LANES))], axis=1)


def _layer_fwd(x, h1, wts, big, l, l_next, next_shards, own_ffn):
    L = x.shape[0]
    tm = min(TILE_MM, L)
    nt = L // tm
    nxt = dict(zip(BIG, next_shards)) if next_shards else {}
    if own_ffn:
        at_proj_in, at_ffn_up = {"own_up": own_ffn[0], "own_down": own_ffn[1]}, nxt
    else:
        at_proj_in = {n: nxt[n] for n in ("w_in", "w_out") if n in nxt}
        at_ffn_up = {n: nxt[n] for n in ("w_up", "w_down") if n in nxt}

    def carried(result, comm):
        return (result[0], list(result[1:])) if comm else (result, [])

    comm = _gather_ici_comm(list(at_proj_in.values())) if at_proj_in else None
    p, got = carried(_mm(h1, big["w_in"], dims=NT, grid=(nt, D_INP // 640, 1),
                         a_spec=((tm, D), lambda i, j, k: (i, 0)), b_spec=((None, 640, D), lambda i, j, k: (0, j, 0)),
                         o_spec=((tm, 640), lambda i, j, k: (i, j)), out_shape=jax.ShapeDtypeStruct((L, D_INP), BF16),
                         tile=(tm, 640), name="proj_in", comm=comm), comm)
    stage1 = dict(zip(at_proj_in, got))
    o_f, sp_f = _gla_fwd(p, wts["gpad_f"], wts["bias_f"], l, None, True, "gla_fwd_f")
    o_tot, sp_b = _gla_fwd(p, wts["gpad_b"], wts["bias_b"], l, o_f, False, "gla_fwd_b")
    y_cat = _mixer_out(p, wts["conv_a"], wts["gh"], l, o_tot, "mixer_out")
    comm = _gather_d2d_comm([stage1["own_up"], stage1["own_down"]]) if own_ffn else None
    y, got = carried(_mm(y_cat, big["w_out"], dims=NN, grid=(nt, 1, 1),
                         a_spec=((tm, D), lambda i, j, k: (i, 0)), b_spec=((None, D, D), lambda i, j, k: (0, 0, 0)),
                         o_spec=((tm, D), lambda i, j, k: (i, 0)), out_shape=jax.ShapeDtypeStruct((L, D), BF16),
                         tile=(tm, D), name="proj_out", comm=comm), comm)
    if own_ffn:
        big = dict(big, w_up=got[0], w_down=got[1].reshape(1, FF_HALF, SH_FF, D))
    x1, h2 = _post_pre(x, y, wts["g2"], l, wts["g3"], l, "post_pre_mix")
    comm = _gather_ici_comm(list(at_ffn_up.values())) if at_ffn_up else None
    u8, got = carried(_mm(h2, big["w_up"], dims=NT, grid=(nt, N_DEV, 1),
                          a_spec=((tm, D), lambda i, j, k: (i, 0)),
                          b_spec=((None, None, SH_FF, D), lambda i, j, k: (0, j, 0, 0)),
                          o_spec=((None, tm, SH_FF), lambda i, j, k: (j, i, 0)),
                          out_shape=jax.ShapeDtypeStruct((N_DEV, L, SH_FF), BF16), tile=(tm, SH_FF), name="ffn_up",
                          comm=comm), comm)
    stage1.update(zip(at_ffn_up, got))
    a, conv = _ffn_act(u8, wts["cw"], l, "ffn_act")
    tm1 = min(TILE_MM_KIN, L)
    comm = _gather_d2d_comm([stage1[n] for n in BIG]) if next_shards else None
    y2, gathered = carried(_mm(a, big["w_down"], dims=NN, grid=(L // tm1, 1, 1), kin=FF_HALF,
                               a_spec=((FF_HALF, tm1, SH_FF), lambda i, j, k: (0, i, 0)),
                               b_spec=((None, FF_HALF, SH_FF, D), lambda i, j, k: (0, 0, 0, 0)),
                               o_spec=((tm1, D), lambda i, j, k: (i, 0)),
                               out_shape=jax.ShapeDtypeStruct((L, D), BF16), tile=(tm1, D), name="ffn_down",
                               comm=comm), comm)
    x2, h1_next = _post_pre(x1, y2, wts["g4"], l, wts["g1"], l_next, "post_pre_ffn")
    saved = dict(x=x, h1=h1, p=p, o_tot=o_tot, sp_f=sp_f, sp_b=sp_b, y_cat=y_cat, y=y, x1=x1, h2=h2, u8=u8, a=a, y2=y2,
                 big=big, conv=conv)
    return x2, h1_next, saved, gathered


def _layer_bwd(dx2, wts, s, l, pending, c_idx, early_ffn, post, below_y2):
    L = dx2.shape[0]
    big = s["big"]
    tm = min(TILE_MM, L)
    nt = L // tm
    tm1 = min(TILE_MM_KIN, L)
    tk = min(TILE_MM_TOKENS, L)
    nkt = L // tk
    dy2, dg4 = post if post else _norm_bwd(s["y2"], wts["g4"], l, dx2, None, "norm_bwd_ffn_post")
    da = _mm(dy2, big["w_down"], dims=NT, grid=(nt, FF_HALF, 1),
             a_spec=((tm, D), lambda i, j, k: (i, 0)), b_spec=((None, None, SH_FF, D), lambda i, j, k: (0, j, 0, 0)),
             o_spec=((None, tm, SH_FF), lambda i, j, k: (j, i, 0)),
             out_shape=jax.ShapeDtypeStruct((FF_HALF, L, SH_FF), BF16), tile=(tm, SH_FF), name="ffn_down_dx",
             comm=_grads_d2d_comm(pending) if pending else None)
    pairs = None
    if pending:
        da, from_sibling = da[0], da[1:]
        pairs = [_pair_sum(g, r, c_idx, BF16, "grads_pair_sum") for g, r in zip(pending, from_sibling)]
    dw_down = _mm(s["a"], dy2, dims=TN, grid=(FF_HALF, 1, nkt),
                  a_spec=((None, tk, SH_FF), lambda i, j, k: (i, k, 0)), b_spec=((tk, D), lambda i, j, k: (k, 0)),
                  o_spec=((SH_FF, D), lambda i, j, k: (i, 0)), out_shape=jax.ShapeDtypeStruct((DFF, D), F32),
                  tile=(SH_FF, D), name="ffn_down_dw")
    du = _ffn_act_bwd(s["conv"], da, "ffn_act_bwd")
    d_u8, dcw = _ffn_conv_t(du.reshape(N_DEV, L, SH_FF), s["u8"], wts["cw"], l, "ffn_conv_t")
    dh2 = _mm(d_u8, big["w_up"], dims=NN, grid=(L // tm1, 1, N_DEV // FF_HALF), kin=FF_HALF,
              a_spec=((FF_HALF, tm1, SH_FF), lambda i, j, k: (k, i, 0)),
              b_spec=((None, FF_HALF, SH_FF, D), lambda i, j, k: (0, k, 0, 0)),
              o_spec=((tm1, D), lambda i, j, k: (i, 0)), out_shape=jax.ShapeDtypeStruct((L, D), BF16),
              tile=(tm1, D), name="ffn_up_dx",
              comm=_grads_ici_comm([q for n, q in zip(BIG, pairs) if n != "w_up"]) if pending else None)
    if pending:
        dh2, rest_parts = dh2[0], dh2[1:]
    dw_up = _mm(d_u8, s["h2"], dims=TN, grid=(N_DEV, 1, nkt),
                a_spec=((None, tk, SH_FF), lambda i, j, k: (i, k, 0)), b_spec=((tk, D), lambda i, j, k: (k, 0)),
                o_spec=((None, SH_FF, D), lambda i, j, k: (i, 0, 0)),
                out_shape=jax.ShapeDtypeStruct((N_DEV, SH_FF, D), F32), tile=(SH_FF, D), name="ffn_up_dw",
                comm=_grads_ici_comm([pairs[BIG.index("w_up")]]) if pending else None)
    parts = None
    if pending:
        dw_up, up_part = dw_up[0], dw_up[1]
        parts = [rest_parts[0], rest_parts[1], up_part, rest_parts[2]]
    dx1, dg3, dy, dg2 = _norm_bwd(s["x1"], wts["g3"], l, dh2, dx2, "norm_bwd_ffn_pre_mix_post",
                                  then=(s["y"], wts["g2"], l))
    dy_cat = _mm(dy, big["w_out"], dims=NT, grid=(nt, 1, 1),
                 a_spec=((tm, D), lambda i, j, k: (i, 0)), b_spec=((None, D, D), lambda i, j, k: (0, 0, 0)),
                 o_spec=((tm, D), lambda i, j, k: (i, 0)), out_shape=jax.ShapeDtypeStruct((L, D), BF16),
                 tile=(tm, D), name="proj_out_dx")
    dw_out = _mm(s["y_cat"], dy, dims=TN, grid=(1, 1, nkt),
                 a_spec=((tk, D), lambda i, j, k: (k, 0)), b_spec=((tk, D), lambda i, j, k: (k, 0)),
                 o_spec=((D, D), lambda i, j, k: (0, 0)), out_shape=jax.ShapeDtypeStruct((D, D), F32),
                 tile=(D, D), name="proj_out_dw")
    dgb, dgc, dgv, dgo, d_o, dconv_a, dgh = _mixer_out_bwd(s["p"], wts["conv_a"], wts["gh"], l, s["o_tot"], dy_cat,
                                                          "mixer_out_bwd")
    part_f = _gla_bwd(s["p"], wts["gpad_f"], wts["bias_f"], l, s["sp_f"], d_o, None, True, "gla_bwd_f")
    dp, dgp_b, dbias_b = _gla_bwd(s["p"], wts["gpad_b"], wts["bias_b"], l, s["sp_b"], d_o,
                                  list(part_f[:4]) + [dgb, dgc, dgv, dgo], False, "gla_bwd_b")
    early = [dw_up[None], dw_down.reshape(1, N_DEV, DFF // N_DEV, D)] if early_ffn else None
    dh1 = _mm(dp, big["w_in"], dims=NN, grid=(L // tm1, 1, 1),
              a_spec=((tm1, D_INP), lambda i, j, k: (i, 0)), b_spec=((None, D_INP, D), lambda i, j, k: (0, 0, 0)),
              o_spec=((tm1, D), lambda i, j, k: (i, 0)), out_shape=jax.ShapeDtypeStruct((L, D), BF16),
              tile=(tm1, D), name="proj_in_dx", comm=_grads_d2d_comm(early) if early_ffn else None)
    if early_ffn:
        dh1, early_sibling = dh1[0], dh1[1:]
        early = [_pair_sum(g, r, c_idx, BF16, "grads_pair_sum") for g, r in zip(early, early_sibling)]
    dw_in = _mm(dp, s["h1"], dims=TN, grid=(D_INP // 640, 1, nkt),
                a_spec=((tk, 640), lambda i, j, k: (k, i)), b_spec=((tk, D), lambda i, j, k: (k, 0)),
                o_spec=((640, D), lambda i, j, k: (i, 0)), out_shape=jax.ShapeDtypeStruct((D_INP, D), F32),
                tile=(640, D), name="proj_in_dw", comm=_grads_ici_comm(early) if early_ffn else None)
    if early_ffn:
        dw_in, early = dw_in[0], dw_in[1:]
    if below_y2 is None:
        dx0, dg1 = _norm_bwd(s["x"], wts["g1"], l, dh1, dx1, "norm_bwd_mix_pre")
        post_below = None
    else:
        dx0, dg1, *post_below = _norm_bwd(s["x"], wts["g1"], l, dh1, dx1, "norm_bwd_mix_pre_ffn_post",
                                          then=(below_y2, wts["g4"], l - 1))
    grads = dict(
        norm_mix_pre=dg1[0], norm_mix_post=dg2[0], norm_ffn_pre=dg3[0], norm_ffn_post=dg4[0],
        gate_bias_fwd=part_f[5][0], gate_bias_bwd=dbias_b[0], gla_head_norm=dgh[0],
        w_in=dw_in[:D_IN].reshape(N_DEV, SH_IN, D), w_out=dw_out.reshape(N_DEV, D // N_DEV, D), w_up=dw_up,
        w_down=dw_down.reshape(N_DEV, DFF // N_DEV, D),
        conv_a=_blocks_from_full(dconv_a, (3, DC // N_DEV), 1),
        gate_up_fwd=_blocks_from_full(part_f[4][:RANK], (RANK, DK // N_DEV), 1),
        gate_up_bwd=_blocks_from_full(dgp_b[RANK:2 * RANK], (RANK, DK // N_DEV), 1),
        conv_ffn=dcw.reshape(N_DEV, 3, SH_FF))
    return dx0, grads, parts, early, post_below


def _matmul_weights(g_in, g_out, g_up, g_down):
    w_in = jnp.concatenate([g_in.reshape(1, D_IN, D), jnp.zeros((1, D_INP - D_IN, D), BF16)], axis=1)
    return dict(w_in=w_in, w_out=g_out.reshape(1, D, D), w_up=g_up,
                w_down=None if g_down is None else g_down.reshape(1, FF_HALF, SH_FF, D))


def _small_weights(g_small, rep):
    small = {n: jnp.moveaxis(t, 0, 1) for n, t in jax.vmap(_unpack_small)(
        jnp.concatenate([g_small[0], jnp.zeros((N_DEV, ROWS_REP, LANES), F32)], axis=1)).items()
        if n in [s[0] for s in SMALL_SHARDED]}
    conv_a = jnp.concatenate([small["conv_a"][:, d] for d in range(N_DEV)], axis=2)
    gate_f = jnp.concatenate([small["gate_up_fwd"][:, d] for d in range(N_DEV)], axis=2).astype(BF16)
    gate_b = jnp.concatenate([small["gate_up_bwd"][:, d] for d in range(N_DEV)], axis=2).astype(BF16)
    zeros = jnp.zeros((DEPTH, LR_BLK, DK), BF16)
    return dict(
        conv_a=conv_a, cw=small["conv_ffn"],
        gpad_f=zeros.at[:, :RANK].set(gate_f), gpad_b=zeros.at[:, RANK:2 * RANK].set(gate_b),
        bias_f=rep["gate_bias_fwd"][:, None, :], bias_b=rep["gate_bias_bwd"][:, None, :],
        gh=rep["gla_head_norm"][:, None, :],
        g1=rep["norm_mix_pre"][:, None, :], g2=rep["norm_mix_post"][:, None, :],
        g3=rep["norm_ffn_pre"][:, None, :], g4=rep["norm_ffn_post"][:, None, :])


def kernel(x, norm_mix_pre, norm_mix_post, norm_ffn_pre, norm_ffn_post, w_in, conv_a, gate_up_fwd, gate_bias_fwd, gate_up_bwd, gate_bias_bwd, gla_head_norm, w_out, w_up, conv_ffn, w_down, loss_target, m_norm_mix_pre, m_norm_mix_post, m_norm_ffn_pre, m_norm_ffn_post, m_w_in, m_conv_a, m_gate_up_fwd, m_gate_bias_fwd, m_gate_up_bwd, m_gate_bias_bwd, m_gla_head_norm, m_w_out, m_w_up, m_conv_ffn, m_w_down, v_norm_mix_pre, v_norm_mix_post, v_norm_ffn_pre, v_norm_ffn_post, v_w_in, v_conv_a, v_gate_up_fwd, v_gate_bias_fwd, v_gate_up_bwd, v_gate_bias_bwd, v_gla_head_norm, v_w_out, v_w_up, v_conv_ffn, v_w_down):
    w = dict(norm_mix_pre=norm_mix_pre, norm_mix_post=norm_mix_post, norm_ffn_pre=norm_ffn_pre,
             norm_ffn_post=norm_ffn_post, w_in=w_in, conv_a=conv_a, gate_up_fwd=gate_up_fwd,
             gate_bias_fwd=gate_bias_fwd, gate_up_bwd=gate_up_bwd, gate_bias_bwd=gate_bias_bwd,
             gla_head_norm=gla_head_norm, w_out=w_out, w_up=w_up, conv_ffn=conv_ffn, w_down=w_down)
    m = dict(norm_mix_pre=m_norm_mix_pre, norm_mix_post=m_norm_mix_post, norm_ffn_pre=m_norm_ffn_pre,
             norm_ffn_post=m_norm_ffn_post, w_in=m_w_in, conv_a=m_conv_a, gate_up_fwd=m_gate_up_fwd,
             gate_bias_fwd=m_gate_bias_fwd, gate_up_bwd=m_gate_up_bwd, gate_bias_bwd=m_gate_bias_bwd,
             gla_head_norm=m_gla_head_norm, w_out=m_w_out, w_up=m_w_up, conv_ffn=m_conv_ffn, w_down=m_w_down)
    v = dict(norm_mix_pre=v_norm_mix_pre, norm_mix_post=v_norm_mix_post, norm_ffn_pre=v_norm_ffn_pre,
             norm_ffn_post=v_norm_ffn_post, w_in=v_w_in, conv_a=v_conv_a, gate_up_fwd=v_gate_up_fwd,
             gate_bias_fwd=v_gate_bias_fwd, gate_up_bwd=v_gate_up_bwd, gate_bias_bwd=v_gate_bias_bwd,
             gla_head_norm=v_gla_head_norm, w_out=v_w_out, w_up=v_w_up, conv_ffn=v_conv_ffn, w_down=v_w_down)
    axes = ("x", "y", "c")
    L = x.shape[1]
    x0 = x.reshape(L, D)
    target = loss_target.reshape(L, D)

    def rows_major(tree):
        return {n: jnp.swapaxes(tree[n], 1, 2) if n in TRANSPOSED else tree[n] for n in BIG}

    w_small = _pack_small(w)
    wt, mt, vt = rows_major(w), rows_major(m), rows_major(v)
    w16 = {n: wt[n].astype(BF16) for n in BIG}
    g_in, g_out, g_small = _all_gather([w16["w_in"][0:1], w16["w_out"][0:1], w_small[None, :ROWS_SSH]],
                                       "gather_weights")
    wts = _small_weights(g_small, w)
    big = {n: t for n, t in _matmul_weights(g_in, g_out, None, None).items() if t is not None}

    h1 = _norm_cast(x0, wts["g1"], 0, "norm_first")
    xl, saved = x0, []
    for l in range(DEPTH):
        nxt = [w16[n][l + 1:l + 2] for n in BIG] if l + 1 < DEPTH else None
        own = [w16["w_up"][0:1], w16["w_down"][0:1]] if l == 0 else None
        xl, h1, s, gathered = _layer_fwd(xl, h1, wts, big, l, min(l + 1, DEPTH - 1), nxt, own)
        saved.append(s)
        if nxt:
            big = _matmul_weights(*gathered)
    dx, sq = _loss_grad(xl, target, "loss_grad")
    loss = lax.psum(0.5 * jnp.sum(sq) / D, axes)

    c_idx = lax.axis_index("c").astype(jnp.int32).reshape(1)
    layer_grads, layer_parts, pending = [None] * DEPTH, [None] * DEPTH, None
    post = None
    for l in reversed(range(DEPTH)):
        dx, layer_grads[l], done, early, post = _layer_bwd(
            dx, wts, saved[l], l, pending, c_idx, l == 0, post, saved[l - 1]["y2"] if l > 0 else None)
        if pending:
            layer_parts[l + 1] = done
        pending = [layer_grads[l][n][None] for n in BIG]
    small_names = [n for n, _, _ in SMALL_SHARDED] + [n for n, _ in REPLICATED]
    stacked = {n: jnp.stack([g[n] for g in layer_grads]) for n in small_names}
    g_small = _pack_small_grads({n: jnp.moveaxis(stacked[n], 0, 1) for n, _, _ in SMALL_SHARDED}, stacked)
    last = [g for n, g in zip(BIG, pending) if n in ("w_in", "w_out")] + [g_small[None]]
    from_sibling = _exchange_sibling(last, "grads_to_sibling")
    pairs = [_pair_sum(g, r, c_idx, BF16 if i < 2 else F32, "grads_pair_sum")
             for i, (g, r) in enumerate(zip(last, from_sibling))]
    parts = _exchange_chips(pairs, "grads_to_chips")
    layer_parts[0] = [parts[0], parts[1], early[0], early[1]]

    results = {}
    for i, n in enumerate(BIG):
        part = jnp.concatenate([layer_parts[l][i] for l in range(DEPTH)], axis=0)
        results[n] = _sum_adamw(part, wt[n], mt[n], vt[n], "sum_adamw")
        if n in TRANSPOSED:
            results[n] = [jnp.swapaxes(r, 1, 2) for r in results[n]]
    small = _sum_adamw(parts[-1], w_small[None], _pack_small(m)[None], _pack_small(v)[None], "sum_adamw_small")
    small = [_unpack_small(buf[0]) for buf in small]
    outs = [loss, dx.reshape(x.shape)]
    for i in range(4):
        outs += [results[n][i] if n in BIG else small[i][n] for n in WEIGHT_ORDER]
    return tuple(outs)
```

```python
import math
from typing import Callable, NamedTuple

import jax
import jax.numpy as jnp
from jax import lax
from jax.experimental import pallas as pl
from jax.experimental.pallas import tpu as pltpu

F32 = jnp.float32
BF16 = jnp.bfloat16

DEPTH = 4
D = 1024
DC = 512
DG = 512
HEADS = 4
HV = 128
HK = 64
DK = 256
RANK = 16
CH = 64
DFF = 2816
D_IN = 3104
D_INP = 3200
LR_BLK = 128
EPS = 1e-6
HB = 16
N_DEV = 8
N_CHIP = 4
LANES = 1024
SH_IN = D_IN // N_DEV
SH_FF = 2 * DFF // N_DEV
FF_HALF = N_DEV // 2

ADAM_LR, ADAM_B1, ADAM_B2, ADAM_EPS, ADAM_WD, ADAM_STEP = 0.001, 0.9, 0.999, 1e-08, 0.01, 10

VMEM_LIMIT = 48 * 1024 * 1024
TILE_TOKENS = 512
TILE_GLA = 512
TILE_FFN = 1024
TILE_NORM = 512
TILE_MM = 2048
TILE_MM_KIN = 1024
TILE_MM_TOKENS = 4096

COL_GB, COL_GC, COL_GV = 0, 1, 2
COL_Q, COL_K = 6, 7
COL_V, COL_GO = 4, 5
COL_LR = 24

BIG = ("w_in", "w_out", "w_up", "w_down")
TRANSPOSED = ("w_in", "w_up")
SMALL_SHARDED = (
    ("conv_a", (DEPTH, 3, DC // N_DEV), 2),
    ("gate_up_fwd", (DEPTH, RANK, DK // N_DEV), 2),
    ("gate_up_bwd", (DEPTH, RANK, DK // N_DEV), 2),
    ("conv_ffn", (DEPTH, 3, SH_FF), 2),
)
REPLICATED = (
    ("norm_mix_pre", (DEPTH, D)), ("norm_mix_post", (DEPTH, D)), ("norm_ffn_pre", (DEPTH, D)),
    ("norm_ffn_post", (DEPTH, D)), ("gate_bias_fwd", (DEPTH, DK)), ("gate_bias_bwd", (DEPTH, DK)),
    ("gla_head_norm", (DEPTH, HV)),
)
WEIGHT_ORDER = ("norm_mix_pre", "norm_mix_post", "norm_ffn_pre", "norm_ffn_post", "w_in", "conv_a", "gate_up_fwd",
                "gate_bias_fwd", "gate_up_bwd", "gate_bias_bwd", "gla_head_norm", "w_out", "w_up", "conv_ffn", "w_down")


def _rows_for(n_elems):
    return (-(-n_elems // LANES) + 7) // 8 * 8


ROWS_SSH = _rows_for(sum(math.prod(s) for _, s, _ in SMALL_SHARDED))
ROWS_REP = _rows_for(sum(math.prod(s) for _, s in REPLICATED))
ROWS_SMALL = ROWS_SSH + ROWS_REP


def _params(sem):
    return pltpu.CompilerParams(dimension_semantics=sem, vmem_limit_bytes=VMEM_LIMIT)


def _silu_parts(x):
    s = 1.0 / (1.0 + jnp.exp(-x))
    return x * s, s


def _rstd(xf):
    return lax.rsqrt(jnp.mean(xf * xf, axis=-1, keepdims=True) + EPS)


NN, NT, TN = ((1,), (0,)), ((1,), (1,)), ((0,), (0,))


def _dot(a, b, dims):
    return lax.dot_general(a, b, (dims, ((), ())), preferred_element_type=F32)


def _mm(a, b, *, dims, grid, a_spec, b_spec, o_spec, out_shape, tile, name, kin=0, comm=None):
    nk = grid[2]
    n_ci = len(comm.ins) if comm else 0
    n_co = len(comm.out_shapes) if comm else 0

    def body(*refs):
        a_ref, b_ref = refs[:2]
        ci = refs[2:2 + n_ci]
        o_ref = refs[2 + n_ci]
        co = refs[3 + n_ci:3 + n_ci + n_co]
        rest = refs[3 + n_ci + n_co:]
        if comm:
            sems, rest = rest[:3], rest[3:]
            step = (pl.program_id(0) * grid[1] + pl.program_id(1)) * grid[2] + pl.program_id(2)

            @pl.when(step == 0)
            def _():
                comm.start(ci, co, *sems)

        if kin:
            prod = _dot(a_ref[0], b_ref[0], dims)
            for d in range(1, kin):
                prod = prod + _dot(a_ref[d], b_ref[d], dims)
        else:
            prod = _dot(a_ref[...], b_ref[...], dims)
        if nk == 1:
            o_ref[...] = prod.astype(o_ref.dtype)
        else:
            acc_ref = rest[0]
            k = pl.program_id(2)

            @pl.when(k == 0)
            def _():
                acc_ref[...] = prod

            @pl.when(k > 0)
            def _():
                acc_ref[...] += prod

            @pl.when(k == nk - 1)
            def _():
                o_ref[...] = acc_ref[...].astype(o_ref.dtype)

        if comm:
            @pl.when(step == grid[0] * grid[1] * grid[2] - 1)
            def _():
                comm.wait(ci, co, *sems)

    acc = [pltpu.VMEM(tile, F32)] if nk > 1 else []
    if not comm:
        return pl.pallas_call(
            body, name=name, grid=grid, in_specs=[pl.BlockSpec(*a_spec), pl.BlockSpec(*b_spec)],
            out_specs=pl.BlockSpec(*o_spec), out_shape=out_shape, scratch_shapes=acc,
            compiler_params=_params(("parallel", "parallel", "arbitrary")),
        )(a, b)
    return pl.pallas_call(
        body, name=name, grid=grid,
        in_specs=[pl.BlockSpec(*a_spec), pl.BlockSpec(*b_spec)] + [HBM_SPEC] * n_ci,
        out_specs=[pl.BlockSpec(*o_spec)] + [HBM_SPEC] * n_co, out_shape=[out_shape] + list(comm.out_shapes),
        scratch_shapes=[pltpu.SemaphoreType.DMA((s,)) for s in comm.sems] + acc,
        input_output_aliases={2 + i: 1 + o for i, o in comm.aliases.items()},
        compiler_params=_params(("arbitrary", "arbitrary", "arbitrary")),
    )(a, b, *comm.ins)


def _halo_maps(tm, n_rows):
    r, last = tm // HB, n_rows // HB - 1
    return (lambda i: jnp.maximum(i * r - 1, 0)), (lambda i: jnp.minimum((i + 1) * r, last))


def _shift(x, prev_blk, next_blk):
    tm = x.shape[0]
    xs = jnp.concatenate([prev_blk, x, next_blk], axis=0)
    n = xs.shape[0]
    down = pltpu.roll(xs, 1, 0)[HB:HB + tm]
    up = pltpu.roll(xs, n - 1, 0)[HB:HB + tm]
    return down, up


def _edge_scales(i, n):
    return jnp.where(i > 0, 1.0, 0.0).astype(F32), jnp.where(i < n - 1, 1.0, 0.0).astype(F32)


def _gain_spec(l, width=D):
    return pl.BlockSpec((None, 1, width), lambda *_: (l, 0, 0))


def _norm_cast(x, g, l, name):
    L = x.shape[0]
    tm = min(TILE_NORM, L)

    def body(x_ref, g_ref, o_ref):
        xf = x_ref[...]
        o_ref[...] = (xf * _rstd(xf) * g_ref[...]).astype(BF16)

    return pl.pallas_call(
        body, name=name, grid=(L // tm,), in_specs=[pl.BlockSpec((tm, D), lambda i: (i, 0)), _gain_spec(l)],
        out_specs=pl.BlockSpec((tm, D), lambda i: (i, 0)), out_shape=jax.ShapeDtypeStruct((L, D), BF16),
        compiler_params=_params(("parallel",)),
    )(x, g)


def _post_pre(x, y, g_post, l_post, g_pre, l_pre, name):
    L = x.shape[0]
    tm = min(TILE_NORM, L)

    def body(x_ref, y_ref, gp_ref, gn_ref, x1_ref, h_ref):
        yf = y_ref[...].astype(F32)
        x1 = x_ref[...] + yf * _rstd(yf) * gp_ref[...]
        x1_ref[...] = x1
        h_ref[...] = (x1 * _rstd(x1) * gn_ref[...]).astype(BF16)

    tile = pl.BlockSpec((tm, D), lambda i: (i, 0))
    return pl.pallas_call(
        body, name=name, grid=(L // tm,), in_specs=[tile, tile, _gain_spec(l_post), _gain_spec(l_pre)],
        out_specs=[tile, tile],
        out_shape=[jax.ShapeDtypeStruct((L, D), F32), jax.ShapeDtypeStruct((L, D), BF16)],
        compiler_params=_params(("parallel",)),
    )(x, y, g_post, g_pre)


def _norm_vjp(y_ref, g_ref, do):
    y = y_ref[...].astype(F32)
    r = _rstd(y)
    z = do * g_ref[...]
    return r * z - y * (r * r * r) * jnp.mean(y * z, axis=-1, keepdims=True), jnp.sum(do * y * r, axis=0, keepdims=True)


def _accumulate_rows(i, parts):
    @pl.when(i == 0)
    def _():
        for ref, val in parts:
            ref[...] = val

    @pl.when(i > 0)
    def _():
        for ref, val in parts:
            ref[...] += val


def _norm_bwd(yin, g, l, dout, dres, name, then=None):
    L = yin.shape[0]
    tm = min(TILE_NORM, L)
    with_res = dres is not None
    norm_vjp = _norm_vjp

    def body(*refs):
        y_ref, g_ref, do_ref = refs[:3]
        refs = refs[3:]
        if with_res:
            dr_ref, refs = refs[0], refs[1:]
        if then:
            y2_ref, g2_ref, refs = refs[0], refs[1], refs[2:]
        din_ref, dg_ref = refs[:2]
        i = pl.program_id(0)
        din, part = norm_vjp(y_ref, g_ref, do_ref[...].astype(F32))
        if with_res:
            din = din + dr_ref[...]
        din_ref[...] = din.astype(din_ref.dtype)
        parts = [(dg_ref, part)]
        if then:
            d2, part2 = norm_vjp(y2_ref, g2_ref, din)
            refs[2][...] = d2.astype(BF16)
            parts.append((refs[3], part2))
        _accumulate_rows(i, parts)

    tile = pl.BlockSpec((tm, D), lambda i: (i, 0))
    gain_out = pl.BlockSpec((1, D), lambda i: (0, 0))
    args = (yin, g, dout) + ((dres,) if with_res else ()) + ((then[0], then[1]) if then else ())
    return pl.pallas_call(
        body, name=name, grid=(L // tm,),
        in_specs=[tile, _gain_spec(l), tile] + ([tile] if with_res else []) + ([tile, _gain_spec(then[2])] if then else []),
        out_specs=[tile, gain_out] + ([tile, gain_out] if then else []),
        out_shape=[jax.ShapeDtypeStruct((L, D), F32 if with_res else BF16), jax.ShapeDtypeStruct((1, D), F32)]
        + ([jax.ShapeDtypeStruct((L, D), BF16), jax.ShapeDtypeStruct((1, D), F32)] if then else []),
        compiler_params=_params(("arbitrary",)),
    )(*args)


GLA_SUB = 256


def _gla_consts(fwd, tb):
    sub = min(GLA_SUB, tb)
    row = lax.broadcasted_iota(jnp.int32, (sub, sub), 0)
    col = lax.broadcasted_iota(jnp.int32, (sub, sub), 1)
    same = (row // CH) == (col // CH)
    tri = same & ((col <= row) if fwd else (col >= row))
    tri_t = same & ((col >= row) if fwd else (col <= row))
    row_st = lax.broadcasted_iota(jnp.int32, (HEADS * CH, CH), 0) & (CH - 1)
    col_st = lax.broadcasted_iota(jnp.int32, (HEADS * CH, CH), 1)
    tri_st = (col_st <= row_st) if fwd else (col_st >= row_st)
    lane_head = lax.broadcasted_iota(jnp.int32, (1, DK), 1) // HK
    head_masks = [lane_head == h for h in range(HEADS)]
    srow = lax.broadcasted_iota(jnp.int32, (DG, DK), 0) // HV
    scol = lax.broadcasted_iota(jnp.int32, (DG, DK), 1) // HK
    return tri.astype(BF16), tri_t.astype(BF16), tri_st, head_masks, srow == scol


def _dot_hilo(tri_b, x):
    hi = x.astype(BF16)
    lo = (x - hi.astype(F32)).astype(BF16)
    sub = tri_b.shape[0]
    return jnp.concatenate([_dot(tri_b, hi[r:r + sub], NN) + _dot(tri_b, lo[r:r + sub], NN)
                            for r in range(0, x.shape[0], sub)], axis=0)


def _gla_block_terms(q_ref, k_ref, lr_ref, gp_ref, bias_ref, tri_b):
    pre = _dot(lr_ref[...], gp_ref[...], NN) + bias_ref[...]
    sig_neg = 1.0 / (1.0 + jnp.exp(pre))
    a = (jnp.minimum(pre, 0.0) - jnp.log(1.0 + jnp.exp(-jnp.abs(pre)))) * (1.0 / 16.0)
    cum = _dot_hilo(tri_b, a)
    cl = jnp.concatenate([jnp.broadcast_to(jnp.min(cum[r:r + CH], axis=0, keepdims=True), (CH, DK))
                          for r in range(0, cum.shape[0], CH)], axis=0)
    e = jnp.exp(cum)
    einv = jnp.exp(-cum)
    eout = jnp.exp(cl - cum)
    q_in = q_ref[...].astype(F32) * e * (HK ** -0.5)
    k = k_ref[...].astype(F32)
    return dict(sig_neg=sig_neg, e=e, einv=einv, eout=eout, decay=jnp.exp(cl), q_in=q_in, k_in=k * einv,
                k_out=k * eout)


def _gla_chunk(t, c, head_masks):
    sl = slice(c * CH, (c + 1) * CH)
    tc = {n: x[sl] for n, x in t.items() if n != "decay"}
    tc["decay"] = jnp.max(t["decay"][c * CH:c * CH + 8], axis=0, keepdims=True)
    tc["q_st"] = jnp.concatenate([jnp.where(mh, tc["q_in"], 0.0) for mh in head_masks], axis=0).astype(BF16)
    return tc


def _gate_specs(l):
    return [pl.BlockSpec((None, LR_BLK, DK), lambda i: (l, 0, 0)), pl.BlockSpec((None, 1, DK), lambda i: (l, 0, 0))]


def _gla_fwd(p, gpad, bias, l, o_prev, fwd, name):
    L = p.shape[0]
    tb = min(TILE_GLA, L)
    nb, ncb, nch = L // tb, tb // CH, L // CH
    blk = (lambda i: i) if fwd else (lambda i: nb - 1 - i)
    with_prev = o_prev is not None

    def body(*refs):
        if with_prev:
            q_ref, k_ref, v_ref, lr_ref, gp_ref, bias_ref, op_ref, o_ref, sp_ref, s_ref = refs
        else:
            q_ref, k_ref, v_ref, lr_ref, gp_ref, bias_ref, o_ref, sp_ref, s_ref = refs
        i = pl.program_id(0)

        @pl.when(i == 0)
        def _():
            s_ref[...] = jnp.zeros_like(s_ref)

        tri_b, _, tri_st, head_masks, blockmask = _gla_consts(fwd, tb)
        terms = _gla_block_terms(q_ref, k_ref, lr_ref, gp_ref, bias_ref, tri_b)
        for c in (range(ncb) if fwd else reversed(range(ncb))):
            rows = pl.ds(c * CH, CH)
            t = _gla_chunk(terms, c, head_masks)
            v = v_ref[rows, :]
            scores = _dot(t["q_st"], t["k_in"].astype(BF16), NT)
            a_st = jnp.where(tri_st, scores, 0.0).astype(BF16)
            r = _dot(a_st, v, NN)
            o_intra = jnp.concatenate([r[h * CH:(h + 1) * CH, h * HV:(h + 1) * HV] for h in range(HEADS)], axis=1)
            s_b = s_ref[...].astype(BF16)
            sp_ref[c] = s_b
            o = o_intra + _dot(t["q_in"].astype(BF16), s_b, NT)
            if with_prev:
                o = o + op_ref[rows, :]
            o_ref[rows, :] = o
            kv_t = _dot(v, t["k_out"].astype(BF16), TN)
            s_ref[...] = s_ref[...] * t["decay"] + jnp.where(blockmask, kv_t, 0.0)

    def col(width, c):
        return pl.BlockSpec((tb, width), lambda i: (blk(i), c))

    in_specs = [col(DK, COL_Q), col(DK, COL_K), col(DG, COL_V), col(LR_BLK, COL_LR)] + _gate_specs(l)
    args = [p, p, p, p, gpad, bias]
    if with_prev:
        in_specs.append(pl.BlockSpec((tb, DG), lambda i: (blk(i), 0)))
        args.append(o_prev)
    return pl.pallas_call(
        body, name=name, grid=(nb,), in_specs=in_specs,
        out_specs=[pl.BlockSpec((tb, DG), lambda i: (blk(i), 0)), pl.BlockSpec((ncb, DG, DK), lambda i: (blk(i), 0, 0))],
        out_shape=[jax.ShapeDtypeStruct((L, DG), F32), jax.ShapeDtypeStruct((nch, DG, DK), BF16)],
        scratch_shapes=[pltpu.VMEM((DG, DK), F32)],
        compiler_params=_params(("arbitrary",)),
    )(*args)


P_COLS = dict(gb=(0, DC), gc=(DC, DC), gv=(2 * DC, DC), q=(3 * DC, DK), k=(3 * DC + DK, DK), v=(3 * DC + 2 * DK, DG),
              go=(3 * DC + 2 * DK + DG, DG), lr=(3 * DC + 2 * DK + 2 * DG, LR_BLK))


def _gla_bwd(p, gpad, bias, l, sprev, d_o, prev, fwd, name):
    L = p.shape[0]
    tb = min(TILE_GLA, L)
    nb, ncb = L // tb, tb // CH
    blk = (lambda i: nb - 1 - i) if fwd else (lambda i: i)
    with_prev = prev is not None

    def body(*refs):
        q_ref, k_ref, v_ref, lr_ref, gp_ref, bias_ref, sp_ref, do_ref = refs[:8]
        rest = refs[8:]
        if with_prev:
            pq_ref, pk_ref, pv_ref, plr_ref, dgb_ref, dgc_ref, dgv_ref, dgo_ref = rest[:8]
            dp_ref, dg_ref, db_ref, ds_ref = rest[8:]

            def put(what, rows, val):
                c0, width = P_COLS[what]
                dp_ref[rows, c0:c0 + width] = val

            for what, ref in (("gb", dgb_ref), ("gc", dgc_ref), ("gv", dgv_ref), ("go", dgo_ref)):
                put(what, slice(None), ref[...])
        else:
            dq_ref, dk_ref, dv_ref, dlr_ref, dg_ref, db_ref, ds_ref = rest
            out_of = dict(q=dq_ref, k=dk_ref, v=dv_ref, lr=dlr_ref)

            def put(what, rows, val):
                out_of[what][rows, :] = val

        i = pl.program_id(0)

        @pl.when(i == 0)
        def _():
            ds_ref[...] = jnp.zeros_like(ds_ref)
            dg_ref[...] = jnp.zeros_like(dg_ref)
            db_ref[...] = jnp.zeros_like(db_ref)

        tri_b, tri_t_b, tri_st, head_masks, blockmask = _gla_consts(fwd, tb)
        terms = _gla_block_terms(q_ref, k_ref, lr_ref, gp_ref, bias_ref, tri_b)
        dcum_of, dcl_of = [None] * ncb, [None] * ncb
        for c in (reversed(range(ncb)) if fwd else range(ncb)):
            rows = pl.ds(c * CH, CH)
            t = _gla_chunk(terms, c, head_masks)
            v = v_ref[rows, :]
            do = do_ref[rows, :]
            q_in, k_in, k_out = t["q_in"], t["k_in"], t["k_out"]
            q_b, k_in_b, k_out_b = q_in.astype(BF16), k_in.astype(BF16), k_out.astype(BF16)
            scores = _dot(t["q_st"], k_in_b, NT)
            a_st = jnp.where(tri_st, scores, 0.0).astype(BF16)
            s_prev = sp_ref[c]
            ds = ds_ref[...]
            ds_b = ds.astype(BF16)

            da_heads = [_dot(do[:, h * HV:(h + 1) * HV], v[:, h * HV:(h + 1) * HV], NT) for h in range(HEADS)]
            da_st = jnp.where(tri_st, jnp.concatenate(da_heads, axis=0), 0.0).astype(BF16)

            dv_heads = [_dot(a_st[h * CH:(h + 1) * CH, :], do[:, h * HV:(h + 1) * HV], TN) for h in range(HEADS)]
            dv = jnp.concatenate(dv_heads, axis=1) + _dot(k_out_b, ds_b, NT)

            x = _dot(da_st, k_in_b, NN)
            dq_in = _dot(do, s_prev, NN)
            for h in range(HEADS):
                dq_in = dq_in + jnp.where(head_masks[h], x[h * CH:(h + 1) * CH, :], 0.0)
            dk_in = _dot(da_st, t["q_st"], TN)
            dk_out = _dot(v, ds_b, NN)
            d_decay = jnp.sum(ds * s_prev.astype(F32), axis=0, keepdims=True)
            ds_ref[...] = ds * t["decay"] + jnp.where(blockmask, _dot(do, q_b, TN), 0.0)

            dq = dq_in * t["e"] * (HK ** -0.5)
            dk = dk_in * t["einv"] + dk_out * t["eout"]
            dko_ko = dk_out * k_out
            dcum_of[c] = dq_in * q_in - dk_in * k_in - dko_ko
            dcl = jnp.sum(dko_ko, axis=0, keepdims=True) + d_decay * t["decay"]
            dcl_of[c] = jnp.broadcast_to(dcl, (CH, DK))
            if with_prev:
                dq = dq + pq_ref[rows, :].astype(F32)
                dk = dk + pk_ref[rows, :].astype(F32)
                dv = dv + pv_ref[rows, :].astype(F32)
            put("q", rows, dq.astype(BF16))
            put("k", rows, dk.astype(BF16))
            put("v", rows, dv.astype(BF16))

        da = _dot_hilo(tri_t_b, jnp.concatenate(dcum_of, axis=0)) + jnp.concatenate(dcl_of, axis=0)
        dpre = da * terms["sig_neg"] * (1.0 / 16.0)
        dpre_b = dpre.astype(BF16)
        dlr = _dot(dpre_b, gp_ref[...], NT)
        dg_ref[...] += _dot(lr_ref[...], dpre_b, TN)
        db_ref[...] += jnp.sum(dpre, axis=0, keepdims=True)
        if with_prev:
            dlr = dlr + plr_ref[...].astype(F32)
        put("lr", slice(None), dlr.astype(BF16))

    def col(width, c):
        return pl.BlockSpec((tb, width), lambda i: (blk(i), c))

    in_specs = [col(DK, COL_Q), col(DK, COL_K), col(DG, COL_V), col(LR_BLK, COL_LR)] + _gate_specs(l) + [
        pl.BlockSpec((ncb, DG, DK), lambda i: (blk(i), 0, 0)), col(DG, 0)]
    args = [p, p, p, p, gpad, bias, sprev, d_o]
    tiles = [col(DK, 0), col(DK, 0), col(DG, 0), col(LR_BLK, 0)]
    shapes = [jax.ShapeDtypeStruct((L, DK), BF16), jax.ShapeDtypeStruct((L, DK), BF16),
              jax.ShapeDtypeStruct((L, DG), BF16), jax.ShapeDtypeStruct((L, LR_BLK), BF16)]
    if with_prev:
        in_specs += tiles + [col(DC, 0)] * 4
        args += list(prev)
        tiles, shapes = [col(D_INP, 0)], [jax.ShapeDtypeStruct((L, D_INP), BF16)]
    return pl.pallas_call(
        body, name=name, grid=(nb,), in_specs=in_specs,
        out_specs=tiles + [pl.BlockSpec((LR_BLK, DK), lambda i: (0, 0)), pl.BlockSpec((1, DK), lambda i: (0, 0))],
        out_shape=shapes + [jax.ShapeDtypeStruct((LR_BLK, DK), F32), jax.ShapeDtypeStruct((1, DK), F32)],
        scratch_shapes=[pltpu.VMEM((DG, DK), F32)],
        compiler_params=_params(("arbitrary",)),
    )(*args)


def _mixer_out(p, conv_a, gh, l, o_tot, name):
    L = p.shape[0]
    tm = min(TILE_TOKENS, L)
    n = L // tm
    pmap, nmap = _halo_maps(tm, L)

    def body(gb_ref, gc_ref, gcp_ref, gcn_ref, gv_ref, gvp_ref, gvn_ref, go_ref, cw_ref, o_ref, gh_ref, y_ref):
        ps, ns = _edge_scales(pl.program_id(0), n)
        z = gc_ref[...].astype(F32) * gv_ref[...].astype(F32)
        zp = gcp_ref[...].astype(F32) * gvp_ref[...].astype(F32) * ps
        zn = gcn_ref[...].astype(F32) * gvn_ref[...].astype(F32) * ns
        z_dn, z_up = _shift(z, zp, zn)
        conv = cw_ref[0:1, :] * z_dn + cw_ref[1:2, :] * z + cw_ref[2:3, :] * z_up
        y_ref[:, 0:DC] = (gb_ref[...].astype(F32) * conv).astype(BF16)
        o = o_ref[...]
        go = go_ref[...].astype(F32)
        for h in range(HEADS):
            oh = o[:, h * HV:(h + 1) * HV]
            on = oh * _rstd(oh) * gh_ref[...]
            act, _ = _silu_parts(go[:, h * HV:(h + 1) * HV])
            y_ref[:, DC + h * HV:DC + (h + 1) * HV] = (act * on).astype(BF16)

    def main(c):
        return pl.BlockSpec((tm, DC), lambda i: (i, c))

    def halo(c, imap):
        return pl.BlockSpec((HB, DC), lambda i: (imap(i), c))

    return pl.pallas_call(
        body, name=name, grid=(n,),
        in_specs=[main(COL_GB), main(COL_GC), halo(COL_GC, pmap), halo(COL_GC, nmap),
                  main(COL_GV), halo(COL_GV, pmap), halo(COL_GV, nmap), main(COL_GO),
                  pl.BlockSpec((None, 3, DC), lambda i: (l, 0, 0)), pl.BlockSpec((tm, DG), lambda i: (i, 0)),
                  _gain_spec(l, HV)],
        out_specs=pl.BlockSpec((tm, D), lambda i: (i, 0)), out_shape=jax.ShapeDtypeStruct((L, D), BF16),
        compiler_params=_params(("parallel",)),
    )(p, p, p, p, p, p, p, p, conv_a, o_tot, gh)


def _mixer_out_bwd(p, conv_a, gh, l, o_tot, dy, name):
    L = p.shape[0]
    tm = min(TILE_TOKENS, L)
    n = L // tm
    pmap, nmap = _halo_maps(tm, L)

    def body(gb_ref, gbp_ref, gbn_ref, gc_ref, gcp_ref, gcn_ref, gv_ref, gvp_ref, gvn_ref, go_ref, cw_ref, o_ref,
             gh_ref, dy_ref, dyp_ref, dyn_ref, dgb_ref, dgc_ref, dgv_ref, dgo_ref, do_ref, dcw_ref, dgh_ref):
        i = pl.program_id(0)
        ps, ns = _edge_scales(i, n)
        gb = gb_ref[...].astype(F32)
        gc = gc_ref[...].astype(F32)
        gv = gv_ref[...].astype(F32)
        z = gc * gv
        zp = gcp_ref[...].astype(F32) * gvp_ref[...].astype(F32) * ps
        zn = gcn_ref[...].astype(F32) * gvn_ref[...].astype(F32) * ns
        z_dn, z_up = _shift(z, zp, zn)
        w0, w1, w2 = cw_ref[0:1, :], cw_ref[1:2, :], cw_ref[2:3, :]
        conv = w0 * z_dn + w1 * z + w2 * z_up
        dya = dy_ref[:, 0:DC].astype(F32)
        dgb_ref[...] = (dya * conv).astype(BF16)
        dc = dya * gb
        dcp = dyp_ref[...].astype(F32) * gbp_ref[...].astype(F32) * ps
        dcn = dyn_ref[...].astype(F32) * gbn_ref[...].astype(F32) * ns
        dc_dn, dc_up = _shift(dc, dcp, dcn)
        dz = w0 * dc_up + w1 * dc + w2 * dc_dn
        dgc_ref[...] = (dz * gv).astype(BF16)
        dgv_ref[...] = (dz * gc).astype(BF16)
        dcw = [jnp.sum(zs * dc, axis=0, keepdims=True) for zs in (z_dn, z, z_up)]

        o = o_ref[...]
        go = go_ref[...].astype(F32)
        dgh = jnp.zeros((1, HV), F32)
        for h in range(HEADS):
            sl = slice(h * HV, (h + 1) * HV)
            oh = o[:, sl]
            r = _rstd(oh)
            act, sg = _silu_parts(go[:, sl])
            dyb = dy_ref[:, DC + h * HV:DC + (h + 1) * HV].astype(F32)
            on = oh * r * gh_ref[...]
            dgo_ref[:, sl] = (dyb * on * (sg + act * (1.0 - sg))).astype(BF16)
            don = dyb * act
            zz = don * gh_ref[...]
            do_ref[:, sl] = (r * zz - oh * (r * r * r) * jnp.mean(oh * zz, axis=-1, keepdims=True)).astype(BF16)
            dgh = dgh + jnp.sum(don * oh * r, axis=0, keepdims=True)

        @pl.when(i == 0)
        def _():
            dcw_ref[...] = jnp.zeros_like(dcw_ref)
            dgh_ref[...] = jnp.zeros_like(dgh_ref)

        for kk in range(3):
            dcw_ref[kk:kk + 1, :] += dcw[kk]
        dgh_ref[...] += dgh

    def main(c):
        return pl.BlockSpec((tm, DC), lambda i: (i, c))

    def halo(c, imap):
        return pl.BlockSpec((HB, DC), lambda i: (imap(i), c))

    tile = pl.BlockSpec((tm, DC), lambda i: (i, 0))
    return pl.pallas_call(
        body, name=name, grid=(n,),
        in_specs=[main(COL_GB), halo(COL_GB, pmap), halo(COL_GB, nmap), main(COL_GC), halo(COL_GC, pmap),
                  halo(COL_GC, nmap), main(COL_GV), halo(COL_GV, pmap), halo(COL_GV, nmap), main(COL_GO),
                  pl.BlockSpec((None, 3, DC), lambda i: (l, 0, 0)), tile, _gain_spec(l, HV),
                  pl.BlockSpec((tm, D), lambda i: (i, 0)), halo(0, pmap), halo(0, nmap)],
        out_specs=[tile, tile, tile, tile, tile, pl.BlockSpec((3, DC), lambda i: (0, 0)),
                   pl.BlockSpec((1, HV), lambda i: (0, 0))],
        out_shape=[jax.ShapeDtypeStruct((L, DC), BF16)] * 5
        + [jax.ShapeDtypeStruct((3, DC), F32), jax.ShapeDtypeStruct((1, HV), F32)],
        compiler_params=_params(("arbitrary",)),
    )(p, p, p, p, p, p, p, p, p, p, conv_a, o_tot, gh, dy, dy, dy)


def _ffn_specs(tm, L, l, row_axis, sh_axis):
    pmap, nmap = _halo_maps(tm, L)

    def u(off, imap=None, rows=tm):
        if imap is None:
            return pl.BlockSpec((None, rows, SH_FF), lambda *g: (g[sh_axis] + off, g[row_axis], 0))
        return pl.BlockSpec((None, rows, SH_FF), lambda *g: (g[sh_axis] + off, imap(g[row_axis]), 0))

    def cw(off):
        return pl.BlockSpec((None, None, 3, SH_FF), lambda *g: (l, g[sh_axis] + off, 0, 0))

    u_specs = [u(0), u(0, pmap, HB), u(0, nmap, HB), u(FF_HALF), u(FF_HALF, pmap, HB), u(FF_HALF, nmap, HB)]
    return u_specs, [cw(0), cw(FF_HALF)]


def _conv3(x_ref, xp_ref, xn_ref, cw_ref, ps, ns):
    x = x_ref[...].astype(F32)
    x_dn, x_up = _shift(x, xp_ref[...].astype(F32) * ps, xn_ref[...].astype(F32) * ns)
    return cw_ref[0:1, :] * x_dn + cw_ref[1:2, :] * x + cw_ref[2:3, :] * x_up, (x_dn, x, x_up)


def _carry(body, *, name, grid, args, in_specs, out_specs, out_shape, scratch, comm):
    n_in, n_out = len(args), len(out_shape)
    n_ci = len(comm.ins) if comm else 0
    n_co = len(comm.out_shapes) if comm else 0

    def wrapped(*refs):
        ins, refs = refs[:n_in], refs[n_in:]
        ci, refs = refs[:n_ci], refs[n_ci:]
        outs, refs = refs[:n_out], refs[n_out:]
        co, refs = refs[:n_co], refs[n_co:]
        if comm:
            sems, refs = refs[:3], refs[3:]
            step = 0
            for ax, size in enumerate(grid):
                step = step * size + pl.program_id(ax)

            @pl.when(step == 0)
            def _():
                comm.start(ci, co, *sems)

        body(ins, outs, refs)
        if comm:
            @pl.when(step == math.prod(grid) - 1)
            def _():
                comm.wait(ci, co, *sems)

    return pl.pallas_call(
        wrapped, name=name, grid=grid, in_specs=list(in_specs) + [HBM_SPEC] * n_ci,
        out_specs=list(out_specs) + [HBM_SPEC] * n_co,
        out_shape=list(out_shape) + list(comm.out_shapes if comm else ()),
        scratch_shapes=([pltpu.SemaphoreType.DMA((s,)) for s in comm.sems] if comm else []) + list(scratch),
        input_output_aliases={n_in + i: n_out + o for i, o in comm.aliases.items()} if comm else {},
        compiler_params=_params(("arbitrary",) * len(grid)),
    )(*args, *(comm.ins if comm else ()))


def _ffn_act(u8, cw, l, name):
    L = u8.shape[1]
    tm = min(TILE_FFN, L)
    n = L // tm
    u_specs, cw_specs = _ffn_specs(tm, L, l, 0, 1)

    def body(ins, outs, scratch):
        g_ref, gp_ref, gn_ref, v_ref, vp_ref, vn_ref, cwg_ref, cwv_ref = ins
        ps, ns = _edge_scales(pl.program_id(0), n)
        gate, _ = _conv3(g_ref, gp_ref, gn_ref, cwg_ref, ps, ns)
        val, _ = _conv3(v_ref, vp_ref, vn_ref, cwv_ref, ps, ns)
        act, _ = _silu_parts(gate)
        outs[0][...] = (act * val).astype(BF16)
        outs[1][0] = gate.astype(BF16)
        outs[1][1] = val.astype(BF16)

    pair = pl.BlockSpec((2, None, tm, SH_FF), lambda i, d: (0, d, i, 0))
    return _carry(
        body, name=name, grid=(n, FF_HALF), args=(u8, u8, u8, u8, u8, u8, cw, cw), in_specs=u_specs + cw_specs,
        out_specs=[pl.BlockSpec((None, tm, SH_FF), lambda i, d: (d, i, 0)), pair],
        out_shape=[jax.ShapeDtypeStruct((FF_HALF, L, SH_FF), BF16),
                   jax.ShapeDtypeStruct((2, FF_HALF, L, SH_FF), BF16)], scratch=[], comm=None)


def _ffn_act_bwd(conv, da, name):
    L = conv.shape[2]
    tm = min(TILE_FFN, L)

    def body(ins, outs, scratch):
        c_ref, da_ref = ins
        act, sg = _silu_parts(c_ref[0].astype(F32))
        da_f = da_ref[...].astype(F32)
        outs[0][0] = (da_f * c_ref[1].astype(F32) * (sg + act * (1.0 - sg))).astype(BF16)
        outs[0][1] = (da_f * act).astype(BF16)

    pair = pl.BlockSpec((2, None, tm, SH_FF), lambda d, i: (0, d, i, 0))
    return _carry(
        body, name=name, grid=(FF_HALF, L // tm), args=(conv, da),
        in_specs=[pair, pl.BlockSpec((None, tm, SH_FF), lambda d, i: (d, i, 0))], out_specs=[pair],
        out_shape=[jax.ShapeDtypeStruct((2, FF_HALF, L, SH_FF), BF16)], scratch=[], comm=None)[0]


def _ffn_conv_t(du8, u8, cw, l, name):
    L = du8.shape[1]
    tm = min(TILE_FFN, L)
    n = L // tm
    pmap, nmap = _halo_maps(tm, L)

    def body(ins, outs, scratch):
        x_ref, xp_ref, xn_ref, u_ref, cw_ref = ins
        d_u_ref, dcw_ref = outs
        i = pl.program_id(1)
        ps, ns = _edge_scales(i, n)
        x = x_ref[...].astype(F32)
        x_dn, x_up = _shift(x, xp_ref[...].astype(F32) * ps, xn_ref[...].astype(F32) * ns)
        d_u_ref[...] = (cw_ref[0:1, :] * x_up + cw_ref[1:2, :] * x + cw_ref[2:3, :] * x_dn).astype(BF16)

        @pl.when(i == 0)
        def _():
            dcw_ref[...] = jnp.zeros_like(dcw_ref)

        u = u_ref[...].astype(F32)
        for kk, xs in enumerate((x_up, x, x_dn)):
            dcw_ref[kk:kk + 1, :] += jnp.sum(u * xs, axis=0, keepdims=True)

    tile = pl.BlockSpec((None, tm, SH_FF), lambda d, i: (d, i, 0))
    return _carry(
        body, name=name, grid=(N_DEV, n), args=(du8, du8, du8, u8, cw),
        in_specs=[tile, pl.BlockSpec((None, HB, SH_FF), lambda d, i: (d, pmap(i), 0)),
                  pl.BlockSpec((None, HB, SH_FF), lambda d, i: (d, nmap(i), 0)), tile,
                  pl.BlockSpec((None, None, 3, SH_FF), lambda d, i: (l, d, 0, 0))],
        out_specs=[tile, pl.BlockSpec((None, 3, SH_FF), lambda d, i: (d, 0, 0))],
        out_shape=[jax.ShapeDtypeStruct((N_DEV, L, SH_FF), BF16), jax.ShapeDtypeStruct((N_DEV, 3, SH_FF), F32)],
        scratch=[], comm=None)


def _loss_grad(xl, target, y2, g, l, name):
    L = xl.shape[0]
    tm = min(TILE_NORM, L)

    def body(x_ref, t_ref, y2_ref, g_ref, dx_ref, sq_ref, dy_ref, dg_ref):
        err = x_ref[...] - t_ref[...]
        dx = err * (1.0 / D)
        dx_ref[...] = dx
        dy, dg_part = _norm_vjp(y2_ref, g_ref, dx)
        dy_ref[...] = dy.astype(BF16)
        _accumulate_rows(pl.program_id(0), [(sq_ref, jnp.sum(err * err, axis=0, keepdims=True)), (dg_ref, dg_part)])

    tile = pl.BlockSpec((tm, D), lambda i: (i, 0))
    row = pl.BlockSpec((1, D), lambda i: (0, 0))
    return pl.pallas_call(
        body, name=name, grid=(L // tm,), in_specs=[tile, tile, tile, _gain_spec(l)],
        out_specs=[tile, row, tile, row],
        out_shape=[jax.ShapeDtypeStruct((L, D), F32), jax.ShapeDtypeStruct((1, D), F32),
                   jax.ShapeDtypeStruct((L, D), BF16), jax.ShapeDtypeStruct((1, D), F32)],
        compiler_params=_params(("arbitrary",)),
    )(xl, target, y2, g)


MESH = pl.DeviceIdType.MESH
HBM_SPEC = pl.BlockSpec(memory_space=pltpu.HBM)


def _position():
    return lax.axis_index("x"), lax.axis_index("y"), lax.axis_index("c")


def _other_chips(x, y):
    return [(1 - x, y), (x, 1 - y), (1 - x, 1 - y)]


def _all_gather(shards, name):
    n = len(shards)

    def body(*refs):
        x_refs, out_refs = refs[:n], refs[n:2 * n]
        send_sems, recv_sems, local_sems = refs[2 * n:]
        x, y, c = _position()
        me, sibling = (x, y, c), (x, y, 1 - c)
        chips = _other_chips(x, y)

        def slot(t, px, py, pc):
            return out_refs[t].at[:, 4 * px + 2 * py + pc]

        def copy(t, k, block, to, from_input=False):
            return pltpu.make_async_remote_copy(
                src_ref=x_refs[t] if from_input else slot(t, *block), dst_ref=slot(t, *block),
                send_sem=send_sems.at[k * n + t], recv_sem=recv_sems.at[k * n + t], device_id=to, device_id_type=MESH)

        mine = [pltpu.make_async_copy(x_refs[t], slot(t, *me), local_sems.at[t]) for t in range(n)]
        for cp in mine:
            cp.start()
        first = [copy(t, 0, me, sibling, True) for t in range(n)]
        first += [copy(t, 1 + j, me, (*chip, c), True) for j, chip in enumerate(chips) for t in range(n)]
        for cp in first:
            cp.start()
        passed = []
        for j, chip in enumerate(chips):
            for t in range(n):
                copy(t, 1 + j, (*chip, c), me).wait_recv()
                passed.append(copy(t, 4 + j, (*chip, c), sibling))
                passed[-1].start()
        for t in range(n):
            copy(t, 0, sibling, me).wait_recv()
        for j, chip in enumerate(chips):
            for t in range(n):
                copy(t, 4 + j, (*chip, 1 - c), me).wait_recv()
        for cp in first + passed:
            cp.wait_send()
        for cp in mine:
            cp.wait()

    return pl.pallas_call(
        body, name=name,
        out_shape=[jax.ShapeDtypeStruct((s.shape[0], N_DEV) + s.shape[1:], s.dtype) for s in shards],
        in_specs=[HBM_SPEC] * n, out_specs=[HBM_SPEC] * n,
        scratch_shapes=[pltpu.SemaphoreType.DMA((7 * n,)), pltpu.SemaphoreType.DMA((7 * n,)),
                        pltpu.SemaphoreType.DMA((n,))],
    )(*shards)


def _exchange_sibling(grads, name):
    n = len(grads)

    def body(*refs):
        g_refs, out_refs, send_sems, recv_sems = refs[:n], refs[n:2 * n], refs[2 * n], refs[2 * n + 1]
        x, y, c = _position()
        copies = [pltpu.make_async_remote_copy(
            src_ref=g_refs[t].at[:, 2 * k + (1 - c)], dst_ref=out_refs[t].at[:, k], send_sem=send_sems.at[k * n + t],
            recv_sem=recv_sems.at[k * n + t], device_id=(x, y, 1 - c), device_id_type=MESH)
            for k in range(N_CHIP) for t in range(n)]
        for cp in copies:
            cp.start()
        for cp in copies:
            cp.wait()

    return pl.pallas_call(
        body, name=name,
        out_shape=[jax.ShapeDtypeStruct((g.shape[0], N_CHIP) + g.shape[2:], g.dtype) for g in grads],
        in_specs=[HBM_SPEC] * n, out_specs=[HBM_SPEC] * n,
        scratch_shapes=[pltpu.SemaphoreType.DMA((N_CHIP * n,)), pltpu.SemaphoreType.DMA((N_CHIP * n,))],
    )(*grads)


class _Comm(NamedTuple):
    ins: tuple
    out_shapes: tuple
    aliases: dict
    sems: tuple
    start: Callable
    wait: Callable


def _comm_of(ins, out_shapes, aliases, sems, copies):
    def start(ci, co, send, recv, local):
        for cp in copies(ci, co, send, recv, local):
            cp.start()

    def wait(ci, co, send, recv, local):
        for cp in copies(ci, co, send, recv, local):
            cp.wait()

    return _Comm(tuple(ins), tuple(out_shapes), aliases, sems, start, wait)


def _remote(src, dst, send, recv, idx, to):
    return pltpu.make_async_remote_copy(src_ref=src, dst_ref=dst, send_sem=send.at[idx], recv_sem=recv.at[idx],
                                        device_id=to, device_id_type=MESH)


def _gather_ici_comm(shards):
    n = len(shards)

    def copies(ci, co, send, recv, local):
        x, y, c = _position()
        me = 4 * x + 2 * y + c
        mine = [pltpu.make_async_copy(ci[t], co[t].at[:, me], local.at[t]) for t in range(n)]
        return mine + [_remote(ci[t], co[t].at[:, me], send, recv, j * n + t, (cx, cy, c))
                       for j, (cx, cy) in enumerate(_other_chips(x, y)) for t in range(n)]

    outs = [jax.ShapeDtypeStruct((1, N_DEV) + s.shape[1:], s.dtype) for s in shards]
    return _comm_of(shards, outs, {}, (3 * n, 3 * n, n), copies)


def _gather_d2d_comm(partials):
    n = len(partials)

    def copies(ci, co, send, recv, local):
        x, y, c = _position()
        return [_remote(co[t].at[:, 4 * cx + 2 * cy + c], co[t].at[:, 4 * cx + 2 * cy + c], send, recv, k * n + t,
                        (x, y, 1 - c))
                for k, (cx, cy) in enumerate([(x, y)] + _other_chips(x, y)) for t in range(n)]

    outs = [jax.ShapeDtypeStruct(p.shape, p.dtype) for p in partials]
    return _comm_of(partials, outs, {t: t for t in range(n)}, (N_CHIP * n, N_CHIP * n, 1), copies)


def _grads_d2d_comm(grads):
    n = len(grads)

    def copies(ci, co, send, recv, local):
        x, y, c = _position()
        return [_remote(ci[t].at[:, 2 * k + (1 - c)], co[t].at[:, k], send, recv, k * n + t, (x, y, 1 - c))
                for k in range(N_CHIP) for t in range(n)]

    outs = [jax.ShapeDtypeStruct((g.shape[0], N_CHIP) + g.shape[2:], g.dtype) for g in grads]
    return _comm_of(grads, outs, {}, (N_CHIP * n, N_CHIP * n, 1), copies)


def _grads_ici_comm(parts):
    n = len(parts)

    def copies(ci, co, send, recv, local):
        x, y, c = _position()
        my_chip = 2 * x + y
        mine = [pltpu.make_async_copy(ci[t].at[:, my_chip], co[t].at[:, my_chip], local.at[t]) for t in range(n)]
        return mine + [_remote(ci[t].at[:, 2 * cx + cy], co[t].at[:, my_chip], send, recv, j * n + t, (cx, cy, c))
                       for j, (cx, cy) in enumerate(_other_chips(x, y)) for t in range(n)]

    outs = [jax.ShapeDtypeStruct(p.shape, p.dtype) for p in parts]
    return _comm_of(parts, outs, {}, (3 * n, 3 * n, n), copies)


def _row_tile(rows):
    fits = [t for t in range(HB, 257, HB) if rows % t == 0]
    return max(fits) if fits else rows


def _pair_sum(g, recv, c_idx, out_dtype, name):
    lay, _, rows, cols = g.shape
    tr = _row_tile(rows)

    def body(c_ref, g_ref, r_ref, o_ref):
        o_ref[...] = (g_ref[...] + r_ref[...]).astype(o_ref.dtype)

    def spec(blk_of):
        return pl.BlockSpec((None, None, tr, cols), lambda l, k, r, c_ref: (l, blk_of(k, c_ref), r, 0))

    return pl.pallas_call(
        body, name=name,
        grid_spec=pltpu.PrefetchScalarGridSpec(
            num_scalar_prefetch=1, grid=(lay, N_CHIP, rows // tr),
            in_specs=[spec(lambda k, c_ref: 2 * k + c_ref[0]), spec(lambda k, c_ref: k)],
            out_specs=spec(lambda k, c_ref: k)),
        out_shape=jax.ShapeDtypeStruct((lay, N_CHIP, rows, cols), out_dtype),
        compiler_params=_params(("parallel", "parallel", "parallel")),
    )(c_idx, g, recv)


def _exchange_chips(parts, name):
    n = len(parts)

    def body(*refs):
        p_refs, out_refs = refs[:n], refs[n:2 * n]
        send_sems, recv_sems, local_sems = refs[2 * n:]
        x, y, c = _position()
        my_chip = 2 * x + y
        mine = [pltpu.make_async_copy(p_refs[t].at[:, my_chip], out_refs[t].at[:, my_chip], local_sems.at[t])
                for t in range(n)]
        for cp in mine:
            cp.start()
        copies = [pltpu.make_async_remote_copy(
            src_ref=p_refs[t].at[:, 2 * cx + cy], dst_ref=out_refs[t].at[:, my_chip], send_sem=send_sems.at[j * n + t],
            recv_sem=recv_sems.at[j * n + t], device_id=(cx, cy, c), device_id_type=MESH)
            for j, (cx, cy) in enumerate(_other_chips(x, y)) for t in range(n)]
        for cp in copies:
            cp.start()
        for cp in copies:
            cp.wait()
        for cp in mine:
            cp.wait()

    return pl.pallas_call(
        body, name=name, out_shape=[jax.ShapeDtypeStruct(p.shape, p.dtype) for p in parts],
        in_specs=[HBM_SPEC] * n, out_specs=[HBM_SPEC] * n,
        scratch_shapes=[pltpu.SemaphoreType.DMA((3 * n,)), pltpu.SemaphoreType.DMA((3 * n,)),
                        pltpu.SemaphoreType.DMA((n,))],
    )(*parts)


def _sum_adamw(parts, w, m, v, name):
    lay, rows, cols = w.shape
    tr = _row_tile(rows)

    def body(p_ref, w_ref, m_ref, v_ref, g_ref, d_ref, nm_ref, nv_ref):
        g = ((p_ref[0].astype(F32) + p_ref[1].astype(F32)) + p_ref[2].astype(F32)) + p_ref[3].astype(F32)
        g_ref[...] = g
        nm = ADAM_B1 * m_ref[...] + (1.0 - ADAM_B1) * g
        nv = ADAM_B2 * v_ref[...] + (1.0 - ADAM_B2) * (g * g)
        nm_ref[...] = nm
        nv_ref[...] = nv
        m_hat = nm / (1.0 - ADAM_B1 ** ADAM_STEP)
        v_hat = nv / (1.0 - ADAM_B2 ** ADAM_STEP)
        d_ref[...] = -ADAM_LR * (m_hat / (jnp.sqrt(v_hat) + ADAM_EPS) + ADAM_WD * w_ref[...])

    tile = pl.BlockSpec((None, tr, cols), lambda l, r: (l, r, 0))
    return pl.pallas_call(
        body, name=name, grid=(lay, rows // tr),
        in_specs=[pl.BlockSpec((None, N_CHIP, tr, cols), lambda l, r: (l, 0, r, 0)), tile, tile, tile],
        out_specs=[tile] * 4, out_shape=[jax.ShapeDtypeStruct((lay, rows, cols), F32)] * 4,
        compiler_params=_params(("parallel", "parallel")),
    )(parts, w, m, v)


def _pad_rows(flat, rows):
    return jnp.pad(flat, (0, rows * LANES - flat.shape[0])).reshape(rows, LANES)


def _pack_small(tree):
    sh = jnp.concatenate([tree[n].reshape(-1) for n, _, _ in SMALL_SHARDED])
    rep = jnp.concatenate([tree[n].reshape(-1) for n, _ in REPLICATED])
    return jnp.concatenate([_pad_rows(sh, ROWS_SSH), _pad_rows(rep, ROWS_REP)], axis=0)


def _unpack_small(buf):
    out = {}
    for flat, items in ((buf[:ROWS_SSH].reshape(-1), [(n, s) for n, s, _ in SMALL_SHARDED]),
                        (buf[ROWS_SSH:].reshape(-1), REPLICATED)):
        off = 0
        for n, s in items:
            out[n] = flat[off:off + math.prod(s)].reshape(s)
            off += math.prod(s)
    return out


def _full_from_blocks(blocks, s, ax):
    return jnp.concatenate([blocks[d] for d in range(N_DEV)], axis=ax)


def _blocks_from_full(full, s, ax):
    return jnp.stack([lax.slice_in_dim(full, d * s[ax], (d + 1) * s[ax], axis=ax) for d in range(N_DEV)])


def _pack_small_grads(sharded_blocks, replicated):
    sh = jnp.concatenate([sharded_blocks[n].reshape(N_DEV, -1) for n, _, _ in SMALL_SHARDED], axis=1)
    sh = jnp.pad(sh, ((0, 0), (0, ROWS_SSH * LANES - sh.shape[1]))).reshape(N_DEV, ROWS_SSH, LANES)
    rep = _pad_rows(jnp.concatenate([replicated[n].reshape(-1) for n, _ in REPLICATED]), ROWS_REP)
    return jnp.concatenate([sh, jnp.broadcast_to(rep[None], (N_DEV, ROWS_REP, LANES))], axis=1)


def _layer_fwd(x, h1, wts, big, l, l_next, next_shards, own_ffn):
    L = x.shape[0]
    tm = min(TILE_MM, L)
    nt = L // tm
    nxt = dict(zip(BIG, next_shards)) if next_shards else {}
    if own_ffn:
        at_proj_in, at_ffn_up = {"own_up": own_ffn[0], "own_down": own_ffn[1]}, nxt
    else:
        at_proj_in = {n: nxt[n] for n in ("w_in", "w_out") if n in nxt}
        at_ffn_up = {n: nxt[n] for n in ("w_up", "w_down") if n in nxt}

    def carried(result, comm):
        return (result[0], list(result[1:])) if comm else (result, [])

    comm = _gather_ici_comm(list(at_proj_in.values())) if at_proj_in else None
    p, got = carried(_mm(h1, big["w_in"], dims=NT, grid=(nt, D_INP // 640, 1),
                         a_spec=((tm, D), lambda i, j, k: (i, 0)), b_spec=((None, 640, D), lambda i, j, k: (0, j, 0)),
                         o_spec=((tm, 640), lambda i, j, k: (i, j)), out_shape=jax.ShapeDtypeStruct((L, D_INP), BF16),
                         tile=(tm, 640), name="proj_in", comm=comm), comm)
    stage1 = dict(zip(at_proj_in, got))
    o_f, sp_f = _gla_fwd(p, wts["gpad_f"], wts["bias_f"], l, None, True, "gla_fwd_f")
    o_tot, sp_b = _gla_fwd(p, wts["gpad_b"], wts["bias_b"], l, o_f, False, "gla_fwd_b")
    y_cat = _mixer_out(p, wts["conv_a"], wts["gh"], l, o_tot, "mixer_out")
    comm = _gather_d2d_comm([stage1["own_up"], stage1["own_down"]]) if own_ffn else None
    y, got = carried(_mm(y_cat, big["w_out"], dims=NN, grid=(nt, 1, 1),
                         a_spec=((tm, D), lambda i, j, k: (i, 0)), b_spec=((None, D, D), lambda i, j, k: (0, 0, 0)),
                         o_spec=((tm, D), lambda i, j, k: (i, 0)), out_shape=jax.ShapeDtypeStruct((L, D), BF16),
                         tile=(tm, D), name="proj_out", comm=comm), comm)
    if own_ffn:
        big = dict(big, w_up=got[0], w_down=got[1].reshape(1, FF_HALF, SH_FF, D))
    x1, h2 = _post_pre(x, y, wts["g2"], l, wts["g3"], l, "post_pre_mix")
    comm = _gather_ici_comm(list(at_ffn_up.values())) if at_ffn_up else None
    u8, got = carried(_mm(h2, big["w_up"], dims=NT, grid=(nt, N_DEV, 1),
                          a_spec=((tm, D), lambda i, j, k: (i, 0)),
                          b_spec=((None, None, SH_FF, D), lambda i, j, k: (0, j, 0, 0)),
                          o_spec=((None, tm, SH_FF), lambda i, j, k: (j, i, 0)),
                          out_shape=jax.ShapeDtypeStruct((N_DEV, L, SH_FF), BF16), tile=(tm, SH_FF), name="ffn_up",
                          comm=comm), comm)
    stage1.update(zip(at_ffn_up, got))
    a, conv = _ffn_act(u8, wts["cw"], l, "ffn_act")
    tm1 = min(TILE_MM_KIN, L)
    comm = _gather_d2d_comm([stage1[n] for n in BIG]) if next_shards else None
    y2, gathered = carried(_mm(a, big["w_down"], dims=NN, grid=(L // tm1, 1, 1), kin=FF_HALF,
                               a_spec=((FF_HALF, tm1, SH_FF), lambda i, j, k: (0, i, 0)),
                               b_spec=((None, FF_HALF, SH_FF, D), lambda i, j, k: (0, 0, 0, 0)),
                               o_spec=((tm1, D), lambda i, j, k: (i, 0)),
                               out_shape=jax.ShapeDtypeStruct((L, D), BF16), tile=(tm1, D), name="ffn_down",
                               comm=comm), comm)
    x2, h1_next = _post_pre(x1, y2, wts["g4"], l, wts["g1"], l_next, "post_pre_ffn")
    saved = dict(x=x, h1=h1, p=p, o_tot=o_tot, sp_f=sp_f, sp_b=sp_b, y_cat=y_cat, y=y, x1=x1, h2=h2, u8=u8, a=a, y2=y2,
                 big=big, conv=conv)
    return x2, h1_next, saved, gathered


def _layer_bwd(dx2, wts, s, l, pending, c_idx, early_ffn, post, below_y2):
    L = dx2.shape[0]
    big = s["big"]
    tm = min(TILE_MM, L)
    nt = L // tm
    tm1 = min(TILE_MM_KIN, L)
    tk = min(TILE_MM_TOKENS, L)
    nkt = L // tk
    dy2, dg4 = post if post else _norm_bwd(s["y2"], wts["g4"], l, dx2, None, "norm_bwd_ffn_post")
    da = _mm(dy2, big["w_down"], dims=NT, grid=(nt, FF_HALF, 1),
             a_spec=((tm, D), lambda i, j, k: (i, 0)), b_spec=((None, None, SH_FF, D), lambda i, j, k: (0, j, 0, 0)),
             o_spec=((None, tm, SH_FF), lambda i, j, k: (j, i, 0)),
             out_shape=jax.ShapeDtypeStruct((FF_HALF, L, SH_FF), BF16), tile=(tm, SH_FF), name="ffn_down_dx",
             comm=_grads_d2d_comm(pending) if pending else None)
    pairs = None
    if pending:
        da, from_sibling = da[0], da[1:]
        pairs = [_pair_sum(g, r, c_idx, BF16, "grads_pair_sum") for g, r in zip(pending, from_sibling)]
    dw_down = _mm(s["a"], dy2, dims=TN, grid=(FF_HALF, 1, nkt),
                  a_spec=((None, tk, SH_FF), lambda i, j, k: (i, k, 0)), b_spec=((tk, D), lambda i, j, k: (k, 0)),
                  o_spec=((SH_FF, D), lambda i, j, k: (i, 0)), out_shape=jax.ShapeDtypeStruct((DFF, D), F32),
                  tile=(SH_FF, D), name="ffn_down_dw")
    du = _ffn_act_bwd(s["conv"], da, "ffn_act_bwd")
    d_u8, dcw = _ffn_conv_t(du.reshape(N_DEV, L, SH_FF), s["u8"], wts["cw"], l, "ffn_conv_t")
    dh2 = _mm(d_u8, big["w_up"], dims=NN, grid=(L // tm1, 1, N_DEV // FF_HALF), kin=FF_HALF,
              a_spec=((FF_HALF, tm1, SH_FF), lambda i, j, k: (k, i, 0)),
              b_spec=((None, FF_HALF, SH_FF, D), lambda i, j, k: (0, k, 0, 0)),
              o_spec=((tm1, D), lambda i, j, k: (i, 0)), out_shape=jax.ShapeDtypeStruct((L, D), BF16),
              tile=(tm1, D), name="ffn_up_dx",
              comm=_grads_ici_comm([q for n, q in zip(BIG, pairs) if n != "w_up"]) if pending else None)
    if pending:
        dh2, rest_parts = dh2[0], dh2[1:]
    dw_up = _mm(d_u8, s["h2"], dims=TN, grid=(N_DEV, 1, nkt),
                a_spec=((None, tk, SH_FF), lambda i, j, k: (i, k, 0)), b_spec=((tk, D), lambda i, j, k: (k, 0)),
                o_spec=((None, SH_FF, D), lambda i, j, k: (i, 0, 0)),
                out_shape=jax.ShapeDtypeStruct((N_DEV, SH_FF, D), F32), tile=(SH_FF, D), name="ffn_up_dw",
                comm=_grads_ici_comm([pairs[BIG.index("w_up")]]) if pending else None)
    parts = None
    if pending:
        dw_up, up_part = dw_up[0], dw_up[1]
        parts = [rest_parts[0], rest_parts[1], up_part, rest_parts[2]]
    dx1, dg3, dy, dg2 = _norm_bwd(s["x1"], wts["g3"], l, dh2, dx2, "norm_bwd_ffn_pre_mix_post",
                                  then=(s["y"], wts["g2"], l))
    dy_cat = _mm(dy, big["w_out"], dims=NT, grid=(nt, 1, 1),
                 a_spec=((tm, D), lambda i, j, k: (i, 0)), b_spec=((None, D, D), lambda i, j, k: (0, 0, 0)),
                 o_spec=((tm, D), lambda i, j, k: (i, 0)), out_shape=jax.ShapeDtypeStruct((L, D), BF16),
                 tile=(tm, D), name="proj_out_dx")
    dw_out = _mm(s["y_cat"], dy, dims=TN, grid=(1, 1, nkt),
                 a_spec=((tk, D), lambda i, j, k: (k, 0)), b_spec=((tk, D), lambda i, j, k: (k, 0)),
                 o_spec=((D, D), lambda i, j, k: (0, 0)), out_shape=jax.ShapeDtypeStruct((D, D), F32),
                 tile=(D, D), name="proj_out_dw")
    dgb, dgc, dgv, dgo, d_o, dconv_a, dgh = _mixer_out_bwd(s["p"], wts["conv_a"], wts["gh"], l, s["o_tot"], dy_cat,
                                                          "mixer_out_bwd")
    part_f = _gla_bwd(s["p"], wts["gpad_f"], wts["bias_f"], l, s["sp_f"], d_o, None, True, "gla_bwd_f")
    dp, dgp_b, dbias_b = _gla_bwd(s["p"], wts["gpad_b"], wts["bias_b"], l, s["sp_b"], d_o,
                                  list(part_f[:4]) + [dgb, dgc, dgv, dgo], False, "gla_bwd_b")
    early = [dw_up[None], dw_down.reshape(1, N_DEV, DFF // N_DEV, D)] if early_ffn else None
    dh1 = _mm(dp, big["w_in"], dims=NN, grid=(L // tm1, 1, 1),
              a_spec=((tm1, D_INP), lambda i, j, k: (i, 0)), b_spec=((None, D_INP, D), lambda i, j, k: (0, 0, 0)),
              o_spec=((tm1, D), lambda i, j, k: (i, 0)), out_shape=jax.ShapeDtypeStruct((L, D), BF16),
              tile=(tm1, D), name="proj_in_dx", comm=_grads_d2d_comm(early) if early_ffn else None)
    if early_ffn:
        dh1, early_sibling = dh1[0], dh1[1:]
        early = [_pair_sum(g, r, c_idx, BF16, "grads_pair_sum") for g, r in zip(early, early_sibling)]
    dw_in = _mm(dp, s["h1"], dims=TN, grid=(D_INP // 640, 1, nkt),
                a_spec=((tk, 640), lambda i, j, k: (k, i)), b_spec=((tk, D), lambda i, j, k: (k, 0)),
                o_spec=((640, D), lambda i, j, k: (i, 0)), out_shape=jax.ShapeDtypeStruct((D_INP, D), F32),
                tile=(640, D), name="proj_in_dw", comm=_grads_ici_comm(early) if early_ffn else None)
    if early_ffn:
        dw_in, early = dw_in[0], dw_in[1:]
    if below_y2 is None:
        dx0, dg1 = _norm_bwd(s["x"], wts["g1"], l, dh1, dx1, "norm_bwd_mix_pre")
        post_below = None
    else:
        dx0, dg1, *post_below = _norm_bwd(s["x"], wts["g1"], l, dh1, dx1, "norm_bwd_mix_pre_ffn_post",
                                          then=(below_y2, wts["g4"], l - 1))
    grads = dict(
        norm_mix_pre=dg1[0], norm_mix_post=dg2[0], norm_ffn_pre=dg3[0], norm_ffn_post=dg4[0],
        gate_bias_fwd=part_f[5][0], gate_bias_bwd=dbias_b[0], gla_head_norm=dgh[0],
        w_in=dw_in[:D_IN].reshape(N_DEV, SH_IN, D), w_out=dw_out.reshape(N_DEV, D // N_DEV, D), w_up=dw_up,
        w_down=dw_down.reshape(N_DEV, DFF // N_DEV, D),
        conv_a=_blocks_from_full(dconv_a, (3, DC // N_DEV), 1),
        gate_up_fwd=_blocks_from_full(part_f[4][:RANK], (RANK, DK // N_DEV), 1),
        gate_up_bwd=_blocks_from_full(dgp_b[RANK:2 * RANK], (RANK, DK // N_DEV), 1),
        conv_ffn=dcw.reshape(N_DEV, 3, SH_FF))
    return dx0, grads, parts, early, post_below


def _matmul_weights(g_in, g_out, g_up, g_down):
    w_in = jnp.concatenate([g_in.reshape(1, D_IN, D), jnp.zeros((1, D_INP - D_IN, D), BF16)], axis=1)
    return dict(w_in=w_in, w_out=g_out.reshape(1, D, D), w_up=g_up,
                w_down=None if g_down is None else g_down.reshape(1, FF_HALF, SH_FF, D))


def _small_weights(g_small, rep):
    small = {n: jnp.moveaxis(t, 0, 1) for n, t in jax.vmap(_unpack_small)(
        jnp.concatenate([g_small[0], jnp.zeros((N_DEV, ROWS_REP, LANES), F32)], axis=1)).items()
        if n in [s[0] for s in SMALL_SHARDED]}
    conv_a = jnp.concatenate([small["conv_a"][:, d] for d in range(N_DEV)], axis=2)
    gate_f = jnp.concatenate([small["gate_up_fwd"][:, d] for d in range(N_DEV)], axis=2).astype(BF16)
    gate_b = jnp.concatenate([small["gate_up_bwd"][:, d] for d in range(N_DEV)], axis=2).astype(BF16)
    zeros = jnp.zeros((DEPTH, LR_BLK, DK), BF16)
    return dict(
        conv_a=conv_a, cw=small["conv_ffn"],
        gpad_f=zeros.at[:, :RANK].set(gate_f), gpad_b=zeros.at[:, RANK:2 * RANK].set(gate_b),
        bias_f=rep["gate_bias_fwd"][:, None, :], bias_b=rep["gate_bias_bwd"][:, None, :],
        gh=rep["gla_head_norm"][:, None, :],
        g1=rep["norm_mix_pre"][:, None, :], g2=rep["norm_mix_post"][:, None, :],
        g3=rep["norm_ffn_pre"][:, None, :], g4=rep["norm_ffn_post"][:, None, :])


def kernel(x, norm_mix_pre, norm_mix_post, norm_ffn_pre, norm_ffn_post, w_in, conv_a, gate_up_fwd, gate_bias_fwd, gate_up_bwd, gate_bias_bwd, gla_head_norm, w_out, w_up, conv_ffn, w_down, loss_target, m_norm_mix_pre, m_norm_mix_post, m_norm_ffn_pre, m_norm_ffn_post, m_w_in, m_conv_a, m_gate_up_fwd, m_gate_bias_fwd, m_gate_up_bwd, m_gate_bias_bwd, m_gla_head_norm, m_w_out, m_w_up, m_conv_ffn, m_w_down, v_norm_mix_pre, v_norm_mix_post, v_norm_ffn_pre, v_norm_ffn_post, v_w_in, v_conv_a, v_gate_up_fwd, v_gate_bias_fwd, v_gate_up_bwd, v_gate_bias_bwd, v_gla_head_norm, v_w_out, v_w_up, v_conv_ffn, v_w_down):
    w = dict(norm_mix_pre=norm_mix_pre, norm_mix_post=norm_mix_post, norm_ffn_pre=norm_ffn_pre,
             norm_ffn_post=norm_ffn_post, w_in=w_in, conv_a=conv_a, gate_up_fwd=gate_up_fwd,
             gate_bias_fwd=gate_bias_fwd, gate_up_bwd=gate_up_bwd, gate_bias_bwd=gate_bias_bwd,
             gla_head_norm=gla_head_norm, w_out=w_out, w_up=w_up, conv_ffn=conv_ffn, w_down=w_down)
    m = dict(norm_mix_pre=m_norm_mix_pre, norm_mix_post=m_norm_mix_post, norm_ffn_pre=m_norm_ffn_pre,
             norm_ffn_post=m_norm_ffn_post, w_in=m_w_in, conv_a=m_conv_a, gate_up_fwd=m_gate_up_fwd,
             gate_bias_fwd=m_gate_bias_fwd, gate_up_bwd=m_gate_up_bwd, gate_bias_bwd=m_gate_bias_bwd,
             gla_head_norm=m_gla_head_norm, w_out=m_w_out, w_up=m_w_up, conv_ffn=m_conv_ffn, w_down=m_w_down)
    v = dict(norm_mix_pre=v_norm_mix_pre, norm_mix_post=v_norm_mix_post, norm_ffn_pre=v_norm_ffn_pre,
             norm_ffn_post=v_norm_ffn_post, w_in=v_w_in, conv_a=v_conv_a, gate_up_fwd=v_gate_up_fwd,
             gate_bias_fwd=v_gate_bias_fwd, gate_up_bwd=v_gate_up_bwd, gate_bias_bwd=v_gate_bias_bwd,
             gla_head_norm=v_gla_head_norm, w_out=v_w_out, w_up=v_w_up, conv_ffn=v_conv_ffn, w_down=v_w_down)
    axes = ("x", "y", "c")
    L = x.shape[1]
    x0 = x.reshape(L, D)
    target = loss_target.reshape(L, D)

    def rows_major(tree):
        return {n: jnp.swapaxes(tree[n], 1, 2) if n in TRANSPOSED else tree[n] for n in BIG}

    w_small = _pack_small(w)
    wt, mt, vt = rows_major(w), rows_major(m), rows_major(v)
    w16 = {n: wt[n].astype(BF16) for n in BIG}
    g_in, g_out, g_small = _all_gather([w16["w_in"][0:1], w16["w_out"][0:1], w_small[None, :ROWS_SSH]],
                                       "gather_weights")
    wts = _small_weights(g_small, w)
    big = {n: t for n, t in _matmul_weights(g_in, g_out, None, None).items() if t is not None}

    h1 = _norm_cast(x0, wts["g1"], 0, "norm_first")
    xl, saved = x0, []
    for l in range(DEPTH):
        nxt = [w16[n][l + 1:l + 2] for n in BIG] if l + 1 < DEPTH else None
        own = [w16["w_up"][0:1], w16["w_down"][0:1]] if l == 0 else None
        xl, h1, s, gathered = _layer_fwd(xl, h1, wts, big, l, min(l + 1, DEPTH - 1), nxt, own)
        saved.append(s)
        if nxt:
            big = _matmul_weights(*gathered)
    dx, sq, *post = _loss_grad(xl, target, saved[-1]["y2"], wts["g4"], DEPTH - 1, "loss_grad")
    loss = lax.psum(0.5 * jnp.sum(sq) / D, axes)

    c_idx = lax.axis_index("c").astype(jnp.int32).reshape(1)
    layer_grads, layer_parts, pending = [None] * DEPTH, [None] * DEPTH, None
    for l in reversed(range(DEPTH)):
        dx, layer_grads[l], done, early, post = _layer_bwd(
            dx, wts, saved[l], l, pending, c_idx, l == 0, post, saved[l - 1]["y2"] if l > 0 else None)
        if pending:
            layer_parts[l + 1] = done
        pending = [layer_grads[l][n][None] for n in BIG]
    small_names = [n for n, _, _ in SMALL_SHARDED] + [n for n, _ in REPLICATED]
    stacked = {n: jnp.stack([g[n] for g in layer_grads]) for n in small_names}
    g_small = _pack_small_grads({n: jnp.moveaxis(stacked[n], 0, 1) for n, _, _ in SMALL_SHARDED}, stacked)
    last = [g for n, g in zip(BIG, pending) if n in ("w_in", "w_out")] + [g_small[None]]
    from_sibling = _exchange_sibling(last, "grads_to_sibling")
    pairs = [_pair_sum(g, r, c_idx, BF16 if i < 2 else F32, "grads_pair_sum")
             for i, (g, r) in enumerate(zip(last, from_sibling))]
    parts = _exchange_chips(pairs, "grads_to_chips")
    layer_parts[0] = [parts[0], parts[1], early[0], early[1]]

    results = {}
    for i, n in enumerate(BIG):
        part = jnp.concatenate([layer_parts[l][i] for l in range(DEPTH)], axis=0)
        results[n] = _sum_adamw(part, wt[n], mt[n], vt[n], "sum_adamw")
        if n in TRANSPOSED:
            results[n] = [jnp.swapaxes(r, 1, 2) for r in results[n]]
    small = _sum_adamw(parts[-1], w_small[None], _pack_small(m)[None], _pack_small(v)[None], "sum_adamw_small")
    small = [_unpack_small(buf[0]) for buf in small]
    outs = [loss, dx.reshape(x.shape)]
    for i in range(4):
        outs += [results[n][i] if n in BIG else small[i][n] for n in WEIGHT_ORDER]
    return tuple(outs)
```

```python
import math
from typing import Callable, NamedTuple

import jax
import jax.numpy as jnp
from jax import lax
from jax.experimental import pallas as pl
from jax.experimental.pallas import tpu as pltpu

F32 = jnp.float32
BF16 = jnp.bfloat16

DEPTH = 4
D = 1024
DC = 512
DG = 512
HEADS = 4
HV = 128
HK = 64
DK = 256
RANK = 16
CH = 64
DFF = 2816
D_IN = 3104
D_INP = 3200
LR_BLK = 128
EPS = 1e-6
HB = 16
N_DEV = 8
N_CHIP = 4
LANES = 1024
SH_IN = D_IN // N_DEV
SH_FF = 2 * DFF // N_DEV
FF_HALF = N_DEV // 2

ADAM_LR, ADAM_B1, ADAM_B2, ADAM_EPS, ADAM_WD, ADAM_STEP = 0.001, 0.9, 0.999, 1e-08, 0.01, 10

VMEM_LIMIT = 48 * 1024 * 1024
TILE_TOKENS = 512
TILE_GLA = 512
TILE_FFN = 1024
TILE_NORM = 512
TILE_MM = 2048
TILE_IN = D_INP // 5
TILE_MM_KIN = 1024
TILE_MM_TOKENS = 4096

COL_GB, COL_GC, COL_GV = 0, 1, 2
COL_Q, COL_K = 6, 7
COL_V, COL_GO = 4, 5
COL_LR = 24

BIG = ("w_in", "w_out", "w_up", "w_down")
TRANSPOSED = ("w_in", "w_up")
SMALL_SHARDED = (
    ("conv_a", (DEPTH, 3, DC // N_DEV), 2),
    ("gate_up_fwd", (DEPTH, RANK, DK // N_DEV), 2),
    ("gate_up_bwd", (DEPTH, RANK, DK // N_DEV), 2),
    ("conv_ffn", (DEPTH, 3, SH_FF), 2),
)
REPLICATED = (
    ("norm_mix_pre", (DEPTH, D)), ("norm_mix_post", (DEPTH, D)), ("norm_ffn_pre", (DEPTH, D)),
    ("norm_ffn_post", (DEPTH, D)), ("gate_bias_fwd", (DEPTH, DK)), ("gate_bias_bwd", (DEPTH, DK)),
    ("gla_head_norm", (DEPTH, HV)),
)
WEIGHT_ORDER = ("norm_mix_pre", "norm_mix_post", "norm_ffn_pre", "norm_ffn_post", "w_in", "conv_a", "gate_up_fwd",
                "gate_bias_fwd", "gate_up_bwd", "gate_bias_bwd", "gla_head_norm", "w_out", "w_up", "conv_ffn", "w_down")


def _rows_for(n_elems):
    return (-(-n_elems // LANES) + 7) // 8 * 8


ROWS_SSH = _rows_for(sum(math.prod(s) for _, s, _ in SMALL_SHARDED))
ROWS_REP = _rows_for(sum(math.prod(s) for _, s in REPLICATED))
ROWS_SMALL = ROWS_SSH + ROWS_REP


def _params(sem):
    return pltpu.CompilerParams(dimension_semantics=sem, vmem_limit_bytes=VMEM_LIMIT)


def _silu_parts(x):
    s = 1.0 / (1.0 + jnp.exp(-x))
    return x * s, s


def _rstd(xf):
    return lax.rsqrt(jnp.mean(xf * xf, axis=-1, keepdims=True) + EPS)


NN, NT, TN = ((1,), (0,)), ((1,), (1,)), ((0,), (0,))


def _dot(a, b, dims):
    return lax.dot_general(a, b, (dims, ((), ())), preferred_element_type=F32)


def _mm(a, b, *, dims, grid, a_spec, b_spec, o_spec, out_shape, tile, name, kin=0, comm=None):
    nk = grid[2]
    n_ci = len(comm.ins) if comm else 0
    n_co = len(comm.out_shapes) if comm else 0

    def body(*refs):
        a_ref, b_ref = refs[:2]
        ci = refs[2:2 + n_ci]
        o_ref = refs[2 + n_ci]
        co = refs[3 + n_ci:3 + n_ci + n_co]
        rest = refs[3 + n_ci + n_co:]
        if comm:
            sems, rest = rest[:3], rest[3:]
            step = (pl.program_id(0) * grid[1] + pl.program_id(1)) * grid[2] + pl.program_id(2)

            @pl.when(step == 0)
            def _():
                comm.start(ci, co, *sems)

        if kin:
            prod = _dot(a_ref[0], b_ref[0], dims)
            for d in range(1, kin):
                prod = prod + _dot(a_ref[d], b_ref[d], dims)
        else:
            prod = _dot(a_ref[...], b_ref[...], dims)
        if nk == 1:
            o_ref[...] = prod.astype(o_ref.dtype)
        else:
            acc_ref = rest[0]
            k = pl.program_id(2)

            @pl.when(k == 0)
            def _():
                acc_ref[...] = prod

            @pl.when(k > 0)
            def _():
                acc_ref[...] += prod

            @pl.when(k == nk - 1)
            def _():
                o_ref[...] = acc_ref[...].astype(o_ref.dtype)

        if comm:
            @pl.when(step == grid[0] * grid[1] * grid[2] - 1)
            def _():
                comm.wait(ci, co, *sems)

    acc = [pltpu.VMEM(tile, F32)] if nk > 1 else []
    if not comm:
        return pl.pallas_call(
            body, name=name, grid=grid, in_specs=[pl.BlockSpec(*a_spec), pl.BlockSpec(*b_spec)],
            out_specs=pl.BlockSpec(*o_spec), out_shape=out_shape, scratch_shapes=acc,
            compiler_params=_params(("parallel", "parallel", "arbitrary")),
        )(a, b)
    return pl.pallas_call(
        body, name=name, grid=grid,
        in_specs=[pl.BlockSpec(*a_spec), pl.BlockSpec(*b_spec)] + [HBM_SPEC] * n_ci,
        out_specs=[pl.BlockSpec(*o_spec)] + [HBM_SPEC] * n_co, out_shape=[out_shape] + list(comm.out_shapes),
        scratch_shapes=[pltpu.SemaphoreType.DMA((s,)) for s in comm.sems] + acc,
        input_output_aliases={2 + i: 1 + o for i, o in comm.aliases.items()},
        compiler_params=_params(("arbitrary", "arbitrary", "arbitrary")),
    )(a, b, *comm.ins)


def _halo_maps(tm, n_rows):
    r, last = tm // HB, n_rows // HB - 1
    return (lambda i: jnp.maximum(i * r - 1, 0)), (lambda i: jnp.minimum((i + 1) * r, last))


def _shift(x, prev_blk, next_blk):
    tm = x.shape[0]
    xs = jnp.concatenate([prev_blk, x, next_blk], axis=0)
    n = xs.shape[0]
    down = pltpu.roll(xs, 1, 0)[HB:HB + tm]
    up = pltpu.roll(xs, n - 1, 0)[HB:HB + tm]
    return down, up


def _edge_scales(i, n):
    return jnp.where(i > 0, 1.0, 0.0).astype(F32), jnp.where(i < n - 1, 1.0, 0.0).astype(F32)


def _gain_spec(l, width=D):
    return pl.BlockSpec((None, 1, width), lambda *_: (l, 0, 0))


def _norm_cast(x, g, l, name):
    L = x.shape[0]
    tm = min(TILE_NORM, L)

    def body(x_ref, g_ref, o_ref):
        xf = x_ref[...]
        o_ref[...] = (xf * _rstd(xf) * g_ref[...]).astype(BF16)

    return pl.pallas_call(
        body, name=name, grid=(L // tm,), in_specs=[pl.BlockSpec((tm, D), lambda i: (i, 0)), _gain_spec(l)],
        out_specs=pl.BlockSpec((tm, D), lambda i: (i, 0)), out_shape=jax.ShapeDtypeStruct((L, D), BF16),
        compiler_params=_params(("parallel",)),
    )(x, g)


def _post_pre(x, y, g_post, l_post, g_pre, l_pre, name):
    L = x.shape[0]
    tm = min(TILE_NORM, L)

    def body(x_ref, y_ref, gp_ref, gn_ref, x1_ref, h_ref):
        yf = y_ref[...].astype(F32)
        x1 = x_ref[...] + yf * _rstd(yf) * gp_ref[...]
        x1_ref[...] = x1
        h_ref[...] = (x1 * _rstd(x1) * gn_ref[...]).astype(BF16)

    tile = pl.BlockSpec((tm, D), lambda i: (i, 0))
    return pl.pallas_call(
        body, name=name, grid=(L // tm,), in_specs=[tile, tile, _gain_spec(l_post), _gain_spec(l_pre)],
        out_specs=[tile, tile],
        out_shape=[jax.ShapeDtypeStruct((L, D), F32), jax.ShapeDtypeStruct((L, D), BF16)],
        compiler_params=_params(("parallel",)),
    )(x, y, g_post, g_pre)


def _norm_vjp(y_ref, g_ref, do):
    y = y_ref[...].astype(F32)
    r = _rstd(y)
    z = do * g_ref[...]
    return r * z - y * (r * r * r) * jnp.mean(y * z, axis=-1, keepdims=True), jnp.sum(do * y * r, axis=0, keepdims=True)


def _accumulate_rows(i, parts):
    @pl.when(i == 0)
    def _():
        for ref, val in parts:
            ref[...] = val

    @pl.when(i > 0)
    def _():
        for ref, val in parts:
            ref[...] += val


def _norm_bwd(yin, g, l, dout, dres, name, then=None):
    L = yin.shape[0]
    tm = min(TILE_NORM, L)
    with_res = dres is not None
    norm_vjp = _norm_vjp

    def body(*refs):
        y_ref, g_ref, do_ref = refs[:3]
        refs = refs[3:]
        if with_res:
            dr_ref, refs = refs[0], refs[1:]
        if then:
            y2_ref, g2_ref, refs = refs[0], refs[1], refs[2:]
        din_ref, dg_ref = refs[:2]
        i = pl.program_id(0)
        din, part = norm_vjp(y_ref, g_ref, do_ref[...].astype(F32))
        if with_res:
            din = din + dr_ref[...]
        din_ref[...] = din.astype(din_ref.dtype)
        parts = [(dg_ref, part)]
        if then:
            d2, part2 = norm_vjp(y2_ref, g2_ref, din)
            refs[2][...] = d2.astype(BF16)
            parts.append((refs[3], part2))
        _accumulate_rows(i, parts)

    tile = pl.BlockSpec((tm, D), lambda i: (i, 0))
    gain_out = pl.BlockSpec((1, D), lambda i: (0, 0))
    args = (yin, g, dout) + ((dres,) if with_res else ()) + ((then[0], then[1]) if then else ())
    return pl.pallas_call(
        body, name=name, grid=(L // tm,),
        in_specs=[tile, _gain_spec(l), tile] + ([tile] if with_res else []) + ([tile, _gain_spec(then[2])] if then else []),
        out_specs=[tile, gain_out] + ([tile, gain_out] if then else []),
        out_shape=[jax.ShapeDtypeStruct((L, D), F32 if with_res else BF16), jax.ShapeDtypeStruct((1, D), F32)]
        + ([jax.ShapeDtypeStruct((L, D), BF16), jax.ShapeDtypeStruct((1, D), F32)] if then else []),
        compiler_params=_params(("arbitrary",)),
    )(*args)


GLA_SUB = 256


def _gla_consts(fwd, tb):
    sub = min(GLA_SUB, tb)
    row = lax.broadcasted_iota(jnp.int32, (sub, sub), 0)
    col = lax.broadcasted_iota(jnp.int32, (sub, sub), 1)
    same = (row // CH) == (col // CH)
    tri = same & ((col <= row) if fwd else (col >= row))
    tri_t = same & ((col >= row) if fwd else (col <= row))
    row_st = lax.broadcasted_iota(jnp.int32, (HEADS * CH, CH), 0) & (CH - 1)
    col_st = lax.broadcasted_iota(jnp.int32, (HEADS * CH, CH), 1)
    tri_st = (col_st <= row_st) if fwd else (col_st >= row_st)
    lane_head = lax.broadcasted_iota(jnp.int32, (1, DK), 1) // HK
    head_masks = [lane_head == h for h in range(HEADS)]
    srow = lax.broadcasted_iota(jnp.int32, (DG, DK), 0) // HV
    scol = lax.broadcasted_iota(jnp.int32, (DG, DK), 1) // HK
    return tri.astype(BF16), tri_t.astype(BF16), tri_st, head_masks, srow == scol


def _dot_hilo(tri_b, x):
    hi = x.astype(BF16)
    lo = (x - hi.astype(F32)).astype(BF16)
    sub = tri_b.shape[0]
    return jnp.concatenate([_dot(tri_b, hi[r:r + sub], NN) + _dot(tri_b, lo[r:r + sub], NN)
                            for r in range(0, x.shape[0], sub)], axis=0)


def _gla_block_terms(q_ref, k_ref, lr_ref, gp_ref, bias_ref, tri_b):
    pre = _dot(lr_ref[...], gp_ref[...], NN) + bias_ref[...]
    sig_neg = 1.0 / (1.0 + jnp.exp(pre))
    a = (jnp.minimum(pre, 0.0) - jnp.log(1.0 + jnp.exp(-jnp.abs(pre)))) * (1.0 / 16.0)
    cum = _dot_hilo(tri_b, a)
    cl = jnp.concatenate([jnp.broadcast_to(jnp.min(cum[r:r + CH], axis=0, keepdims=True), (CH, DK))
                          for r in range(0, cum.shape[0], CH)], axis=0)
    e = jnp.exp(cum)
    einv = jnp.exp(-cum)
    eout = jnp.exp(cl - cum)
    q_in = q_ref[...].astype(F32) * e * (HK ** -0.5)
    k = k_ref[...].astype(F32)
    return dict(sig_neg=sig_neg, e=e, einv=einv, eout=eout, decay=jnp.exp(cl), q_in=q_in, k_in=k * einv,
                k_out=k * eout)


def _gla_chunk(t, c, head_masks):
    sl = slice(c * CH, (c + 1) * CH)
    tc = {n: x[sl] for n, x in t.items() if n != "decay"}
    tc["decay"] = jnp.max(t["decay"][c * CH:c * CH + 8], axis=0, keepdims=True)
    tc["q_st"] = jnp.concatenate([jnp.where(mh, tc["q_in"], 0.0) for mh in head_masks], axis=0).astype(BF16)
    return tc


def _gate_specs(l):
    return [pl.BlockSpec((None, LR_BLK, DK), lambda i: (l, 0, 0)), pl.BlockSpec((None, 1, DK), lambda i: (l, 0, 0))]


def _gla_fwd_both(p, gpads, biases, l, name):
    L = p.shape[0]
    tb = min(TILE_GLA, L)
    nb, ncb, nch = L // tb, tb // CH, L // CH

    def body(*refs):
        ins, outs, states = refs[:12], refs[12:16], refs[16:]
        i = pl.program_id(0)

        @pl.when(i == 0)
        def _():
            for s_ref in states:
                s_ref[...] = jnp.zeros_like(s_ref)

        work = []
        for d, fwd in enumerate((True, False)):
            q_ref, k_ref, v_ref, lr_ref, gp_ref, bias_ref = ins[6 * d:6 * d + 6]
            tri_b, _, tri_st, head_masks, blockmask = _gla_consts(fwd, tb)
            terms = _gla_block_terms(q_ref, k_ref, lr_ref, gp_ref, bias_ref, tri_b)
            order = list(range(ncb)) if fwd else list(reversed(range(ncb)))
            work.append((order, terms, tri_st, head_masks, blockmask, v_ref, outs[d], outs[2 + d], states[d]))
        for step in range(ncb):
            for order, terms, tri_st, head_masks, blockmask, v_ref, o_ref, sp_ref, s_ref in work:
                c = order[step]
                rows = pl.ds(c * CH, CH)
                t = _gla_chunk(terms, c, head_masks)
                v = v_ref[rows, :]
                scores = _dot(t["q_st"], t["k_in"].astype(BF16), NT)
                a_st = jnp.where(tri_st, scores, 0.0).astype(BF16)
                r = _dot(a_st, v, NN)
                o_intra = jnp.concatenate([r[h * CH:(h + 1) * CH, h * HV:(h + 1) * HV] for h in range(HEADS)], axis=1)
                s_b = s_ref[...].astype(BF16)
                sp_ref[c] = s_b
                o_ref[rows, :] = o_intra + _dot(t["q_in"].astype(BF16), s_b, NT)
                kv_t = _dot(v, t["k_out"].astype(BF16), TN)
                s_ref[...] = s_ref[...] * t["decay"] + jnp.where(blockmask, kv_t, 0.0)

    in_specs, args = [], []
    for blk, gpad, bias in (((lambda i: i), gpads[0], biases[0]), ((lambda i: nb - 1 - i), gpads[1], biases[1])):
        def col(width, c, blk=blk):
            return pl.BlockSpec((tb, width), lambda i: (blk(i), c))
        in_specs += [col(DK, COL_Q), col(DK, COL_K), col(DG, COL_V), col(LR_BLK, COL_LR)] + _gate_specs(l)
        args += [p, p, p, p, gpad, bias]
    o_specs = [pl.BlockSpec((tb, DG), lambda i: (i, 0)), pl.BlockSpec((tb, DG), lambda i: (nb - 1 - i, 0)),
               pl.BlockSpec((ncb, DG, DK), lambda i: (i, 0, 0)), pl.BlockSpec((ncb, DG, DK), lambda i: (nb - 1 - i, 0, 0))]
    return pl.pallas_call(
        body, name=name, grid=(nb,), in_specs=in_specs, out_specs=o_specs,
        out_shape=[jax.ShapeDtypeStruct((L, DG), F32)] * 2 + [jax.ShapeDtypeStruct((nch, DG, DK), BF16)] * 2,
        scratch_shapes=[pltpu.VMEM((DG, DK), F32)] * 2,
        compiler_params=_params(("arbitrary",)),
    )(*args)


P_COLS = dict(gb=(0, DC), gc=(DC, DC), gv=(2 * DC, DC), q=(3 * DC, DK), k=(3 * DC + DK, DK), v=(3 * DC + 2 * DK, DG),
              go=(3 * DC + 2 * DK + DG, DG), lr=(3 * DC + 2 * DK + 2 * DG, LR_BLK))


def _gla_bwd(p, gpad, bias, l, sprev, d_o, prev, fwd, name):
    L = p.shape[0]
    tb = min(TILE_GLA, L)
    nb, ncb = L // tb, tb // CH
    blk = (lambda i: nb - 1 - i) if fwd else (lambda i: i)
    with_prev = prev is not None

    def body(*refs):
        q_ref, k_ref, v_ref, lr_ref, gp_ref, bias_ref, sp_ref, do_ref = refs[:8]
        rest = refs[8:]
        if with_prev:
            pq_ref, pk_ref, pv_ref, plr_ref, dgb_ref, dgc_ref, dgv_ref, dgo_ref = rest[:8]
            dp_ref, dg_ref, db_ref, ds_ref = rest[8:]

            def put(what, rows, val):
                c0, width = P_COLS[what]
                dp_ref[rows, c0:c0 + width] = val

            for what, ref in (("gb", dgb_ref), ("gc", dgc_ref), ("gv", dgv_ref), ("go", dgo_ref)):
                put(what, slice(None), ref[...])
        else:
            dq_ref, dk_ref, dv_ref, dlr_ref, dg_ref, db_ref, ds_ref = rest
            out_of = dict(q=dq_ref, k=dk_ref, v=dv_ref, lr=dlr_ref)

            def put(what, rows, val):
                out_of[what][rows, :] = val

        i = pl.program_id(0)

        @pl.when(i == 0)
        def _():
            ds_ref[...] = jnp.zeros_like(ds_ref)
            dg_ref[...] = jnp.zeros_like(dg_ref)
            db_ref[...] = jnp.zeros_like(db_ref)

        tri_b, tri_t_b, tri_st, head_masks, blockmask = _gla_consts(fwd, tb)
        terms = _gla_block_terms(q_ref, k_ref, lr_ref, gp_ref, bias_ref, tri_b)
        dcum_of, dcl_of = [None] * ncb, [None] * ncb
        for c in (reversed(range(ncb)) if fwd else range(ncb)):
            rows = pl.ds(c * CH, CH)
            t = _gla_chunk(terms, c, head_masks)
            v = v_ref[rows, :]
            do = do_ref[rows, :]
            q_in, k_in, k_out = t["q_in"], t["k_in"], t["k_out"]
            q_b, k_in_b, k_out_b = q_in.astype(BF16), k_in.astype(BF16), k_out.astype(BF16)
            scores = _dot(t["q_st"], k_in_b, NT)
            a_st = jnp.where(tri_st, scores, 0.0).astype(BF16)
            s_prev = sp_ref[c]
            ds = ds_ref[...]
            ds_b = ds.astype(BF16)

            da_heads = [_dot(do[:, h * HV:(h + 1) * HV], v[:, h * HV:(h + 1) * HV], NT) for h in range(HEADS)]
            da_st = jnp.where(tri_st, jnp.concatenate(da_heads, axis=0), 0.0).astype(BF16)

            dv_heads = [_dot(a_st[h * CH:(h + 1) * CH, :], do[:, h * HV:(h + 1) * HV], TN) for h in range(HEADS)]
            dv = jnp.concatenate(dv_heads, axis=1) + _dot(k_out_b, ds_b, NT)

            x = _dot(da_st, k_in_b, NN)
            dq_in = _dot(do, s_prev, NN)
            for h in range(HEADS):
                dq_in = dq_in + jnp.where(head_masks[h], x[h * CH:(h + 1) * CH, :], 0.0)
            dk_in = _dot(da_st, t["q_st"], TN)
            dk_out = _dot(v, ds_b, NN)
            d_decay = jnp.sum(ds * s_prev.astype(F32), axis=0, keepdims=True)
            ds_ref[...] = ds * t["decay"] + jnp.where(blockmask, _dot(do, q_b, TN), 0.0)

            dq = dq_in * t["e"] * (HK ** -0.5)
            dk = dk_in * t["einv"] + dk_out * t["eout"]
            dko_ko = dk_out * k_out
            dcum_of[c] = dq_in * q_in - dk_in * k_in - dko_ko
            dcl = jnp.sum(dko_ko, axis=0, keepdims=True) + d_decay * t["decay"]
            dcl_of[c] = jnp.broadcast_to(dcl, (CH, DK))
            if with_prev:
                dq = dq + pq_ref[rows, :].astype(F32)
                dk = dk + pk_ref[rows, :].astype(F32)
                dv = dv + pv_ref[rows, :].astype(F32)
            put("q", rows, dq.astype(BF16))
            put("k", rows, dk.astype(BF16))
            put("v", rows, dv.astype(BF16))

        da = _dot_hilo(tri_t_b, jnp.concatenate(dcum_of, axis=0)) + jnp.concatenate(dcl_of, axis=0)
        dpre = da * terms["sig_neg"] * (1.0 / 16.0)
        dpre_b = dpre.astype(BF16)
        dlr = _dot(dpre_b, gp_ref[...], NT)
        dg_ref[...] += _dot(lr_ref[...], dpre_b, TN)
        db_ref[...] += jnp.sum(dpre, axis=0, keepdims=True)
        if with_prev:
            dlr = dlr + plr_ref[...].astype(F32)
        put("lr", slice(None), dlr.astype(BF16))

    def col(width, c):
        return pl.BlockSpec((tb, width), lambda i: (blk(i), c))

    in_specs = [col(DK, COL_Q), col(DK, COL_K), col(DG, COL_V), col(LR_BLK, COL_LR)] + _gate_specs(l) + [
        pl.BlockSpec((ncb, DG, DK), lambda i: (blk(i), 0, 0)), col(DG, 0)]
    args = [p, p, p, p, gpad, bias, sprev, d_o]
    tiles = [col(DK, 0), col(DK, 0), col(DG, 0), col(LR_BLK, 0)]
    shapes = [jax.ShapeDtypeStruct((L, DK), BF16), jax.ShapeDtypeStruct((L, DK), BF16),
              jax.ShapeDtypeStruct((L, DG), BF16), jax.ShapeDtypeStruct((L, LR_BLK), BF16)]
    if with_prev:
        in_specs += tiles + [col(DC, 0)] * 4
        args += list(prev)
        tiles, shapes = [col(D_INP, 0)], [jax.ShapeDtypeStruct((L, D_INP), BF16)]
    return pl.pallas_call(
        body, name=name, grid=(nb,), in_specs=in_specs,
        out_specs=tiles + [pl.BlockSpec((LR_BLK, DK), lambda i: (0, 0)), pl.BlockSpec((1, DK), lambda i: (0, 0))],
        out_shape=shapes + [jax.ShapeDtypeStruct((LR_BLK, DK), F32), jax.ShapeDtypeStruct((1, DK), F32)],
        scratch_shapes=[pltpu.VMEM((DG, DK), F32)],
        compiler_params=_params(("arbitrary",)),
    )(*args)


def _mixer_out(p, conv_a, gh, l, o_f, o_b, name):
    L = p.shape[0]
    tm = min(TILE_TOKENS, L)
    n = L // tm
    pmap, nmap = _halo_maps(tm, L)

    def body(gb_ref, gc_ref, gcp_ref, gcn_ref, gv_ref, gvp_ref, gvn_ref, go_ref, cw_ref, of_ref, ob_ref, gh_ref, y_ref,
             o_ref):
        o_ref[...] = of_ref[...] + ob_ref[...]
        ps, ns = _edge_scales(pl.program_id(0), n)
        z = gc_ref[...].astype(F32) * gv_ref[...].astype(F32)
        zp = gcp_ref[...].astype(F32) * gvp_ref[...].astype(F32) * ps
        zn = gcn_ref[...].astype(F32) * gvn_ref[...].astype(F32) * ns
        z_dn, z_up = _shift(z, zp, zn)
        conv = cw_ref[0:1, :] * z_dn + cw_ref[1:2, :] * z + cw_ref[2:3, :] * z_up
        y_ref[:, 0:DC] = (gb_ref[...].astype(F32) * conv).astype(BF16)
        o = o_ref[...]
        go = go_ref[...].astype(F32)
        for h in range(HEADS):
            oh = o[:, h * HV:(h + 1) * HV]
            on = oh * _rstd(oh) * gh_ref[...]
            act, _ = _silu_parts(go[:, h * HV:(h + 1) * HV])
            y_ref[:, DC + h * HV:DC + (h + 1) * HV] = (act * on).astype(BF16)

    def main(c):
        return pl.BlockSpec((tm, DC), lambda i: (i, c))

    def halo(c, imap):
        return pl.BlockSpec((HB, DC), lambda i: (imap(i), c))

    return pl.pallas_call(
        body, name=name, grid=(n,),
        in_specs=[main(COL_GB), main(COL_GC), halo(COL_GC, pmap), halo(COL_GC, nmap),
                  main(COL_GV), halo(COL_GV, pmap), halo(COL_GV, nmap), main(COL_GO),
                  pl.BlockSpec((None, 3, DC), lambda i: (l, 0, 0)), pl.BlockSpec((tm, DG), lambda i: (i, 0)),
                  pl.BlockSpec((tm, DG), lambda i: (i, 0)), _gain_spec(l, HV)],
        out_specs=[pl.BlockSpec((tm, D), lambda i: (i, 0)), pl.BlockSpec((tm, DG), lambda i: (i, 0))],
        out_shape=[jax.ShapeDtypeStruct((L, D), BF16), jax.ShapeDtypeStruct((L, DG), F32)],
        compiler_params=_params(("parallel",)),
    )(p, p, p, p, p, p, p, p, conv_a, o_f, o_b, gh)


def _mixer_out_bwd(p, conv_a, gh, l, o_tot, dy, name):
    L = p.shape[0]
    tm = min(TILE_TOKENS, L)
    n = L // tm
    pmap, nmap = _halo_maps(tm, L)

    def body(gb_ref, gbp_ref, gbn_ref, gc_ref, gcp_ref, gcn_ref, gv_ref, gvp_ref, gvn_ref, go_ref, cw_ref, o_ref,
             gh_ref, dy_ref, dyp_ref, dyn_ref, dgb_ref, dgc_ref, dgv_ref, dgo_ref, do_ref, dcw_ref, dgh_ref):
        i = pl.program_id(0)
        ps, ns = _edge_scales(i, n)
        gb = gb_ref[...].astype(F32)
        gc = gc_ref[...].astype(F32)
        gv = gv_ref[...].astype(F32)
        z = gc * gv
        zp = gcp_ref[...].astype(F32) * gvp_ref[...].astype(F32) * ps
        zn = gcn_ref[...].astype(F32) * gvn_ref[...].astype(F32) * ns
        z_dn, z_up = _shift(z, zp, zn)
        w0, w1, w2 = cw_ref[0:1, :], cw_ref[1:2, :], cw_ref[2:3, :]
        conv = w0 * z_dn + w1 * z + w2 * z_up
        dya = dy_ref[:, 0:DC].astype(F32)
        dgb_ref[...] = (dya * conv).astype(BF16)
        dc = dya * gb
        dcp = dyp_ref[...].astype(F32) * gbp_ref[...].astype(F32) * ps
        dcn = dyn_ref[...].astype(F32) * gbn_ref[...].astype(F32) * ns
        dc_dn, dc_up = _shift(dc, dcp, dcn)
        dz = w0 * dc_up + w1 * dc + w2 * dc_dn
        dgc_ref[...] = (dz * gv).astype(BF16)
        dgv_ref[...] = (dz * gc).astype(BF16)
        dcw = [jnp.sum(zs * dc, axis=0, keepdims=True) for zs in (z_dn, z, z_up)]

        o = o_ref[...]
        go = go_ref[...].astype(F32)
        dgh = jnp.zeros((1, HV), F32)
        for h in range(HEADS):
            sl = slice(h * HV, (h + 1) * HV)
            oh = o[:, sl]
            r = _rstd(oh)
            act, sg = _silu_parts(go[:, sl])
            dyb = dy_ref[:, DC + h * HV:DC + (h + 1) * HV].astype(F32)
            on = oh * r * gh_ref[...]
            dgo_ref[:, sl] = (dyb * on * (sg + act * (1.0 - sg))).astype(BF16)
            don = dyb * act
            zz = don * gh_ref[...]
            do_ref[:, sl] = (r * zz - oh * (r * r * r) * jnp.mean(oh * zz, axis=-1, keepdims=True)).astype(BF16)
            dgh = dgh + jnp.sum(don * oh * r, axis=0, keepdims=True)

        @pl.when(i == 0)
        def _():
            dcw_ref[...] = jnp.zeros_like(dcw_ref)
            dgh_ref[...] = jnp.zeros_like(dgh_ref)

        for kk in range(3):
            dcw_ref[kk:kk + 1, :] += dcw[kk]
        dgh_ref[...] += dgh

    def main(c):
        return pl.BlockSpec((tm, DC), lambda i: (i, c))

    def halo(c, imap):
        return pl.BlockSpec((HB, DC), lambda i: (imap(i), c))

    tile = pl.BlockSpec((tm, DC), lambda i: (i, 0))
    return pl.pallas_call(
        body, name=name, grid=(n,),
        in_specs=[main(COL_GB), halo(COL_GB, pmap), halo(COL_GB, nmap), main(COL_GC), halo(COL_GC, pmap),
                  halo(COL_GC, nmap), main(COL_GV), halo(COL_GV, pmap), halo(COL_GV, nmap), main(COL_GO),
                  pl.BlockSpec((None, 3, DC), lambda i: (l, 0, 0)), tile, _gain_spec(l, HV),
                  pl.BlockSpec((tm, D), lambda i: (i, 0)), halo(0, pmap), halo(0, nmap)],
        out_specs=[tile, tile, tile, tile, tile, pl.BlockSpec((3, DC), lambda i: (0, 0)),
                   pl.BlockSpec((1, HV), lambda i: (0, 0))],
        out_shape=[jax.ShapeDtypeStruct((L, DC), BF16)] * 5
        + [jax.ShapeDtypeStruct((3, DC), F32), jax.ShapeDtypeStruct((1, HV), F32)],
        compiler_params=_params(("arbitrary",)),
    )(p, p, p, p, p, p, p, p, p, p, conv_a, o_tot, gh, dy, dy, dy)


def _ffn_specs(tm, L, l, row_axis, sh_axis):
    pmap, nmap = _halo_maps(tm, L)

    def u(off, imap=None, rows=tm):
        if imap is None:
            return pl.BlockSpec((None, rows, SH_FF), lambda *g: (g[sh_axis] + off, g[row_axis], 0))
        return pl.BlockSpec((None, rows, SH_FF), lambda *g: (g[sh_axis] + off, imap(g[row_axis]), 0))

    def cw(off):
        return pl.BlockSpec((None, None, 3, SH_FF), lambda *g: (l, g[sh_axis] + off, 0, 0))

    u_specs = [u(0), u(0, pmap, HB), u(0, nmap, HB), u(FF_HALF), u(FF_HALF, pmap, HB), u(FF_HALF, nmap, HB)]
    return u_specs, [cw(0), cw(FF_HALF)]


def _conv3(x_ref, xp_ref, xn_ref, cw_ref, ps, ns):
    x = x_ref[...].astype(F32)
    x_dn, x_up = _shift(x, xp_ref[...].astype(F32) * ps, xn_ref[...].astype(F32) * ns)
    return cw_ref[0:1, :] * x_dn + cw_ref[1:2, :] * x + cw_ref[2:3, :] * x_up, (x_dn, x, x_up)


def _carry(body, *, name, grid, args, in_specs, out_specs, out_shape, scratch, comm):
    n_in, n_out = len(args), len(out_shape)
    n_ci = len(comm.ins) if comm else 0
    n_co = len(comm.out_shapes) if comm else 0

    def wrapped(*refs):
        ins, refs = refs[:n_in], refs[n_in:]
        ci, refs = refs[:n_ci], refs[n_ci:]
        outs, refs = refs[:n_out], refs[n_out:]
        co, refs = refs[:n_co], refs[n_co:]
        if comm:
            sems, refs = refs[:3], refs[3:]
            step = 0
            for ax, size in enumerate(grid):
                step = step * size + pl.program_id(ax)

            @pl.when(step == 0)
            def _():
                comm.start(ci, co, *sems)

        body(ins, outs, refs)
        if comm:
            @pl.when(step == math.prod(grid) - 1)
            def _():
                comm.wait(ci, co, *sems)

    return pl.pallas_call(
        wrapped, name=name, grid=grid, in_specs=list(in_specs) + [HBM_SPEC] * n_ci,
        out_specs=list(out_specs) + [HBM_SPEC] * n_co,
        out_shape=list(out_shape) + list(comm.out_shapes if comm else ()),
        scratch_shapes=([pltpu.SemaphoreType.DMA((s,)) for s in comm.sems] if comm else []) + list(scratch),
        input_output_aliases={n_in + i: n_out + o for i, o in comm.aliases.items()} if comm else {},
        compiler_params=_params(("arbitrary",) * len(grid)),
    )(*args, *(comm.ins if comm else ()))


def _ffn_act(u8, cw, l, name):
    L = u8.shape[1]
    tm = min(TILE_FFN, L)
    n = L // tm
    u_specs, cw_specs = _ffn_specs(tm, L, l, 0, 1)

    def body(ins, outs, scratch):
        g_ref, gp_ref, gn_ref, v_ref, vp_ref, vn_ref, cwg_ref, cwv_ref = ins
        ps, ns = _edge_scales(pl.program_id(0), n)
        gate, _ = _conv3(g_ref, gp_ref, gn_ref, cwg_ref, ps, ns)
        val, _ = _conv3(v_ref, vp_ref, vn_ref, cwv_ref, ps, ns)
        act, _ = _silu_parts(gate)
        outs[0][...] = (act * val).astype(BF16)
        outs[1][0] = gate.astype(BF16)
        outs[1][1] = val.astype(BF16)

    pair = pl.BlockSpec((2, None, tm, SH_FF), lambda i, d: (0, d, i, 0))
    return _carry(
        body, name=name, grid=(n, FF_HALF), args=(u8, u8, u8, u8, u8, u8, cw, cw), in_specs=u_specs + cw_specs,
        out_specs=[pl.BlockSpec((None, tm, SH_FF), lambda i, d: (d, i, 0)), pair],
        out_shape=[jax.ShapeDtypeStruct((FF_HALF, L, SH_FF), BF16),
                   jax.ShapeDtypeStruct((2, FF_HALF, L, SH_FF), BF16)], scratch=[], comm=None)


def _ffn_act_bwd(conv, da, name):
    L = conv.shape[2]
    tm = min(TILE_FFN, L)

    def body(ins, outs, scratch):
        c_ref, da_ref = ins
        act, sg = _silu_parts(c_ref[0].astype(F32))
        da_f = da_ref[...].astype(F32)
        outs[0][0] = (da_f * c_ref[1].astype(F32) * (sg + act * (1.0 - sg))).astype(BF16)
        outs[0][1] = (da_f * act).astype(BF16)

    pair = pl.BlockSpec((2, None, tm, SH_FF), lambda d, i: (0, d, i, 0))
    return _carry(
        body, name=name, grid=(FF_HALF, L // tm), args=(conv, da),
        in_specs=[pair, pl.BlockSpec((None, tm, SH_FF), lambda d, i: (d, i, 0))], out_specs=[pair],
        out_shape=[jax.ShapeDtypeStruct((2, FF_HALF, L, SH_FF), BF16)], scratch=[], comm=None)[0]


def _ffn_conv_t(du8, u8, cw, l, name):
    L = du8.shape[1]
    tm = min(TILE_FFN, L)
    n = L // tm
    pmap, nmap = _halo_maps(tm, L)

    def body(ins, outs, scratch):
        x_ref, xp_ref, xn_ref, u_ref, cw_ref = ins
        d_u_ref, dcw_ref = outs
        i = pl.program_id(1)
        ps, ns = _edge_scales(i, n)
        x = x_ref[...].astype(F32)
        x_dn, x_up = _shift(x, xp_ref[...].astype(F32) * ps, xn_ref[...].astype(F32) * ns)
        d_u_ref[...] = (cw_ref[0:1, :] * x_up + cw_ref[1:2, :] * x + cw_ref[2:3, :] * x_dn).astype(BF16)

        @pl.when(i == 0)
        def _():
            dcw_ref[...] = jnp.zeros_like(dcw_ref)

        u = u_ref[...].astype(F32)
        for kk, xs in enumerate((x_up, x, x_dn)):
            dcw_ref[kk:kk + 1, :] += jnp.sum(u * xs, axis=0, keepdims=True)

    tile = pl.BlockSpec((None, tm, SH_FF), lambda d, i: (d, i, 0))
    return _carry(
        body, name=name, grid=(N_DEV, n), args=(du8, du8, du8, u8, cw),
        in_specs=[tile, pl.BlockSpec((None, HB, SH_FF), lambda d, i: (d, pmap(i), 0)),
                  pl.BlockSpec((None, HB, SH_FF), lambda d, i: (d, nmap(i), 0)), tile,
                  pl.BlockSpec((None, None, 3, SH_FF), lambda d, i: (l, d, 0, 0))],
        out_specs=[tile, pl.BlockSpec((None, 3, SH_FF), lambda d, i: (d, 0, 0))],
        out_shape=[jax.ShapeDtypeStruct((N_DEV, L, SH_FF), BF16), jax.ShapeDtypeStruct((N_DEV, 3, SH_FF), F32)],
        scratch=[], comm=None)


def _loss_grad(xl, target, y2, g, l, name):
    L = xl.shape[0]
    tm = min(TILE_NORM, L)

    def body(x_ref, t_ref, y2_ref, g_ref, dx_ref, sq_ref, dy_ref, dg_ref):
        err = x_ref[...] - t_ref[...]
        dx = err * (1.0 / D)
        dx_ref[...] = dx
        dy, dg_part = _norm_vjp(y2_ref, g_ref, dx)
        dy_ref[...] = dy.astype(BF16)
        _accumulate_rows(pl.program_id(0), [(sq_ref, jnp.sum(err * err, axis=0, keepdims=True)), (dg_ref, dg_part)])

    tile = pl.BlockSpec((tm, D), lambda i: (i, 0))
    row = pl.BlockSpec((1, D), lambda i: (0, 0))
    return pl.pallas_call(
        body, name=name, grid=(L // tm,), in_specs=[tile, tile, tile, _gain_spec(l)],
        out_specs=[tile, row, tile, row],
        out_shape=[jax.ShapeDtypeStruct((L, D), F32), jax.ShapeDtypeStruct((1, D), F32),
                   jax.ShapeDtypeStruct((L, D), BF16), jax.ShapeDtypeStruct((1, D), F32)],
        compiler_params=_params(("arbitrary",)),
    )(xl, target, y2, g)


MESH = pl.DeviceIdType.MESH
HBM_SPEC = pl.BlockSpec(memory_space=pltpu.HBM)


def _position():
    return lax.axis_index("x"), lax.axis_index("y"), lax.axis_index("c")


def _other_chips(x, y):
    return [(1 - x, y), (x, 1 - y), (1 - x, 1 - y)]


def _all_gather(shards, name):
    n = len(shards)

    def body(*refs):
        x_refs, out_refs = refs[:n], refs[n:2 * n]
        send_sems, recv_sems, local_sems = refs[2 * n:]
        x, y, c = _position()
        me, sibling = (x, y, c), (x, y, 1 - c)
        chips = _other_chips(x, y)

        def slot(t, px, py, pc):
            return out_refs[t].at[:, 4 * px + 2 * py + pc]

        def copy(t, k, block, to, from_input=False):
            return pltpu.make_async_remote_copy(
                src_ref=x_refs[t] if from_input else slot(t, *block), dst_ref=slot(t, *block),
                send_sem=send_sems.at[k * n + t], recv_sem=recv_sems.at[k * n + t], device_id=to, device_id_type=MESH)

        mine = [pltpu.make_async_copy(x_refs[t], slot(t, *me), local_sems.at[t]) for t in range(n)]
        for cp in mine:
            cp.start()
        first = [copy(t, 0, me, sibling, True) for t in range(n)]
        first += [copy(t, 1 + j, me, (*chip, c), True) for j, chip in enumerate(chips) for t in range(n)]
        for cp in first:
            cp.start()
        passed = []
        for j, chip in enumerate(chips):
            for t in range(n):
                copy(t, 1 + j, (*chip, c), me).wait_recv()
                passed.append(copy(t, 4 + j, (*chip, c), sibling))
                passed[-1].start()
        for t in range(n):
            copy(t, 0, sibling, me).wait_recv()
        for j, chip in enumerate(chips):
            for t in range(n):
                copy(t, 4 + j, (*chip, 1 - c), me).wait_recv()
        for cp in first + passed:
            cp.wait_send()
        for cp in mine:
            cp.wait()

    return pl.pallas_call(
        body, name=name,
        out_shape=[jax.ShapeDtypeStruct((s.shape[0], N_DEV) + s.shape[1:], s.dtype) for s in shards],
        in_specs=[HBM_SPEC] * n, out_specs=[HBM_SPEC] * n,
        scratch_shapes=[pltpu.SemaphoreType.DMA((7 * n,)), pltpu.SemaphoreType.DMA((7 * n,)),
                        pltpu.SemaphoreType.DMA((n,))],
    )(*shards)


def _exchange_sibling(grads, name):
    n = len(grads)

    def body(*refs):
        g_refs, out_refs, send_sems, recv_sems = refs[:n], refs[n:2 * n], refs[2 * n], refs[2 * n + 1]
        x, y, c = _position()
        copies = [pltpu.make_async_remote_copy(
            src_ref=g_refs[t].at[:, 2 * k + (1 - c)], dst_ref=out_refs[t].at[:, k], send_sem=send_sems.at[k * n + t],
            recv_sem=recv_sems.at[k * n + t], device_id=(x, y, 1 - c), device_id_type=MESH)
            for k in range(N_CHIP) for t in range(n)]
        for cp in copies:
            cp.start()
        for cp in copies:
            cp.wait()

    return pl.pallas_call(
        body, name=name,
        out_shape=[jax.ShapeDtypeStruct((g.shape[0], N_CHIP) + g.shape[2:], g.dtype) for g in grads],
        in_specs=[HBM_SPEC] * n, out_specs=[HBM_SPEC] * n,
        scratch_shapes=[pltpu.SemaphoreType.DMA((N_CHIP * n,)), pltpu.SemaphoreType.DMA((N_CHIP * n,))],
    )(*grads)


class _Comm(NamedTuple):
    ins: tuple
    out_shapes: tuple
    aliases: dict
    sems: tuple
    start: Callable
    wait: Callable


def _comm_of(ins, out_shapes, aliases, sems, copies):
    def start(ci, co, send, recv, local):
        for cp in copies(ci, co, send, recv, local):
            cp.start()

    def wait(ci, co, send, recv, local):
        for cp in copies(ci, co, send, recv, local):
            cp.wait()

    return _Comm(tuple(ins), tuple(out_shapes), aliases, sems, start, wait)


def _remote(src, dst, send, recv, idx, to):
    return pltpu.make_async_remote_copy(src_ref=src, dst_ref=dst, send_sem=send.at[idx], recv_sem=recv.at[idx],
                                        device_id=to, device_id_type=MESH)


def _gather_ici_comm(shards):
    n = len(shards)

    def copies(ci, co, send, recv, local):
        x, y, c = _position()
        me = 4 * x + 2 * y + c
        mine = [pltpu.make_async_copy(ci[t], co[t].at[:, me], local.at[t]) for t in range(n)]
        return mine + [_remote(ci[t], co[t].at[:, me], send, recv, j * n + t, (cx, cy, c))
                       for j, (cx, cy) in enumerate(_other_chips(x, y)) for t in range(n)]

    outs = [jax.ShapeDtypeStruct((1, N_DEV) + s.shape[1:], s.dtype) for s in shards]
    return _comm_of(shards, outs, {}, (3 * n, 3 * n, n), copies)


def _gather_d2d_comm(partials):
    n = len(partials)

    def copies(ci, co, send, recv, local):
        x, y, c = _position()
        return [_remote(co[t].at[:, 4 * cx + 2 * cy + c], co[t].at[:, 4 * cx + 2 * cy + c], send, recv, k * n + t,
                        (x, y, 1 - c))
                for k, (cx, cy) in enumerate([(x, y)] + _other_chips(x, y)) for t in range(n)]

    outs = [jax.ShapeDtypeStruct(p.shape, p.dtype) for p in partials]
    return _comm_of(partials, outs, {t: t for t in range(n)}, (N_CHIP * n, N_CHIP * n, 1), copies)


def _grads_d2d_comm(grads):
    n = len(grads)

    def copies(ci, co, send, recv, local):
        x, y, c = _position()
        return [_remote(ci[t].at[:, 2 * k + (1 - c)], co[t].at[:, k], send, recv, k * n + t, (x, y, 1 - c))
                for k in range(N_CHIP) for t in range(n)]

    outs = [jax.ShapeDtypeStruct((g.shape[0], N_CHIP) + g.shape[2:], g.dtype) for g in grads]
    return _comm_of(grads, outs, {}, (N_CHIP * n, N_CHIP * n, 1), copies)


def _grads_ici_comm(parts):
    n = len(parts)

    def copies(ci, co, send, recv, local):
        x, y, c = _position()
        my_chip = 2 * x + y
        mine = [pltpu.make_async_copy(ci[t].at[:, my_chip], co[t].at[:, my_chip], local.at[t]) for t in range(n)]
        return mine + [_remote(ci[t].at[:, 2 * cx + cy], co[t].at[:, my_chip], send, recv, j * n + t, (cx, cy, c))
                       for j, (cx, cy) in enumerate(_other_chips(x, y)) for t in range(n)]

    outs = [jax.ShapeDtypeStruct(p.shape, p.dtype) for p in parts]
    return _comm_of(parts, outs, {}, (3 * n, 3 * n, n), copies)


def _row_tile(rows):
    fits = [t for t in range(HB, 257, HB) if rows % t == 0]
    return max(fits) if fits else rows


def _pair_sum(g, recv, c_idx, out_dtype, name):
    lay, _, rows, cols = g.shape
    tr = _row_tile(rows)

    def body(c_ref, g_ref, r_ref, o_ref):
        o_ref[...] = (g_ref[...] + r_ref[...]).astype(o_ref.dtype)

    def spec(blk_of):
        return pl.BlockSpec((None, None, tr, cols), lambda l, k, r, c_ref: (l, blk_of(k, c_ref), r, 0))

    return pl.pallas_call(
        body, name=name,
        grid_spec=pltpu.PrefetchScalarGridSpec(
            num_scalar_prefetch=1, grid=(lay, N_CHIP, rows // tr),
            in_specs=[spec(lambda k, c_ref: 2 * k + c_ref[0]), spec(lambda k, c_ref: k)],
            out_specs=spec(lambda k, c_ref: k)),
        out_shape=jax.ShapeDtypeStruct((lay, N_CHIP, rows, cols), out_dtype),
        compiler_params=_params(("parallel", "parallel", "parallel")),
    )(c_idx, g, recv)


def _exchange_chips(parts, name):
    n = len(parts)

    def body(*refs):
        p_refs, out_refs = refs[:n], refs[n:2 * n]
        send_sems, recv_sems, local_sems = refs[2 * n:]
        x, y, c = _position()
        my_chip = 2 * x + y
        mine = [pltpu.make_async_copy(p_refs[t].at[:, my_chip], out_refs[t].at[:, my_chip], local_sems.at[t])
                for t in range(n)]
        for cp in mine:
            cp.start()
        copies = [pltpu.make_async_remote_copy(
            src_ref=p_refs[t].at[:, 2 * cx + cy], dst_ref=out_refs[t].at[:, my_chip], send_sem=send_sems.at[j * n + t],
            recv_sem=recv_sems.at[j * n + t], device_id=(cx, cy, c), device_id_type=MESH)
            for j, (cx, cy) in enumerate(_other_chips(x, y)) for t in range(n)]
        for cp in copies:
            cp.start()
        for cp in copies:
            cp.wait()
        for cp in mine:
            cp.wait()

    return pl.pallas_call(
        body, name=name, out_shape=[jax.ShapeDtypeStruct(p.shape, p.dtype) for p in parts],
        in_specs=[HBM_SPEC] * n, out_specs=[HBM_SPEC] * n,
        scratch_shapes=[pltpu.SemaphoreType.DMA((3 * n,)), pltpu.SemaphoreType.DMA((3 * n,)),
                        pltpu.SemaphoreType.DMA((n,))],
    )(*parts)


def _sum_adamw(parts, w, m, v, name):
    lay, rows, cols = w.shape
    tr = _row_tile(rows)

    def body(p_ref, w_ref, m_ref, v_ref, g_ref, d_ref, nm_ref, nv_ref):
        g = ((p_ref[0].astype(F32) + p_ref[1].astype(F32)) + p_ref[2].astype(F32)) + p_ref[3].astype(F32)
        g_ref[...] = g
        nm = ADAM_B1 * m_ref[...] + (1.0 - ADAM_B1) * g
        nv = ADAM_B2 * v_ref[...] + (1.0 - ADAM_B2) * (g * g)
        nm_ref[...] = nm
        nv_ref[...] = nv
        m_hat = nm / (1.0 - ADAM_B1 ** ADAM_STEP)
        v_hat = nv / (1.0 - ADAM_B2 ** ADAM_STEP)
        d_ref[...] = -ADAM_LR * (m_hat / (jnp.sqrt(v_hat) + ADAM_EPS) + ADAM_WD * w_ref[...])

    tile = pl.BlockSpec((None, tr, cols), lambda l, r: (l, r, 0))
    return pl.pallas_call(
        body, name=name, grid=(lay, rows // tr),
        in_specs=[pl.BlockSpec((None, N_CHIP, tr, cols), lambda l, r: (l, 0, r, 0)), tile, tile, tile],
        out_specs=[tile] * 4, out_shape=[jax.ShapeDtypeStruct((lay, rows, cols), F32)] * 4,
        compiler_params=_params(("parallel", "parallel")),
    )(parts, w, m, v)


def _pad_rows(flat, rows):
    return jnp.pad(flat, (0, rows * LANES - flat.shape[0])).reshape(rows, LANES)


def _pack_small(tree):
    sh = jnp.concatenate([tree[n].reshape(-1) for n, _, _ in SMALL_SHARDED])
    rep = jnp.concatenate([tree[n].reshape(-1) for n, _ in REPLICATED])
    return jnp.concatenate([_pad_rows(sh, ROWS_SSH), _pad_rows(rep, ROWS_REP)], axis=0)


def _unpack_small(buf):
    out = {}
    for flat, items in ((buf[:ROWS_SSH].reshape(-1), [(n, s) for n, s, _ in SMALL_SHARDED]),
                        (buf[ROWS_SSH:].reshape(-1), REPLICATED)):
        off = 0
        for n, s in items:
            out[n] = flat[off:off + math.prod(s)].reshape(s)
            off += math.prod(s)
    return out


def _blocks_from_full(full, s, ax):
    return jnp.stack([lax.slice_in_dim(full, d * s[ax], (d + 1) * s[ax], axis=ax) for d in range(N_DEV)])


def _pack_small_grads(sharded_blocks, replicated):
    sh = jnp.concatenate([sharded_blocks[n].reshape(N_DEV, -1) for n, _, _ in SMALL_SHARDED], axis=1)
    sh = jnp.pad(sh, ((0, 0), (0, ROWS_SSH * LANES - sh.shape[1]))).reshape(N_DEV, ROWS_SSH, LANES)
    rep = _pad_rows(jnp.concatenate([replicated[n].reshape(-1) for n, _ in REPLICATED]), ROWS_REP)
    return jnp.concatenate([sh, jnp.broadcast_to(rep[None], (N_DEV, ROWS_REP, LANES))], axis=1)


def _layer_fwd(x, h1, wts, big, l, l_next, next_shards, own_ffn):
    L = x.shape[0]
    tm = min(TILE_MM, L)
    nt = L // tm
    nxt = dict(zip(BIG, next_shards)) if next_shards else {}
    if own_ffn:
        at_proj_in, at_ffn_up = {"own_up": own_ffn[0], "own_down": own_ffn[1]}, nxt
    else:
        at_proj_in = {n: nxt[n] for n in ("w_in", "w_out") if n in nxt}
        at_ffn_up = {n: nxt[n] for n in ("w_up", "w_down") if n in nxt}

    def carried(result, comm):
        return (result[0], list(result[1:])) if comm else (result, [])

    comm = _gather_ici_comm(list(at_proj_in.values())) if at_proj_in else None
    p, got = carried(_mm(h1, big["w_in"], dims=NT, grid=(nt, D_INP // TILE_IN, 1),
                         a_spec=((tm, D), lambda i, j, k: (i, 0)), b_spec=((None, TILE_IN, D), lambda i, j, k: (0, j, 0)),
                         o_spec=((tm, TILE_IN), lambda i, j, k: (i, j)), out_shape=jax.ShapeDtypeStruct((L, D_INP), BF16),
                         tile=(tm, TILE_IN), name="proj_in", comm=comm), comm)
    stage1 = dict(zip(at_proj_in, got))
    o_f, o_b, sp_f, sp_b = _gla_fwd_both(p, (wts["gpad_f"], wts["gpad_b"]), (wts["bias_f"], wts["bias_b"]), l,
                                         "gla_fwd_both")
    y_cat, o_tot = _mixer_out(p, wts["conv_a"], wts["gh"], l, o_f, o_b, "mixer_out")
    comm = _gather_d2d_comm([stage1["own_up"], stage1["own_down"]]) if own_ffn else None
    y, got = carried(_mm(y_cat, big["w_out"], dims=NN, grid=(nt, 1, 1),
                         a_spec=((tm, D), lambda i, j, k: (i, 0)), b_spec=((None, D, D), lambda i, j, k: (0, 0, 0)),
                         o_spec=((tm, D), lambda i, j, k: (i, 0)), out_shape=jax.ShapeDtypeStruct((L, D), BF16),
                         tile=(tm, D), name="proj_out", comm=comm), comm)
    if own_ffn:
        big = dict(big, w_up=got[0], w_down=got[1].reshape(1, FF_HALF, SH_FF, D))
    x1, h2 = _post_pre(x, y, wts["g2"], l, wts["g3"], l, "post_pre_mix")
    comm = _gather_ici_comm(list(at_ffn_up.values())) if at_ffn_up else None
    u8, got = carried(_mm(h2, big["w_up"], dims=NT, grid=(nt, N_DEV, 1),
                          a_spec=((tm, D), lambda i, j, k: (i, 0)),
                          b_spec=((None, None, SH_FF, D), lambda i, j, k: (0, j, 0, 0)),
                          o_spec=((None, tm, SH_FF), lambda i, j, k: (j, i, 0)),
                          out_shape=jax.ShapeDtypeStruct((N_DEV, L, SH_FF), BF16), tile=(tm, SH_FF), name="ffn_up",
                          comm=comm), comm)
    stage1.update(zip(at_ffn_up, got))
    a, conv = _ffn_act(u8, wts["cw"], l, "ffn_act")
    tm1 = min(TILE_MM_KIN, L)
    comm = _gather_d2d_comm([stage1[n] for n in BIG]) if next_shards else None
    y2, gathered = carried(_mm(a, big["w_down"], dims=NN, grid=(L // tm1, 1, 1), kin=FF_HALF,
                               a_spec=((FF_HALF, tm1, SH_FF), lambda i, j, k: (0, i, 0)),
                               b_spec=((None, FF_HALF, SH_FF, D), lambda i, j, k: (0, 0, 0, 0)),
                               o_spec=((tm1, D), lambda i, j, k: (i, 0)),
                               out_shape=jax.ShapeDtypeStruct((L, D), BF16), tile=(tm1, D), name="ffn_down",
                               comm=comm), comm)
    x2, h1_next = _post_pre(x1, y2, wts["g4"], l, wts["g1"], l_next, "post_pre_ffn")
    saved = dict(x=x, h1=h1, p=p, o_tot=o_tot, sp_f=sp_f, sp_b=sp_b, y_cat=y_cat, y=y, x1=x1, h2=h2, u8=u8, a=a, y2=y2,
                 big=big, conv=conv)
    return x2, h1_next, saved, gathered


def _layer_bwd(dx2, wts, s, l, pending, c_idx, early_ffn, post, below_y2):
    L = dx2.shape[0]
    big = s["big"]
    tm = min(TILE_MM, L)
    nt = L // tm
    tm1 = min(TILE_MM_KIN, L)
    tk = min(TILE_MM_TOKENS, L)
    nkt = L // tk
    dy2, dg4 = post if post else _norm_bwd(s["y2"], wts["g4"], l, dx2, None, "norm_bwd_ffn_post")
    da = _mm(dy2, big["w_down"], dims=NT, grid=(nt, FF_HALF, 1),
             a_spec=((tm, D), lambda i, j, k: (i, 0)), b_spec=((None, None, SH_FF, D), lambda i, j, k: (0, j, 0, 0)),
             o_spec=((None, tm, SH_FF), lambda i, j, k: (j, i, 0)),
             out_shape=jax.ShapeDtypeStruct((FF_HALF, L, SH_FF), BF16), tile=(tm, SH_FF), name="ffn_down_dx",
             comm=_grads_d2d_comm(pending) if pending else None)
    pairs = None
    if pending:
        da, from_sibling = da[0], da[1:]
        pairs = [_pair_sum(g, r, c_idx, BF16, "grads_pair_sum") for g, r in zip(pending, from_sibling)]
    dw_down = _mm(s["a"], dy2, dims=TN, grid=(FF_HALF, 1, nkt),
                  a_spec=((None, tk, SH_FF), lambda i, j, k: (i, k, 0)), b_spec=((tk, D), lambda i, j, k: (k, 0)),
                  o_spec=((SH_FF, D), lambda i, j, k: (i, 0)), out_shape=jax.ShapeDtypeStruct((DFF, D), F32),
                  tile=(SH_FF, D), name="ffn_down_dw")
    du = _ffn_act_bwd(s["conv"], da, "ffn_act_bwd")
    d_u8, dcw = _ffn_conv_t(du.reshape(N_DEV, L, SH_FF), s["u8"], wts["cw"], l, "ffn_conv_t")
    dh2 = _mm(d_u8, big["w_up"], dims=NN, grid=(L // tm1, 1, N_DEV // FF_HALF), kin=FF_HALF,
              a_spec=((FF_HALF, tm1, SH_FF), lambda i, j, k: (k, i, 0)),
              b_spec=((None, FF_HALF, SH_FF, D), lambda i, j, k: (0, k, 0, 0)),
              o_spec=((tm1, D), lambda i, j, k: (i, 0)), out_shape=jax.ShapeDtypeStruct((L, D), BF16),
              tile=(tm1, D), name="ffn_up_dx",
              comm=_grads_ici_comm([q for n, q in zip(BIG, pairs) if n != "w_up"]) if pending else None)
    if pending:
        dh2, rest_parts = dh2[0], dh2[1:]
    dw_up = _mm(d_u8, s["h2"], dims=TN, grid=(N_DEV, 1, nkt),
                a_spec=((None, tk, SH_FF), lambda i, j, k: (i, k, 0)), b_spec=((tk, D), lambda i, j, k: (k, 0)),
                o_spec=((None, SH_FF, D), lambda i, j, k: (i, 0, 0)),
                out_shape=jax.ShapeDtypeStruct((N_DEV, SH_FF, D), F32), tile=(SH_FF, D), name="ffn_up_dw",
                comm=_grads_ici_comm([pairs[BIG.index("w_up")]]) if pending else None)
    parts = None
    if pending:
        dw_up, up_part = dw_up[0], dw_up[1]
        parts = [rest_parts[0], rest_parts[1], up_part, rest_parts[2]]
    dx1, dg3, dy, dg2 = _norm_bwd(s["x1"], wts["g3"], l, dh2, dx2, "norm_bwd_ffn_pre_mix_post",
                                  then=(s["y"], wts["g2"], l))
    dy_cat = _mm(dy, big["w_out"], dims=NT, grid=(nt, 1, 1),
                 a_spec=((tm, D), lambda i, j, k: (i, 0)), b_spec=((None, D, D), lambda i, j, k: (0, 0, 0)),
                 o_spec=((tm, D), lambda i, j, k: (i, 0)), out_shape=jax.ShapeDtypeStruct((L, D), BF16),
                 tile=(tm, D), name="proj_out_dx")
    dw_out = _mm(s["y_cat"], dy, dims=TN, grid=(1, 1, nkt),
                 a_spec=((tk, D), lambda i, j, k: (k, 0)), b_spec=((tk, D), lambda i, j, k: (k, 0)),
                 o_spec=((D, D), lambda i, j, k: (0, 0)), out_shape=jax.ShapeDtypeStruct((D, D), F32),
                 tile=(D, D), name="proj_out_dw")
    dgb, dgc, dgv, dgo, d_o, dconv_a, dgh = _mixer_out_bwd(s["p"], wts["conv_a"], wts["gh"], l, s["o_tot"], dy_cat,
                                                          "mixer_out_bwd")
    part_f = _gla_bwd(s["p"], wts["gpad_f"], wts["bias_f"], l, s["sp_f"], d_o, None, True, "gla_bwd_f")
    dp, dgp_b, dbias_b = _gla_bwd(s["p"], wts["gpad_b"], wts["bias_b"], l, s["sp_b"], d_o,
                                  list(part_f[:4]) + [dgb, dgc, dgv, dgo], False, "gla_bwd_b")
    early = [dw_up[None], dw_down.reshape(1, N_DEV, DFF // N_DEV, D)] if early_ffn else None
    dh1 = _mm(dp, big["w_in"], dims=NN, grid=(L // tm1, 1, 1),
              a_spec=((tm1, D_INP), lambda i, j, k: (i, 0)), b_spec=((None, D_INP, D), lambda i, j, k: (0, 0, 0)),
              o_spec=((tm1, D), lambda i, j, k: (i, 0)), out_shape=jax.ShapeDtypeStruct((L, D), BF16),
              tile=(tm1, D), name="proj_in_dx", comm=_grads_d2d_comm(early) if early_ffn else None)
    if early_ffn:
        dh1, early_sibling = dh1[0], dh1[1:]
        early = [_pair_sum(g, r, c_idx, BF16, "grads_pair_sum") for g, r in zip(early, early_sibling)]
    dw_in = _mm(dp, s["h1"], dims=TN, grid=(D_INP // TILE_IN, 1, nkt),
                a_spec=((tk, TILE_IN), lambda i, j, k: (k, i)), b_spec=((tk, D), lambda i, j, k: (k, 0)),
                o_spec=((TILE_IN, D), lambda i, j, k: (i, 0)), out_shape=jax.ShapeDtypeStruct((D_INP, D), F32),
                tile=(TILE_IN, D), name="proj_in_dw", comm=_grads_ici_comm(early) if early_ffn else None)
    if early_ffn:
        dw_in, early = dw_in[0], dw_in[1:]
    if below_y2 is None:
        dx0, dg1 = _norm_bwd(s["x"], wts["g1"], l, dh1, dx1, "norm_bwd_mix_pre")
        post_below = None
    else:
        dx0, dg1, *post_below = _norm_bwd(s["x"], wts["g1"], l, dh1, dx1, "norm_bwd_mix_pre_ffn_post",
                                          then=(below_y2, wts["g4"], l - 1))
    grads = dict(
        norm_mix_pre=dg1[0], norm_mix_post=dg2[0], norm_ffn_pre=dg3[0], norm_ffn_post=dg4[0],
        gate_bias_fwd=part_f[5][0], gate_bias_bwd=dbias_b[0], gla_head_norm=dgh[0],
        w_in=dw_in[:D_IN].reshape(N_DEV, SH_IN, D), w_out=dw_out.reshape(N_DEV, D // N_DEV, D), w_up=dw_up,
        w_down=dw_down.reshape(N_DEV, DFF // N_DEV, D),
        conv_a=_blocks_from_full(dconv_a, (3, DC // N_DEV), 1),
        gate_up_fwd=_blocks_from_full(part_f[4][:RANK], (RANK, DK // N_DEV), 1),
        gate_up_bwd=_blocks_from_full(dgp_b[RANK:2 * RANK], (RANK, DK // N_DEV), 1),
        conv_ffn=dcw.reshape(N_DEV, 3, SH_FF))
    return dx0, grads, parts, early, post_below


def _matmul_weights(g_in, g_out, g_up, g_down):
    w_in = jnp.concatenate([g_in.reshape(1, D_IN, D), jnp.zeros((1, D_INP - D_IN, D), BF16)], axis=1)
    return dict(w_in=w_in, w_out=g_out.reshape(1, D, D), w_up=g_up,
                w_down=None if g_down is None else g_down.reshape(1, FF_HALF, SH_FF, D))


def _small_weights(g_small, rep):
    small = {n: jnp.moveaxis(t, 0, 1) for n, t in jax.vmap(_unpack_small)(
        jnp.concatenate([g_small[0], jnp.zeros((N_DEV, ROWS_REP, LANES), F32)], axis=1)).items()
        if n in [s[0] for s in SMALL_SHARDED]}
    conv_a = jnp.concatenate([small["conv_a"][:, d] for d in range(N_DEV)], axis=2)
    gate_f = jnp.concatenate([small["gate_up_fwd"][:, d] for d in range(N_DEV)], axis=2).astype(BF16)
    gate_b = jnp.concatenate([small["gate_up_bwd"][:, d] for d in range(N_DEV)], axis=2).astype(BF16)
    zeros = jnp.zeros((DEPTH, LR_BLK, DK), BF16)
    return dict(
        conv_a=conv_a, cw=small["conv_ffn"],
        gpad_f=zeros.at[:, :RANK].set(gate_f), gpad_b=zeros.at[:, RANK:2 * RANK].set(gate_b),
        bias_f=rep["gate_bias_fwd"][:, None, :], bias_b=rep["gate_bias_bwd"][:, None, :],
        gh=rep["gla_head_norm"][:, None, :],
        g1=rep["norm_mix_pre"][:, None, :], g2=rep["norm_mix_post"][:, None, :],
        g3=rep["norm_ffn_pre"][:, None, :], g4=rep["norm_ffn_post"][:, None, :])


def kernel(x, norm_mix_pre, norm_mix_post, norm_ffn_pre, norm_ffn_post, w_in, conv_a, gate_up_fwd, gate_bias_fwd, gate_up_bwd, gate_bias_bwd, gla_head_norm, w_out, w_up, conv_ffn, w_down, loss_target, m_norm_mix_pre, m_norm_mix_post, m_norm_ffn_pre, m_norm_ffn_post, m_w_in, m_conv_a, m_gate_up_fwd, m_gate_bias_fwd, m_gate_up_bwd, m_gate_bias_bwd, m_gla_head_norm, m_w_out, m_w_up, m_conv_ffn, m_w_down, v_norm_mix_pre, v_norm_mix_post, v_norm_ffn_pre, v_norm_ffn_post, v_w_in, v_conv_a, v_gate_up_fwd, v_gate_bias_fwd, v_gate_up_bwd, v_gate_bias_bwd, v_gla_head_norm, v_w_out, v_w_up, v_conv_ffn, v_w_down):
    w = dict(norm_mix_pre=norm_mix_pre, norm_mix_post=norm_mix_post, norm_ffn_pre=norm_ffn_pre,
             norm_ffn_post=norm_ffn_post, w_in=w_in, conv_a=conv_a, gate_up_fwd=gate_up_fwd,
             gate_bias_fwd=gate_bias_fwd, gate_up_bwd=gate_up_bwd, gate_bias_bwd=gate_bias_bwd,
             gla_head_norm=gla_head_norm, w_out=w_out, w_up=w_up, conv_ffn=conv_ffn, w_down=w_down)
    m = dict(norm_mix_pre=m_norm_mix_pre, norm_mix_post=m_norm_mix_post, norm_ffn_pre=m_norm_ffn_pre,
             norm_ffn_post=m_norm_ffn_post, w_in=m_w_in, conv_a=m_conv_a, gate_up_fwd=m_gate_up_fwd,
             gate_bias_fwd=m_gate_bias_fwd, gate_up_bwd=m_gate_up_bwd, gate_bias_bwd=m_gate_bias_bwd,
             gla_head_norm=m_gla_head_norm, w_out=m_w_out, w_up=m_w_up, conv_ffn=m_conv_ffn, w_down=m_w_down)
    v = dict(norm_mix_pre=v_norm_mix_pre, norm_mix_post=v_norm_mix_post, norm_ffn_pre=v_norm_ffn_pre,
             norm_ffn_post=v_norm_ffn_post, w_in=v_w_in, conv_a=v_conv_a, gate_up_fwd=v_gate_up_fwd,
             gate_bias_fwd=v_gate_bias_fwd, gate_up_bwd=v_gate_up_bwd, gate_bias_bwd=v_gate_bias_bwd,
             gla_head_norm=v_gla_head_norm, w_out=v_w_out, w_up=v_w_up, conv_ffn=v_conv_ffn, w_down=v_w_down)
    axes = ("x", "y", "c")
    L = x.shape[1]
    x0 = x.reshape(L, D)
    target = loss_target.reshape(L, D)

    def rows_major(tree):
        return {n: jnp.swapaxes(tree[n], 1, 2) if n in TRANSPOSED else tree[n] for n in BIG}

    w_small = _pack_small(w)
    wt, mt, vt = rows_major(w), rows_major(m), rows_major(v)
    w16 = {n: wt[n].astype(BF16) for n in BIG}
    g_in, g_out, g_small = _all_gather([w16["w_in"][0:1], w16["w_out"][0:1], w_small[None, :ROWS_SSH]],
                                       "gather_weights")
    wts = _small_weights(g_small, w)
    big = {n: t for n, t in _matmul_weights(g_in, g_out, None, None).items() if t is not None}

    h1 = _norm_cast(x0, wts["g1"], 0, "norm_first")
    xl, saved = x0, []
    for l in range(DEPTH):
        nxt = [w16[n][l + 1:l + 2] for n in BIG] if l + 1 < DEPTH else None
        own = [w16["w_up"][0:1], w16["w_down"][0:1]] if l == 0 else None
        xl, h1, s, gathered = _layer_fwd(xl, h1, wts, big, l, min(l + 1, DEPTH - 1), nxt, own)
        saved.append(s)
        if nxt:
            big = _matmul_weights(*gathered)
    dx, sq, *post = _loss_grad(xl, target, saved[-1]["y2"], wts["g4"], DEPTH - 1, "loss_grad")
    loss = lax.psum(0.5 * jnp.sum(sq) / D, axes)

    c_idx = lax.axis_index("c").astype(jnp.int32).reshape(1)
    layer_grads, layer_parts, pending = [None] * DEPTH, [None] * DEPTH, None
    for l in reversed(range(DEPTH)):
        dx, layer_grads[l], done, early, post = _layer_bwd(
            dx, wts, saved[l], l, pending, c_idx, l == 0, post, saved[l - 1]["y2"] if l > 0 else None)
        if pending:
            layer_parts[l + 1] = done
        pending = [layer_grads[l][n][None] for n in BIG]
    small_names = [n for n, _, _ in SMALL_SHARDED] + [n for n, _ in REPLICATED]
    stacked = {n: jnp.stack([g[n] for g in layer_grads]) for n in small_names}
    g_small = _pack_small_grads({n: jnp.moveaxis(stacked[n], 0, 1) for n, _, _ in SMALL_SHARDED}, stacked)
    last = [g for n, g in zip(BIG, pending) if n in ("w_in", "w_out")] + [g_small[None]]
    from_sibling = _exchange_sibling(last, "grads_to_sibling")
    pairs = [_pair_sum(g, r, c_idx, BF16 if i < 2 else F32, "grads_pair_sum")
             for i, (g, r) in enumerate(zip(last, from_sibling))]
    parts = _exchange_chips(pairs, "grads_to_chips")
    layer_parts[0] = [parts[0], parts[1], early[0], early[1]]

    results = {}
    for i, n in enumerate(BIG):
        part = jnp.concatenate([layer_parts[l][i] for l in range(DEPTH)], axis=0)
        results[n] = _sum_adamw(part, wt[n], mt[n], vt[n], "sum_adamw")
        if n in TRANSPOSED:
            results[n] = [jnp.swapaxes(r, 1, 2) for r in results[n]]
    small = _sum_adamw(parts[-1], w_small[None], _pack_small(m)[None], _pack_small(v)[None], "sum_adamw_small")
    small = [_unpack_small(buf[0]) for buf in small]
    outs = [loss, dx.reshape(x.shape)]
    for i in range(4):
        outs += [results[n][i] if n in BIG else small[i][n] for n in WEIGHT_ORDER]
    return tuple(outs)
```
